```python
import math
import jax, jax.numpy as jnp
from jax import lax
import numpy as np

D_MODEL = 1024
BATCH = 8
SEQ = 2048
DEPTH = 2

CHUNK = 64
N_META = 16
S5_WIDTH = D_MODEL // 2
S5_GROUP = 16
S5_GROUPS = S5_WIDTH // S5_GROUP
S5_STATE = 64
S5_DT_MIN = 1e-3
S5_DT_MAX = 1e-1
DA_HEADS = 4
DA_HEAD_DIM = 64
DA_WIDTH = DA_HEADS * 2 * DA_HEAD_DIM
Q_BLOCK = 128
D_RNN = 1280
LRU_BLOCKS = 10
LRU_BLOCK_W = D_RNN // LRU_BLOCKS
CONV_WIDTH = 4
LRU_C = 8.0
N_GROUPS = 4
EXPERTS_PER_GROUP = 8
N_EXPERTS = N_GROUPS * EXPERTS_PER_GROUP
TOP_K_FINE = 2
D_EXPERT = 512
EXPERT_BLOCK = 128
ALPHA = (2 * DEPTH) ** 0.25
BETA = (8 * DEPTH) ** -0.25
LN_EPS = 1e-5
NEG_INF = -1e30

kernel_name = "hybrid_s5_diffattn_rglru_hmoe_block"

F32 = jnp.float32


def _layer_norm(z, g, b):
    zf = z.astype(F32)
    mu = jnp.mean(zf, -1, keepdims=True)
    var = jnp.mean(jnp.square(zf - mu), -1, keepdims=True)
    return (zf - mu) * lax.rsqrt(var + LN_EPS) * g.astype(F32) + b.astype(F32)


def _complex_linear_combine(e1, e2):
    a1r, a1i, b1r, b1i = e1
    a2r, a2i, b2r, b2i = e2
    return (a2r * a1r - a2i * a1i, a2r * a1i + a2i * a1r,
            a2r * b1r - a2i * b1i + b2r, a2r * b1i + a2i * b1r + b2i)


def _real_linear_combine(e1, e2):
    a1, b1 = e1
    a2, b2 = e2
    return (a2 * a1, a2 * b1 + b2)


def _s5(u, lam_re, lam_im, log_dt, b_re, b_im, c_re, c_im, d, w_glu, b_glu):
    bsz, L, _ = u.shape
    ug = u.reshape(bsz, L, S5_GROUPS, S5_GROUP)
    dt = jnp.exp(log_dt.astype(F32))[:, None]
    lr = jnp.minimum(lam_re.astype(F32), -1e-4)
    li = lam_im.astype(F32)
    mag = jnp.exp(lr * dt)
    ar = mag * jnp.cos(li * dt)
    ai = mag * jnp.sin(li * dt)
    den = lr * lr + li * li
    nr, ni = ar - 1.0, ai
    fr = ((nr * lr + ni * li) / den)[..., None]
    fi = ((ni * lr - nr * li) / den)[..., None]
    br, bi = b_re.astype(F32), b_im.astype(F32)
    bbr = fr * br - fi * bi
    bbi = fr * bi + fi * br
    bur = jnp.einsum('blgh,gph->blgp', ug, bbr)
    bui = jnp.einsum('blgh,gph->blgp', ug, bbi)
    a_r = jnp.broadcast_to(ar, (1, L, S5_GROUPS, S5_STATE))
    a_i = jnp.broadcast_to(ai, (1, L, S5_GROUPS, S5_STATE))
    _, _, hr, hi = lax.associative_scan(_complex_linear_combine, (a_r, a_i, bur, bui), axis=1)
    y = (jnp.einsum('blgp,ghp->blgh', hr, c_re.astype(F32))
         - jnp.einsum('blgp,ghp->blgh', hi, c_im.astype(F32)))
    y = y.reshape(bsz, L, S5_WIDTH) + d.astype(F32) * u
    y = jax.nn.gelu(y)
    return y * jax.nn.sigmoid(y @ w_glu.astype(F32) + b_glu.astype(F32))


def _diff_attention(q, k, v, lq1, lk1, lq2, lk2, subln_g, lam_init):
    bsz, L = q.shape[:2]
    Lp = -(-L // Q_BLOCK) * Q_BLOCK
    nqb = Lp // Q_BLOCK
    pad = Lp - L
    padw = lambda t: jnp.pad(t, ((0, 0), (0, pad)) + ((0, 0),) * (t.ndim - 2))
    q, k, v = padw(q), padw(k), padw(v)
    pos = jnp.arange(Lp)
    cid = jnp.where(pos < N_META, 0, 1 + (pos - N_META) // CHUNK)
    cid = jnp.where(pos < L, cid, Lp)
    lam = (jnp.exp(jnp.sum(lq1.astype(F32) * lk1.astype(F32)))
           - jnp.exp(jnp.sum(lq2.astype(F32) * lk2.astype(F32))) + lam_init)
    scale = DA_HEAD_DIM ** -0.5
    qb = q.reshape(bsz, nqb, Q_BLOCK, DA_HEADS, 2, DA_HEAD_DIM).transpose(1, 0, 2, 3, 4, 5)
    cqb = cid.reshape(nqb, Q_BLOCK)

    def block(args):
        q_blk, c_blk = args
        s = jnp.einsum('bqhcd,bkhcd->bchqk', q_blk, k) * scale
        mask = cid[None, :] <= c_blk[:, None]
        p = jax.nn.softmax(jnp.where(mask, s, NEG_INF), axis=-1)
        w = p[:, 0] - lam * p[:, 1]
        return jnp.einsum('bhqk,bkhe->bqhe', w, v)

    o = lax.map(block, (qb, cqb))
    o = o.transpose(1, 0, 2, 3, 4).reshape(bsz, Lp, DA_HEADS, 2 * DA_HEAD_DIM)[:, :L]
    o = o * lax.rsqrt(jnp.mean(o * o, -1, keepdims=True) + LN_EPS) * subln_g.astype(F32)
    return (o * (1.0 - lam_init)).reshape(bsz, L, DA_WIDTH)


def _s5_diffattn_mixer(h, w_in, lam_re, lam_im, log_dt, b_re, b_im, c_re, c_im, d, w_glu, b_glu,
                       lq1, lk1, lq2, lk2, subln_g, w_out, lam_init):
    bsz, L, _ = h.shape
    z = (h @ w_in).astype(F32)
    u = z[..., :S5_WIDTH]
    o = S5_WIDTH
    q = z[..., o:o + DA_WIDTH].reshape(bsz, L, DA_HEADS, 2, DA_HEAD_DIM)
    k = z[..., o + DA_WIDTH:o + 2 * DA_WIDTH].reshape(bsz, L, DA_HEADS, 2, DA_HEAD_DIM)
    v = z[..., o + 2 * DA_WIDTH:].reshape(bsz, L, DA_HEADS, 2 * DA_HEAD_DIM)
    y_s5 = _s5(u, lam_re, lam_im, log_dt, b_re, b_im, c_re, c_im, d, w_glu, b_glu)
    y_da = _diff_attention(q, k, v, lq1, lk1, lq2, lk2, subln_g, lam_init)
    return jnp.concatenate([y_s5, y_da], axis=-1) @ w_out.astype(F32)


def _rglru_mixer(h, w_in, conv_w, conv_b, w_a, b_a, w_x, b_x, lru_lambda, w_out):
    bsz, L, _ = h.shape
    z = (h @ w_in).astype(F32)
    gate_br, rec = z[..., :D_RNN], z[..., D_RNN:]
    rp = jnp.pad(rec, ((0, 0), (CONV_WIDTH - 1, 0), (0, 0)))
    cw = conv_w.astype(F32)
    xc = sum(rp[:, j:j + L] * cw[j] for j in range(CONV_WIDTH)) + conv_b.astype(F32)
    xb = xc.reshape(bsz, L, LRU_BLOCKS, LRU_BLOCK_W)
    r = jax.nn.sigmoid(jnp.einsum('blnd,nde->blne', xb, w_a.astype(F32))
                       + b_a.astype(F32).reshape(LRU_BLOCKS, LRU_BLOCK_W))
    ig = jax.nn.sigmoid(jnp.einsum('blnd,nde->blne', xb, w_x.astype(F32))
                        + b_x.astype(F32).reshape(LRU_BLOCKS, LRU_BLOCK_W))
    log_a = -LRU_C * r * jax.nn.softplus(-lru_lambda.astype(F32).reshape(LRU_BLOCKS, LRU_BLOCK_W))
    a = jnp.exp(log_a)
    bterm = jnp.sqrt(-jnp.expm1(2.0 * log_a)) * (ig * xb)
    _, hs = lax.associative_scan(_real_linear_combine, (a, bterm), axis=1)
    y = jax.nn.gelu(gate_br) * hs.reshape(bsz, L, D_RNN)
    return y @ w_out.astype(F32)


def _hier_moe(h, w_coarse, b_coarse, w_fine, b_fine, w_gate, w_up, w_down):
    bsz, L, d = h.shape
    x2 = h.reshape(-1, d)
    T = x2.shape[0]
    xf = x2.astype(F32)
    coarse = xf @ w_coarse.astype(F32) + b_coarse.astype(F32)
    p_coarse = jax.nn.softmax(coarse, axis=-1)
    grp = jnp.argmax(coarse, axis=-1)
    p_grp = jnp.take_along_axis(p_coarse, grp[:, None], axis=1)
    fine_all = jnp.einsum('td,gde->tge', xf, w_fine.astype(F32)) + b_fine.astype(F32)
    fine = jnp.take_along_axis(fine_all, grp[:, None, None], axis=1)[:, 0]
    top_v, top_i = lax.top_k(fine, TOP_K_FINE)
    gates = p_grp * jax.nn.softmax(top_v, axis=-1)
    experts = grp[:, None] * EXPERTS_PER_GROUP + top_i
    n_assign = T * TOP_K_FINE
    nb = -(-(n_assign + N_EXPERTS * (EXPERT_BLOCK - 1)) // EXPERT_BLOCK)
    e_flat = experts.reshape(-1)
    tok_flat = jnp.repeat(jnp.arange(T), TOP_K_FINE)
    g_flat = gates.reshape(-1)
    order = jnp.argsort(e_flat)
    e_s, tok_s, g_s = e_flat[order], tok_flat[order], g_flat[order]
    counts = jnp.bincount(e_flat, length=N_EXPERTS)
    padded = (counts + EXPERT_BLOCK - 1) // EXPERT_BLOCK * EXPERT_BLOCK
    pad_end = jnp.cumsum(padded)
    pad_start = pad_end - padded
    raw_start = jnp.cumsum(counts) - counts
    dest = pad_start[e_s] + jnp.arange(n_assign) - raw_start[e_s]
    slot_tok = jnp.full((nb * EXPERT_BLOCK,), T, jnp.int32).at[dest].set(tok_s)
    slot_gate = jnp.zeros((nb * EXPERT_BLOCK,), F32).at[dest].set(g_s)
    block_expert = jnp.minimum(
        jnp.searchsorted(pad_end, jnp.arange(nb) * EXPERT_BLOCK, side='right'), N_EXPERTS - 1)
    x_ext = jnp.concatenate([x2, jnp.zeros((1, d), x2.dtype)], axis=0)
    xb = x_ext[slot_tok].reshape(nb, EXPERT_BLOCK, d)

    def expert_block(args):
        xblk, e = args
        hid = jax.nn.silu(xblk @ w_gate[e]) * (xblk @ w_up[e])
        return hid @ w_down[e]

    yb = lax.map(expert_block, (xb, block_expert)).reshape(-1, d)
    y = jnp.zeros((T + 1, d), F32).at[slot_tok].add(yb.astype(F32) * slot_gate[:, None])
    return y[:T].reshape(bsz, L, d)


def setup_inputs(seed: int = 0) -> dict:
    key = jax.random.key(seed)
    keys = iter(jax.random.split(key, 64))
    nrm = lambda shape, scale: jax.random.normal(next(keys), shape, F32) * scale
    p = {}
    p['x'] = nrm((BATCH, SEQ, D_MODEL), 1.0)
    p['meta'] = nrm((N_META, D_MODEL), 1.0)

    def ln(prefix):
        p[prefix + '_g'] = 1.0 + nrm((D_MODEL,), 0.02)
        p[prefix + '_b'] = nrm((D_MODEL,), 0.02)

    def moe(prefix):
        p[prefix + '_w_coarse'] = nrm((D_MODEL, N_GROUPS), D_MODEL ** -0.5)
        p[prefix + '_b_coarse'] = nrm((N_GROUPS,), 0.01)
        p[prefix + '_w_fine'] = nrm((N_GROUPS, D_MODEL, EXPERTS_PER_GROUP), D_MODEL ** -0.5)
        p[prefix + '_b_fine'] = nrm((N_GROUPS, EXPERTS_PER_GROUP), 0.01)
        p[prefix + '_w_gate'] = nrm((N_EXPERTS, D_MODEL, D_EXPERT), D_MODEL ** -0.5)
        p[prefix + '_w_up'] = nrm((N_EXPERTS, D_MODEL, D_EXPERT), D_MODEL ** -0.5)
        p[prefix + '_w_down'] = nrm((N_EXPERTS, D_EXPERT, D_MODEL), D_EXPERT ** -0.5 * BETA)

    ln('l0_ln1')
    p['l0_w_in'] = nrm((D_MODEL, S5_WIDTH + 3 * DA_WIDTH), D_MODEL ** -0.5)
    p['l0_s5_lambda_re'] = -0.5 + nrm((S5_GROUPS, S5_STATE), 0.01)
    p['l0_s5_lambda_im'] = math.pi * jnp.arange(S5_STATE, dtype=F32)[None, :] + nrm((S5_GROUPS, S5_STATE), 0.01)
    p['l0_s5_log_dt'] = jax.random.uniform(next(keys), (S5_GROUPS,), F32, math.log(S5_DT_MIN), math.log(S5_DT_MAX))
    p['l0_s5_b_re'] = nrm((S5_GROUPS, S5_STATE, S5_GROUP), (2 * S5_GROUP) ** -0.5)
    p['l0_s5_b_im'] = nrm((S5_GROUPS, S5_STATE, S5_GROUP), (2 * S5_GROUP) ** -0.5)
    p['l0_s5_c_re'] = nrm((S5_GROUPS, S5_GROUP, S5_STATE), (2 * S5_STATE) ** -0.5)
    p['l0_s5_c_im'] = nrm((S5_GROUPS, S5_GROUP, S5_STATE), (2 * S5_STATE) ** -0.5)
    p['l0_s5_d'] = nrm((S5_WIDTH,), 1.0)
    p['l0_s5_w_glu'] = nrm((S5_WIDTH, S5_WIDTH), S5_WIDTH ** -0.5)
    p['l0_s5_b_glu'] = nrm((S5_WIDTH,), 0.02)
    p['l0_da_lq1'] = nrm((DA_HEAD_DIM,), 0.1)
    p['l0_da_lk1'] = nrm((DA_HEAD_DIM,), 0.1)
    p['l0_da_lq2'] = nrm((DA_HEAD_DIM,), 0.1)
    p['l0_da_lk2'] = nrm((DA_HEAD_DIM,), 0.1)
    p['l0_da_subln_g'] = 1.0 + nrm((2 * DA_HEAD_DIM,), 0.02)
    p['l0_w_out'] = nrm((S5_WIDTH + DA_WIDTH, D_MODEL), (S5_WIDTH + DA_WIDTH) ** -0.5 * BETA)
    ln('l0_ln2')
    moe('l0_moe')
    ln('l1_ln1')
    p['l1_w_in'] = nrm((D_MODEL, 2 * D_RNN), D_MODEL ** -0.5)
    p['l1_conv_w'] = nrm((CONV_WIDTH, D_RNN), CONV_WIDTH ** -0.5)
    p['l1_conv_b'] = nrm((D_RNN,), 0.02)
    p['l1_lru_w_a'] = nrm((LRU_BLOCKS, LRU_BLOCK_W, LRU_BLOCK_W), LRU_BLOCK_W ** -0.5)
    p['l1_lru_b_a'] = nrm((D_RNN,), 0.02)
    p['l1_lru_w_x'] = nrm((LRU_BLOCKS, LRU_BLOCK_W, LRU_BLOCK_W), LRU_BLOCK_W ** -0.5)
    p['l1_lru_b_x'] = nrm((D_RNN,), 0.02)
    a_c = jax.random.uniform(next(keys), (D_RNN,), F32, 0.9, 0.999)
    s = a_c ** (1.0 / LRU_C)
    p['l1_lru_lambda'] = jnp.log(s) - jnp.log1p(-s)
    p['l1_w_out'] = nrm((D_RNN, D_MODEL), D_RNN ** -0.5 * BETA)
    ln('l1_ln2')
    moe('l1_moe')
    return p


def reference(x, meta,
              l0_ln1_g, l0_ln1_b, l0_w_in, l0_s5_lambda_re, l0_s5_lambda_im, l0_s5_log_dt,
              l0_s5_b_re, l0_s5_b_im, l0_s5_c_re, l0_s5_c_im, l0_s5_d, l0_s5_w_glu, l0_s5_b_glu,
              l0_da_lq1, l0_da_lk1, l0_da_lq2, l0_da_lk2, l0_da_subln_g, l0_w_out,
              l0_ln2_g, l0_ln2_b, l0_moe_w_coarse, l0_moe_b_coarse, l0_moe_w_fine, l0_moe_b_fine,
              l0_moe_w_gate, l0_moe_w_up, l0_moe_w_down,
              l1_ln1_g, l1_ln1_b, l1_w_in, l1_conv_w, l1_conv_b, l1_lru_w_a, l1_lru_b_a,
              l1_lru_w_x, l1_lru_b_x, l1_lru_lambda, l1_w_out,
              l1_ln2_g, l1_ln2_b, l1_moe_w_coarse, l1_moe_b_coarse, l1_moe_w_fine, l1_moe_b_fine,
              l1_moe_w_gate, l1_moe_w_up, l1_moe_w_down):
    dt = x.dtype
    bsz = x.shape[0]
    h = jnp.concatenate([jnp.broadcast_to(meta[None].astype(dt), (bsz, N_META, D_MODEL)), x], axis=1)
    mixers = [
        (l0_w_in, l0_s5_lambda_re, l0_s5_lambda_im, l0_s5_log_dt, l0_s5_b_re, l0_s5_b_im,
         l0_s5_c_re, l0_s5_c_im, l0_s5_d, l0_s5_w_glu, l0_s5_b_glu,
         l0_da_lq1, l0_da_lk1, l0_da_lq2, l0_da_lk2, l0_da_subln_g, l0_w_out),
        (l1_w_in, l1_conv_w, l1_conv_b, l1_lru_w_a, l1_lru_b_a, l1_lru_w_x, l1_lru_b_x,
         l1_lru_lambda, l1_w_out),
    ]
    norms = [((l0_ln1_g, l0_ln1_b), (l0_ln2_g, l0_ln2_b)),
             ((l1_ln1_g, l1_ln1_b), (l1_ln2_g, l1_ln2_b))]
    moes = [(l0_moe_w_coarse, l0_moe_b_coarse, l0_moe_w_fine, l0_moe_b_fine,
             l0_moe_w_gate, l0_moe_w_up, l0_moe_w_down),
            (l1_moe_w_coarse, l1_moe_b_coarse, l1_moe_w_fine, l1_moe_b_fine,
             l1_moe_w_gate, l1_moe_w_up, l1_moe_w_down)]
    for i in range(DEPTH):
        if i % 2 == 0:
            lam_init = 0.8 - 0.6 * math.exp(-0.3 * i)
            mix = _s5_diffattn_mixer(h, *mixers[i], lam_init=lam_init)
        else:
            mix = _rglru_mixer(h, *mixers[i])
        h = _layer_norm(ALPHA * h + mix, *norms[i][0]).astype(dt)
        h = _layer_norm(ALPHA * h + _hier_moe(h, *moes[i]), *norms[i][1]).astype(dt)
    return h[:, N_META:]
```

```python
import functools
import math

import jax
import jax.numpy as jnp
from jax import lax
from jax.experimental import pallas as pl
from jax.experimental.pallas import tpu as pltpu

F32 = jnp.float32
BF16 = jnp.bfloat16

D_MODEL = 1024
DEPTH = 2
CHUNK = 64
N_META = 16
S5_WIDTH = 512
S5_GROUP = 16
S5_GROUPS = 32
S5_STATE = 64
DA_HEADS = 4
DA_HEAD_DIM = 64
DA_WIDTH = 512
D_RNN = 1280
LRU_BLOCKS = 10
LRU_BLOCK_W = 128
CONV_WIDTH = 4
LRU_C = 8.0
N_GROUPS = 4
EXPERTS_PER_GROUP = 8
N_EXPERTS = 32
TOP_K_FINE = 2
D_EXPERT = 512
ALPHA = (2 * DEPTH) ** 0.25
LN_EPS = 1e-5
NEG_INF = -1e30

NB = 8
LANES = 128
FRONT_PAD = LANES - N_META
PAD_ROWS = FRONT_PAD * NB
Q_TILE = 128
K_TILE = 128
TOK_TILE = 512
SCAN_STEPS = 32
EXPERT_TILE = 256
COMBINE_TILE = 256
ROUTER_LANES = 128
VMEM_LIMIT = 48 * 1024 * 1024


def _cparams(n_axes, vmem=VMEM_LIMIT):
    return pltpu.CompilerParams(dimension_semantics=("arbitrary",) * n_axes,
                                vmem_limit_bytes=vmem)


def _gelu(x):
    return 0.5 * x * (1.0 + jnp.tanh(math.sqrt(2.0 / math.pi) * (x + 0.044715 * (x * x * x))))


def _sigmoid(x):
    return 1.0 / (1.0 + jnp.exp(-x))


def _layer_norm_rows(r, g, b):
    mu = jnp.mean(r, axis=-1, keepdims=True)
    c = r - mu
    var = jnp.mean(c * c, axis=-1, keepdims=True)
    return c * lax.rsqrt(var + LN_EPS) * g + b


def _split_bf16(w):
    hi = w.astype(BF16)
    lo = (w - hi.astype(F32)).astype(BF16)
    return hi, lo


def _inproj0_kernel(h_ref, w_ref, u_ref, q_ref, k_ref, v_ref, zs_ref, *, tl):
    x = h_ref[...].astype(BF16)
    z = jnp.dot(x, w_ref[...], preferred_element_type=F32)
    u_ref[...] = z[:, :S5_WIDTH]
    n_slab = 3 * DA_WIDTH // LANES
    for s in range(n_slab):
        zs_ref[s] = z[:, S5_WIDTH + LANES * s:S5_WIDTH + LANES * (s + 1)]
    dsts = (q_ref, k_ref, v_ref)
    per = DA_WIDTH // LANES
    for b in range(NB):
        for s in range(n_slab):
            c = (s % per) * LANES
            dsts[s // per][b, :, c:c + LANES] = zs_ref[s, pl.ds(b, tl, stride=NB), :].astype(BF16)


def _inproj0(h, w_bf16, lp):
    t_rows = h.shape[0]
    tm = TOK_TILE
    tl = tm // NB
    qkv_shape = jax.ShapeDtypeStruct((NB, lp, DA_WIDTH), BF16)
    qkv_spec = pl.BlockSpec((NB, tl, DA_WIDTH), lambda i: (0, i, 0))
    return pl.pallas_call(
        functools.partial(_inproj0_kernel, tl=tl),
        grid=(t_rows // tm,),
        in_specs=[pl.BlockSpec((tm, D_MODEL), lambda i: (i, 0)),
                  pl.BlockSpec(w_bf16.shape, lambda i: (0, 0))],
        out_specs=[pl.BlockSpec((tm, S5_WIDTH), lambda i: (i, 0)), qkv_spec, qkv_spec, qkv_spec],
        out_shape=[jax.ShapeDtypeStruct((t_rows, S5_WIDTH), F32), qkv_shape, qkv_shape, qkv_shape],
        scratch_shapes=[pltpu.VMEM((3 * DA_WIDTH // LANES, tm, LANES), F32)],
        compiler_params=_cparams(1),
        name="l0_inproj",
    )(h, w_bf16)


S5_SLABS = S5_WIDTH // LANES
S5_SLAB_STATE = (S5_GROUPS // S5_SLABS) * S5_STATE
S5_NSTATE = S5_GROUPS * S5_STATE


def _s5_kernel(u_ref, bmat_ref, are_ref, aim_ref, cre_ref, cim_ref, d_ref, wglu_ref, bglu_ref,
               y_ref, hre_ref, him_ref, st_ref, *, steps):
    i = pl.program_id(0)
    rows = steps * NB

    @pl.when(i == 0)
    def _():
        st_ref[...] = jnp.zeros_like(st_ref)

    row = i * rows + lax.broadcasted_iota(jnp.int32, (rows, 1), 0)
    u = jnp.where(row >= PAD_ROWS, u_ref[...], 0.0)
    ub = u.astype(BF16)
    for s in range(S5_SLABS):
        bu = jnp.dot(ub[:, LANES * s:LANES * (s + 1)], bmat_ref[s], preferred_element_type=F32)
        hre_ref[:, S5_SLAB_STATE * s:S5_SLAB_STATE * (s + 1)] = bu[:, :S5_SLAB_STATE]
        him_ref[:, S5_SLAB_STATE * s:S5_SLAB_STATE * (s + 1)] = bu[:, S5_SLAB_STATE:]

    cw = 512
    for cg in range(S5_NSTATE // cw):
        c0 = cg * cw
        a_r = are_ref[:, c0:c0 + cw]
        a_i = aim_ref[:, c0:c0 + cw]

        def step(t, carry, c0=c0, a_r=a_r, a_i=a_i):
            sr, si = carry
            r0 = pl.multiple_of(t * NB, NB)
            br = hre_ref[pl.ds(r0, NB), c0:c0 + cw]
            bi = him_ref[pl.ds(r0, NB), c0:c0 + cw]
            nr = a_r * sr - a_i * si + br
            ni = a_r * si + a_i * sr + bi
            hre_ref[pl.ds(r0, NB), c0:c0 + cw] = nr
            him_ref[pl.ds(r0, NB), c0:c0 + cw] = ni
            return nr, ni

        sr, si = lax.fori_loop(0, steps, step, (st_ref[0, :, c0:c0 + cw], st_ref[1, :, c0:c0 + cw]))
        st_ref[0, :, c0:c0 + cw] = sr
        st_ref[1, :, c0:c0 + cw] = si

    ys = []
    for s in range(S5_SLABS):
        hr = hre_ref[:, S5_SLAB_STATE * s:S5_SLAB_STATE * (s + 1)].astype(BF16)
        hi = him_ref[:, S5_SLAB_STATE * s:S5_SLAB_STATE * (s + 1)].astype(BF16)
        ys.append(jnp.dot(hr, cre_ref[s], preferred_element_type=F32)
                  + jnp.dot(hi, cim_ref[s], preferred_element_type=F32))
    y = jnp.concatenate(ys, axis=1) + d_ref[...] * u
    y = _gelu(y)
    gate = _sigmoid(jnp.dot(y.astype(BF16), wglu_ref[...], preferred_element_type=F32) + bglu_ref[...])
    y_ref[...] = (y * gate).astype(BF16)


def _s5_params(lam_re, lam_im, log_dt, b_re, b_im, c_re, c_im):
    dt = jnp.exp(log_dt.astype(F32))[:, None]
    lr = jnp.minimum(lam_re.astype(F32), -1e-4)
    li = lam_im.astype(F32)
    mag = jnp.exp(lr * dt)
    ar = mag * jnp.cos(li * dt)
    ai = mag * jnp.sin(li * dt)
    den = lr * lr + li * li
    nr, ni = ar - 1.0, ai
    fr = ((nr * lr + ni * li) / den)[..., None]
    fi = ((ni * lr - nr * li) / den)[..., None]
    br, bi = b_re.astype(F32), b_im.astype(F32)
    bbr = fr * br - fi * bi
    bbi = fr * bi + fi * br
    gps = S5_GROUPS // S5_SLABS
    eye = jnp.eye(gps, dtype=F32)

    def in_slab(m):
        m4 = m.reshape(S5_SLABS, gps, S5_STATE, S5_GROUP)
        return jnp.einsum('sgph,gk->sghkp', m4, eye).reshape(S5_SLABS, LANES, S5_SLAB_STATE)

    def out_slab(m):
        m4 = m.reshape(S5_SLABS, gps, S5_GROUP, S5_STATE)
        return jnp.einsum('sghp,gk->sgpkh', m4, eye).reshape(S5_SLABS, S5_SLAB_STATE, LANES)

    bmat = jnp.concatenate([in_slab(bbr), in_slab(bbi)], axis=2).astype(BF16)
    cre = out_slab(c_re.astype(F32)).astype(BF16)
    cim = out_slab(-c_im.astype(F32)).astype(BF16)
    a_re = jnp.broadcast_to(ar.reshape(1, S5_NSTATE), (NB, S5_NSTATE))
    a_im = jnp.broadcast_to(ai.reshape(1, S5_NSTATE), (NB, S5_NSTATE))
    return bmat, a_re, a_im, cre, cim


def _s5(u, bmat, a_re, a_im, cre, cim, d, w_glu, b_glu):
    t_rows = u.shape[0]
    steps = SCAN_STEPS
    rows = steps * NB
    full = lambda a: pl.BlockSpec(a.shape, lambda i: (0,) * a.ndim)
    d2 = d.astype(F32).reshape(1, S5_WIDTH)
    bg2 = b_glu.astype(F32).reshape(1, S5_WIDTH)
    wg = w_glu.astype(BF16)
    return pl.pallas_call(
        functools.partial(_s5_kernel, steps=steps),
        grid=(t_rows // rows,),
        in_specs=[pl.BlockSpec((rows, S5_WIDTH), lambda i: (i, 0)),
                  full(bmat), full(a_re), full(a_im), full(cre), full(cim), full(d2), full(wg), full(bg2)],
        out_specs=pl.BlockSpec((rows, S5_WIDTH), lambda i: (i, 0)),
        out_shape=jax.ShapeDtypeStruct((t_rows, S5_WIDTH), BF16),
        scratch_shapes=[pltpu.VMEM((rows, S5_NSTATE), F32), pltpu.VMEM((rows, S5_NSTATE), F32),
                        pltpu.VMEM((2, NB, S5_NSTATE), F32)],
        compiler_params=_cparams(1),
        name="l0_s5",
    )(u, bmat, a_re, a_im, cre, cim, d2, wg, bg2)


def _diffattn_kernel(lam_ref, q_ref, k_ref, v_ref, g_ref, o_ref, *, lam_init):
    qi = pl.program_id(2)
    lam = lam_ref[0]
    q = q_ref[0]
    lane = lax.broadcasted_iota(jnp.int32, (Q_TILE, LANES), 1)
    zero = jnp.zeros_like(q)
    q1 = jnp.where(lane < DA_HEAD_DIM, q, zero)
    q2 = jnp.where(lane >= DA_HEAD_DIM, q, zero)
    qpos = qi * Q_TILE + lax.broadcasted_iota(jnp.int32, (Q_TILE, K_TILE), 0)
    qblk = qpos // CHUNK
    kloc = lax.broadcasted_iota(jnp.int32, (Q_TILE, K_TILE), 1)
    nt = (((1,), (1,)), ((), ()))

    def body(j, carry):
        m1, l1, a1, m2, l2, a2 = carry
        k0 = pl.multiple_of(j * K_TILE, K_TILE)
        kt = k_ref[0, pl.ds(k0, K_TILE), :]
        vt = v_ref[0, pl.ds(k0, K_TILE), :]
        kpos = j * K_TILE + kloc
        mask = (kpos >= FRONT_PAD) & (kpos // CHUNK <= qblk)

        def half(qh, m, l, a):
            s = lax.dot_general(qh, kt, nt, preferred_element_type=F32)
            s = jnp.where(mask, s, NEG_INF)
            mn = jnp.maximum(m, jnp.max(s, axis=-1, keepdims=True))
            p = jnp.exp(s - mn)
            al = jnp.exp(m - mn)
            l = al * l + jnp.sum(p, axis=-1, keepdims=True)
            a = al * a + jnp.dot(p.astype(BF16), vt, preferred_element_type=F32)
            return mn, l, a

        m1, l1, a1 = half(q1, m1, l1, a1)
        m2, l2, a2 = half(q2, m2, l2, a2)
        return m1, l1, a1, m2, l2, a2

    mz = jnp.full((Q_TILE, 1), NEG_INF, F32)
    lz = jnp.zeros((Q_TILE, 1), F32)
    az = jnp.zeros((Q_TILE, LANES), F32)
    m1, l1, a1, m2, l2, a2 = lax.fori_loop(0, qi + 1, body, (mz, lz, az, mz, lz, az))
    o = a1 / l1 - lam * (a2 / l2)
    o = o * lax.rsqrt(jnp.mean(o * o, axis=-1, keepdims=True) + LN_EPS) * g_ref[...]
    o_ref[0] = o * (1.0 - lam_init)


def _diffattn(q, k, v, lam, subln_g, lam_init):
    nb, lp, _ = q.shape
    nq = lp // Q_TILE
    g2 = subln_g.astype(F32).reshape(1, 2 * DA_HEAD_DIM)
    return pl.pallas_call(
        functools.partial(_diffattn_kernel, lam_init=lam_init),
        grid=(nb, DA_HEADS, nq),
        in_specs=[pl.BlockSpec(memory_space=pltpu.SMEM),
                  pl.BlockSpec((1, Q_TILE, LANES), lambda b, h, i: (b, i, h)),
                  pl.BlockSpec((1, lp, LANES), lambda b, h, i: (b, 0, h)),
                  pl.BlockSpec((1, lp, LANES), lambda b, h, i: (b, 0, h)),
                  pl.BlockSpec((1, LANES), lambda b, h, i: (0, 0))],
        out_specs=pl.BlockSpec((1, Q_TILE, LANES), lambda b, h, i: (b, i, h)),
        out_shape=jax.ShapeDtypeStruct((nb, lp, DA_WIDTH), F32),
        compiler_params=_cparams(3),
        name="l0_diffattn",
    )(lam.reshape(1), q, k, v, g2)


def _router_logits(h1, whi_ref, wlo_ref, rb_ref):
    hi = h1.astype(BF16)
    lo = (h1 - hi.astype(F32)).astype(BF16)
    return (jnp.dot(hi, whi_ref[...], preferred_element_type=F32)
            + jnp.dot(lo, whi_ref[...], preferred_element_type=F32)
            + jnp.dot(hi, wlo_ref[...], preferred_element_type=F32) + rb_ref[...])


def _outproj0_kernel(ys_ref, yda_ref, h_ref, w_ref, g_ref, b_ref, whi_ref, wlo_ref, rb_ref,
                     h1_ref, lg_ref, das_ref, *, tl):
    per = DA_WIDTH // LANES
    for b in range(NB):
        for s in range(per):
            das_ref[s, pl.ds(b, tl, stride=NB), :] = yda_ref[b, :, LANES * s:LANES * (s + 1)]
    da = jnp.concatenate([das_ref[s] for s in range(per)], axis=1).astype(BF16)
    mix = (jnp.dot(ys_ref[...], w_ref[:S5_WIDTH, :], preferred_element_type=F32)
           + jnp.dot(da, w_ref[S5_WIDTH:, :], preferred_element_type=F32))
    h1 = _layer_norm_rows(ALPHA * h_ref[...] + mix, g_ref[...], b_ref[...])
    h1_ref[...] = h1
    lg_ref[...] = _router_logits(h1, whi_ref, wlo_ref, rb_ref)


def _outproj1_kernel(y_ref, h_ref, w_ref, g_ref, b_ref, whi_ref, wlo_ref, rb_ref, h1_ref, lg_ref):
    mix = jnp.dot(y_ref[...], w_ref[...], preferred_element_type=F32)
    h1 = _layer_norm_rows(ALPHA * h_ref[...] + mix, g_ref[...], b_ref[...])
    h1_ref[...] = h1
    lg_ref[...] = _router_logits(h1, whi_ref, wlo_ref, rb_ref)


def _router_weights(w_coarse, b_coarse, w_fine, b_fine):
    wf = jnp.transpose(w_fine.astype(F32), (1, 0, 2)).reshape(D_MODEL, N_EXPERTS)
    w = jnp.concatenate([w_coarse.astype(F32), wf], axis=1)
    w = jnp.pad(w, ((0, 0), (0, ROUTER_LANES - w.shape[1])))
    b = jnp.concatenate([b_coarse.astype(F32), b_fine.astype(F32).reshape(-1)])
    b = jnp.pad(b, (0, ROUTER_LANES - b.shape[0])).reshape(1, ROUTER_LANES)
    whi, wlo = _split_bf16(w)
    return whi, wlo, b


def _outproj(ys, yda, h, w_out, ln_g, ln_b, router):
    t_rows = h.shape[0]
    tm = TOK_TILE
    tl = tm // NB
    whi, wlo, rb = router
    w = w_out.astype(BF16)
    g2 = ln_g.astype(F32).reshape(1, D_MODEL)
    b2 = ln_b.astype(F32).reshape(1, D_MODEL)
    full = lambda a: pl.BlockSpec(a.shape, lambda i: (0,) * a.ndim)
    row = lambda width: pl.BlockSpec((tm, width), lambda i: (i, 0))
    common_in = [row(D_MODEL), full(w), full(g2), full(b2), full(whi), full(wlo), full(rb)]
    out_specs = [row(D_MODEL), row(ROUTER_LANES)]
    out_shape = [jax.ShapeDtypeStruct((t_rows, D_MODEL), F32),
                 jax.ShapeDtypeStruct((t_rows, ROUTER_LANES), F32)]
    if yda is not None:
        return pl.pallas_call(
            functools.partial(_outproj0_kernel, tl=tl),
            grid=(t_rows // tm,),
            in_specs=[row(S5_WIDTH), pl.BlockSpec((NB, tl, DA_WIDTH), lambda i: (0, i, 0))] + common_in,
            out_specs=out_specs, out_shape=out_shape,
            scratch_shapes=[pltpu.VMEM((DA_WIDTH // LANES, tm, LANES), F32)],
            compiler_params=_cparams(1),
            name="l0_outproj",
        )(ys, yda, h, w, g2, b2, whi, wlo, rb)
    return pl.pallas_call(
        _outproj1_kernel,
        grid=(t_rows // tm,),
        in_specs=[row(ys.shape[1])] + common_in,
        out_specs=out_specs, out_shape=out_shape,
        compiler_params=_cparams(1),
        name="l1_outproj",
    )(ys, h, w, g2, b2, whi, wlo, rb)


def _route(logits, n_slots):
    t_rows = logits.shape[0]
    coarse = logits[:, :N_GROUPS]
    fine_all = logits[:, N_GROUPS:N_GROUPS + N_EXPERTS].reshape(t_rows, N_GROUPS, EXPERTS_PER_GROUP)
    p_coarse = jax.nn.softmax(coarse, axis=-1)
    grp = jnp.argmax(coarse, axis=-1)
    p_grp = jnp.take_along_axis(p_coarse, grp[:, None], axis=1)
    fine = jnp.take_along_axis(fine_all, grp[:, None, None], axis=1)[:, 0]
    top_v, top_i = lax.top_k(fine, TOP_K_FINE)
    gates = p_grp * jax.nn.softmax(top_v, axis=-1)
    experts = (grp[:, None] * EXPERTS_PER_GROUP + top_i).astype(jnp.int32)
    valid = (jnp.arange(t_rows) >= PAD_ROWS)[:, None]
    experts = jnp.where(valid, experts, N_EXPERTS)
    gates = jnp.where(valid, gates, 0.0)
    e_flat = experts.reshape(-1)
    onehot = (e_flat[:, None] == jnp.arange(N_EXPERTS + 1)[None, :]).astype(jnp.int32)
    csum = jnp.cumsum(onehot, axis=0)
    rank = jnp.take_along_axis(csum, e_flat[:, None], axis=1)[:, 0] - 1
    counts = csum[-1, :N_EXPERTS]
    padded = (counts + EXPERT_TILE - 1) // EXPERT_TILE * EXPERT_TILE
    pad_end = jnp.cumsum(padded)
    pad_start = jnp.concatenate([pad_end - padded, jnp.array([n_slots], jnp.int32)])
    dest = jnp.where(e_flat < N_EXPERTS, pad_start[e_flat] + rank, n_slots).astype(jnp.int32)
    tok = jnp.repeat(jnp.arange(t_rows, dtype=jnp.int32), TOP_K_FINE)
    slot_tok = jnp.zeros((n_slots,), jnp.int32).at[dest].set(tok, mode='drop')
    nblk = n_slots // EXPERT_TILE
    block_expert = jnp.minimum(
        jnp.searchsorted(pad_end, jnp.arange(nblk, dtype=jnp.int32) * EXPERT_TILE, side='right'),
        N_EXPERTS - 1).astype(jnp.int32)
    n_used = (pad_end[-1] // EXPERT_TILE).astype(jnp.int32).reshape(1)
    pos = jnp.where(dest < n_slots, dest, 0)
    return slot_tok, block_expert, n_used, pos, gates.astype(F32)


def _expert_kernel(be_ref, nused_ref, tok_cur_ref, tok_nxt_ref, x_hbm, wg_ref, wu_ref, wd_ref,
                   y_ref, xbuf_ref, sem_ref):
    i = pl.program_id(0)
    n_used = nused_ref[0]
    slot = lax.rem(i, 2)

    def gather(tok_ref, dst_slot):
        def issue(r8, _):
            for rr in range(8):
                r = r8 * 8 + rr
                pltpu.make_async_copy(x_hbm.at[pl.ds(tok_ref[0, 0, r], 1), :],
                                      xbuf_ref.at[dst_slot, pl.ds(r, 1), :],
                                      sem_ref.at[dst_slot]).start()
            return 0
        lax.fori_loop(0, EXPERT_TILE // 8, issue, 0)

    @pl.when(i == 0)
    def _():
        gather(tok_cur_ref, 0)

    @pl.when(i + 1 < n_used)
    def _():
        gather(tok_nxt_ref, 1 - slot)

    @pl.when(i < n_used)
    def _():
        pltpu.make_async_copy(x_hbm.at[pl.ds(0, EXPERT_TILE), :], xbuf_ref.at[slot],
                              sem_ref.at[slot]).wait()
        x = xbuf_ref[slot].astype(BF16)
        g = jnp.dot(x, wg_ref[0].astype(BF16), preferred_element_type=F32)
        u = jnp.dot(x, wu_ref[0].astype(BF16), preferred_element_type=F32)
        hid = (g * _sigmoid(g) * u).astype(BF16)
        y_ref[...] = jnp.dot(hid, wd_ref[0].astype(BF16), preferred_element_type=F32)

    @pl.when(i >= n_used)
    def _():
        y_ref[...] = jnp.zeros_like(y_ref)


def _experts(h1, slot_tok, block_expert, n_used, w_gate, w_up, w_down):
    n_slots = slot_tok.shape[0]
    nblk = n_slots // EXPERT_TILE
    tok3 = slot_tok.reshape(nblk, 1, EXPERT_TILE)
    smem_blk = lambda fn: pl.BlockSpec((1, 1, EXPERT_TILE), fn, memory_space=pltpu.SMEM)
    grid_spec = pltpu.PrefetchScalarGridSpec(
        num_scalar_prefetch=2,
        grid=(nblk,),
        in_specs=[smem_blk(lambda i, be, nu: (i, 0, 0)),
                  smem_blk(lambda i, be, nu: (jnp.minimum(i + 1, nblk - 1), 0, 0)),
                  pl.BlockSpec(memory_space=pl.ANY),
                  pl.BlockSpec((1, D_MODEL, D_EXPERT), lambda i, be, nu: (be[i], 0, 0)),
                  pl.BlockSpec((1, D_MODEL, D_EXPERT), lambda i, be, nu: (be[i], 0, 0)),
                  pl.BlockSpec((1, D_EXPERT, D_MODEL), lambda i, be, nu: (be[i], 0, 0))],
        out_specs=pl.BlockSpec((EXPERT_TILE, D_MODEL), lambda i, be, nu: (i, 0)),
        scratch_shapes=[pltpu.VMEM((2, EXPERT_TILE, D_MODEL), F32), pltpu.SemaphoreType.DMA((2,))],
    )
    return pl.pallas_call(
        _expert_kernel,
        grid_spec=grid_spec,
        out_shape=jax.ShapeDtypeStruct((n_slots, D_MODEL), F32),
        compiler_params=_cparams(1),
        name="moe_experts",
    )(block_expert, n_used, tok3, tok3, h1, w_gate, w_up, w_down)


def _combine_kernel(pos_cur_ref, pos_nxt_ref, yb_hbm, gate_ref, h_ref, g_ref, b_ref, o_ref,
                    ybuf_ref, sem_ref, *rest, first_tile, n_tiles, to_batch_major):
    i = pl.program_id(0)
    slot = lax.rem(i, 2)
    tm = COMBINE_TILE

    def gather(pos_ref, dst_slot):
        def issue(r4, _):
            for rr in range(4):
                r = r4 * 4 + rr
                for kk in range(TOP_K_FINE):
                    pltpu.make_async_copy(
                        yb_hbm.at[pl.ds(pos_ref[0, 0, TOP_K_FINE * r + kk], 1), :],
                        ybuf_ref.at[dst_slot, kk, pl.ds(r, 1), :],
                        sem_ref.at[dst_slot]).start()
            return 0
        lax.fori_loop(0, tm // 4, issue, 0)

    @pl.when(i == 0)
    def _():
        gather(pos_cur_ref, 0)

    @pl.when(i + 1 < n_tiles)
    def _():
        gather(pos_nxt_ref, 1 - slot)

    for kk in range(TOP_K_FINE):
        pltpu.make_async_copy(yb_hbm.at[pl.ds(0, tm), :], ybuf_ref.at[slot, kk],
                              sem_ref.at[slot]).wait()
    gts = gate_ref[...]
    moe = ybuf_ref[slot, 0] * gts[:, 0:1] + ybuf_ref[slot, 1] * gts[:, 1:2]
    h2 = _layer_norm_rows(ALPHA * h_ref[...] + moe, g_ref[...], b_ref[...])
    if not to_batch_major:
        o_ref[...] = h2
    else:
        hs_ref = rest[0]
        tl = tm // NB
        for s in range(D_MODEL // LANES):
            hs_ref[s] = h2[:, LANES * s:LANES * (s + 1)]
        for b in range(NB):
            for s in range(D_MODEL // LANES):
                o_ref[b, :, LANES * s:LANES * (s + 1)] = hs_ref[s, pl.ds(b, tl, stride=NB), :]


def _combine(yb, pos, gates, h1, ln_g, ln_b, *, final):
    t_rows = h1.shape[0]
    tm = COMBINE_TILE
    first_tile = (LANES * NB) // tm if final else 0
    n_tiles = t_rows // tm - first_tile
    pos3 = pos.reshape(t_rows // tm, 1, TOP_K_FINE * tm)
    g2 = ln_g.astype(F32).reshape(1, D_MODEL)
    b2 = ln_b.astype(F32).reshape(1, D_MODEL)
    full = lambda a: pl.BlockSpec(a.shape, lambda i: (0,) * a.ndim)
    smem_blk = lambda fn: pl.BlockSpec((1, 1, TOP_K_FINE * tm), fn, memory_space=pltpu.SMEM)
    last = t_rows // tm - 1
    in_specs = [smem_blk(lambda i: (i + first_tile, 0, 0)),
                smem_blk(lambda i: (jnp.minimum(i + first_tile + 1, last), 0, 0)),
                pl.BlockSpec(memory_space=pl.ANY),
                pl.BlockSpec((tm, TOP_K_FINE), lambda i: (i + first_tile, 0)),
                pl.BlockSpec((tm, D_MODEL), lambda i: (i + first_tile, 0)),
                full(g2), full(b2)]
    scratch = [pltpu.VMEM((2, TOP_K_FINE, tm, D_MODEL), F32), pltpu.SemaphoreType.DMA((2,))]
    if final:
        tl = tm // NB
        seq = t_rows // NB - LANES
        out_specs = pl.BlockSpec((NB, tl, D_MODEL), lambda i: (0, i, 0))
        out_shape = jax.ShapeDtypeStruct((NB, seq, D_MODEL), F32)
        scratch.append(pltpu.VMEM((D_MODEL // LANES, tm, LANES), F32))
    else:
        out_specs = pl.BlockSpec((tm, D_MODEL), lambda i: (i, 0))
        out_shape = jax.ShapeDtypeStruct((t_rows, D_MODEL), F32)
    return pl.pallas_call(
        functools.partial(_combine_kernel, first_tile=first_tile, n_tiles=n_tiles, to_batch_major=final),
        grid=(n_tiles,),
        in_specs=in_specs, out_specs=out_specs, out_shape=out_shape,
        scratch_shapes=scratch,
        compiler_params=_cparams(1),
        name="moe_combine_final" if final else "moe_combine",
    )(pos3, pos3, yb, gates, h1, g2, b2)


def _moe_block(h1, logits, w_gate, w_up, w_down, ln_g, ln_b, *, final):
    t_rows = h1.shape[0]
    n_assign = (t_rows - PAD_ROWS) * TOP_K_FINE
    nblk = -(-(n_assign + N_EXPERTS * (EXPERT_TILE - 1)) // EXPERT_TILE)
    n_slots = nblk * EXPERT_TILE
    slot_tok, block_expert, n_used, pos, gates = _route(logits, n_slots)
    yb = _experts(h1, slot_tok, block_expert, n_used, w_gate, w_up, w_down)
    return _combine(yb, pos, gates, h1, ln_g, ln_b, final=final)


def _inproj1_kernel(h_ref, w_ref, gg_ref, rec_ref):
    x = h_ref[...].astype(BF16)
    z = jnp.dot(x, w_ref[...], preferred_element_type=F32)
    gg_ref[...] = _gelu(z[:, :D_RNN]).astype(BF16)
    rec_ref[...] = z[:, D_RNN:]


def _inproj1(h, w_bf16):
    t_rows = h.shape[0]
    tm = TOK_TILE
    row = lambda width: pl.BlockSpec((tm, width), lambda i: (i, 0))
    return pl.pallas_call(
        _inproj1_kernel,
        grid=(t_rows // tm,),
        in_specs=[row(D_MODEL), pl.BlockSpec(w_bf16.shape, lambda i: (0, 0))],
        out_specs=[row(D_RNN), row(D_RNN)],
        out_shape=[jax.ShapeDtypeStruct((t_rows, D_RNN), BF16), jax.ShapeDtypeStruct((t_rows, D_RNN), F32)],
        compiler_params=_cparams(1),
        name="l1_inproj",
    )(h, w_bf16)


CONV_HALO = (CONV_WIDTH - 1) * NB


def _rglru_kernel(rec_ref, gg_ref, cw_ref, cb_ref, wax_ref, ba_ref, bx_ref, sp_ref,
                  y_ref, rp_ref, a_ref, b_ref, st_ref, *, steps):
    i = pl.program_id(0)
    rows = steps * NB

    @pl.when(i == 0)
    def _():
        st_ref[...] = jnp.zeros_like(st_ref)
        rp_ref[0:CONV_HALO, :] = jnp.zeros((CONV_HALO, D_RNN), F32)

    row = i * rows + lax.broadcasted_iota(jnp.int32, (rows, 1), 0)
    real = row >= PAD_ROWS
    rp_ref[CONV_HALO:CONV_HALO + rows, :] = jnp.where(real, rec_ref[...], 0.0)
    xc = cb_ref[...] + sum(rp_ref[NB * j:NB * j + rows, :] * cw_ref[j:j + 1, :] for j in range(CONV_WIDTH))
    tail = rp_ref[rows:rows + CONV_HALO, :]
    rp_ref[0:CONV_HALO, :] = tail

    for n in range(LRU_BLOCKS):
        cs = slice(LRU_BLOCK_W * n, LRU_BLOCK_W * (n + 1))
        xb = xc[:, cs]
        ra = jnp.dot(xb.astype(BF16), wax_ref[n], preferred_element_type=F32)
        r = _sigmoid(ra[:, :LRU_BLOCK_W] + ba_ref[:, cs])
        ig = _sigmoid(ra[:, LRU_BLOCK_W:] + bx_ref[:, cs])
        log_a = -LRU_C * r * sp_ref[:, cs]
        a = jnp.exp(log_a)
        bt = jnp.sqrt(1.0 - a * a) * (ig * xb)
        a_ref[:, cs] = a
        b_ref[:, cs] = jnp.where(real, bt, 0.0)

    def step(t, h):
        r0 = pl.multiple_of(t * NB, NB)
        hn = a_ref[pl.ds(r0, NB), :] * h + b_ref[pl.ds(r0, NB), :]
        b_ref[pl.ds(r0, NB), :] = hn
        return hn

    st_ref[...] = lax.fori_loop(0, steps, step, st_ref[...])
    y_ref[...] = (gg_ref[...].astype(F32) * b_ref[...]).astype(BF16)


def _rglru(rec, gg, conv_w, conv_b, w_a, b_a, w_x, b_x, lru_lambda):
    t_rows = rec.shape[0]
    steps = SCAN_STEPS
    rows = steps * NB
    cw = conv_w.astype(F32)
    cb = conv_b.astype(F32).reshape(1, D_RNN)
    wax = jnp.concatenate([w_a.astype(F32), w_x.astype(F32)], axis=2).astype(BF16)
    ba = b_a.astype(F32).reshape(1, D_RNN)
    bx = b_x.astype(F32).reshape(1, D_RNN)
    sp = jax.nn.softplus(-lru_lambda.astype(F32)).reshape(1, D_RNN)
    full = lambda a: pl.BlockSpec(a.shape, lambda i: (0,) * a.ndim)
    row = pl.BlockSpec((rows, D_RNN), lambda i: (i, 0))
    return pl.pallas_call(
        functools.partial(_rglru_kernel, steps=steps),
        grid=(t_rows // rows,),
        in_specs=[row, row, full(cw), full(cb), full(wax), full(ba), full(bx), full(sp)],
        out_specs=row,
        out_shape=jax.ShapeDtypeStruct((t_rows, D_RNN), BF16),
        scratch_shapes=[pltpu.VMEM((rows + CONV_HALO, D_RNN), F32), pltpu.VMEM((rows, D_RNN), F32),
                        pltpu.VMEM((rows, D_RNN), F32), pltpu.VMEM((NB, D_RNN), F32)],
        compiler_params=_cparams(1),
        name="l1_rglru",
    )(rec, gg, cw, cb, wax, ba, bx, sp)


def kernel(x, meta, l0_ln1_g, l0_ln1_b, l0_w_in, l0_s5_lambda_re, l0_s5_lambda_im, l0_s5_log_dt, l0_s5_b_re, l0_s5_b_im, l0_s5_c_re, l0_s5_c_im, l0_s5_d, l0_s5_w_glu, l0_s5_b_glu, l0_da_lq1, l0_da_lk1, l0_da_lq2, l0_da_lk2, l0_da_subln_g, l0_w_out, l0_ln2_g, l0_ln2_b, l0_moe_w_coarse, l0_moe_b_coarse, l0_moe_w_fine, l0_moe_b_fine, l0_moe_w_gate, l0_moe_w_up, l0_moe_w_down, l1_ln1_g, l1_ln1_b, l1_w_in, l1_conv_w, l1_conv_b, l1_lru_w_a, l1_lru_b_a, l1_lru_w_x, l1_lru_b_x, l1_lru_lambda, l1_w_out, l1_ln2_g, l1_ln2_b, l1_moe_w_coarse, l1_moe_b_coarse, l1_moe_w_fine, l1_moe_b_fine, l1_moe_w_gate, l1_moe_w_up, l1_moe_w_down):
    bsz, seq, _ = x.shape
    assert bsz == NB and seq % Q_TILE == 0
    dt = x.dtype
    lp = FRONT_PAD + N_META + seq
    h = jnp.concatenate([
        jnp.zeros((FRONT_PAD, NB, D_MODEL), dt),
        jnp.broadcast_to(meta.astype(dt)[:, None, :], (N_META, NB, D_MODEL)),
        jnp.transpose(x, (1, 0, 2))], axis=0).reshape(lp * NB, D_MODEL)

    col_scale = jnp.concatenate([jnp.ones((S5_WIDTH,), F32),
                                 jnp.full((DA_WIDTH,), DA_HEAD_DIM ** -0.5, F32),
                                 jnp.ones((2 * DA_WIDTH,), F32)])
    w_in0 = (l0_w_in.astype(F32) * col_scale[None, :]).astype(BF16)
    u, q, k, v = _inproj0(h, w_in0, lp)
    s5p = _s5_params(l0_s5_lambda_re, l0_s5_lambda_im, l0_s5_log_dt,
                     l0_s5_b_re, l0_s5_b_im, l0_s5_c_re, l0_s5_c_im)
    y_s5 = _s5(u, *s5p, l0_s5_d, l0_s5_w_glu, l0_s5_b_glu)
    lam_init = 0.8 - 0.6 * math.exp(-0.3 * 0)
    lam = (jnp.exp(jnp.sum(l0_da_lq1.astype(F32) * l0_da_lk1.astype(F32)))
           - jnp.exp(jnp.sum(l0_da_lq2.astype(F32) * l0_da_lk2.astype(F32))) + lam_init)
    y_da = _diffattn(q, k, v, lam, l0_da_subln_g, lam_init)
    router0 = _router_weights(l0_moe_w_coarse, l0_moe_b_coarse, l0_moe_w_fine, l0_moe_b_fine)
    h, logits = _outproj(y_s5, y_da, h, l0_w_out, l0_ln1_g, l0_ln1_b, router0)
    h = _moe_block(h, logits, l0_moe_w_gate, l0_moe_w_up, l0_moe_w_down, l0_ln2_g, l0_ln2_b, final=False)

    gg, rec = _inproj1(h, l1_w_in.astype(BF16))
    y = _rglru(rec, gg, l1_conv_w, l1_conv_b, l1_lru_w_a, l1_lru_b_a, l1_lru_w_x, l1_lru_b_x, l1_lru_lambda)
    router1 = _router_weights(l1_moe_w_coarse, l1_moe_b_coarse, l1_moe_w_fine, l1_moe_b_fine)
    h, logits = _outproj(y, None, h, l1_w_out, l1_ln1_g, l1_ln1_b, router1)
    out = _moe_block(h, logits, l1_moe_w_gate, l1_moe_w_up, l1_moe_w_down, l1_ln2_g, l1_ln2_b, final=True)
    return out.astype(dt)
```

```python
import functools
import math

import jax
import jax.numpy as jnp
from jax import lax
from jax.experimental import pallas as pl
from jax.experimental.pallas import tpu as pltpu

F32 = jnp.float32
BF16 = jnp.bfloat16

D_MODEL = 1024
DEPTH = 2
CHUNK = 64
N_META = 16
S5_WIDTH = 512
S5_GROUP = 16
S5_GROUPS = 32
S5_STATE = 64
DA_HEADS = 4
DA_HEAD_DIM = 64
DA_WIDTH = 512
D_RNN = 1280
LRU_BLOCKS = 10
LRU_BLOCK_W = 128
CONV_WIDTH = 4
LRU_C = 8.0
N_GROUPS = 4
EXPERTS_PER_GROUP = 8
N_EXPERTS = 32
TOP_K_FINE = 2
D_EXPERT = 512
ALPHA = (2 * DEPTH) ** 0.25
LN_EPS = 1e-5
NEG_INF = -1e30

NB = 8
LANES = 128
FRONT_PAD = LANES - N_META
PAD_ROWS = FRONT_PAD * NB
Q_TILE = 256
K_TILE = 512
TOK_TILE = 512
SCAN_STEPS = 32
EXPERT_TILE = 256
COMBINE_TILE = 256
ROUTER_LANES = 128
VMEM_LIMIT = 48 * 1024 * 1024


def _cparams(n_axes, vmem=VMEM_LIMIT):
    return pltpu.CompilerParams(dimension_semantics=("arbitrary",) * n_axes,
                                vmem_limit_bytes=vmem)


def _gelu(x):
    return 0.5 * x * (1.0 + jnp.tanh(math.sqrt(2.0 / math.pi) * (x + 0.044715 * (x * x * x))))


def _sigmoid(x):
    return 1.0 / (1.0 + jnp.exp(-x))


def _layer_norm_rows(r, g, b):
    mu = jnp.mean(r, axis=-1, keepdims=True)
    c = r - mu
    var = jnp.mean(c * c, axis=-1, keepdims=True)
    return c * lax.rsqrt(var + LN_EPS) * g + b


def _split_bf16(w):
    hi = w.astype(BF16)
    lo = (w - hi.astype(F32)).astype(BF16)
    return hi, lo


def _inproj0_kernel(h_ref, w_ref, u_ref, q_ref, k_ref, v_ref, zs_ref, *, tl):
    x = h_ref[...].astype(BF16)
    z = jnp.dot(x, w_ref[...], preferred_element_type=F32)
    u_ref[...] = z[:, :S5_WIDTH]
    n_slab = 3 * DA_WIDTH // LANES
    for s in range(n_slab):
        zs_ref[s] = z[:, S5_WIDTH + LANES * s:S5_WIDTH + LANES * (s + 1)]
    dsts = (q_ref, k_ref, v_ref)
    per = DA_WIDTH // LANES
    for b in range(NB):
        for s in range(n_slab):
            c = (s % per) * LANES
            dsts[s // per][b, :, c:c + LANES] = zs_ref[s, pl.ds(b, tl, stride=NB), :].astype(BF16)


def _inproj0(h, w_bf16, lp):
    t_rows = h.shape[0]
    tm = TOK_TILE
    tl = tm // NB
    qkv_shape = jax.ShapeDtypeStruct((NB, lp, DA_WIDTH), BF16)
    qkv_spec = pl.BlockSpec((NB, tl, DA_WIDTH), lambda i: (0, i, 0))
    return pl.pallas_call(
        functools.partial(_inproj0_kernel, tl=tl),
        grid=(t_rows // tm,),
        in_specs=[pl.BlockSpec((tm, D_MODEL), lambda i: (i, 0)),
                  pl.BlockSpec(w_bf16.shape, lambda i: (0, 0))],
        out_specs=[pl.BlockSpec((tm, S5_WIDTH), lambda i: (i, 0)), qkv_spec, qkv_spec, qkv_spec],
        out_shape=[jax.ShapeDtypeStruct((t_rows, S5_WIDTH), F32), qkv_shape, qkv_shape, qkv_shape],
        scratch_shapes=[pltpu.VMEM((3 * DA_WIDTH // LANES, tm, LANES), F32)],
        compiler_params=_cparams(1),
        name="l0_inproj",
    )(h, w_bf16)


S5_SLABS = S5_WIDTH // LANES
S5_SLAB_STATE = (S5_GROUPS // S5_SLABS) * S5_STATE
S5_NSTATE = S5_GROUPS * S5_STATE


def _s5_kernel(u_ref, bmat_ref, are_ref, aim_ref, cre_ref, cim_ref, d_ref, wglu_ref, bglu_ref,
               y_ref, hre_ref, him_ref, st_ref, *, steps):
    i = pl.program_id(0)
    rows = steps * NB

    @pl.when(i == 0)
    def _():
        st_ref[...] = jnp.zeros_like(st_ref)

    row = i * rows + lax.broadcasted_iota(jnp.int32, (rows, 1), 0)
    u = jnp.where(row >= PAD_ROWS, u_ref[...], 0.0)
    ub = u.astype(BF16)
    for s in range(S5_SLABS):
        bu = jnp.dot(ub[:, LANES * s:LANES * (s + 1)], bmat_ref[s], preferred_element_type=F32)
        hre_ref[:, S5_SLAB_STATE * s:S5_SLAB_STATE * (s + 1)] = bu[:, :S5_SLAB_STATE]
        him_ref[:, S5_SLAB_STATE * s:S5_SLAB_STATE * (s + 1)] = bu[:, S5_SLAB_STATE:]

    cw = 512
    for cg in range(S5_NSTATE // cw):
        c0 = cg * cw
        a_r = are_ref[:, c0:c0 + cw]
        a_i = aim_ref[:, c0:c0 + cw]

        def step(t, carry, c0=c0, a_r=a_r, a_i=a_i):
            sr, si = carry
            r0 = pl.multiple_of(t * NB, NB)
            br = hre_ref[pl.ds(r0, NB), c0:c0 + cw]
            bi = him_ref[pl.ds(r0, NB), c0:c0 + cw]
            nr = a_r * sr - a_i * si + br
            ni = a_r * si + a_i * sr + bi
            hre_ref[pl.ds(r0, NB), c0:c0 + cw] = nr
            him_ref[pl.ds(r0, NB), c0:c0 + cw] = ni
            return nr, ni

        sr, si = lax.fori_loop(0, steps, step, (st_ref[0, :, c0:c0 + cw], st_ref[1, :, c0:c0 + cw]))
        st_ref[0, :, c0:c0 + cw] = sr
        st_ref[1, :, c0:c0 + cw] = si

    ys = []
    for s in range(S5_SLABS):
        hr = hre_ref[:, S5_SLAB_STATE * s:S5_SLAB_STATE * (s + 1)].astype(BF16)
        hi = him_ref[:, S5_SLAB_STATE * s:S5_SLAB_STATE * (s + 1)].astype(BF16)
        ys.append(jnp.dot(hr, cre_ref[s], preferred_element_type=F32)
                  + jnp.dot(hi, cim_ref[s], preferred_element_type=F32))
    y = jnp.concatenate(ys, axis=1) + d_ref[...] * u
    y = _gelu(y)
    gate = _sigmoid(jnp.dot(y.astype(BF16), wglu_ref[...], preferred_element_type=F32) + bglu_ref[...])
    y_ref[...] = (y * gate).astype(BF16)


def _s5_params(lam_re, lam_im, log_dt, b_re, b_im, c_re, c_im):
    dt = jnp.exp(log_dt.astype(F32))[:, None]
    lr = jnp.minimum(lam_re.astype(F32), -1e-4)
    li = lam_im.astype(F32)
    mag = jnp.exp(lr * dt)
    ar = mag * jnp.cos(li * dt)
    ai = mag * jnp.sin(li * dt)
    den = lr * lr + li * li
    nr, ni = ar - 1.0, ai
    fr = ((nr * lr + ni * li) / den)[..., None]
    fi = ((ni * lr - nr * li) / den)[..., None]
    br, bi = b_re.astype(F32), b_im.astype(F32)
    bbr = fr * br - fi * bi
    bbi = fr * bi + fi * br
    gps = S5_GROUPS // S5_SLABS
    eye = jnp.eye(gps, dtype=F32)

    def in_slab(m):
        m4 = m.reshape(S5_SLABS, gps, S5_STATE, S5_GROUP)
        return jnp.einsum('sgph,gk->sghkp', m4, eye).reshape(S5_SLABS, LANES, S5_SLAB_STATE)

    def out_slab(m):
        m4 = m.reshape(S5_SLABS, gps, S5_GROUP, S5_STATE)
        return jnp.einsum('sghp,gk->sgpkh', m4, eye).reshape(S5_SLABS, S5_SLAB_STATE, LANES)

    bmat = jnp.concatenate([in_slab(bbr), in_slab(bbi)], axis=2).astype(BF16)
    cre = out_slab(c_re.astype(F32)).astype(BF16)
    cim = out_slab(-c_im.astype(F32)).astype(BF16)
    a_re = jnp.broadcast_to(ar.reshape(1, S5_NSTATE), (NB, S5_NSTATE))
    a_im = jnp.broadcast_to(ai.reshape(1, S5_NSTATE), (NB, S5_NSTATE))
    return bmat, a_re, a_im, cre, cim


def _s5(u, bmat, a_re, a_im, cre, cim, d, w_glu, b_glu):
    t_rows = u.shape[0]
    steps = SCAN_STEPS
    rows = steps * NB
    full = lambda a: pl.BlockSpec(a.shape, lambda i: (0,) * a.ndim)
    d2 = d.astype(F32).reshape(1, S5_WIDTH)
    bg2 = b_glu.astype(F32).reshape(1, S5_WIDTH)
    wg = w_glu.astype(BF16)
    return pl.pallas_call(
        functools.partial(_s5_kernel, steps=steps),
        grid=(t_rows // rows,),
        in_specs=[pl.BlockSpec((rows, S5_WIDTH), lambda i: (i, 0)),
                  full(bmat), full(a_re), full(a_im), full(cre), full(cim), full(d2), full(wg), full(bg2)],
        out_specs=pl.BlockSpec((rows, S5_WIDTH), lambda i: (i, 0)),
        out_shape=jax.ShapeDtypeStruct((t_rows, S5_WIDTH), BF16),
        scratch_shapes=[pltpu.VMEM((rows, S5_NSTATE), F32), pltpu.VMEM((rows, S5_NSTATE), F32),
                        pltpu.VMEM((2, NB, S5_NSTATE), F32)],
        compiler_params=_cparams(1),
        name="l0_s5",
    )(u, bmat, a_re, a_im, cre, cim, d2, wg, bg2)


def _diffattn_kernel(lam_ref, q_ref, k_ref, v_ref, g_ref, o_ref,
                     s_ref, q1_ref, q2_ref, m_ref, l_ref, a_ref, *, lam_init, lp):
    lam = lam_ref[0]
    tq, tk = Q_TILE, K_TILE
    lane = lax.broadcasted_iota(jnp.int32, (tq, LANES), 1)
    qrow = lax.broadcasted_iota(jnp.int32, (tq, tk), 0)
    kloc = lax.broadcasted_iota(jnp.int32, (tq, tk), 1)
    nt = (((1,), (1,)), ((), ()))
    qs = (q1_ref, q2_ref)

    def key_start(j):
        return pl.multiple_of(jnp.minimum(j * tk, lp - tk), LANES)

    def lane_fold(x, op):
        r = x[:, :LANES]
        for c in range(1, x.shape[1] // LANES):
            r = op(r, x[:, LANES * c:LANES * (c + 1)])
        return r

    def q_tile(i, _):
        q0 = pl.multiple_of(jnp.minimum(i * tq, lp - tq), LANES)
        q = q_ref[0, pl.ds(q0, tq), :]
        zero = jnp.zeros_like(q)
        q1_ref[...] = jnp.where(lane < DA_HEAD_DIM, q, zero)
        q2_ref[...] = jnp.where(lane >= DA_HEAD_DIM, q, zero)
        qblk = jnp.right_shift(q0 + qrow, 6)
        n_kt = (q0 + tq + tk - 1) // tk
        m_ref[...] = jnp.full(m_ref.shape, NEG_INF, F32)
        l_ref[...] = jnp.zeros(l_ref.shape, F32)
        a_ref[...] = jnp.zeros(a_ref.shape, F32)

        def scores(j, _, masked):
            k0 = key_start(j)
            kt = k_ref[0, pl.ds(k0, tk), :]
            if masked:
                kpos = k0 + kloc
                mask = (kpos >= jnp.maximum(FRONT_PAD, j * tk)) & (jnp.right_shift(kpos, 6) <= qblk)
            for h in range(2):
                s = lax.dot_general(qs[h][...], kt, nt, preferred_element_type=F32)
                if masked:
                    s = jnp.where(mask, s, NEG_INF)
                s_ref[h, j] = s
                m_ref[h] = jnp.maximum(m_ref[h], lane_fold(s, jnp.maximum))
            return 0

        n_full = q0 // tk
        lax.fori_loop(0, jnp.minimum(1, n_full), functools.partial(scores, masked=True), 0)
        lax.fori_loop(1, n_full, functools.partial(scores, masked=False), 0)
        lax.fori_loop(n_full, n_kt, functools.partial(scores, masked=True), 0)
        m = [jnp.max(m_ref[h], axis=-1, keepdims=True) for h in range(2)]

        def values(j, _):
            vt = v_ref[0, pl.ds(key_start(j), tk), :]
            for h in range(2):
                p = jnp.exp(s_ref[h, j] - m[h])
                l_ref[h] += lane_fold(p, jnp.add)
                a_ref[h] += jnp.dot(p.astype(BF16), vt, preferred_element_type=F32)
            return 0

        lax.fori_loop(0, n_kt, values, 0)
        l1 = jnp.sum(l_ref[0], axis=-1, keepdims=True)
        l2 = jnp.sum(l_ref[1], axis=-1, keepdims=True)
        o = a_ref[0] / l1 - lam * (a_ref[1] / l2)
        o = o * lax.rsqrt(jnp.mean(o * o, axis=-1, keepdims=True) + LN_EPS) * g_ref[...]
        o_ref[0, pl.ds(q0, tq), :] = o * (1.0 - lam_init)
        return 0

    lax.fori_loop(0, pl.cdiv(lp, tq), q_tile, 0)


def _diffattn(q, k, v, lam, subln_g, lam_init):
    nb, lp, _ = q.shape
    g2 = subln_g.astype(F32).reshape(1, 2 * DA_HEAD_DIM)
    seq_spec = pl.BlockSpec((1, lp, LANES), lambda b, h: (b, 0, h))
    return pl.pallas_call(
        functools.partial(_diffattn_kernel, lam_init=lam_init, lp=lp),
        grid=(nb, DA_HEADS),
        in_specs=[pl.BlockSpec(memory_space=pltpu.SMEM), seq_spec, seq_spec, seq_spec,
                  pl.BlockSpec((1, LANES), lambda b, h: (0, 0))],
        out_specs=seq_spec,
        out_shape=jax.ShapeDtypeStruct((nb, lp, DA_WIDTH), F32),
        scratch_shapes=[pltpu.VMEM((2, pl.cdiv(lp, K_TILE), Q_TILE, K_TILE), F32),
                        pltpu.VMEM((Q_TILE, LANES), BF16), pltpu.VMEM((Q_TILE, LANES), BF16),
                        pltpu.VMEM((2, Q_TILE, LANES), F32), pltpu.VMEM((2, Q_TILE, LANES), F32),
                        pltpu.VMEM((2, Q_TILE, LANES), F32)],
        compiler_params=_cparams(2),
        name="l0_diffattn",
    )(lam.reshape(1), q, k, v, g2)


def _router_logits(h1, whi_ref, wlo_ref, rb_ref):
    hi = h1.astype(BF16)
    lo = (h1 - hi.astype(F32)).astype(BF16)
    return (jnp.dot(hi, whi_ref[...], preferred_element_type=F32)
            + jnp.dot(lo, whi_ref[...], preferred_element_type=F32)
            + jnp.dot(hi, wlo_ref[...], preferred_element_type=F32) + rb_ref[...])


def _outproj0_kernel(ys_ref, yda_ref, h_ref, w_ref, g_ref, b_ref, whi_ref, wlo_ref, rb_ref,
                     h1_ref, lg_ref, das_ref, *, tl):
    per = DA_WIDTH // LANES
    for b in range(NB):
        for s in range(per):
            das_ref[s, pl.ds(b, tl, stride=NB), :] = yda_ref[b, :, LANES * s:LANES * (s + 1)]
    da = jnp.concatenate([das_ref[s] for s in range(per)], axis=1).astype(BF16)
    mix = (jnp.dot(ys_ref[...], w_ref[:S5_WIDTH, :], preferred_element_type=F32)
           + jnp.dot(da, w_ref[S5_WIDTH:, :], preferred_element_type=F32))
    h1 = _layer_norm_rows(ALPHA * h_ref[...] + mix, g_ref[...], b_ref[...])
    h1_ref[...] = h1
    lg_ref[...] = _router_logits(h1, whi_ref, wlo_ref, rb_ref)


def _outproj1_kernel(y_ref, h_ref, w_ref, g_ref, b_ref, whi_ref, wlo_ref, rb_ref, h1_ref, lg_ref):
    mix = jnp.dot(y_ref[...], w_ref[...], preferred_element_type=F32)
    h1 = _layer_norm_rows(ALPHA * h_ref[...] + mix, g_ref[...], b_ref[...])
    h1_ref[...] = h1
    lg_ref[...] = _router_logits(h1, whi_ref, wlo_ref, rb_ref)


def _router_weights(w_coarse, b_coarse, w_fine, b_fine):
    wf = jnp.transpose(w_fine.astype(F32), (1, 0, 2)).reshape(D_MODEL, N_EXPERTS)
    w = jnp.concatenate([w_coarse.astype(F32), wf], axis=1)
    w = jnp.pad(w, ((0, 0), (0, ROUTER_LANES - w.shape[1])))
    b = jnp.concatenate([b_coarse.astype(F32), b_fine.astype(F32).reshape(-1)])
    b = jnp.pad(b, (0, ROUTER_LANES - b.shape[0])).reshape(1, ROUTER_LANES)
    whi, wlo = _split_bf16(w)
    return whi, wlo, b


def _outproj(ys, yda, h, w_out, ln_g, ln_b, router):
    t_rows = h.shape[0]
    tm = TOK_TILE
    tl = tm // NB
    whi, wlo, rb = router
    w = w_out.astype(BF16)
    g2 = ln_g.astype(F32).reshape(1, D_MODEL)
    b2 = ln_b.astype(F32).reshape(1, D_MODEL)
    full = lambda a: pl.BlockSpec(a.shape, lambda i: (0,) * a.ndim)
    row = lambda width: pl.BlockSpec((tm, width), lambda i: (i, 0))
    common_in = [row(D_MODEL), full(w), full(g2), full(b2), full(whi), full(wlo), full(rb)]
    out_specs = [row(D_MODEL), row(ROUTER_LANES)]
    out_shape = [jax.ShapeDtypeStruct((t_rows, D_MODEL), F32),
                 jax.ShapeDtypeStruct((t_rows, ROUTER_LANES), F32)]
    if yda is not None:
        return pl.pallas_call(
            functools.partial(_outproj0_kernel, tl=tl),
            grid=(t_rows // tm,),
            in_specs=[row(S5_WIDTH), pl.BlockSpec((NB, tl, DA_WIDTH), lambda i: (0, i, 0))] + common_in,
            out_specs=out_specs, out_shape=out_shape,
            scratch_shapes=[pltpu.VMEM((DA_WIDTH // LANES, tm, LANES), F32)],
            compiler_params=_cparams(1),
            name="l0_outproj",
        )(ys, yda, h, w, g2, b2, whi, wlo, rb)
    return pl.pallas_call(
        _outproj1_kernel,
        grid=(t_rows // tm,),
        in_specs=[row(ys.shape[1])] + common_in,
        out_specs=out_specs, out_shape=out_shape,
        compiler_params=_cparams(1),
        name="l1_outproj",
    )(ys, h, w, g2, b2, whi, wlo, rb)


def _route(logits, n_slots):
    t_rows = logits.shape[0]
    coarse = logits[:, :N_GROUPS]
    fine_all = logits[:, N_GROUPS:N_GROUPS + N_EXPERTS].reshape(t_rows, N_GROUPS, EXPERTS_PER_GROUP)
    p_coarse = jax.nn.softmax(coarse, axis=-1)
    grp = jnp.argmax(coarse, axis=-1)
    p_grp = jnp.take_along_axis(p_coarse, grp[:, None], axis=1)
    fine = jnp.take_along_axis(fine_all, grp[:, None, None], axis=1)[:, 0]
    top_v, top_i = lax.top_k(fine, TOP_K_FINE)
    gates = p_grp * jax.nn.softmax(top_v, axis=-1)
    experts = (grp[:, None] * EXPERTS_PER_GROUP + top_i).astype(jnp.int32)
    valid = (jnp.arange(t_rows) >= PAD_ROWS)[:, None]
    experts = jnp.where(valid, experts, N_EXPERTS)
    gates = jnp.where(valid, gates, 0.0)
    e_flat = experts.reshape(-1)
    onehot = (e_flat[:, None] == jnp.arange(N_EXPERTS + 1)[None, :]).astype(jnp.int32)
    csum = jnp.cumsum(onehot, axis=0)
    rank = jnp.take_along_axis(csum, e_flat[:, None], axis=1)[:, 0] - 1
    counts = csum[-1, :N_EXPERTS]
    padded = (counts + EXPERT_TILE - 1) // EXPERT_TILE * EXPERT_TILE
    pad_end = jnp.cumsum(padded)
    pad_start = jnp.concatenate([pad_end - padded, jnp.array([n_slots], jnp.int32)])
    dest = jnp.where(e_flat < N_EXPERTS, pad_start[e_flat] + rank, n_slots).astype(jnp.int32)
    tok = jnp.repeat(jnp.arange(t_rows, dtype=jnp.int32), TOP_K_FINE)
    slot_tok = jnp.zeros((n_slots,), jnp.int32).at[dest].set(tok, mode='drop')
    nblk = n_slots // EXPERT_TILE
    block_expert = jnp.minimum(
        jnp.searchsorted(pad_end, jnp.arange(nblk, dtype=jnp.int32) * EXPERT_TILE, side='right'),
        N_EXPERTS - 1).astype(jnp.int32)
    n_used = (pad_end[-1] // EXPERT_TILE).astype(jnp.int32).reshape(1)
    pos = jnp.where(dest < n_slots, dest, 0)
    return slot_tok, block_expert, n_used, pos, gates.astype(F32)


def _expert_kernel(be_ref, nused_ref, tok_cur_ref, tok_nxt_ref, x_hbm, wg_ref, wu_ref, wd_ref,
                   y_ref, xbuf_ref, sem_ref):
    i = pl.program_id(0)
    n_used = nused_ref[0]
    slot = lax.rem(i, 2)

    def gather(tok_ref, dst_slot):
        def issue(r8, _):
            for rr in range(8):
                r = r8 * 8 + rr
                pltpu.make_async_copy(x_hbm.at[pl.ds(tok_ref[0, 0, r], 1), :],
                                      xbuf_ref.at[dst_slot, pl.ds(r, 1), :],
                                      sem_ref.at[dst_slot]).start()
            return 0
        lax.fori_loop(0, EXPERT_TILE // 8, issue, 0)

    @pl.when(i == 0)
    def _():
        gather(tok_cur_ref, 0)

    @pl.when(i + 1 < n_used)
    def _():
        gather(tok_nxt_ref, 1 - slot)

    @pl.when(i < n_used)
    def _():
        pltpu.make_async_copy(x_hbm.at[pl.ds(0, EXPERT_TILE), :], xbuf_ref.at[slot],
                              sem_ref.at[slot]).wait()
        x = xbuf_ref[slot].astype(BF16)
        g = jnp.dot(x, wg_ref[0].astype(BF16), preferred_element_type=F32)
        u = jnp.dot(x, wu_ref[0].astype(BF16), preferred_element_type=F32)
        hid = (g * _sigmoid(g) * u).astype(BF16)
        y_ref[...] = jnp.dot(hid, wd_ref[0].astype(BF16), preferred_element_type=F32)

    @pl.when(i >= n_used)
    def _():
        y_ref[...] = jnp.zeros_like(y_ref)


def _experts(h1, slot_tok, block_expert, n_used, w_gate, w_up, w_down):
    n_slots = slot_tok.shape[0]
    nblk = n_slots // EXPERT_TILE
    tok3 = slot_tok.reshape(nblk, 1, EXPERT_TILE)
    smem_blk = lambda fn: pl.BlockSpec((1, 1, EXPERT_TILE), fn, memory_space=pltpu.SMEM)
    grid_spec = pltpu.PrefetchScalarGridSpec(
        num_scalar_prefetch=2,
        grid=(nblk,),
        in_specs=[smem_blk(lambda i, be, nu: (i, 0, 0)),
                  smem_blk(lambda i, be, nu: (jnp.minimum(i + 1, nblk - 1), 0, 0)),
                  pl.BlockSpec(memory_space=pl.ANY),
                  pl.BlockSpec((1, D_MODEL, D_EXPERT), lambda i, be, nu: (be[i], 0, 0)),
                  pl.BlockSpec((1, D_MODEL, D_EXPERT), lambda i, be, nu: (be[i], 0, 0)),
                  pl.BlockSpec((1, D_EXPERT, D_MODEL), lambda i, be, nu: (be[i], 0, 0))],
        out_specs=pl.BlockSpec((EXPERT_TILE, D_MODEL), lambda i, be, nu: (i, 0)),
        scratch_shapes=[pltpu.VMEM((2, EXPERT_TILE, D_MODEL), F32), pltpu.SemaphoreType.DMA((2,))],
    )
    return pl.pallas_call(
        _expert_kernel,
        grid_spec=grid_spec,
        out_shape=jax.ShapeDtypeStruct((n_slots, D_MODEL), F32),
        compiler_params=_cparams(1),
        name="moe_experts",
    )(block_expert, n_used, tok3, tok3, h1, w_gate, w_up, w_down)


def _combine_kernel(pos_cur_ref, pos_nxt_ref, yb_hbm, gate_ref, h_ref, g_ref, b_ref, o_ref,
                    ybuf_ref, sem_ref, *rest, first_tile, n_tiles, to_batch_major):
    i = pl.program_id(0)
    slot = lax.rem(i, 2)
    tm = COMBINE_TILE

    def gather(pos_ref, dst_slot):
        def issue(r4, _):
            for rr in range(4):
                r = r4 * 4 + rr
                for kk in range(TOP_K_FINE):
                    pltpu.make_async_copy(
                        yb_hbm.at[pl.ds(pos_ref[0, 0, TOP_K_FINE * r + kk], 1), :],
                        ybuf_ref.at[dst_slot, kk, pl.ds(r, 1), :],
                        sem_ref.at[dst_slot]).start()
            return 0
        lax.fori_loop(0, tm // 4, issue, 0)

    @pl.when(i == 0)
    def _():
        gather(pos_cur_ref, 0)

    @pl.when(i + 1 < n_tiles)
    def _():
        gather(pos_nxt_ref, 1 - slot)

    for kk in range(TOP_K_FINE):
        pltpu.make_async_copy(yb_hbm.at[pl.ds(0, tm), :], ybuf_ref.at[slot, kk],
                              sem_ref.at[slot]).wait()
    gts = gate_ref[...]
    moe = ybuf_ref[slot, 0] * gts[:, 0:1] + ybuf_ref[slot, 1] * gts[:, 1:2]
    h2 = _layer_norm_rows(ALPHA * h_ref[...] + moe, g_ref[...], b_ref[...])
    if not to_batch_major:
        o_ref[...] = h2
    else:
        hs_ref = rest[0]
        tl = tm // NB
        for s in range(D_MODEL // LANES):
            hs_ref[s] = h2[:, LANES * s:LANES * (s + 1)]
        for b in range(NB):
            for s in range(D_MODEL // LANES):
                o_ref[b, :, LANES * s:LANES * (s + 1)] = hs_ref[s, pl.ds(b, tl, stride=NB), :]


def _combine(yb, pos, gates, h1, ln_g, ln_b, *, final):
    t_rows = h1.shape[0]
    tm = COMBINE_TILE
    first_tile = (LANES * NB) // tm if final else 0
    n_tiles = t_rows // tm - first_tile
    pos3 = pos.reshape(t_rows // tm, 1, TOP_K_FINE * tm)
    g2 = ln_g.astype(F32).reshape(1, D_MODEL)
    b2 = ln_b.astype(F32).reshape(1, D_MODEL)
    full = lambda a: pl.BlockSpec(a.shape, lambda i: (0,) * a.ndim)
    smem_blk = lambda fn: pl.BlockSpec((1, 1, TOP_K_FINE * tm), fn, memory_space=pltpu.SMEM)
    last = t_rows // tm - 1
    in_specs = [smem_blk(lambda i: (i + first_tile, 0, 0)),
                smem_blk(lambda i: (jnp.minimum(i + first_tile + 1, last), 0, 0)),
                pl.BlockSpec(memory_space=pl.ANY),
                pl.BlockSpec((tm, TOP_K_FINE), lambda i: (i + first_tile, 0)),
                pl.BlockSpec((tm, D_MODEL), lambda i: (i + first_tile, 0)),
                full(g2), full(b2)]
    scratch = [pltpu.VMEM((2, TOP_K_FINE, tm, D_MODEL), F32), pltpu.SemaphoreType.DMA((2,))]
    if final:
        tl = tm // NB
        seq = t_rows // NB - LANES
        out_specs = pl.BlockSpec((NB, tl, D_MODEL), lambda i: (0, i, 0))
        out_shape = jax.ShapeDtypeStruct((NB, seq, D_MODEL), F32)
        scratch.append(pltpu.VMEM((D_MODEL // LANES, tm, LANES), F32))
    else:
        out_specs = pl.BlockSpec((tm, D_MODEL), lambda i: (i, 0))
        out_shape = jax.ShapeDtypeStruct((t_rows, D_MODEL), F32)
    return pl.pallas_call(
        functools.partial(_combine_kernel, first_tile=first_tile, n_tiles=n_tiles, to_batch_major=final),
        grid=(n_tiles,),
        in_specs=in_specs, out_specs=out_specs, out_shape=out_shape,
        scratch_shapes=scratch,
        compiler_params=_cparams(1),
        name="moe_combine_final" if final else "moe_combine",
    )(pos3, pos3, yb, gates, h1, g2, b2)


def _moe_block(h1, logits, w_gate, w_up, w_down, ln_g, ln_b, *, final):
    t_rows = h1.shape[0]
    n_assign = (t_rows - PAD_ROWS) * TOP_K_FINE
    nblk = -(-(n_assign + N_EXPERTS * (EXPERT_TILE - 1)) // EXPERT_TILE)
    n_slots = nblk * EXPERT_TILE
    slot_tok, block_expert, n_used, pos, gates = _route(logits, n_slots)
    yb = _experts(h1, slot_tok, block_expert, n_used, w_gate, w_up, w_down)
    return _combine(yb, pos, gates, h1, ln_g, ln_b, final=final)


def _inproj1_kernel(h_ref, w_ref, gg_ref, rec_ref):
    x = h_ref[...].astype(BF16)
    z = jnp.dot(x, w_ref[...], preferred_element_type=F32)
    gg_ref[...] = _gelu(z[:, :D_RNN]).astype(BF16)
    rec_ref[...] = z[:, D_RNN:]


def _inproj1(h, w_bf16):
    t_rows = h.shape[0]
    tm = TOK_TILE
    row = lambda width: pl.BlockSpec((tm, width), lambda i: (i, 0))
    return pl.pallas_call(
        _inproj1_kernel,
        grid=(t_rows // tm,),
        in_specs=[row(D_MODEL), pl.BlockSpec(w_bf16.shape, lambda i: (0, 0))],
        out_specs=[row(D_RNN), row(D_RNN)],
        out_shape=[jax.ShapeDtypeStruct((t_rows, D_RNN), BF16), jax.ShapeDtypeStruct((t_rows, D_RNN), F32)],
        compiler_params=_cparams(1),
        name="l1_inproj",
    )(h, w_bf16)


CONV_HALO = (CONV_WIDTH - 1) * NB


def _rglru_kernel(rec_ref, gg_ref, cw_ref, cb_ref, wax_ref, ba_ref, bx_ref, sp_ref,
                  y_ref, rp_ref, a_ref, b_ref, st_ref, *, steps):
    i = pl.program_id(0)
    rows = steps * NB

    @pl.when(i == 0)
    def _():
        st_ref[...] = jnp.zeros_like(st_ref)
        rp_ref[0:CONV_HALO, :] = jnp.zeros((CONV_HALO, D_RNN), F32)

    row = i * rows + lax.broadcasted_iota(jnp.int32, (rows, 1), 0)
    real = row >= PAD_ROWS
    rp_ref[CONV_HALO:CONV_HALO + rows, :] = jnp.where(real, rec_ref[...], 0.0)
    xc = cb_ref[...] + sum(rp_ref[NB * j:NB * j + rows, :] * cw_ref[j:j + 1, :] for j in range(CONV_WIDTH))
    tail = rp_ref[rows:rows + CONV_HALO, :]
    rp_ref[0:CONV_HALO, :] = tail

    for n in range(LRU_BLOCKS):
        cs = slice(LRU_BLOCK_W * n, LRU_BLOCK_W * (n + 1))
        xb = xc[:, cs]
        ra = jnp.dot(xb.astype(BF16), wax_ref[n], preferred_element_type=F32)
        r = _sigmoid(ra[:, :LRU_BLOCK_W] + ba_ref[:, cs])
        ig = _sigmoid(ra[:, LRU_BLOCK_W:] + bx_ref[:, cs])
        log_a = -LRU_C * r * sp_ref[:, cs]
        a = jnp.exp(log_a)
        bt = jnp.sqrt(1.0 - a * a) * (ig * xb)
        a_ref[:, cs] = a
        b_ref[:, cs] = jnp.where(real, bt, 0.0)

    def step(t, h):
        r0 = pl.multiple_of(t * NB, NB)
        hn = a_ref[pl.ds(r0, NB), :] * h + b_ref[pl.ds(r0, NB), :]
        b_ref[pl.ds(r0, NB), :] = hn
        return hn

    st_ref[...] = lax.fori_loop(0, steps, step, st_ref[...])
    y_ref[...] = (gg_ref[...].astype(F32) * b_ref[...]).astype(BF16)


def _rglru(rec, gg, conv_w, conv_b, w_a, b_a, w_x, b_x, lru_lambda):
    t_rows = rec.shape[0]
    steps = SCAN_STEPS
    rows = steps * NB
    cw = conv_w.astype(F32)
    cb = conv_b.astype(F32).reshape(1, D_RNN)
    wax = jnp.concatenate([w_a.astype(F32), w_x.astype(F32)], axis=2).astype(BF16)
    ba = b_a.astype(F32).reshape(1, D_RNN)
    bx = b_x.astype(F32).reshape(1, D_RNN)
    sp = jax.nn.softplus(-lru_lambda.astype(F32)).reshape(1, D_RNN)
    full = lambda a: pl.BlockSpec(a.shape, lambda i: (0,) * a.ndim)
    row = pl.BlockSpec((rows, D_RNN), lambda i: (i, 0))
    return pl.pallas_call(
        functools.partial(_rglru_kernel, steps=steps),
        grid=(t_rows // rows,),
        in_specs=[row, row, full(cw), full(cb), full(wax), full(ba), full(bx), full(sp)],
        out_specs=row,
        out_shape=jax.ShapeDtypeStruct((t_rows, D_RNN), BF16),
        scratch_shapes=[pltpu.VMEM((rows + CONV_HALO, D_RNN), F32), pltpu.VMEM((rows, D_RNN), F32),
                        pltpu.VMEM((rows, D_RNN), F32), pltpu.VMEM((NB, D_RNN), F32)],
        compiler_params=_cparams(1),
        name="l1_rglru",
    )(rec, gg, cw, cb, wax, ba, bx, sp)


def kernel(x, meta, l0_ln1_g, l0_ln1_b, l0_w_in, l0_s5_lambda_re, l0_s5_lambda_im, l0_s5_log_dt, l0_s5_b_re, l0_s5_b_im, l0_s5_c_re, l0_s5_c_im, l0_s5_d, l0_s5_w_glu, l0_s5_b_glu, l0_da_lq1, l0_da_lk1, l0_da_lq2, l0_da_lk2, l0_da_subln_g, l0_w_out, l0_ln2_g, l0_ln2_b, l0_moe_w_coarse, l0_moe_b_coarse, l0_moe_w_fine, l0_moe_b_fine, l0_moe_w_gate, l0_moe_w_up, l0_moe_w_down, l1_ln1_g, l1_ln1_b, l1_w_in, l1_conv_w, l1_conv_b, l1_lru_w_a, l1_lru_b_a, l1_lru_w_x, l1_lru_b_x, l1_lru_lambda, l1_w_out, l1_ln2_g, l1_ln2_b, l1_moe_w_coarse, l1_moe_b_coarse, l1_moe_w_fine, l1_moe_b_fine, l1_moe_w_gate, l1_moe_w_up, l1_moe_w_down):
    bsz, seq, _ = x.shape
    assert bsz == NB and seq % Q_TILE == 0
    dt = x.dtype
    lp = FRONT_PAD + N_META + seq
    h = jnp.concatenate([
        jnp.zeros((FRONT_PAD, NB, D_MODEL), dt),
        jnp.broadcast_to(meta.astype(dt)[:, None, :], (N_META, NB, D_MODEL)),
        jnp.transpose(x, (1, 0, 2))], axis=0).reshape(lp * NB, D_MODEL)

    col_scale = jnp.concatenate([jnp.ones((S5_WIDTH,), F32),
                                 jnp.full((DA_WIDTH,), DA_HEAD_DIM ** -0.5, F32),
                                 jnp.ones((2 * DA_WIDTH,), F32)])
    w_in0 = (l0_w_in.astype(F32) * col_scale[None, :]).astype(BF16)
    u, q, k, v = _inproj0(h, w_in0, lp)
    s5p = _s5_params(l0_s5_lambda_re, l0_s5_lambda_im, l0_s5_log_dt,
                     l0_s5_b_re, l0_s5_b_im, l0_s5_c_re, l0_s5_c_im)
    y_s5 = _s5(u, *s5p, l0_s5_d, l0_s5_w_glu, l0_s5_b_glu)
    lam_init = 0.8 - 0.6 * math.exp(-0.3 * 0)
    lam = (jnp.exp(jnp.sum(l0_da_lq1.astype(F32) * l0_da_lk1.astype(F32)))
           - jnp.exp(jnp.sum(l0_da_lq2.astype(F32) * l0_da_lk2.astype(F32))) + lam_init)
    y_da = _diffattn(q, k, v, lam, l0_da_subln_g, lam_init)
    router0 = _router_weights(l0_moe_w_coarse, l0_moe_b_coarse, l0_moe_w_fine, l0_moe_b_fine)
    h, logits = _outproj(y_s5, y_da, h, l0_w_out, l0_ln1_g, l0_ln1_b, router0)
    h = _moe_block(h, logits, l0_moe_w_gate, l0_moe_w_up, l0_moe_w_down, l0_ln2_g, l0_ln2_b, final=False)

    gg, rec = _inproj1(h, l1_w_in.astype(BF16))
    y = _rglru(rec, gg, l1_conv_w, l1_conv_b, l1_lru_w_a, l1_lru_b_a, l1_lru_w_x, l1_lru_b_x, l1_lru_lambda)
    router1 = _router_weights(l1_moe_w_coarse, l1_moe_b_coarse, l1_moe_w_fine, l1_moe_b_fine)
    h, logits = _outproj(y, None, h, l1_w_out, l1_ln1_g, l1_ln1_b, router1)
    out = _moe_block(h, logits, l1_moe_w_gate, l1_moe_w_up, l1_moe_w_down, l1_ln2_g, l1_ln2_b, final=True)
    return out.astype(dt)
```

```python
import functools
import math

import jax
import jax.numpy as jnp
from jax import lax
from jax.experimental import pallas as pl
from jax.experimental.pallas import tpu as pltpu

F32 = jnp.float32
BF16 = jnp.bfloat16

D_MODEL = 1024
DEPTH = 2
CHUNK = 64
N_META = 16
S5_WIDTH = 512
S5_GROUP = 16
S5_GROUPS = 32
S5_STATE = 64
DA_HEADS = 4
DA_HEAD_DIM = 64
DA_WIDTH = 512
D_RNN = 1280
LRU_BLOCKS = 10
LRU_BLOCK_W = 128
CONV_WIDTH = 4
LRU_C = 8.0
N_GROUPS = 4
EXPERTS_PER_GROUP = 8
N_EXPERTS = 32
TOP_K_FINE = 2
D_EXPERT = 512
ALPHA = (2 * DEPTH) ** 0.25
LN_EPS = 1e-5
NEG_INF = -1e30

NB = 8
LANES = 128
FRONT_PAD = LANES - N_META
PAD_ROWS = FRONT_PAD * NB
Q_TILE = 256
K_TILE = 512
TOK_TILE = 512
SCAN_STEPS = 32
EXPERT_TILE = 256
COMBINE_TILE = 256
ROUTER_LANES = 128
VMEM_LIMIT = 48 * 1024 * 1024


def _cparams(n_axes, vmem=VMEM_LIMIT):
    return pltpu.CompilerParams(dimension_semantics=("arbitrary",) * n_axes,
                                vmem_limit_bytes=vmem)


def _gelu(x):
    return 0.5 * x * (1.0 + jnp.tanh(math.sqrt(2.0 / math.pi) * (x + 0.044715 * (x * x * x))))


def _sigmoid(x):
    return 1.0 / (1.0 + jnp.exp(-x))


def _layer_norm_rows(r, g, b):
    mu = jnp.mean(r, axis=-1, keepdims=True)
    c = r - mu
    var = jnp.mean(c * c, axis=-1, keepdims=True)
    return c * lax.rsqrt(var + LN_EPS) * g + b


def _split_bf16(w):
    hi = w.astype(BF16)
    lo = (w - hi.astype(F32)).astype(BF16)
    return hi, lo


def _inproj0_kernel(h_ref, w_ref, u_ref, q_ref, k_ref, v_ref, zs_ref, *, tl):
    x = h_ref[...].astype(BF16)
    z = jnp.dot(x, w_ref[...], preferred_element_type=F32)
    u_ref[...] = z[:, :S5_WIDTH]
    n_slab = 3 * DA_WIDTH // LANES
    for s in range(n_slab):
        zs_ref[s] = z[:, S5_WIDTH + LANES * s:S5_WIDTH + LANES * (s + 1)]
    dsts = (q_ref, k_ref, v_ref)
    per = DA_WIDTH // LANES
    for b in range(NB):
        for s in range(n_slab):
            c = (s % per) * LANES
            dsts[s // per][b, :, c:c + LANES] = zs_ref[s, pl.ds(b, tl, stride=NB), :].astype(BF16)


def _inproj0(h, w_bf16, lp):
    t_rows = h.shape[0]
    tm = TOK_TILE
    tl = tm // NB
    qkv_shape = jax.ShapeDtypeStruct((NB, lp, DA_WIDTH), BF16)
    qkv_spec = pl.BlockSpec((NB, tl, DA_WIDTH), lambda i: (0, i, 0))
    return pl.pallas_call(
        functools.partial(_inproj0_kernel, tl=tl),
        grid=(t_rows // tm,),
        in_specs=[pl.BlockSpec((tm, D_MODEL), lambda i: (i, 0)),
                  pl.BlockSpec(w_bf16.shape, lambda i: (0, 0))],
        out_specs=[pl.BlockSpec((tm, S5_WIDTH), lambda i: (i, 0)), qkv_spec, qkv_spec, qkv_spec],
        out_shape=[jax.ShapeDtypeStruct((t_rows, S5_WIDTH), F32), qkv_shape, qkv_shape, qkv_shape],
        scratch_shapes=[pltpu.VMEM((3 * DA_WIDTH // LANES, tm, LANES), F32)],
        compiler_params=_cparams(1),
        name="l0_inproj",
    )(h, w_bf16)


S5_SLABS = S5_WIDTH // LANES
S5_SLAB_STATE = (S5_GROUPS // S5_SLABS) * S5_STATE
S5_NSTATE = S5_GROUPS * S5_STATE


def _s5_kernel(u_ref, bmat_ref, are_ref, aim_ref, cre_ref, cim_ref, d_ref, wglu_ref, bglu_ref,
               y_ref, hre_ref, him_ref, st_ref, *, steps):
    i = pl.program_id(0)
    rows = steps * NB

    @pl.when(i == 0)
    def _():
        st_ref[...] = jnp.zeros_like(st_ref)

    row = i * rows + lax.broadcasted_iota(jnp.int32, (rows, 1), 0)
    u = jnp.where(row >= PAD_ROWS, u_ref[...], 0.0)
    ub = u.astype(BF16)
    for s in range(S5_SLABS):
        bu = jnp.dot(ub[:, LANES * s:LANES * (s + 1)], bmat_ref[s], preferred_element_type=F32)
        hre_ref[:, S5_SLAB_STATE * s:S5_SLAB_STATE * (s + 1)] = bu[:, :S5_SLAB_STATE]
        him_ref[:, S5_SLAB_STATE * s:S5_SLAB_STATE * (s + 1)] = bu[:, S5_SLAB_STATE:]

    cw = 512
    for cg in range(S5_NSTATE // cw):
        c0 = cg * cw
        a_r = are_ref[:, c0:c0 + cw]
        a_i = aim_ref[:, c0:c0 + cw]

        def step(t, carry, c0=c0, a_r=a_r, a_i=a_i):
            sr, si = carry
            r0 = pl.multiple_of(t * NB, NB)
            br = hre_ref[pl.ds(r0, NB), c0:c0 + cw]
            bi = him_ref[pl.ds(r0, NB), c0:c0 + cw]
            nr = a_r * sr - a_i * si + br
            ni = a_r * si + a_i * sr + bi
            hre_ref[pl.ds(r0, NB), c0:c0 + cw] = nr
            him_ref[pl.ds(r0, NB), c0:c0 + cw] = ni
            return nr, ni

        sr, si = lax.fori_loop(0, steps, step, (st_ref[0, :, c0:c0 + cw], st_ref[1, :, c0:c0 + cw]))
        st_ref[0, :, c0:c0 + cw] = sr
        st_ref[1, :, c0:c0 + cw] = si

    ys = []
    for s in range(S5_SLABS):
        hr = hre_ref[:, S5_SLAB_STATE * s:S5_SLAB_STATE * (s + 1)].astype(BF16)
        hi = him_ref[:, S5_SLAB_STATE * s:S5_SLAB_STATE * (s + 1)].astype(BF16)
        ys.append(jnp.dot(hr, cre_ref[s], preferred_element_type=F32)
                  + jnp.dot(hi, cim_ref[s], preferred_element_type=F32))
    y = jnp.concatenate(ys, axis=1) + d_ref[...] * u
    y = _gelu(y)
    gate = _sigmoid(jnp.dot(y.astype(BF16), wglu_ref[...], preferred_element_type=F32) + bglu_ref[...])
    y_ref[...] = (y * gate).astype(BF16)


def _s5_params(lam_re, lam_im, log_dt, b_re, b_im, c_re, c_im):
    dt = jnp.exp(log_dt.astype(F32))[:, None]
    lr = jnp.minimum(lam_re.astype(F32), -1e-4)
    li = lam_im.astype(F32)
    mag = jnp.exp(lr * dt)
    ar = mag * jnp.cos(li * dt)
    ai = mag * jnp.sin(li * dt)
    den = lr * lr + li * li
    nr, ni = ar - 1.0, ai
    fr = ((nr * lr + ni * li) / den)[..., None]
    fi = ((ni * lr - nr * li) / den)[..., None]
    br, bi = b_re.astype(F32), b_im.astype(F32)
    bbr = fr * br - fi * bi
    bbi = fr * bi + fi * br
    gps = S5_GROUPS // S5_SLABS
    eye = jnp.eye(gps, dtype=F32)

    def in_slab(m):
        m4 = m.reshape(S5_SLABS, gps, S5_STATE, S5_GROUP)
        return jnp.einsum('sgph,gk->sghkp', m4, eye).reshape(S5_SLABS, LANES, S5_SLAB_STATE)

    def out_slab(m):
        m4 = m.reshape(S5_SLABS, gps, S5_GROUP, S5_STATE)
        return jnp.einsum('sghp,gk->sgpkh', m4, eye).reshape(S5_SLABS, S5_SLAB_STATE, LANES)

    bmat = jnp.concatenate([in_slab(bbr), in_slab(bbi)], axis=2).astype(BF16)
    cre = out_slab(c_re.astype(F32)).astype(BF16)
    cim = out_slab(-c_im.astype(F32)).astype(BF16)
    a_re = jnp.broadcast_to(ar.reshape(1, S5_NSTATE), (NB, S5_NSTATE))
    a_im = jnp.broadcast_to(ai.reshape(1, S5_NSTATE), (NB, S5_NSTATE))
    return bmat, a_re, a_im, cre, cim


def _s5(u, bmat, a_re, a_im, cre, cim, d, w_glu, b_glu):
    t_rows = u.shape[0]
    steps = SCAN_STEPS
    rows = steps * NB
    full = lambda a: pl.BlockSpec(a.shape, lambda i: (0,) * a.ndim)
    d2 = d.astype(F32).reshape(1, S5_WIDTH)
    bg2 = b_glu.astype(F32).reshape(1, S5_WIDTH)
    wg = w_glu.astype(BF16)
    return pl.pallas_call(
        functools.partial(_s5_kernel, steps=steps),
        grid=(t_rows // rows,),
        in_specs=[pl.BlockSpec((rows, S5_WIDTH), lambda i: (i, 0)),
                  full(bmat), full(a_re), full(a_im), full(cre), full(cim), full(d2), full(wg), full(bg2)],
        out_specs=pl.BlockSpec((rows, S5_WIDTH), lambda i: (i, 0)),
        out_shape=jax.ShapeDtypeStruct((t_rows, S5_WIDTH), BF16),
        scratch_shapes=[pltpu.VMEM((rows, S5_NSTATE), F32), pltpu.VMEM((rows, S5_NSTATE), F32),
                        pltpu.VMEM((2, NB, S5_NSTATE), F32)],
        compiler_params=_cparams(1),
        name="l0_s5",
    )(u, bmat, a_re, a_im, cre, cim, d2, wg, bg2)


def _diffattn_kernel(lam_ref, q_ref, k_ref, v_ref, g_ref, o_ref,
                     s_ref, q1_ref, q2_ref, m_ref, l_ref, a_ref, *, lam_init, lp):
    lam = lam_ref[0]
    tq, tk = Q_TILE, K_TILE
    lane = lax.broadcasted_iota(jnp.int32, (tq, LANES), 1)
    qrow = lax.broadcasted_iota(jnp.int32, (tq, tk), 0)
    kloc = lax.broadcasted_iota(jnp.int32, (tq, tk), 1)
    nt = (((1,), (1,)), ((), ()))
    qs = (q1_ref, q2_ref)

    def key_start(j):
        return pl.multiple_of(jnp.minimum(j * tk, lp - tk), LANES)

    def lane_fold(x, op):
        r = x[:, :LANES]
        for c in range(1, x.shape[1] // LANES):
            r = op(r, x[:, LANES * c:LANES * (c + 1)])
        return r

    def q_tile(i, _):
        q0 = pl.multiple_of(jnp.minimum(i * tq, lp - tq), LANES)
        q = q_ref[0, pl.ds(q0, tq), :]
        zero = jnp.zeros_like(q)
        q1_ref[...] = jnp.where(lane < DA_HEAD_DIM, q, zero)
        q2_ref[...] = jnp.where(lane >= DA_HEAD_DIM, q, zero)
        qblk = jnp.right_shift(q0 + qrow, 6)
        n_kt = (q0 + tq + tk - 1) // tk
        m_ref[...] = jnp.full(m_ref.shape, NEG_INF, F32)
        l_ref[...] = jnp.zeros(l_ref.shape, F32)
        a_ref[...] = jnp.zeros(a_ref.shape, F32)

        def scores(j, _, masked):
            k0 = key_start(j)
            kt = k_ref[0, pl.ds(k0, tk), :]
            if masked:
                kpos = k0 + kloc
                mask = (kpos >= jnp.maximum(FRONT_PAD, j * tk)) & (jnp.right_shift(kpos, 6) <= qblk)
            for h in range(2):
                s = lax.dot_general(qs[h][...], kt, nt, preferred_element_type=F32)
                if masked:
                    s = jnp.where(mask, s, NEG_INF)
                s_ref[h, j] = s
                m_ref[h] = jnp.maximum(m_ref[h], lane_fold(s, jnp.maximum))
            return 0

        n_full = q0 // tk
        lax.fori_loop(0, jnp.minimum(1, n_full), functools.partial(scores, masked=True), 0)
        lax.fori_loop(1, n_full, functools.partial(scores, masked=False), 0)
        lax.fori_loop(n_full, n_kt, functools.partial(scores, masked=True), 0)
        m = [jnp.max(m_ref[h], axis=-1, keepdims=True) for h in range(2)]

        def values(j, _):
            vt = v_ref[0, pl.ds(key_start(j), tk), :]
            for h in range(2):
                p = jnp.exp(s_ref[h, j] - m[h])
                l_ref[h] += lane_fold(p, jnp.add)
                a_ref[h] += jnp.dot(p.astype(BF16), vt, preferred_element_type=F32)
            return 0

        lax.fori_loop(0, n_kt, values, 0)
        l1 = jnp.sum(l_ref[0], axis=-1, keepdims=True)
        l2 = jnp.sum(l_ref[1], axis=-1, keepdims=True)
        o = a_ref[0] / l1 - lam * (a_ref[1] / l2)
        o = o * lax.rsqrt(jnp.mean(o * o, axis=-1, keepdims=True) + LN_EPS) * g_ref[...]
        o_ref[0, pl.ds(q0, tq), :] = o * (1.0 - lam_init)
        return 0

    lax.fori_loop(0, pl.cdiv(lp, tq), q_tile, 0)


def _diffattn(q, k, v, lam, subln_g, lam_init):
    nb, lp, _ = q.shape
    g2 = subln_g.astype(F32).reshape(1, 2 * DA_HEAD_DIM)
    seq_spec = pl.BlockSpec((1, lp, LANES), lambda b, h: (b, 0, h))
    return pl.pallas_call(
        functools.partial(_diffattn_kernel, lam_init=lam_init, lp=lp),
        grid=(nb, DA_HEADS),
        in_specs=[pl.BlockSpec(memory_space=pltpu.SMEM), seq_spec, seq_spec, seq_spec,
                  pl.BlockSpec((1, LANES), lambda b, h: (0, 0))],
        out_specs=seq_spec,
        out_shape=jax.ShapeDtypeStruct((nb, lp, DA_WIDTH), F32),
        scratch_shapes=[pltpu.VMEM((2, pl.cdiv(lp, K_TILE), Q_TILE, K_TILE), F32),
                        pltpu.VMEM((Q_TILE, LANES), BF16), pltpu.VMEM((Q_TILE, LANES), BF16),
                        pltpu.VMEM((2, Q_TILE, LANES), F32), pltpu.VMEM((2, Q_TILE, LANES), F32),
                        pltpu.VMEM((2, Q_TILE, LANES), F32)],
        compiler_params=_cparams(2),
        name="l0_diffattn",
    )(lam.reshape(1), q, k, v, g2)


def _router_logits(h1, whi_ref, wlo_ref, rb_ref):
    hi = h1.astype(BF16)
    lo = (h1 - hi.astype(F32)).astype(BF16)
    return (jnp.dot(hi, whi_ref[...], preferred_element_type=F32)
            + jnp.dot(lo, whi_ref[...], preferred_element_type=F32)
            + jnp.dot(hi, wlo_ref[...], preferred_element_type=F32) + rb_ref[...])


def _outproj0_kernel(ys_ref, yda_ref, h_ref, w_ref, g_ref, b_ref, whi_ref, wlo_ref, rb_ref,
                     h1_ref, h1r_ref, lg_ref, das_ref, *, tl):
    per = DA_WIDTH // LANES
    for b in range(NB):
        for s in range(per):
            das_ref[s, pl.ds(b, tl, stride=NB), :] = yda_ref[b, :, LANES * s:LANES * (s + 1)]
    da = jnp.concatenate([das_ref[s] for s in range(per)], axis=1).astype(BF16)
    mix = (jnp.dot(ys_ref[...], w_ref[:S5_WIDTH, :], preferred_element_type=F32)
           + jnp.dot(da, w_ref[S5_WIDTH:, :], preferred_element_type=F32))
    h1 = _layer_norm_rows(ALPHA * h_ref[...] + mix, g_ref[...], b_ref[...])
    h1_ref[...] = h1
    _to_row_tiles(h1r_ref, h1)
    lg_ref[...] = _router_logits(h1, whi_ref, wlo_ref, rb_ref)


def _outproj1_kernel(y_ref, h_ref, w_ref, g_ref, b_ref, whi_ref, wlo_ref, rb_ref,
                     h1_ref, h1r_ref, lg_ref):
    mix = jnp.dot(y_ref[...], w_ref[...], preferred_element_type=F32)
    h1 = _layer_norm_rows(ALPHA * h_ref[...] + mix, g_ref[...], b_ref[...])
    h1_ref[...] = h1
    _to_row_tiles(h1r_ref, h1)
    lg_ref[...] = _router_logits(h1, whi_ref, wlo_ref, rb_ref)


def _router_weights(w_coarse, b_coarse, w_fine, b_fine):
    wf = jnp.transpose(w_fine.astype(F32), (1, 0, 2)).reshape(D_MODEL, N_EXPERTS)
    w = jnp.concatenate([w_coarse.astype(F32), wf], axis=1)
    w = jnp.pad(w, ((0, 0), (0, ROUTER_LANES - w.shape[1])))
    b = jnp.concatenate([b_coarse.astype(F32), b_fine.astype(F32).reshape(-1)])
    b = jnp.pad(b, (0, ROUTER_LANES - b.shape[0])).reshape(1, ROUTER_LANES)
    whi, wlo = _split_bf16(w)
    return whi, wlo, b


def _outproj(ys, yda, h, w_out, ln_g, ln_b, router):
    t_rows = h.shape[0]
    tm = TOK_TILE
    tl = tm // NB
    whi, wlo, rb = router
    w = w_out.astype(BF16)
    g2 = ln_g.astype(F32).reshape(1, D_MODEL)
    b2 = ln_b.astype(F32).reshape(1, D_MODEL)
    full = lambda a: pl.BlockSpec(a.shape, lambda i: (0,) * a.ndim)
    row = lambda width: pl.BlockSpec((tm, width), lambda i: (i, 0))
    common_in = [row(D_MODEL), full(w), full(g2), full(b2), full(whi), full(wlo), full(rb)]
    out_specs = [row(D_MODEL), pl.BlockSpec((tm * ROW_TILES, LANES), lambda i: (i, 0)), row(ROUTER_LANES)]
    out_shape = [jax.ShapeDtypeStruct((t_rows, D_MODEL), F32),
                 jax.ShapeDtypeStruct((t_rows * ROW_TILES, LANES), F32),
                 jax.ShapeDtypeStruct((t_rows, ROUTER_LANES), F32)]
    if yda is not None:
        return pl.pallas_call(
            functools.partial(_outproj0_kernel, tl=tl),
            grid=(t_rows // tm,),
            in_specs=[row(S5_WIDTH), pl.BlockSpec((NB, tl, DA_WIDTH), lambda i: (0, i, 0))] + common_in,
            out_specs=out_specs, out_shape=out_shape,
            scratch_shapes=[pltpu.VMEM((DA_WIDTH // LANES, tm, LANES), F32)],
            compiler_params=_cparams(1),
            name="l0_outproj",
        )(ys, yda, h, w, g2, b2, whi, wlo, rb)
    return pl.pallas_call(
        _outproj1_kernel,
        grid=(t_rows // tm,),
        in_specs=[row(ys.shape[1])] + common_in,
        out_specs=out_specs, out_shape=out_shape,
        compiler_params=_cparams(1),
        name="l1_outproj",
    )(ys, h, w, g2, b2, whi, wlo, rb)


def _route(logits, n_slots):
    t_rows = logits.shape[0]
    coarse = logits[:, :N_GROUPS]
    fine_all = logits[:, N_GROUPS:N_GROUPS + N_EXPERTS].reshape(t_rows, N_GROUPS, EXPERTS_PER_GROUP)
    p_coarse = jax.nn.softmax(coarse, axis=-1)
    grp = jnp.argmax(coarse, axis=-1)
    p_grp = jnp.take_along_axis(p_coarse, grp[:, None], axis=1)
    fine = jnp.take_along_axis(fine_all, grp[:, None, None], axis=1)[:, 0]
    top_v, top_i = lax.top_k(fine, TOP_K_FINE)
    gates = p_grp * jax.nn.softmax(top_v, axis=-1)
    experts = (grp[:, None] * EXPERTS_PER_GROUP + top_i).astype(jnp.int32)
    valid = (jnp.arange(t_rows) >= PAD_ROWS)[:, None]
    experts = jnp.where(valid, experts, N_EXPERTS)
    gates = jnp.where(valid, gates, 0.0)
    e_flat = experts.reshape(-1)
    onehot = (e_flat[:, None] == jnp.arange(N_EXPERTS + 1)[None, :]).astype(jnp.int32)
    csum = jnp.cumsum(onehot, axis=0)
    rank = jnp.take_along_axis(csum, e_flat[:, None], axis=1)[:, 0] - 1
    counts = csum[-1, :N_EXPERTS]
    padded = (counts + EXPERT_TILE - 1) // EXPERT_TILE * EXPERT_TILE
    pad_end = jnp.cumsum(padded)
    pad_start = jnp.concatenate([pad_end - padded, jnp.array([n_slots], jnp.int32)])
    dest = jnp.where(e_flat < N_EXPERTS, pad_start[e_flat] + rank, n_slots).astype(jnp.int32)
    tok = jnp.repeat(jnp.arange(t_rows, dtype=jnp.int32), TOP_K_FINE)
    slot_tok = jnp.zeros((n_slots,), jnp.int32).at[dest].set(tok, mode='drop')
    nblk = n_slots // EXPERT_TILE
    block_expert = jnp.minimum(
        jnp.searchsorted(pad_end, jnp.arange(nblk, dtype=jnp.int32) * EXPERT_TILE, side='right'),
        N_EXPERTS - 1).astype(jnp.int32)
    n_used = (pad_end[-1] // EXPERT_TILE).astype(jnp.int32).reshape(1)
    pos = jnp.where(dest < n_slots, dest, 0)
    return slot_tok * ROW_TILES, block_expert, n_used, pos * ROW_TILES, gates.astype(F32)


ROW_TILES = D_MODEL // LANES


def _to_row_tiles(dst_ref, val):
    rows = val.shape[0]
    for s in range(ROW_TILES):
        dst_ref[pl.ds(s, rows, stride=ROW_TILES), :] = val[:, LANES * s:LANES * (s + 1)]


def _from_row_tiles(src_ref, rows):
    return jnp.concatenate([src_ref[pl.ds(s, rows, stride=ROW_TILES), :] for s in range(ROW_TILES)], axis=1)


def _expert_kernel(be_ref, nused_ref, tok_cur_ref, tok_nxt_ref, x_hbm, wg_ref, wu_ref, wd_ref,
                   y_ref, xb0_ref, xb1_ref, sem_ref):
    i = pl.program_id(0)
    n_used = nused_ref[0]
    tb = EXPERT_TILE
    bufs = (xb0_ref, xb1_ref)

    def gather(tok_ref, p):
        for r in range(tb):
            row = pl.multiple_of(tok_ref[0, 0, r], ROW_TILES)
            pltpu.make_async_copy(x_hbm.at[pl.ds(row, ROW_TILES), :],
                                  bufs[p].at[pl.ds(ROW_TILES * r, ROW_TILES), :], sem_ref.at[p]).start()

    def wait(p):
        pltpu.make_async_copy(x_hbm.at[pl.ds(0, tb * ROW_TILES), :], bufs[p], sem_ref.at[p]).wait()

    @pl.when(i == 0)
    def _():
        gather(tok_cur_ref, 0)

    for p in range(2):
        @pl.when((lax.rem(i, 2) == p) & (i < n_used))
        def _(p=p):
            wait(p)
            gather(tok_nxt_ref, 1 - p)
            x = _from_row_tiles(bufs[p], tb).astype(BF16)
            g = jnp.dot(x, wg_ref[0].astype(BF16), preferred_element_type=F32)
            u = jnp.dot(x, wu_ref[0].astype(BF16), preferred_element_type=F32)
            hid = (g * _sigmoid(g) * u).astype(BF16)
            _to_row_tiles(y_ref, jnp.dot(hid, wd_ref[0].astype(BF16), preferred_element_type=F32))

        @pl.when((lax.rem(i, 2) == p) & (i == n_used))
        def _(p=p):
            wait(p)

    @pl.when(i >= n_used)
    def _():
        y_ref[...] = jnp.zeros_like(y_ref)


def _experts(h1r, slot_row, block_expert, n_used, w_gate, w_up, w_down):
    n_slots = slot_row.shape[0]
    nblk = n_slots // EXPERT_TILE
    tok3 = slot_row.reshape(nblk, 1, EXPERT_TILE)
    smem_blk = lambda fn: pl.BlockSpec((1, 1, EXPERT_TILE), fn, memory_space=pltpu.SMEM)
    grid_spec = pltpu.PrefetchScalarGridSpec(
        num_scalar_prefetch=2,
        grid=(nblk,),
        in_specs=[smem_blk(lambda i, be, nu: (i, 0, 0)),
                  smem_blk(lambda i, be, nu: (jnp.minimum(i + 1, nblk - 1), 0, 0)),
                  pl.BlockSpec(memory_space=pl.ANY),
                  pl.BlockSpec((1, D_MODEL, D_EXPERT), lambda i, be, nu: (be[i], 0, 0)),
                  pl.BlockSpec((1, D_MODEL, D_EXPERT), lambda i, be, nu: (be[i], 0, 0)),
                  pl.BlockSpec((1, D_EXPERT, D_MODEL), lambda i, be, nu: (be[i], 0, 0))],
        out_specs=pl.BlockSpec((EXPERT_TILE * ROW_TILES, LANES), lambda i, be, nu: (i, 0)),
        scratch_shapes=[pltpu.VMEM((EXPERT_TILE * ROW_TILES, LANES), F32),
                        pltpu.VMEM((EXPERT_TILE * ROW_TILES, LANES), F32),
                        pltpu.SemaphoreType.DMA((2,))],
    )
    return pl.pallas_call(
        _expert_kernel,
        grid_spec=grid_spec,
        out_shape=jax.ShapeDtypeStruct((n_slots * ROW_TILES, LANES), F32),
        compiler_params=_cparams(1),
        name="moe_experts",
    )(block_expert, n_used, tok3, tok3, h1r, w_gate, w_up, w_down)


def _combine_kernel(pos_cur_ref, pos_nxt_ref, yb_hbm, gate_ref, h_ref, g_ref, b_ref, o_ref,
                    ybuf_ref, sem_ref, *rest, first_tile, n_tiles, to_batch_major):
    i = pl.program_id(0)
    tm = COMBINE_TILE

    def gather(pos_ref, p):
        for r in range(tm):
            for kk in range(TOP_K_FINE):
                row = pl.multiple_of(pos_ref[0, 0, TOP_K_FINE * r + kk], ROW_TILES)
                pltpu.make_async_copy(yb_hbm.at[pl.ds(row, ROW_TILES), :],
                                      ybuf_ref.at[p, kk, pl.ds(ROW_TILES * r, ROW_TILES), :],
                                      sem_ref.at[p]).start()

    def finish(p, look_ahead):
        for kk in range(TOP_K_FINE):
            pltpu.make_async_copy(yb_hbm.at[pl.ds(0, tm * ROW_TILES), :], ybuf_ref.at[p, kk],
                                  sem_ref.at[p]).wait()
        if look_ahead:
            gather(pos_nxt_ref, 1 - p)
        gts = gate_ref[...]
        moe = (_from_row_tiles(ybuf_ref.at[p, 0], tm) * gts[:, 0:1]
               + _from_row_tiles(ybuf_ref.at[p, 1], tm) * gts[:, 1:2])
        h2 = _layer_norm_rows(ALPHA * h_ref[...] + moe, g_ref[...], b_ref[...])
        if not to_batch_major:
            o_ref[...] = h2
        else:
            hs_ref = rest[0]
            tl = tm // NB
            for s in range(D_MODEL // LANES):
                hs_ref[s] = h2[:, LANES * s:LANES * (s + 1)]
            for b in range(NB):
                for s in range(D_MODEL // LANES):
                    o_ref[b, :, LANES * s:LANES * (s + 1)] = hs_ref[s, pl.ds(b, tl, stride=NB), :]

    @pl.when(i == 0)
    def _():
        gather(pos_cur_ref, 0)

    for p in range(2):
        @pl.when((lax.rem(i, 2) == p) & (i < n_tiles - 1))
        def _(p=p):
            finish(p, True)

    @pl.when(i == n_tiles - 1)
    def _():
        finish((n_tiles - 1) % 2, False)


def _combine(yb, pos, gates, h1, ln_g, ln_b, *, final):
    t_rows = h1.shape[0]
    tm = COMBINE_TILE
    first_tile = (LANES * NB) // tm if final else 0
    n_tiles = t_rows // tm - first_tile
    pos3 = pos.reshape(t_rows // tm, 1, TOP_K_FINE * tm)
    g2 = ln_g.astype(F32).reshape(1, D_MODEL)
    b2 = ln_b.astype(F32).reshape(1, D_MODEL)
    full = lambda a: pl.BlockSpec(a.shape, lambda i: (0,) * a.ndim)
    smem_blk = lambda fn: pl.BlockSpec((1, 1, TOP_K_FINE * tm), fn, memory_space=pltpu.SMEM)
    last = t_rows // tm - 1
    in_specs = [smem_blk(lambda i: (i + first_tile, 0, 0)),
                smem_blk(lambda i: (jnp.minimum(i + first_tile + 1, last), 0, 0)),
                pl.BlockSpec(memory_space=pl.ANY),
                pl.BlockSpec((tm, TOP_K_FINE), lambda i: (i + first_tile, 0)),
                pl.BlockSpec((tm, D_MODEL), lambda i: (i + first_tile, 0)),
                full(g2), full(b2)]
    scratch = [pltpu.VMEM((2, TOP_K_FINE, tm * ROW_TILES, LANES), F32), pltpu.SemaphoreType.DMA((2,))]
    if final:
        tl = tm // NB
        seq = t_rows // NB - LANES
        out_specs = pl.BlockSpec((NB, tl, D_MODEL), lambda i: (0, i, 0))
        out_shape = jax.ShapeDtypeStruct((NB, seq, D_MODEL), F32)
        scratch.append(pltpu.VMEM((D_MODEL // LANES, tm, LANES), F32))
    else:
        out_specs = pl.BlockSpec((tm, D_MODEL), lambda i: (i, 0))
        out_shape = jax.ShapeDtypeStruct((t_rows, D_MODEL), F32)
    return pl.pallas_call(
        functools.partial(_combine_kernel, first_tile=first_tile, n_tiles=n_tiles, to_batch_major=final),
        grid=(n_tiles,),
        in_specs=in_specs, out_specs=out_specs, out_shape=out_shape,
        scratch_shapes=scratch,
        compiler_params=_cparams(1),
        name="moe_combine_final" if final else "moe_combine",
    )(pos3, pos3, yb, gates, h1, g2, b2)


def _moe_block(h1, h1r, logits, w_gate, w_up, w_down, ln_g, ln_b, *, final):
    t_rows = h1.shape[0]
    n_assign = (t_rows - PAD_ROWS) * TOP_K_FINE
    nblk = -(-(n_assign + N_EXPERTS * (EXPERT_TILE - 1)) // EXPERT_TILE) + 1
    n_slots = nblk * EXPERT_TILE
    slot_row, block_expert, n_used, pos, gates = _route(logits, n_slots)
    yb = _experts(h1r, slot_row, block_expert, n_used, w_gate, w_up, w_down)
    return _combine(yb, pos, gates, h1, ln_g, ln_b, final=final)


def _inproj1_kernel(h_ref, w_ref, gg_ref, rec_ref):
    x = h_ref[...].astype(BF16)
    z = jnp.dot(x, w_ref[...], preferred_element_type=F32)
    gg_ref[...] = _gelu(z[:, :D_RNN]).astype(BF16)
    rec_ref[...] = z[:, D_RNN:]


def _inproj1(h, w_bf16):
    t_rows = h.shape[0]
    tm = TOK_TILE
    row = lambda width: pl.BlockSpec((tm, width), lambda i: (i, 0))
    return pl.pallas_call(
        _inproj1_kernel,
        grid=(t_rows // tm,),
        in_specs=[row(D_MODEL), pl.BlockSpec(w_bf16.shape, lambda i: (0, 0))],
        out_specs=[row(D_RNN), row(D_RNN)],
        out_shape=[jax.ShapeDtypeStruct((t_rows, D_RNN), BF16), jax.ShapeDtypeStruct((t_rows, D_RNN), F32)],
        compiler_params=_cparams(1),
        name="l1_inproj",
    )(h, w_bf16)


CONV_HALO = (CONV_WIDTH - 1) * NB


def _rglru_kernel(rec_ref, gg_ref, cw_ref, cb_ref, wax_ref, ba_ref, bx_ref, sp_ref,
                  y_ref, rp_ref, a_ref, b_ref, st_ref, *, steps):
    i = pl.program_id(0)
    rows = steps * NB

    @pl.when(i == 0)
    def _():
        st_ref[...] = jnp.zeros_like(st_ref)
        rp_ref[0:CONV_HALO, :] = jnp.zeros((CONV_HALO, D_RNN), F32)

    row = i * rows + lax.broadcasted_iota(jnp.int32, (rows, 1), 0)
    real = row >= PAD_ROWS
    rp_ref[CONV_HALO:CONV_HALO + rows, :] = jnp.where(real, rec_ref[...], 0.0)
    xc = cb_ref[...] + sum(rp_ref[NB * j:NB * j + rows, :] * cw_ref[j:j + 1, :] for j in range(CONV_WIDTH))
    tail = rp_ref[rows:rows + CONV_HALO, :]
    rp_ref[0:CONV_HALO, :] = tail

    for n in range(LRU_BLOCKS):
        cs = slice(LRU_BLOCK_W * n, LRU_BLOCK_W * (n + 1))
        xb = xc[:, cs]
        ra = jnp.dot(xb.astype(BF16), wax_ref[n], preferred_element_type=F32)
        r = _sigmoid(ra[:, :LRU_BLOCK_W] + ba_ref[:, cs])
        ig = _sigmoid(ra[:, LRU_BLOCK_W:] + bx_ref[:, cs])
        log_a = -LRU_C * r * sp_ref[:, cs]
        a = jnp.exp(log_a)
        bt = jnp.sqrt(1.0 - a * a) * (ig * xb)
        a_ref[:, cs] = a
        b_ref[:, cs] = jnp.where(real, bt, 0.0)

    def step(t, h):
        r0 = pl.multiple_of(t * NB, NB)
        hn = a_ref[pl.ds(r0, NB), :] * h + b_ref[pl.ds(r0, NB), :]
        b_ref[pl.ds(r0, NB), :] = hn
        return hn

    st_ref[...] = lax.fori_loop(0, steps, step, st_ref[...])
    y_ref[...] = (gg_ref[...].astype(F32) * b_ref[...]).astype(BF16)


def _rglru(rec, gg, conv_w, conv_b, w_a, b_a, w_x, b_x, lru_lambda):
    t_rows = rec.shape[0]
    steps = SCAN_STEPS
    rows = steps * NB
    cw = conv_w.astype(F32)
    cb = conv_b.astype(F32).reshape(1, D_RNN)
    wax = jnp.concatenate([w_a.astype(F32), w_x.astype(F32)], axis=2).astype(BF16)
    ba = b_a.astype(F32).reshape(1, D_RNN)
    bx = b_x.astype(F32).reshape(1, D_RNN)
    sp = jax.nn.softplus(-lru_lambda.astype(F32)).reshape(1, D_RNN)
    full = lambda a: pl.BlockSpec(a.shape, lambda i: (0,) * a.ndim)
    row = pl.BlockSpec((rows, D_RNN), lambda i: (i, 0))
    return pl.pallas_call(
        functools.partial(_rglru_kernel, steps=steps),
        grid=(t_rows // rows,),
        in_specs=[row, row, full(cw), full(cb), full(wax), full(ba), full(bx), full(sp)],
        out_specs=row,
        out_shape=jax.ShapeDtypeStruct((t_rows, D_RNN), BF16),
        scratch_shapes=[pltpu.VMEM((rows + CONV_HALO, D_RNN), F32), pltpu.VMEM((rows, D_RNN), F32),
                        pltpu.VMEM((rows, D_RNN), F32), pltpu.VMEM((NB, D_RNN), F32)],
        compiler_params=_cparams(1),
        name="l1_rglru",
    )(rec, gg, cw, cb, wax, ba, bx, sp)


def kernel(x, meta, l0_ln1_g, l0_ln1_b, l0_w_in, l0_s5_lambda_re, l0_s5_lambda_im, l0_s5_log_dt, l0_s5_b_re, l0_s5_b_im, l0_s5_c_re, l0_s5_c_im, l0_s5_d, l0_s5_w_glu, l0_s5_b_glu, l0_da_lq1, l0_da_lk1, l0_da_lq2, l0_da_lk2, l0_da_subln_g, l0_w_out, l0_ln2_g, l0_ln2_b, l0_moe_w_coarse, l0_moe_b_coarse, l0_moe_w_fine, l0_moe_b_fine, l0_moe_w_gate, l0_moe_w_up, l0_moe_w_down, l1_ln1_g, l1_ln1_b, l1_w_in, l1_conv_w, l1_conv_b, l1_lru_w_a, l1_lru_b_a, l1_lru_w_x, l1_lru_b_x, l1_lru_lambda, l1_w_out, l1_ln2_g, l1_ln2_b, l1_moe_w_coarse, l1_moe_b_coarse, l1_moe_w_fine, l1_moe_b_fine, l1_moe_w_gate, l1_moe_w_up, l1_moe_w_down):
    bsz, seq, _ = x.shape
    assert bsz == NB and seq % Q_TILE == 0
    dt = x.dtype
    lp = FRONT_PAD + N_META + seq
    h = jnp.concatenate([
        jnp.zeros((FRONT_PAD, NB, D_MODEL), dt),
        jnp.broadcast_to(meta.astype(dt)[:, None, :], (N_META, NB, D_MODEL)),
        jnp.transpose(x, (1, 0, 2))], axis=0).reshape(lp * NB, D_MODEL)

    col_scale = jnp.concatenate([jnp.ones((S5_WIDTH,), F32),
                                 jnp.full((DA_WIDTH,), DA_HEAD_DIM ** -0.5, F32),
                                 jnp.ones((2 * DA_WIDTH,), F32)])
    w_in0 = (l0_w_in.astype(F32) * col_scale[None, :]).astype(BF16)
    u, q, k, v = _inproj0(h, w_in0, lp)
    s5p = _s5_params(l0_s5_lambda_re, l0_s5_lambda_im, l0_s5_log_dt,
                     l0_s5_b_re, l0_s5_b_im, l0_s5_c_re, l0_s5_c_im)
    y_s5 = _s5(u, *s5p, l0_s5_d, l0_s5_w_glu, l0_s5_b_glu)
    lam_init = 0.8 - 0.6 * math.exp(-0.3 * 0)
    lam = (jnp.exp(jnp.sum(l0_da_lq1.astype(F32) * l0_da_lk1.astype(F32)))
           - jnp.exp(jnp.sum(l0_da_lq2.astype(F32) * l0_da_lk2.astype(F32))) + lam_init)
    y_da = _diffattn(q, k, v, lam, l0_da_subln_g, lam_init)
    router0 = _router_weights(l0_moe_w_coarse, l0_moe_b_coarse, l0_moe_w_fine, l0_moe_b_fine)
    h, hr, logits = _outproj(y_s5, y_da, h, l0_w_out, l0_ln1_g, l0_ln1_b, router0)
    h = _moe_block(h, hr, logits, l0_moe_w_gate, l0_moe_w_up, l0_moe_w_down, l0_ln2_g, l0_ln2_b, final=False)

    gg, rec = _inproj1(h, l1_w_in.astype(BF16))
    y = _rglru(rec, gg, l1_conv_w, l1_conv_b, l1_lru_w_a, l1_lru_b_a, l1_lru_w_x, l1_lru_b_x, l1_lru_lambda)
    router1 = _router_weights(l1_moe_w_coarse, l1_moe_b_coarse, l1_moe_w_fine, l1_moe_b_fine)
    h, hr, logits = _outproj(y, None, h, l1_w_out, l1_ln1_g, l1_ln1_b, router1)
    out = _moe_block(h, hr, logits, l1_moe_w_gate, l1_moe_w_up, l1_moe_w_down, l1_ln2_g, l1_ln2_b, final=True)
    return out.astype(dt)
```

```python
import functools
import math

import jax
import jax.numpy as jnp
from jax import lax
from jax.experimental import pallas as pl
from jax.experimental.pallas import tpu as pltpu

F32 = jnp.float32
BF16 = jnp.bfloat16

D_MODEL = 1024
DEPTH = 2
CHUNK = 64
N_META = 16
S5_WIDTH = 512
S5_GROUP = 16
S5_GROUPS = 32
S5_STATE = 64
DA_HEADS = 4
DA_HEAD_DIM = 64
DA_WIDTH = 512
D_RNN = 1280
LRU_BLOCKS = 10
LRU_BLOCK_W = 128
CONV_WIDTH = 4
LRU_C = 8.0
N_GROUPS = 4
EXPERTS_PER_GROUP = 8
N_EXPERTS = 32
TOP_K_FINE = 2
D_EXPERT = 512
ALPHA = (2 * DEPTH) ** 0.25
LN_EPS = 1e-5
NEG_INF = -1e30

NB = 8
LANES = 128
FRONT_PAD = LANES - N_META
PAD_ROWS = FRONT_PAD * NB
Q_TILE = 256
K_TILE = 512
TOK_TILE = 512
SCAN_STEPS = 32
EXPERT_TILE = 256
COMBINE_TILE = 256
ROUTER_LANES = 128
VMEM_LIMIT = 48 * 1024 * 1024


def _cparams(n_axes, vmem=VMEM_LIMIT):
    return pltpu.CompilerParams(dimension_semantics=("arbitrary",) * n_axes,
                                vmem_limit_bytes=vmem)


def _gelu(x):
    return 0.5 * x * (1.0 + jnp.tanh(math.sqrt(2.0 / math.pi) * (x + 0.044715 * (x * x * x))))


def _sigmoid(x):
    return 1.0 / (1.0 + jnp.exp(-x))


def _layer_norm_rows(r, g, b):
    mu = jnp.mean(r, axis=-1, keepdims=True)
    c = r - mu
    var = jnp.mean(c * c, axis=-1, keepdims=True)
    return c * lax.rsqrt(var + LN_EPS) * g + b


def _split_bf16(w):
    hi = w.astype(BF16)
    lo = (w - hi.astype(F32)).astype(BF16)
    return hi, lo


def _inproj0_kernel(h_ref, w_ref, u_ref, q_ref, k_ref, v_ref, zs_ref, *, tl):
    x = h_ref[...].astype(BF16)
    z = jnp.dot(x, w_ref[...], preferred_element_type=F32)
    u_ref[...] = z[:, :S5_WIDTH]
    n_slab = 3 * DA_WIDTH // LANES
    for s in range(n_slab):
        zs_ref[s] = z[:, S5_WIDTH + LANES * s:S5_WIDTH + LANES * (s + 1)]
    dsts = (q_ref, k_ref, v_ref)
    per = DA_WIDTH // LANES
    for b in range(NB):
        for s in range(n_slab):
            c = (s % per) * LANES
            dsts[s // per][b, :, c:c + LANES] = zs_ref[s, pl.ds(b, tl, stride=NB), :].astype(BF16)


def _inproj0(h, w_bf16, lp):
    t_rows = h.shape[0]
    tm = TOK_TILE
    tl = tm // NB
    qkv_shape = jax.ShapeDtypeStruct((NB, lp, DA_WIDTH), BF16)
    qkv_spec = pl.BlockSpec((NB, tl, DA_WIDTH), lambda i: (0, i, 0))
    return pl.pallas_call(
        functools.partial(_inproj0_kernel, tl=tl),
        grid=(t_rows // tm,),
        in_specs=[pl.BlockSpec((tm, D_MODEL), lambda i: (i, 0)),
                  pl.BlockSpec(w_bf16.shape, lambda i: (0, 0))],
        out_specs=[pl.BlockSpec((tm, S5_WIDTH), lambda i: (i, 0)), qkv_spec, qkv_spec, qkv_spec],
        out_shape=[jax.ShapeDtypeStruct((t_rows, S5_WIDTH), F32), qkv_shape, qkv_shape, qkv_shape],
        scratch_shapes=[pltpu.VMEM((3 * DA_WIDTH // LANES, tm, LANES), F32)],
        compiler_params=_cparams(1),
        name="l0_inproj",
    )(h, w_bf16)


S5_SLABS = S5_WIDTH // LANES
S5_SLAB_STATE = (S5_GROUPS // S5_SLABS) * S5_STATE
S5_NSTATE = S5_GROUPS * S5_STATE


def _s5_kernel(u_ref, bmat_ref, are_ref, aim_ref, cre_ref, cim_ref, d_ref, wglu_ref, bglu_ref,
               y_ref, hre_ref, him_ref, st_ref, *, steps):
    i = pl.program_id(0)
    rows = steps * NB

    @pl.when(i == 0)
    def _():
        st_ref[...] = jnp.zeros_like(st_ref)

    row = i * rows + lax.broadcasted_iota(jnp.int32, (rows, 1), 0)
    u = jnp.where(row >= PAD_ROWS, u_ref[...], 0.0)
    ub = u.astype(BF16)
    for s in range(S5_SLABS):
        bu = jnp.dot(ub[:, LANES * s:LANES * (s + 1)], bmat_ref[s], preferred_element_type=F32)
        hre_ref[:, S5_SLAB_STATE * s:S5_SLAB_STATE * (s + 1)] = bu[:, :S5_SLAB_STATE]
        him_ref[:, S5_SLAB_STATE * s:S5_SLAB_STATE * (s + 1)] = bu[:, S5_SLAB_STATE:]

    cw = 512
    for cg in range(S5_NSTATE // cw):
        c0 = cg * cw
        a_r = are_ref[:, c0:c0 + cw]
        a_i = aim_ref[:, c0:c0 + cw]

        def step(t, carry, c0=c0, a_r=a_r, a_i=a_i):
            sr, si = carry
            r0 = pl.multiple_of(t * NB, NB)
            br = hre_ref[pl.ds(r0, NB), c0:c0 + cw]
            bi = him_ref[pl.ds(r0, NB), c0:c0 + cw]
            nr = a_r * sr - a_i * si + br
            ni = a_r * si + a_i * sr + bi
            hre_ref[pl.ds(r0, NB), c0:c0 + cw] = nr
            him_ref[pl.ds(r0, NB), c0:c0 + cw] = ni
            return nr, ni

        sr, si = lax.fori_loop(0, steps, step, (st_ref[0, :, c0:c0 + cw], st_ref[1, :, c0:c0 + cw]))
        st_ref[0, :, c0:c0 + cw] = sr
        st_ref[1, :, c0:c0 + cw] = si

    ys = []
    for s in range(S5_SLABS):
        hr = hre_ref[:, S5_SLAB_STATE * s:S5_SLAB_STATE * (s + 1)].astype(BF16)
        hi = him_ref[:, S5_SLAB_STATE * s:S5_SLAB_STATE * (s + 1)].astype(BF16)
        ys.append(jnp.dot(hr, cre_ref[s], preferred_element_type=F32)
                  + jnp.dot(hi, cim_ref[s], preferred_element_type=F32))
    y = jnp.concatenate(ys, axis=1) + d_ref[...] * u
    y = _gelu(y)
    gate = _sigmoid(jnp.dot(y.astype(BF16), wglu_ref[...], preferred_element_type=F32) + bglu_ref[...])
    y_ref[...] = (y * gate).astype(BF16)


def _s5_params(lam_re, lam_im, log_dt, b_re, b_im, c_re, c_im):
    dt = jnp.exp(log_dt.astype(F32))[:, None]
    lr = jnp.minimum(lam_re.astype(F32), -1e-4)
    li = lam_im.astype(F32)
    mag = jnp.exp(lr * dt)
    ar = mag * jnp.cos(li * dt)
    ai = mag * jnp.sin(li * dt)
    den = lr * lr + li * li
    nr, ni = ar - 1.0, ai
    fr = ((nr * lr + ni * li) / den)[..., None]
    fi = ((ni * lr - nr * li) / den)[..., None]
    br, bi = b_re.astype(F32), b_im.astype(F32)
    bbr = fr * br - fi * bi
    bbi = fr * bi + fi * br
    gps = S5_GROUPS // S5_SLABS
    eye = jnp.eye(gps, dtype=F32)

    def in_slab(m):
        m4 = m.reshape(S5_SLABS, gps, S5_STATE, S5_GROUP)
        return jnp.einsum('sgph,gk->sghkp', m4, eye).reshape(S5_SLABS, LANES, S5_SLAB_STATE)

    def out_slab(m):
        m4 = m.reshape(S5_SLABS, gps, S5_GROUP, S5_STATE)
        return jnp.einsum('sghp,gk->sgpkh', m4, eye).reshape(S5_SLABS, S5_SLAB_STATE, LANES)

    bmat = jnp.concatenate([in_slab(bbr), in_slab(bbi)], axis=2).astype(BF16)
    cre = out_slab(c_re.astype(F32)).astype(BF16)
    cim = out_slab(-c_im.astype(F32)).astype(BF16)
    a_re = jnp.broadcast_to(ar.reshape(1, S5_NSTATE), (NB, S5_NSTATE))
    a_im = jnp.broadcast_to(ai.reshape(1, S5_NSTATE), (NB, S5_NSTATE))
    return bmat, a_re, a_im, cre, cim


def _s5(u, bmat, a_re, a_im, cre, cim, d, w_glu, b_glu):
    t_rows = u.shape[0]
    steps = SCAN_STEPS
    rows = steps * NB
    full = lambda a: pl.BlockSpec(a.shape, lambda i: (0,) * a.ndim)
    d2 = d.astype(F32).reshape(1, S5_WIDTH)
    bg2 = b_glu.astype(F32).reshape(1, S5_WIDTH)
    wg = w_glu.astype(BF16)
    return pl.pallas_call(
        functools.partial(_s5_kernel, steps=steps),
        grid=(t_rows // rows,),
        in_specs=[pl.BlockSpec((rows, S5_WIDTH), lambda i: (i, 0)),
                  full(bmat), full(a_re), full(a_im), full(cre), full(cim), full(d2), full(wg), full(bg2)],
        out_specs=pl.BlockSpec((rows, S5_WIDTH), lambda i: (i, 0)),
        out_shape=jax.ShapeDtypeStruct((t_rows, S5_WIDTH), BF16),
        scratch_shapes=[pltpu.VMEM((rows, S5_NSTATE), F32), pltpu.VMEM((rows, S5_NSTATE), F32),
                        pltpu.VMEM((2, NB, S5_NSTATE), F32)],
        compiler_params=_cparams(1),
        name="l0_s5",
    )(u, bmat, a_re, a_im, cre, cim, d2, wg, bg2)


def _diffattn_kernel(lam_ref, q_ref, k_ref, v_ref, g_ref, o_ref,
                     s_ref, q1_ref, q2_ref, m_ref, l_ref, a_ref, *, lam_init, lp):
    lam = lam_ref[0]
    tq, tk = Q_TILE, K_TILE
    lane = lax.broadcasted_iota(jnp.int32, (tq, LANES), 1)
    qrow = lax.broadcasted_iota(jnp.int32, (tq, tk), 0)
    kloc = lax.broadcasted_iota(jnp.int32, (tq, tk), 1)
    nt = (((1,), (1,)), ((), ()))
    qs = (q1_ref, q2_ref)

    def key_start(j):
        return pl.multiple_of(jnp.minimum(j * tk, lp - tk), LANES)

    def lane_fold(x, op):
        r = x[:, :LANES]
        for c in range(1, x.shape[1] // LANES):
            r = op(r, x[:, LANES * c:LANES * (c + 1)])
        return r

    def q_tile(i, _):
        q0 = pl.multiple_of(jnp.minimum(i * tq, lp - tq), LANES)
        q = q_ref[0, pl.ds(q0, tq), :]
        zero = jnp.zeros_like(q)
        q1_ref[...] = jnp.where(lane < DA_HEAD_DIM, q, zero)
        q2_ref[...] = jnp.where(lane >= DA_HEAD_DIM, q, zero)
        qblk = jnp.right_shift(q0 + qrow, 6)
        n_kt = (q0 + tq + tk - 1) // tk
        m_ref[...] = jnp.full(m_ref.shape, NEG_INF, F32)
        l_ref[...] = jnp.zeros(l_ref.shape, F32)
        a_ref[...] = jnp.zeros(a_ref.shape, F32)

        def scores(j, _, masked):
            k0 = key_start(j)
            kt = k_ref[0, pl.ds(k0, tk), :]
            if masked:
                kpos = k0 + kloc
                mask = (kpos >= jnp.maximum(FRONT_PAD, j * tk)) & (jnp.right_shift(kpos, 6) <= qblk)
            for h in range(2):
                s = lax.dot_general(qs[h][...], kt, nt, preferred_element_type=F32)
                if masked:
                    s = jnp.where(mask, s, NEG_INF)
                s_ref[h, j] = s
                m_ref[h] = jnp.maximum(m_ref[h], lane_fold(s, jnp.maximum))
            return 0

        n_full = q0 // tk
        lax.fori_loop(0, jnp.minimum(1, n_full), functools.partial(scores, masked=True), 0)
        lax.fori_loop(1, n_full, functools.partial(scores, masked=False), 0)
        lax.fori_loop(n_full, n_kt, functools.partial(scores, masked=True), 0)
        m = [jnp.max(m_ref[h], axis=-1, keepdims=True) for h in range(2)]

        def values(j, _):
            vt = v_ref[0, pl.ds(key_start(j), tk), :]
            for h in range(2):
                p = jnp.exp(s_ref[h, j] - m[h])
                l_ref[h] += lane_fold(p, jnp.add)
                a_ref[h] += jnp.dot(p.astype(BF16), vt, preferred_element_type=F32)
            return 0

        lax.fori_loop(0, n_kt, values, 0)
        l1 = jnp.sum(l_ref[0], axis=-1, keepdims=True)
        l2 = jnp.sum(l_ref[1], axis=-1, keepdims=True)
        o = a_ref[0] / l1 - lam * (a_ref[1] / l2)
        o = o * lax.rsqrt(jnp.mean(o * o, axis=-1, keepdims=True) + LN_EPS) * g_ref[...]
        o_ref[0, pl.ds(q0, tq), :] = o * (1.0 - lam_init)
        return 0

    lax.fori_loop(0, pl.cdiv(lp, tq), q_tile, 0)


def _diffattn(q, k, v, lam, subln_g, lam_init):
    nb, lp, _ = q.shape
    g2 = subln_g.astype(F32).reshape(1, 2 * DA_HEAD_DIM)
    seq_spec = pl.BlockSpec((1, lp, LANES), lambda b, h: (b, 0, h))
    return pl.pallas_call(
        functools.partial(_diffattn_kernel, lam_init=lam_init, lp=lp),
        grid=(nb, DA_HEADS),
        in_specs=[pl.BlockSpec(memory_space=pltpu.SMEM), seq_spec, seq_spec, seq_spec,
                  pl.BlockSpec((1, LANES), lambda b, h: (0, 0))],
        out_specs=seq_spec,
        out_shape=jax.ShapeDtypeStruct((nb, lp, DA_WIDTH), F32),
        scratch_shapes=[pltpu.VMEM((2, pl.cdiv(lp, K_TILE), Q_TILE, K_TILE), F32),
                        pltpu.VMEM((Q_TILE, LANES), BF16), pltpu.VMEM((Q_TILE, LANES), BF16),
                        pltpu.VMEM((2, Q_TILE, LANES), F32), pltpu.VMEM((2, Q_TILE, LANES), F32),
                        pltpu.VMEM((2, Q_TILE, LANES), F32)],
        compiler_params=_cparams(2),
        name="l0_diffattn",
    )(lam.reshape(1), q, k, v, g2)


def _router_logits(h1, whi_ref, wlo_ref, rb_ref):
    hi = h1.astype(BF16)
    lo = (h1 - hi.astype(F32)).astype(BF16)
    return (jnp.dot(hi, whi_ref[...], preferred_element_type=F32)
            + jnp.dot(lo, whi_ref[...], preferred_element_type=F32)
            + jnp.dot(hi, wlo_ref[...], preferred_element_type=F32) + rb_ref[...])


ROUTE_E, ROUTE_GATE, ROUTE_RANK = 0, 2, 4


def _route_rows(lg, cnt_ref, first_row):
    rows = lg.shape[0]
    lane = lax.broadcasted_iota(jnp.int32, lg.shape, 1)
    lanef = lane.astype(F32)
    valid = (first_row + lax.broadcasted_iota(jnp.int32, (rows, 1), 0)) >= PAD_ROWS
    ninf = float('-inf')
    first = lambda hit: jnp.min(jnp.where(hit, lanef, float(LANES)), axis=1, keepdims=True)
    cm = jnp.where(lane < N_GROUPS, lg, ninf)
    cmax = jnp.max(cm, axis=1, keepdims=True)
    p_grp = 1.0 / jnp.sum(jnp.exp(cm - cmax), axis=1, keepdims=True)
    lo = N_GROUPS + EXPERTS_PER_GROUP * first(cm == cmax)
    fm = jnp.where((lanef >= lo) & (lanef < lo + EXPERTS_PER_GROUP), lg, ninf)
    v1 = jnp.max(fm, axis=1, keepdims=True)
    i1 = first(fm == v1)
    fm2 = jnp.where(lanef == i1, ninf, fm)
    v2 = jnp.max(fm2, axis=1, keepdims=True)
    i2 = first(fm2 == v2)
    t = jnp.exp(v2 - v1)
    g1 = p_grp / (1.0 + t)
    g2 = p_grp * t / (1.0 + t)
    oh1 = (lanef == i1) & valid
    oh2 = (lanef == i2) & valid
    m = jnp.where(oh1 | oh2, 1.0, 0.0)
    ri = lax.broadcasted_iota(jnp.int32, (rows, rows), 0)
    ci = lax.broadcasted_iota(jnp.int32, (rows, rows), 1)
    earlier = jnp.where(ci < ri, 1.0, 0.0).astype(BF16)
    before = jnp.dot(earlier, m.astype(BF16), preferred_element_type=F32) + cnt_ref[0:1, :]
    r1 = jnp.sum(jnp.where(oh1, before, 0.0), axis=1, keepdims=True)
    r2 = jnp.sum(jnp.where(oh2, before, 0.0), axis=1, keepdims=True)
    cnt_ref[...] = cnt_ref[...] + jnp.sum(m, axis=0, keepdims=True)
    zero = jnp.zeros_like(g1)
    rec = jnp.zeros(lg.shape, F32)
    for k, val in enumerate((i1 - N_GROUPS, i2 - N_GROUPS, jnp.where(valid, g1, zero),
                             jnp.where(valid, g2, zero), r1, r2)):
        rec = jnp.where(lane == k, val, rec)
    return rec


def _route_tile(h1, whi_ref, wlo_ref, rb_ref, rt_ref, cnt_ref):
    i = pl.program_id(0)

    @pl.when(i == 0)
    def _():
        cnt_ref[...] = jnp.zeros_like(cnt_ref)

    rt_ref[...] = _route_rows(_router_logits(h1, whi_ref, wlo_ref, rb_ref), cnt_ref, i * h1.shape[0])


def _outproj0_kernel(ys_ref, yda_ref, h_ref, w_ref, g_ref, b_ref, whi_ref, wlo_ref, rb_ref,
                     h1_ref, h1r_ref, rt_ref, cnt_ref, das_ref, *, tl):
    per = DA_WIDTH // LANES
    for b in range(NB):
        for s in range(per):
            das_ref[s, pl.ds(b, tl, stride=NB), :] = yda_ref[b, :, LANES * s:LANES * (s + 1)]
    da = jnp.concatenate([das_ref[s] for s in range(per)], axis=1).astype(BF16)
    mix = (jnp.dot(ys_ref[...], w_ref[:S5_WIDTH, :], preferred_element_type=F32)
           + jnp.dot(da, w_ref[S5_WIDTH:, :], preferred_element_type=F32))
    h1 = _layer_norm_rows(ALPHA * h_ref[...] + mix, g_ref[...], b_ref[...])
    h1_ref[...] = h1
    _to_row_tiles(h1r_ref, h1)
    _route_tile(h1, whi_ref, wlo_ref, rb_ref, rt_ref, cnt_ref)


def _outproj1_kernel(y_ref, h_ref, w_ref, g_ref, b_ref, whi_ref, wlo_ref, rb_ref,
                     h1_ref, h1r_ref, rt_ref, cnt_ref):
    mix = jnp.dot(y_ref[...], w_ref[...], preferred_element_type=F32)
    h1 = _layer_norm_rows(ALPHA * h_ref[...] + mix, g_ref[...], b_ref[...])
    h1_ref[...] = h1
    _to_row_tiles(h1r_ref, h1)
    _route_tile(h1, whi_ref, wlo_ref, rb_ref, rt_ref, cnt_ref)


def _router_weights(w_coarse, b_coarse, w_fine, b_fine):
    wf = jnp.transpose(w_fine.astype(F32), (1, 0, 2)).reshape(D_MODEL, N_EXPERTS)
    w = jnp.concatenate([w_coarse.astype(F32), wf], axis=1)
    w = jnp.pad(w, ((0, 0), (0, ROUTER_LANES - w.shape[1])))
    b = jnp.concatenate([b_coarse.astype(F32), b_fine.astype(F32).reshape(-1)])
    b = jnp.pad(b, (0, ROUTER_LANES - b.shape[0])).reshape(1, ROUTER_LANES)
    whi, wlo = _split_bf16(w)
    return whi, wlo, b


def _outproj(ys, yda, h, w_out, ln_g, ln_b, router):
    t_rows = h.shape[0]
    tm = TOK_TILE
    tl = tm // NB
    whi, wlo, rb = router
    w = w_out.astype(BF16)
    g2 = ln_g.astype(F32).reshape(1, D_MODEL)
    b2 = ln_b.astype(F32).reshape(1, D_MODEL)
    full = lambda a: pl.BlockSpec(a.shape, lambda i: (0,) * a.ndim)
    row = lambda width: pl.BlockSpec((tm, width), lambda i: (i, 0))
    common_in = [row(D_MODEL), full(w), full(g2), full(b2), full(whi), full(wlo), full(rb)]
    out_specs = [row(D_MODEL), pl.BlockSpec((tm * ROW_TILES, LANES), lambda i: (i, 0)), row(ROUTER_LANES),
                 pl.BlockSpec((NB, ROUTER_LANES), lambda i: (0, 0))]
    out_shape = [jax.ShapeDtypeStruct((t_rows, D_MODEL), F32),
                 jax.ShapeDtypeStruct((t_rows * ROW_TILES, LANES), F32),
                 jax.ShapeDtypeStruct((t_rows, ROUTER_LANES), F32),
                 jax.ShapeDtypeStruct((NB, ROUTER_LANES), F32)]
    if yda is not None:
        return pl.pallas_call(
            functools.partial(_outproj0_kernel, tl=tl),
            grid=(t_rows // tm,),
            in_specs=[row(S5_WIDTH), pl.BlockSpec((NB, tl, DA_WIDTH), lambda i: (0, i, 0))] + common_in,
            out_specs=out_specs, out_shape=out_shape,
            scratch_shapes=[pltpu.VMEM((DA_WIDTH // LANES, tm, LANES), F32)],
            compiler_params=_cparams(1),
            name="l0_outproj",
        )(ys, yda, h, w, g2, b2, whi, wlo, rb)
    return pl.pallas_call(
        _outproj1_kernel,
        grid=(t_rows // tm,),
        in_specs=[row(ys.shape[1])] + common_in,
        out_specs=out_specs, out_shape=out_shape,
        compiler_params=_cparams(1),
        name="l1_outproj",
    )(ys, h, w, g2, b2, whi, wlo, rb)


def _slots(route, counts, n_slots):
    t_rows = route.shape[0]
    experts = route[:, ROUTE_E:ROUTE_E + TOP_K_FINE].astype(jnp.int32)
    gates = route[:, ROUTE_GATE:ROUTE_GATE + TOP_K_FINE]
    rank = route[:, ROUTE_RANK:ROUTE_RANK + TOP_K_FINE].astype(jnp.int32)
    cnt = counts[0, N_GROUPS:N_GROUPS + N_EXPERTS].astype(jnp.int32)
    padded = (cnt + EXPERT_TILE - 1) // EXPERT_TILE * EXPERT_TILE
    pad_end = jnp.cumsum(padded)
    pad_start = pad_end - padded
    valid = (jnp.arange(t_rows) >= PAD_ROWS)[:, None]
    dest = jnp.where(valid, pad_start[experts] + rank, n_slots).astype(jnp.int32)
    tok = jnp.broadcast_to(jnp.arange(t_rows, dtype=jnp.int32)[:, None], dest.shape)
    slot_tok = jnp.zeros((n_slots,), jnp.int32).at[dest.reshape(-1)].set(tok.reshape(-1), mode='drop')
    nblk = n_slots // EXPERT_TILE
    block_expert = jnp.minimum(
        jnp.searchsorted(pad_end, jnp.arange(nblk, dtype=jnp.int32) * EXPERT_TILE, side='right'),
        N_EXPERTS - 1).astype(jnp.int32)
    n_used = (pad_end[-1] // EXPERT_TILE).astype(jnp.int32).reshape(1)
    pos = jnp.where(valid, dest, 0).reshape(-1)
    return slot_tok * ROW_TILES, block_expert, n_used, pos * ROW_TILES, gates


ROW_TILES = D_MODEL // LANES


def _to_row_tiles(dst_ref, val):
    rows = val.shape[0]
    for s in range(ROW_TILES):
        dst_ref[pl.ds(s, rows, stride=ROW_TILES), :] = val[:, LANES * s:LANES * (s + 1)]


def _from_row_tiles(src_ref, rows):
    return jnp.concatenate([src_ref[pl.ds(s, rows, stride=ROW_TILES), :] for s in range(ROW_TILES)], axis=1)


def _expert_kernel(be_ref, nused_ref, tok_cur_ref, tok_nxt_ref, x_hbm, wg_ref, wu_ref, wd_ref,
                   y_ref, xb0_ref, xb1_ref, sem_ref):
    i = pl.program_id(0)
    n_used = nused_ref[0]
    tb = EXPERT_TILE
    bufs = (xb0_ref, xb1_ref)

    def gather(tok_ref, p):
        for r in range(tb):
            row = pl.multiple_of(tok_ref[0, 0, r], ROW_TILES)
            pltpu.make_async_copy(x_hbm.at[pl.ds(row, ROW_TILES), :],
                                  bufs[p].at[pl.ds(ROW_TILES * r, ROW_TILES), :],
                                  sem_ref.at[p]).start(priority=r % 2)

    def wait(p):
        pltpu.make_async_copy(x_hbm.at[pl.ds(0, tb * ROW_TILES), :], bufs[p], sem_ref.at[p]).wait()

    @pl.when(i == 0)
    def _():
        gather(tok_cur_ref, 0)

    for p in range(2):
        @pl.when((lax.rem(i, 2) == p) & (i < n_used))
        def _(p=p):
            wait(p)
            gather(tok_nxt_ref, 1 - p)
            x = _from_row_tiles(bufs[p], tb).astype(BF16)
            g = jnp.dot(x, wg_ref[0].astype(BF16), preferred_element_type=F32)
            u = jnp.dot(x, wu_ref[0].astype(BF16), preferred_element_type=F32)
            hid = (g * _sigmoid(g) * u).astype(BF16)
            _to_row_tiles(y_ref, jnp.dot(hid, wd_ref[0].astype(BF16), preferred_element_type=F32))

        @pl.when((lax.rem(i, 2) == p) & (i == n_used))
        def _(p=p):
            wait(p)

    @pl.when(i >= n_used)
    def _():
        y_ref[...] = jnp.zeros_like(y_ref)


def _experts(h1r, slot_row, block_expert, n_used, w_gate, w_up, w_down):
    n_slots = slot_row.shape[0]
    nblk = n_slots // EXPERT_TILE
    tok3 = slot_row.reshape(nblk, 1, EXPERT_TILE)
    smem_blk = lambda fn: pl.BlockSpec((1, 1, EXPERT_TILE), fn, memory_space=pltpu.SMEM)
    grid_spec = pltpu.PrefetchScalarGridSpec(
        num_scalar_prefetch=2,
        grid=(nblk,),
        in_specs=[smem_blk(lambda i, be, nu: (i, 0, 0)),
                  smem_blk(lambda i, be, nu: (jnp.minimum(i + 1, nblk - 1), 0, 0)),
                  pl.BlockSpec(memory_space=pl.ANY),
                  pl.BlockSpec((1, D_MODEL, D_EXPERT), lambda i, be, nu: (be[i], 0, 0)),
                  pl.BlockSpec((1, D_MODEL, D_EXPERT), lambda i, be, nu: (be[i], 0, 0)),
                  pl.BlockSpec((1, D_EXPERT, D_MODEL), lambda i, be, nu: (be[i], 0, 0))],
        out_specs=pl.BlockSpec((EXPERT_TILE * ROW_TILES, LANES), lambda i, be, nu: (i, 0)),
        scratch_shapes=[pltpu.VMEM((EXPERT_TILE * ROW_TILES, LANES), F32),
                        pltpu.VMEM((EXPERT_TILE * ROW_TILES, LANES), F32),
                        pltpu.SemaphoreType.DMA((2,))],
    )
    return pl.pallas_call(
        _expert_kernel,
        grid_spec=grid_spec,
        out_shape=jax.ShapeDtypeStruct((n_slots * ROW_TILES, LANES), F32),
        compiler_params=_cparams(1),
        name="moe_experts",
    )(block_expert, n_used, tok3, tok3, h1r, w_gate, w_up, w_down)


def _combine_kernel(pos_cur_ref, pos_nxt_ref, yb_hbm, gate_ref, h_ref, g_ref, b_ref, o_ref,
                    ybuf_ref, sem_ref, *rest, first_tile, n_tiles, to_batch_major):
    i = pl.program_id(0)
    tm = COMBINE_TILE

    def gather(pos_ref, p):
        for r in range(tm):
            for kk in range(TOP_K_FINE):
                row = pl.multiple_of(pos_ref[0, 0, TOP_K_FINE * r + kk], ROW_TILES)
                pltpu.make_async_copy(yb_hbm.at[pl.ds(row, ROW_TILES), :],
                                      ybuf_ref.at[p, kk, pl.ds(ROW_TILES * r, ROW_TILES), :],
                                      sem_ref.at[p]).start(priority=kk)

    def finish(p, look_ahead):
        for kk in range(TOP_K_FINE):
            pltpu.make_async_copy(yb_hbm.at[pl.ds(0, tm * ROW_TILES), :], ybuf_ref.at[p, kk],
                                  sem_ref.at[p]).wait()
        if look_ahead:
            gather(pos_nxt_ref, 1 - p)
        gts = gate_ref[...]
        moe = (_from_row_tiles(ybuf_ref.at[p, 0], tm) * gts[:, 0:1]
               + _from_row_tiles(ybuf_ref.at[p, 1], tm) * gts[:, 1:2])
        h2 = _layer_norm_rows(ALPHA * h_ref[...] + moe, g_ref[...], b_ref[...])
        if not to_batch_major:
            o_ref[...] = h2
        else:
            hs_ref = rest[0]
            tl = tm // NB
            for s in range(D_MODEL // LANES):
                hs_ref[s] = h2[:, LANES * s:LANES * (s + 1)]
            for b in range(NB):
                for s in range(D_MODEL // LANES):
                    o_ref[b, :, LANES * s:LANES * (s + 1)] = hs_ref[s, pl.ds(b, tl, stride=NB), :]

    @pl.when(i == 0)
    def _():
        gather(pos_cur_ref, 0)

    for p in range(2):
        @pl.when((lax.rem(i, 2) == p) & (i < n_tiles - 1))
        def _(p=p):
            finish(p, True)

    @pl.when(i == n_tiles - 1)
    def _():
        finish((n_tiles - 1) % 2, False)


def _combine(yb, pos, gates, h1, ln_g, ln_b, *, final):
    t_rows = h1.shape[0]
    tm = COMBINE_TILE
    first_tile = (LANES * NB) // tm if final else 0
    n_tiles = t_rows // tm - first_tile
    pos3 = pos.reshape(t_rows // tm, 1, TOP_K_FINE * tm)
    g2 = ln_g.astype(F32).reshape(1, D_MODEL)
    b2 = ln_b.astype(F32).reshape(1, D_MODEL)
    full = lambda a: pl.BlockSpec(a.shape, lambda i: (0,) * a.ndim)
    smem_blk = lambda fn: pl.BlockSpec((1, 1, TOP_K_FINE * tm), fn, memory_space=pltpu.SMEM)
    last = t_rows // tm - 1
    in_specs = [smem_blk(lambda i: (i + first_tile, 0, 0)),
                smem_blk(lambda i: (jnp.minimum(i + first_tile + 1, last), 0, 0)),
                pl.BlockSpec(memory_space=pl.ANY),
                pl.BlockSpec((tm, TOP_K_FINE), lambda i: (i + first_tile, 0)),
                pl.BlockSpec((tm, D_MODEL), lambda i: (i + first_tile, 0)),
                full(g2), full(b2)]
    scratch = [pltpu.VMEM((2, TOP_K_FINE, tm * ROW_TILES, LANES), F32), pltpu.SemaphoreType.DMA((2,))]
    if final:
        tl = tm // NB
        seq = t_rows // NB - LANES
        out_specs = pl.BlockSpec((NB, tl, D_MODEL), lambda i: (0, i, 0))
        out_shape = jax.ShapeDtypeStruct((NB, seq, D_MODEL), F32)
        scratch.append(pltpu.VMEM((D_MODEL // LANES, tm, LANES), F32))
    else:
        out_specs = pl.BlockSpec((tm, D_MODEL), lambda i: (i, 0))
        out_shape = jax.ShapeDtypeStruct((t_rows, D_MODEL), F32)
    return pl.pallas_call(
        functools.partial(_combine_kernel, first_tile=first_tile, n_tiles=n_tiles, to_batch_major=final),
        grid=(n_tiles,),
        in_specs=in_specs, out_specs=out_specs, out_shape=out_shape,
        scratch_shapes=scratch,
        compiler_params=_cparams(1),
        name="moe_combine_final" if final else "moe_combine",
    )(pos3, pos3, yb, gates, h1, g2, b2)


def _moe_block(h1, h1r, route, counts, w_gate, w_up, w_down, ln_g, ln_b, *, final):
    t_rows = h1.shape[0]
    n_assign = (t_rows - PAD_ROWS) * TOP_K_FINE
    nblk = -(-(n_assign + N_EXPERTS * (EXPERT_TILE - 1)) // EXPERT_TILE) + 1
    n_slots = nblk * EXPERT_TILE
    slot_row, block_expert, n_used, pos, gates = _slots(route, counts, n_slots)
    yb = _experts(h1r, slot_row, block_expert, n_used, w_gate, w_up, w_down)
    return _combine(yb, pos, gates, h1, ln_g, ln_b, final=final)


def _inproj1_kernel(h_ref, w_ref, gg_ref, rec_ref):
    x = h_ref[...].astype(BF16)
    z = jnp.dot(x, w_ref[...], preferred_element_type=F32)
    gg_ref[...] = _gelu(z[:, :D_RNN]).astype(BF16)
    rec_ref[...] = z[:, D_RNN:]


def _inproj1(h, w_bf16):
    t_rows = h.shape[0]
    tm = TOK_TILE
    row = lambda width: pl.BlockSpec((tm, width), lambda i: (i, 0))
    return pl.pallas_call(
        _inproj1_kernel,
        grid=(t_rows // tm,),
        in_specs=[row(D_MODEL), pl.BlockSpec(w_bf16.shape, lambda i: (0, 0))],
        out_specs=[row(D_RNN), row(D_RNN)],
        out_shape=[jax.ShapeDtypeStruct((t_rows, D_RNN), BF16), jax.ShapeDtypeStruct((t_rows, D_RNN), F32)],
        compiler_params=_cparams(1),
        name="l1_inproj",
    )(h, w_bf16)


CONV_HALO = (CONV_WIDTH - 1) * NB


def _rglru_kernel(rec_ref, gg_ref, cw_ref, cb_ref, wax_ref, ba_ref, bx_ref, sp_ref,
                  y_ref, rp_ref, a_ref, b_ref, st_ref, *, steps):
    i = pl.program_id(0)
    rows = steps * NB

    @pl.when(i == 0)
    def _():
        st_ref[...] = jnp.zeros_like(st_ref)
        rp_ref[0:CONV_HALO, :] = jnp.zeros((CONV_HALO, D_RNN), F32)

    row = i * rows + lax.broadcasted_iota(jnp.int32, (rows, 1), 0)
    real = row >= PAD_ROWS
    rp_ref[CONV_HALO:CONV_HALO + rows, :] = jnp.where(real, rec_ref[...], 0.0)
    xc = cb_ref[...] + sum(rp_ref[NB * j:NB * j + rows, :] * cw_ref[j:j + 1, :] for j in range(CONV_WIDTH))
    tail = rp_ref[rows:rows + CONV_HALO, :]
    rp_ref[0:CONV_HALO, :] = tail

    for n in range(LRU_BLOCKS):
        cs = slice(LRU_BLOCK_W * n, LRU_BLOCK_W * (n + 1))
        xb = xc[:, cs]
        ra = jnp.dot(xb.astype(BF16), wax_ref[n], preferred_element_type=F32)
        r = _sigmoid(ra[:, :LRU_BLOCK_W] + ba_ref[:, cs])
        ig = _sigmoid(ra[:, LRU_BLOCK_W:] + bx_ref[:, cs])
        log_a = -LRU_C * r * sp_ref[:, cs]
        a = jnp.exp(log_a)
        bt = jnp.sqrt(1.0 - a * a) * (ig * xb)
        a_ref[:, cs] = a
        b_ref[:, cs] = jnp.where(real, bt, 0.0)

    def step(t, h):
        r0 = pl.multiple_of(t * NB, NB)
        hn = a_ref[pl.ds(r0, NB), :] * h + b_ref[pl.ds(r0, NB), :]
        b_ref[pl.ds(r0, NB), :] = hn
        return hn

    st_ref[...] = lax.fori_loop(0, steps, step, st_ref[...])
    y_ref[...] = (gg_ref[...].astype(F32) * b_ref[...]).astype(BF16)


def _rglru(rec, gg, conv_w, conv_b, w_a, b_a, w_x, b_x, lru_lambda):
    t_rows = rec.shape[0]
    steps = SCAN_STEPS
    rows = steps * NB
    cw = conv_w.astype(F32)
    cb = conv_b.astype(F32).reshape(1, D_RNN)
    wax = jnp.concatenate([w_a.astype(F32), w_x.astype(F32)], axis=2).astype(BF16)
    ba = b_a.astype(F32).reshape(1, D_RNN)
    bx = b_x.astype(F32).reshape(1, D_RNN)
    sp = jax.nn.softplus(-lru_lambda.astype(F32)).reshape(1, D_RNN)
    full = lambda a: pl.BlockSpec(a.shape, lambda i: (0,) * a.ndim)
    row = pl.BlockSpec((rows, D_RNN), lambda i: (i, 0))
    return pl.pallas_call(
        functools.partial(_rglru_kernel, steps=steps),
        grid=(t_rows // rows,),
        in_specs=[row, row, full(cw), full(cb), full(wax), full(ba), full(bx), full(sp)],
        out_specs=row,
        out_shape=jax.ShapeDtypeStruct((t_rows, D_RNN), BF16),
        scratch_shapes=[pltpu.VMEM((rows + CONV_HALO, D_RNN), F32), pltpu.VMEM((rows, D_RNN), F32),
                        pltpu.VMEM((rows, D_RNN), F32), pltpu.VMEM((NB, D_RNN), F32)],
        compiler_params=_cparams(1),
        name="l1_rglru",
    )(rec, gg, cw, cb, wax, ba, bx, sp)


def kernel(x, meta, l0_ln1_g, l0_ln1_b, l0_w_in, l0_s5_lambda_re, l0_s5_lambda_im, l0_s5_log_dt, l0_s5_b_re, l0_s5_b_im, l0_s5_c_re, l0_s5_c_im, l0_s5_d, l0_s5_w_glu, l0_s5_b_glu, l0_da_lq1, l0_da_lk1, l0_da_lq2, l0_da_lk2, l0_da_subln_g, l0_w_out, l0_ln2_g, l0_ln2_b, l0_moe_w_coarse, l0_moe_b_coarse, l0_moe_w_fine, l0_moe_b_fine, l0_moe_w_gate, l0_moe_w_up, l0_moe_w_down, l1_ln1_g, l1_ln1_b, l1_w_in, l1_conv_w, l1_conv_b, l1_lru_w_a, l1_lru_b_a, l1_lru_w_x, l1_lru_b_x, l1_lru_lambda, l1_w_out, l1_ln2_g, l1_ln2_b, l1_moe_w_coarse, l1_moe_b_coarse, l1_moe_w_fine, l1_moe_b_fine, l1_moe_w_gate, l1_moe_w_up, l1_moe_w_down):
    bsz, seq, _ = x.shape
    assert bsz == NB and seq % Q_TILE == 0
    dt = x.dtype
    lp = FRONT_PAD + N_META + seq
    h = jnp.concatenate([
        jnp.zeros((FRONT_PAD, NB, D_MODEL), dt),
        jnp.broadcast_to(meta.astype(dt)[:, None, :], (N_META, NB, D_MODEL)),
        jnp.transpose(x, (1, 0, 2))], axis=0).reshape(lp * NB, D_MODEL)

    col_scale = jnp.concatenate([jnp.ones((S5_WIDTH,), F32),
                                 jnp.full((DA_WIDTH,), DA_HEAD_DIM ** -0.5, F32),
                                 jnp.ones((2 * DA_WIDTH,), F32)])
    w_in0 = (l0_w_in.astype(F32) * col_scale[None, :]).astype(BF16)
    u, q, k, v = _inproj0(h, w_in0, lp)
    s5p = _s5_params(l0_s5_lambda_re, l0_s5_lambda_im, l0_s5_log_dt,
                     l0_s5_b_re, l0_s5_b_im, l0_s5_c_re, l0_s5_c_im)
    y_s5 = _s5(u, *s5p, l0_s5_d, l0_s5_w_glu, l0_s5_b_glu)
    lam_init = 0.8 - 0.6 * math.exp(-0.3 * 0)
    lam = (jnp.exp(jnp.sum(l0_da_lq1.astype(F32) * l0_da_lk1.astype(F32)))
           - jnp.exp(jnp.sum(l0_da_lq2.astype(F32) * l0_da_lk2.astype(F32))) + lam_init)
    y_da = _diffattn(q, k, v, lam, l0_da_subln_g, lam_init)
    router0 = _router_weights(l0_moe_w_coarse, l0_moe_b_coarse, l0_moe_w_fine, l0_moe_b_fine)
    h, hr, route, counts = _outproj(y_s5, y_da, h, l0_w_out, l0_ln1_g, l0_ln1_b, router0)
    h = _moe_block(h, hr, route, counts, l0_moe_w_gate, l0_moe_w_up, l0_moe_w_down, l0_ln2_g, l0_ln2_b, final=False)

    gg, rec = _inproj1(h, l1_w_in.astype(BF16))
    y = _rglru(rec, gg, l1_conv_w, l1_conv_b, l1_lru_w_a, l1_lru_b_a, l1_lru_w_x, l1_lru_b_x, l1_lru_lambda)
    router1 = _router_weights(l1_moe_w_coarse, l1_moe_b_coarse, l1_moe_w_fine, l1_moe_b_fine)
    h, hr, route, counts = _outproj(y, None, h, l1_w_out, l1_ln1_g, l1_ln1_b, router1)
    out = _moe_block(h, hr, route, counts, l1_moe_w_gate, l1_moe_w_up, l1_moe_w_down, l1_ln2_g, l1_ln2_b, final=True)
    return out.astype(dt)
```

```python
import functools
import math

import jax
import jax.numpy as jnp
from jax import lax
from jax.experimental import pallas as pl
from jax.experimental.pallas import tpu as pltpu

F32 = jnp.float32
BF16 = jnp.bfloat16

D_MODEL = 1024
DEPTH = 2
CHUNK = 64
N_META = 16
S5_WIDTH = 512
S5_GROUP = 16
S5_GROUPS = 32
S5_STATE = 64
DA_HEADS = 4
DA_HEAD_DIM = 64
DA_WIDTH = 512
D_RNN = 1280
LRU_BLOCKS = 10
LRU_BLOCK_W = 128
CONV_WIDTH = 4
LRU_C = 8.0
N_GROUPS = 4
EXPERTS_PER_GROUP = 8
N_EXPERTS = 32
TOP_K_FINE = 2
D_EXPERT = 512
ALPHA = (2 * DEPTH) ** 0.25
LN_EPS = 1e-5
NEG_INF = -1e30

NB = 8
LANES = 128
FRONT_PAD = LANES - N_META
PAD_ROWS = FRONT_PAD * NB
Q_TILE = 256
K_TILE = 512
TOK_TILE = 512
SCAN_STEPS = 32
EXPERT_TILE = 256
COMBINE_TILE = 256
ROUTER_LANES = 128
VMEM_LIMIT = 48 * 1024 * 1024


def _cparams(n_axes, vmem=VMEM_LIMIT):
    return pltpu.CompilerParams(dimension_semantics=("arbitrary",) * n_axes,
                                vmem_limit_bytes=vmem)


def _gelu(x):
    return 0.5 * x * (1.0 + jnp.tanh(math.sqrt(2.0 / math.pi) * (x + 0.044715 * (x * x * x))))


def _sigmoid(x):
    return 1.0 / (1.0 + jnp.exp(-x))


def _layer_norm_rows(r, g, b):
    mu = jnp.mean(r, axis=-1, keepdims=True)
    c = r - mu
    var = jnp.mean(c * c, axis=-1, keepdims=True)
    return c * lax.rsqrt(var + LN_EPS) * g + b


def _split_bf16(w):
    hi = w.astype(BF16)
    lo = (w - hi.astype(F32)).astype(BF16)
    return hi, lo


def _inproj0_kernel(h_ref, w_ref, u_ref, q_ref, k_ref, v_ref, zs_ref, *, tl):
    x = h_ref[...].astype(BF16)
    z = jnp.dot(x, w_ref[...], preferred_element_type=F32)
    u_ref[...] = z[:, :S5_WIDTH]
    n_slab = 3 * DA_WIDTH // LANES
    for s in range(n_slab):
        zs_ref[s] = z[:, S5_WIDTH + LANES * s:S5_WIDTH + LANES * (s + 1)]
    dsts = (q_ref, k_ref, v_ref)
    per = DA_WIDTH // LANES
    for b in range(NB):
        for s in range(n_slab):
            c = (s % per) * LANES
            dsts[s // per][b, :, c:c + LANES] = zs_ref[s, pl.ds(b, tl, stride=NB), :].astype(BF16)


def _inproj0(h, w_bf16, lp):
    t_rows = h.shape[0]
    tm = TOK_TILE
    tl = tm // NB
    qkv_shape = jax.ShapeDtypeStruct((NB, lp, DA_WIDTH), BF16)
    qkv_spec = pl.BlockSpec((NB, tl, DA_WIDTH), lambda i: (0, i, 0))
    return pl.pallas_call(
        functools.partial(_inproj0_kernel, tl=tl),
        grid=(t_rows // tm,),
        in_specs=[pl.BlockSpec((tm, D_MODEL), lambda i: (i, 0)),
                  pl.BlockSpec(w_bf16.shape, lambda i: (0, 0))],
        out_specs=[pl.BlockSpec((tm, S5_WIDTH), lambda i: (i, 0)), qkv_spec, qkv_spec, qkv_spec],
        out_shape=[jax.ShapeDtypeStruct((t_rows, S5_WIDTH), F32), qkv_shape, qkv_shape, qkv_shape],
        scratch_shapes=[pltpu.VMEM((3 * DA_WIDTH // LANES, tm, LANES), F32)],
        compiler_params=_cparams(1),
        name="l0_inproj",
    )(h, w_bf16)


S5_SLABS = S5_WIDTH // LANES
S5_SLAB_STATE = (S5_GROUPS // S5_SLABS) * S5_STATE
S5_NSTATE = S5_GROUPS * S5_STATE


def _s5_kernel(u_ref, bmat_ref, are_ref, aim_ref, cre_ref, cim_ref, d_ref, wglu_ref, bglu_ref,
               y_ref, hre_ref, him_ref, st_ref, *, steps):
    i = pl.program_id(0)
    rows = steps * NB

    @pl.when(i == 0)
    def _():
        st_ref[...] = jnp.zeros_like(st_ref)

    row = i * rows + lax.broadcasted_iota(jnp.int32, (rows, 1), 0)
    u = jnp.where(row >= PAD_ROWS, u_ref[...], 0.0)
    ub = u.astype(BF16)
    for s in range(S5_SLABS):
        bu = jnp.dot(ub[:, LANES * s:LANES * (s + 1)], bmat_ref[s], preferred_element_type=F32)
        hre_ref[:, S5_SLAB_STATE * s:S5_SLAB_STATE * (s + 1)] = bu[:, :S5_SLAB_STATE]
        him_ref[:, S5_SLAB_STATE * s:S5_SLAB_STATE * (s + 1)] = bu[:, S5_SLAB_STATE:]

    cw = 512
    for cg in range(S5_NSTATE // cw):
        c0 = cg * cw
        a_r = are_ref[:, c0:c0 + cw]
        a_i = aim_ref[:, c0:c0 + cw]

        def step(t, carry, c0=c0, a_r=a_r, a_i=a_i):
            sr, si = carry
            r0 = pl.multiple_of(t * NB, NB)
            br = hre_ref[pl.ds(r0, NB), c0:c0 + cw]
            bi = him_ref[pl.ds(r0, NB), c0:c0 + cw]
            nr = a_r * sr - a_i * si + br
            ni = a_r * si + a_i * sr + bi
            hre_ref[pl.ds(r0, NB), c0:c0 + cw] = nr
            him_ref[pl.ds(r0, NB), c0:c0 + cw] = ni
            return nr, ni

        sr, si = lax.fori_loop(0, steps, step, (st_ref[0, :, c0:c0 + cw], st_ref[1, :, c0:c0 + cw]))
        st_ref[0, :, c0:c0 + cw] = sr
        st_ref[1, :, c0:c0 + cw] = si

    ys = []
    for s in range(S5_SLABS):
        hr = hre_ref[:, S5_SLAB_STATE * s:S5_SLAB_STATE * (s + 1)].astype(BF16)
        hi = him_ref[:, S5_SLAB_STATE * s:S5_SLAB_STATE * (s + 1)].astype(BF16)
        ys.append(jnp.dot(hr, cre_ref[s], preferred_element_type=F32)
                  + jnp.dot(hi, cim_ref[s], preferred_element_type=F32))
    y = jnp.concatenate(ys, axis=1) + d_ref[...] * u
    y = _gelu(y)
    gate = _sigmoid(jnp.dot(y.astype(BF16), wglu_ref[...], preferred_element_type=F32) + bglu_ref[...])
    y_ref[...] = (y * gate).astype(BF16)


def _s5_params(lam_re, lam_im, log_dt, b_re, b_im, c_re, c_im):
    dt = jnp.exp(log_dt.astype(F32))[:, None]
    lr = jnp.minimum(lam_re.astype(F32), -1e-4)
    li = lam_im.astype(F32)
    mag = jnp.exp(lr * dt)
    ar = mag * jnp.cos(li * dt)
    ai = mag * jnp.sin(li * dt)
    den = lr * lr + li * li
    nr, ni = ar - 1.0, ai
    fr = ((nr * lr + ni * li) / den)[..., None]
    fi = ((ni * lr - nr * li) / den)[..., None]
    br, bi = b_re.astype(F32), b_im.astype(F32)
    bbr = fr * br - fi * bi
    bbi = fr * bi + fi * br
    gps = S5_GROUPS // S5_SLABS
    eye = jnp.eye(gps, dtype=F32)

    def in_slab(m):
        m4 = m.reshape(S5_SLABS, gps, S5_STATE, S5_GROUP)
        return jnp.einsum('sgph,gk->sghkp', m4, eye).reshape(S5_SLABS, LANES, S5_SLAB_STATE)

    def out_slab(m):
        m4 = m.reshape(S5_SLABS, gps, S5_GROUP, S5_STATE)
        return jnp.einsum('sghp,gk->sgpkh', m4, eye).reshape(S5_SLABS, S5_SLAB_STATE, LANES)

    bmat = jnp.concatenate([in_slab(bbr), in_slab(bbi)], axis=2).astype(BF16)
    cre = out_slab(c_re.astype(F32)).astype(BF16)
    cim = out_slab(-c_im.astype(F32)).astype(BF16)
    a_re = jnp.broadcast_to(ar.reshape(1, S5_NSTATE), (NB, S5_NSTATE))
    a_im = jnp.broadcast_to(ai.reshape(1, S5_NSTATE), (NB, S5_NSTATE))
    return bmat, a_re, a_im, cre, cim


def _s5(u, bmat, a_re, a_im, cre, cim, d, w_glu, b_glu):
    t_rows = u.shape[0]
    steps = SCAN_STEPS
    rows = steps * NB
    full = lambda a: pl.BlockSpec(a.shape, lambda i: (0,) * a.ndim)
    d2 = d.astype(F32).reshape(1, S5_WIDTH)
    bg2 = b_glu.astype(F32).reshape(1, S5_WIDTH)
    wg = w_glu.astype(BF16)
    return pl.pallas_call(
        functools.partial(_s5_kernel, steps=steps),
        grid=(t_rows // rows,),
        in_specs=[pl.BlockSpec((rows, S5_WIDTH), lambda i: (i, 0)),
                  full(bmat), full(a_re), full(a_im), full(cre), full(cim), full(d2), full(wg), full(bg2)],
        out_specs=pl.BlockSpec((rows, S5_WIDTH), lambda i: (i, 0)),
        out_shape=jax.ShapeDtypeStruct((t_rows, S5_WIDTH), BF16),
        scratch_shapes=[pltpu.VMEM((rows, S5_NSTATE), F32), pltpu.VMEM((rows, S5_NSTATE), F32),
                        pltpu.VMEM((2, NB, S5_NSTATE), F32)],
        compiler_params=_cparams(1),
        name="l0_s5",
    )(u, bmat, a_re, a_im, cre, cim, d2, wg, bg2)


def _diffattn_kernel(lam_ref, q_ref, k_ref, v_ref, g_ref, o_ref,
                     s_ref, q1_ref, q2_ref, m_ref, l_ref, a_ref, *, lam_init, lp):
    lam = lam_ref[0]
    tq, tk = Q_TILE, K_TILE
    lane = lax.broadcasted_iota(jnp.int32, (tq, LANES), 1)
    qrow = lax.broadcasted_iota(jnp.int32, (tq, tk), 0)
    kloc = lax.broadcasted_iota(jnp.int32, (tq, tk), 1)
    nt = (((1,), (1,)), ((), ()))
    qs = (q1_ref, q2_ref)

    def key_start(j):
        return pl.multiple_of(jnp.minimum(j * tk, lp - tk), LANES)

    def lane_fold(x, op):
        r = x[:, :LANES]
        for c in range(1, x.shape[1] // LANES):
            r = op(r, x[:, LANES * c:LANES * (c + 1)])
        return r

    def q_tile(i, _):
        q0 = pl.multiple_of(jnp.minimum(i * tq, lp - tq), LANES)
        q = q_ref[0, pl.ds(q0, tq), :]
        zero = jnp.zeros_like(q)
        q1_ref[...] = jnp.where(lane < DA_HEAD_DIM, q, zero)
        q2_ref[...] = jnp.where(lane >= DA_HEAD_DIM, q, zero)
        qblk = jnp.right_shift(q0 + qrow, 6)
        n_kt = (q0 + tq + tk - 1) // tk
        m_ref[...] = jnp.full(m_ref.shape, NEG_INF, F32)
        l_ref[...] = jnp.zeros(l_ref.shape, F32)
        a_ref[...] = jnp.zeros(a_ref.shape, F32)

        def scores(j, _, masked):
            k0 = key_start(j)
            kt = k_ref[0, pl.ds(k0, tk), :]
            if masked:
                kpos = k0 + kloc
                mask = (kpos >= jnp.maximum(FRONT_PAD, j * tk)) & (jnp.right_shift(kpos, 6) <= qblk)
            for h in range(2):
                s = lax.dot_general(qs[h][...], kt, nt, preferred_element_type=F32)
                if masked:
                    s = jnp.where(mask, s, NEG_INF)
                s_ref[h, j] = s
                m_ref[h] = jnp.maximum(m_ref[h], lane_fold(s, jnp.maximum))
            return 0

        n_full = q0 // tk
        lax.fori_loop(0, jnp.minimum(1, n_full), functools.partial(scores, masked=True), 0)
        lax.fori_loop(1, n_full, functools.partial(scores, masked=False), 0)
        lax.fori_loop(n_full, n_kt, functools.partial(scores, masked=True), 0)
        m = [jnp.max(m_ref[h], axis=-1, keepdims=True) for h in range(2)]

        def values(j, _):
            vt = v_ref[0, pl.ds(key_start(j), tk), :]
            for h in range(2):
                p = jnp.exp(s_ref[h, j] - m[h])
                l_ref[h] += lane_fold(p, jnp.add)
                a_ref[h] += jnp.dot(p.astype(BF16), vt, preferred_element_type=F32)
            return 0

        lax.fori_loop(0, n_kt, values, 0)
        l1 = jnp.sum(l_ref[0], axis=-1, keepdims=True)
        l2 = jnp.sum(l_ref[1], axis=-1, keepdims=True)
        o = a_ref[0] / l1 - lam * (a_ref[1] / l2)
        o = o * lax.rsqrt(jnp.mean(o * o, axis=-1, keepdims=True) + LN_EPS) * g_ref[...]
        o_ref[0, pl.ds(q0, tq), :] = o * (1.0 - lam_init)
        return 0

    lax.fori_loop(0, pl.cdiv(lp, tq), q_tile, 0)


def _diffattn(q, k, v, lam, subln_g, lam_init):
    nb, lp, _ = q.shape
    g2 = subln_g.astype(F32).reshape(1, 2 * DA_HEAD_DIM)
    seq_spec = pl.BlockSpec((1, lp, LANES), lambda b, h: (b, 0, h))
    return pl.pallas_call(
        functools.partial(_diffattn_kernel, lam_init=lam_init, lp=lp),
        grid=(nb, DA_HEADS),
        in_specs=[pl.BlockSpec(memory_space=pltpu.SMEM), seq_spec, seq_spec, seq_spec,
                  pl.BlockSpec((1, LANES), lambda b, h: (0, 0))],
        out_specs=seq_spec,
        out_shape=jax.ShapeDtypeStruct((nb, lp, DA_WIDTH), F32),
        scratch_shapes=[pltpu.VMEM((2, pl.cdiv(lp, K_TILE), Q_TILE, K_TILE), F32),
                        pltpu.VMEM((Q_TILE, LANES), BF16), pltpu.VMEM((Q_TILE, LANES), BF16),
                        pltpu.VMEM((2, Q_TILE, LANES), F32), pltpu.VMEM((2, Q_TILE, LANES), F32),
                        pltpu.VMEM((2, Q_TILE, LANES), F32)],
        compiler_params=_cparams(2),
        name="l0_diffattn",
    )(lam.reshape(1), q, k, v, g2)


def _router_logits(h1, whi_ref, wlo_ref, rb_ref):
    hi = h1.astype(BF16)
    lo = (h1 - hi.astype(F32)).astype(BF16)
    return (jnp.dot(hi, whi_ref[...], preferred_element_type=F32)
            + jnp.dot(lo, whi_ref[...], preferred_element_type=F32)
            + jnp.dot(hi, wlo_ref[...], preferred_element_type=F32) + rb_ref[...])


ROUTE_E, ROUTE_GATE, ROUTE_RANK = 0, 2, 4


def _route_rows(lg, cnt_ref, first_row):
    rows = lg.shape[0]
    lane = lax.broadcasted_iota(jnp.int32, lg.shape, 1)
    lanef = lane.astype(F32)
    valid = (first_row + lax.broadcasted_iota(jnp.int32, (rows, 1), 0)) >= PAD_ROWS
    ninf = float('-inf')
    first = lambda hit: jnp.min(jnp.where(hit, lanef, float(LANES)), axis=1, keepdims=True)
    cm = jnp.where(lane < N_GROUPS, lg, ninf)
    cmax = jnp.max(cm, axis=1, keepdims=True)
    p_grp = 1.0 / jnp.sum(jnp.exp(cm - cmax), axis=1, keepdims=True)
    lo = N_GROUPS + EXPERTS_PER_GROUP * first(cm == cmax)
    fm = jnp.where((lanef >= lo) & (lanef < lo + EXPERTS_PER_GROUP), lg, ninf)
    v1 = jnp.max(fm, axis=1, keepdims=True)
    i1 = first(fm == v1)
    fm2 = jnp.where(lanef == i1, ninf, fm)
    v2 = jnp.max(fm2, axis=1, keepdims=True)
    i2 = first(fm2 == v2)
    t = jnp.exp(v2 - v1)
    g1 = p_grp / (1.0 + t)
    g2 = p_grp * t / (1.0 + t)
    oh1 = (lanef == i1) & valid
    oh2 = (lanef == i2) & valid
    m = jnp.where(oh1 | oh2, 1.0, 0.0)
    ri = lax.broadcasted_iota(jnp.int32, (rows, rows), 0)
    ci = lax.broadcasted_iota(jnp.int32, (rows, rows), 1)
    earlier = jnp.where(ci < ri, 1.0, 0.0).astype(BF16)
    before = jnp.dot(earlier, m.astype(BF16), preferred_element_type=F32) + cnt_ref[0:1, :]
    r1 = jnp.sum(jnp.where(oh1, before, 0.0), axis=1, keepdims=True)
    r2 = jnp.sum(jnp.where(oh2, before, 0.0), axis=1, keepdims=True)
    cnt_ref[...] = cnt_ref[...] + jnp.sum(m, axis=0, keepdims=True)
    zero = jnp.zeros_like(g1)
    rec = jnp.zeros(lg.shape, F32)
    for k, val in enumerate((i1 - N_GROUPS, i2 - N_GROUPS, jnp.where(valid, g1, zero),
                             jnp.where(valid, g2, zero), r1, r2)):
        rec = jnp.where(lane == k, val, rec)
    return rec


def _route_tile(h1, whi_ref, wlo_ref, rb_ref, rt_ref, cnt_ref):
    i = pl.program_id(0)

    @pl.when(i == 0)
    def _():
        cnt_ref[...] = jnp.zeros_like(cnt_ref)

    rt_ref[...] = _route_rows(_router_logits(h1, whi_ref, wlo_ref, rb_ref), cnt_ref, i * h1.shape[0])


def _outproj0_kernel(ys_ref, yda_ref, h_ref, w_ref, g_ref, b_ref, whi_ref, wlo_ref, rb_ref,
                     h1_ref, h1r_ref, rt_ref, cnt_ref, das_ref, *, tl):
    per = DA_WIDTH // LANES
    for b in range(NB):
        for s in range(per):
            das_ref[s, pl.ds(b, tl, stride=NB), :] = yda_ref[b, :, LANES * s:LANES * (s + 1)]
    da = jnp.concatenate([das_ref[s] for s in range(per)], axis=1).astype(BF16)
    mix = (jnp.dot(ys_ref[...], w_ref[:S5_WIDTH, :], preferred_element_type=F32)
           + jnp.dot(da, w_ref[S5_WIDTH:, :], preferred_element_type=F32))
    h1 = _layer_norm_rows(ALPHA * h_ref[...] + mix, g_ref[...], b_ref[...])
    h1_ref[...] = h1
    _to_row_tiles(h1r_ref, h1)
    _route_tile(h1, whi_ref, wlo_ref, rb_ref, rt_ref, cnt_ref)


def _outproj1_kernel(y_ref, h_ref, w_ref, g_ref, b_ref, whi_ref, wlo_ref, rb_ref,
                     h1_ref, h1r_ref, rt_ref, cnt_ref):
    mix = jnp.dot(y_ref[...], w_ref[...], preferred_element_type=F32)
    h1 = _layer_norm_rows(ALPHA * h_ref[...] + mix, g_ref[...], b_ref[...])
    h1_ref[...] = h1
    _to_row_tiles(h1r_ref, h1)
    _route_tile(h1, whi_ref, wlo_ref, rb_ref, rt_ref, cnt_ref)


def _router_weights(w_coarse, b_coarse, w_fine, b_fine):
    wf = jnp.transpose(w_fine.astype(F32), (1, 0, 2)).reshape(D_MODEL, N_EXPERTS)
    w = jnp.concatenate([w_coarse.astype(F32), wf], axis=1)
    w = jnp.pad(w, ((0, 0), (0, ROUTER_LANES - w.shape[1])))
    b = jnp.concatenate([b_coarse.astype(F32), b_fine.astype(F32).reshape(-1)])
    b = jnp.pad(b, (0, ROUTER_LANES - b.shape[0])).reshape(1, ROUTER_LANES)
    whi, wlo = _split_bf16(w)
    return whi, wlo, b


def _outproj(ys, yda, h, w_out, ln_g, ln_b, router):
    t_rows = h.shape[0]
    tm = TOK_TILE
    tl = tm // NB
    whi, wlo, rb = router
    w = w_out.astype(BF16)
    g2 = ln_g.astype(F32).reshape(1, D_MODEL)
    b2 = ln_b.astype(F32).reshape(1, D_MODEL)
    full = lambda a: pl.BlockSpec(a.shape, lambda i: (0,) * a.ndim)
    row = lambda width: pl.BlockSpec((tm, width), lambda i: (i, 0))
    common_in = [row(D_MODEL), full(w), full(g2), full(b2), full(whi), full(wlo), full(rb)]
    out_specs = [row(D_MODEL), pl.BlockSpec((tm * ROW_TILES, LANES), lambda i: (i, 0)), row(ROUTER_LANES),
                 pl.BlockSpec((NB, ROUTER_LANES), lambda i: (0, 0))]
    out_shape = [jax.ShapeDtypeStruct((t_rows, D_MODEL), F32),
                 jax.ShapeDtypeStruct((t_rows * ROW_TILES, LANES), F32),
                 jax.ShapeDtypeStruct((t_rows, ROUTER_LANES), F32),
                 jax.ShapeDtypeStruct((NB, ROUTER_LANES), F32)]
    if yda is not None:
        return pl.pallas_call(
            functools.partial(_outproj0_kernel, tl=tl),
            grid=(t_rows // tm,),
            in_specs=[row(S5_WIDTH), pl.BlockSpec((NB, tl, DA_WIDTH), lambda i: (0, i, 0))] + common_in,
            out_specs=out_specs, out_shape=out_shape,
            scratch_shapes=[pltpu.VMEM((DA_WIDTH // LANES, tm, LANES), F32)],
            compiler_params=_cparams(1),
            name="l0_outproj",
        )(ys, yda, h, w, g2, b2, whi, wlo, rb)
    return pl.pallas_call(
        _outproj1_kernel,
        grid=(t_rows // tm,),
        in_specs=[row(ys.shape[1])] + common_in,
        out_specs=out_specs, out_shape=out_shape,
        compiler_params=_cparams(1),
        name="l1_outproj",
    )(ys, h, w, g2, b2, whi, wlo, rb)


def _slots(route, counts, n_slots):
    t_rows = route.shape[0]
    experts = route[:, ROUTE_E:ROUTE_E + TOP_K_FINE].astype(jnp.int32)
    gates = route[:, ROUTE_GATE:ROUTE_GATE + TOP_K_FINE]
    rank = route[:, ROUTE_RANK:ROUTE_RANK + TOP_K_FINE].astype(jnp.int32)
    cnt = counts[0, N_GROUPS:N_GROUPS + N_EXPERTS].astype(jnp.int32)
    padded = (cnt + EXPERT_TILE - 1) // EXPERT_TILE * EXPERT_TILE
    pad_end = jnp.cumsum(padded)
    pad_start = pad_end - padded
    valid = (jnp.arange(t_rows) >= PAD_ROWS)[:, None]
    dest = jnp.where(valid, pad_start[experts] + rank, n_slots).astype(jnp.int32)
    tok = jnp.broadcast_to(jnp.arange(t_rows, dtype=jnp.int32)[:, None], dest.shape)
    slot_tok = jnp.zeros((n_slots,), jnp.int32).at[dest.reshape(-1)].set(tok.reshape(-1), mode='drop')
    nblk = n_slots // EXPERT_TILE
    blk_start = jnp.arange(nblk, dtype=jnp.int32) * EXPERT_TILE
    block_expert = jnp.minimum(jnp.sum((pad_end[None, :] <= blk_start[:, None]).astype(jnp.int32), axis=1),
                               N_EXPERTS - 1)
    n_used = (pad_end[-1] // EXPERT_TILE).astype(jnp.int32).reshape(1)
    pos = jnp.where(valid, dest, 0).reshape(-1)
    return slot_tok * ROW_TILES, block_expert, n_used, pos * ROW_TILES, gates


ROW_TILES = D_MODEL // LANES


def _to_row_tiles(dst_ref, val):
    rows = val.shape[0]
    for s in range(ROW_TILES):
        dst_ref[pl.ds(s, rows, stride=ROW_TILES), :] = val[:, LANES * s:LANES * (s + 1)]


def _from_row_tiles(src_ref, rows):
    return jnp.concatenate([src_ref[pl.ds(s, rows, stride=ROW_TILES), :] for s in range(ROW_TILES)], axis=1)


LOOKAHEAD = 2
N_BUF = LOOKAHEAD + 1


def _expert_kernel(be_ref, nused_ref, *refs):
    tok_refs = refs[:N_BUF]
    x_hbm, wg_ref, wu_ref, wd_ref, y_ref = refs[N_BUF:N_BUF + 5]
    bufs = refs[N_BUF + 5:2 * N_BUF + 5]
    sem_ref = refs[2 * N_BUF + 5]
    i = pl.program_id(0)
    n_used = nused_ref[0]
    tb = EXPERT_TILE
    turn = lax.rem(i, N_BUF)

    def gather(tok_ref, p):
        for r in range(tb):
            row = pl.multiple_of(tok_ref[0, 0, r], ROW_TILES)
            pltpu.make_async_copy(x_hbm.at[pl.ds(row, ROW_TILES), :],
                                  bufs[p].at[pl.ds(ROW_TILES * r, ROW_TILES), :],
                                  sem_ref.at[p]).start(priority=r % 2)

    def wait(p):
        pltpu.make_async_copy(x_hbm.at[pl.ds(0, tb * ROW_TILES), :], bufs[p], sem_ref.at[p]).wait()

    @pl.when(i == 0)
    def _():
        for d in range(LOOKAHEAD):
            gather(tok_refs[d], d)

    for p in range(N_BUF):
        @pl.when((turn == p) & (i < n_used))
        def _(p=p):
            wait(p)
            gather(tok_refs[LOOKAHEAD], (p + LOOKAHEAD) % N_BUF)
            x = _from_row_tiles(bufs[p], tb).astype(BF16)
            g = jnp.dot(x, wg_ref[0].astype(BF16), preferred_element_type=F32)
            u = jnp.dot(x, wu_ref[0].astype(BF16), preferred_element_type=F32)
            hid = (g * _sigmoid(g) * u).astype(BF16)
            _to_row_tiles(y_ref, jnp.dot(hid, wd_ref[0].astype(BF16), preferred_element_type=F32))

        @pl.when((turn == p) & (i >= n_used) & (i < n_used + LOOKAHEAD))
        def _(p=p):
            wait(p)

    @pl.when(i >= n_used)
    def _():
        y_ref[...] = jnp.zeros_like(y_ref)


def _experts(h1r, slot_row, block_expert, n_used, w_gate, w_up, w_down):
    n_slots = slot_row.shape[0]
    nblk = n_slots // EXPERT_TILE
    tok3 = slot_row.reshape(nblk, 1, EXPERT_TILE)
    ahead = lambda d: pl.BlockSpec((1, 1, EXPERT_TILE), lambda i, be, nu: (jnp.minimum(i + d, nblk - 1), 0, 0),
                                   memory_space=pltpu.SMEM)
    tile_buf = pltpu.VMEM((EXPERT_TILE * ROW_TILES, LANES), F32)
    grid_spec = pltpu.PrefetchScalarGridSpec(
        num_scalar_prefetch=2,
        grid=(nblk,),
        in_specs=[ahead(d) for d in range(N_BUF)] + [
            pl.BlockSpec(memory_space=pl.ANY),
            pl.BlockSpec((1, D_MODEL, D_EXPERT), lambda i, be, nu: (be[i], 0, 0)),
            pl.BlockSpec((1, D_MODEL, D_EXPERT), lambda i, be, nu: (be[i], 0, 0)),
            pl.BlockSpec((1, D_EXPERT, D_MODEL), lambda i, be, nu: (be[i], 0, 0))],
        out_specs=pl.BlockSpec((EXPERT_TILE * ROW_TILES, LANES), lambda i, be, nu: (i, 0)),
        scratch_shapes=[tile_buf] * N_BUF + [pltpu.SemaphoreType.DMA((N_BUF,))],
    )
    return pl.pallas_call(
        _expert_kernel,
        grid_spec=grid_spec,
        out_shape=jax.ShapeDtypeStruct((n_slots * ROW_TILES, LANES), F32),
        compiler_params=_cparams(1),
        name="moe_experts",
    )(block_expert, n_used, *([tok3] * N_BUF), h1r, w_gate, w_up, w_down)


def _combine_kernel(*refs, first_tile, n_tiles, to_batch_major):
    pos_refs = refs[:N_BUF]
    yb_hbm, gate_ref, h_ref, g_ref, b_ref, o_ref, ybuf_ref, sem_ref = refs[N_BUF:N_BUF + 8]
    rest = refs[N_BUF + 8:]
    i = pl.program_id(0)
    tm = COMBINE_TILE
    turn = lax.rem(i, N_BUF)

    def gather(pos_ref, p):
        for r in range(tm):
            for kk in range(TOP_K_FINE):
                row = pl.multiple_of(pos_ref[0, 0, TOP_K_FINE * r + kk], ROW_TILES)
                pltpu.make_async_copy(yb_hbm.at[pl.ds(row, ROW_TILES), :],
                                      ybuf_ref.at[p, kk, pl.ds(ROW_TILES * r, ROW_TILES), :],
                                      sem_ref.at[p]).start(priority=kk)

    def finish(p, look_ahead):
        for kk in range(TOP_K_FINE):
            pltpu.make_async_copy(yb_hbm.at[pl.ds(0, tm * ROW_TILES), :], ybuf_ref.at[p, kk],
                                  sem_ref.at[p]).wait()
        if look_ahead:
            gather(pos_refs[LOOKAHEAD], (p + LOOKAHEAD) % N_BUF)
        gts = gate_ref[...]
        moe = (_from_row_tiles(ybuf_ref.at[p, 0], tm) * gts[:, 0:1]
               + _from_row_tiles(ybuf_ref.at[p, 1], tm) * gts[:, 1:2])
        h2 = _layer_norm_rows(ALPHA * h_ref[...] + moe, g_ref[...], b_ref[...])
        if not to_batch_major:
            o_ref[...] = h2
        else:
            hs_ref = rest[0]
            tl = tm // NB
            for s in range(D_MODEL // LANES):
                hs_ref[s] = h2[:, LANES * s:LANES * (s + 1)]
            for b in range(NB):
                for s in range(D_MODEL // LANES):
                    o_ref[b, :, LANES * s:LANES * (s + 1)] = hs_ref[s, pl.ds(b, tl, stride=NB), :]

    @pl.when(i == 0)
    def _():
        for d in range(LOOKAHEAD):
            gather(pos_refs[d], d)

    for p in range(N_BUF):
        @pl.when((turn == p) & (i < n_tiles - LOOKAHEAD))
        def _(p=p):
            finish(p, True)

    @pl.when(i >= n_tiles - LOOKAHEAD)
    def _():
        finish(turn, False)


def _combine(yb, pos, gates, h1, ln_g, ln_b, *, final):
    t_rows = h1.shape[0]
    tm = COMBINE_TILE
    first_tile = (LANES * NB) // tm if final else 0
    n_tiles = t_rows // tm - first_tile
    pos3 = pos.reshape(t_rows // tm, 1, TOP_K_FINE * tm)
    g2 = ln_g.astype(F32).reshape(1, D_MODEL)
    b2 = ln_b.astype(F32).reshape(1, D_MODEL)
    full = lambda a: pl.BlockSpec(a.shape, lambda i: (0,) * a.ndim)
    last = t_rows // tm - 1
    ahead = lambda d: pl.BlockSpec((1, 1, TOP_K_FINE * tm),
                                   lambda i: (jnp.minimum(i + first_tile + d, last), 0, 0),
                                   memory_space=pltpu.SMEM)
    in_specs = [ahead(d) for d in range(N_BUF)] + [
                pl.BlockSpec(memory_space=pl.ANY),
                pl.BlockSpec((tm, TOP_K_FINE), lambda i: (i + first_tile, 0)),
                pl.BlockSpec((tm, D_MODEL), lambda i: (i + first_tile, 0)),
                full(g2), full(b2)]
    scratch = [pltpu.VMEM((N_BUF, TOP_K_FINE, tm * ROW_TILES, LANES), F32), pltpu.SemaphoreType.DMA((N_BUF,))]
    if final:
        tl = tm // NB
        seq = t_rows // NB - LANES
        out_specs = pl.BlockSpec((NB, tl, D_MODEL), lambda i: (0, i, 0))
        out_shape = jax.ShapeDtypeStruct((NB, seq, D_MODEL), F32)
        scratch.append(pltpu.VMEM((D_MODEL // LANES, tm, LANES), F32))
    else:
        out_specs = pl.BlockSpec((tm, D_MODEL), lambda i: (i, 0))
        out_shape = jax.ShapeDtypeStruct((t_rows, D_MODEL), F32)
    return pl.pallas_call(
        functools.partial(_combine_kernel, first_tile=first_tile, n_tiles=n_tiles, to_batch_major=final),
        grid=(n_tiles,),
        in_specs=in_specs, out_specs=out_specs, out_shape=out_shape,
        scratch_shapes=scratch,
        compiler_params=_cparams(1),
        name="moe_combine_final" if final else "moe_combine",
    )(*([pos3] * N_BUF), yb, gates, h1, g2, b2)


def _moe_block(h1, h1r, route, counts, w_gate, w_up, w_down, ln_g, ln_b, *, final):
    t_rows = h1.shape[0]
    n_assign = (t_rows - PAD_ROWS) * TOP_K_FINE
    nblk = -(-(n_assign + N_EXPERTS * (EXPERT_TILE - 1)) // EXPERT_TILE) + LOOKAHEAD
    n_slots = nblk * EXPERT_TILE
    slot_row, block_expert, n_used, pos, gates = _slots(route, counts, n_slots)
    yb = _experts(h1r, slot_row, block_expert, n_used, w_gate, w_up, w_down)
    return _combine(yb, pos, gates, h1, ln_g, ln_b, final=final)


def _inproj1_kernel(h_ref, w_ref, gg_ref, rec_ref):
    x = h_ref[...].astype(BF16)
    z = jnp.dot(x, w_ref[...], preferred_element_type=F32)
    gg_ref[...] = _gelu(z[:, :D_RNN]).astype(BF16)
    rec_ref[...] = z[:, D_RNN:]


def _inproj1(h, w_bf16):
    t_rows = h.shape[0]
    tm = TOK_TILE
    row = lambda width: pl.BlockSpec((tm, width), lambda i: (i, 0))
    return pl.pallas_call(
        _inproj1_kernel,
        grid=(t_rows // tm,),
        in_specs=[row(D_MODEL), pl.BlockSpec(w_bf16.shape, lambda i: (0, 0))],
        out_specs=[row(D_RNN), row(D_RNN)],
        out_shape=[jax.ShapeDtypeStruct((t_rows, D_RNN), BF16), jax.ShapeDtypeStruct((t_rows, D_RNN), F32)],
        compiler_params=_cparams(1),
        name="l1_inproj",
    )(h, w_bf16)


CONV_HALO = (CONV_WIDTH - 1) * NB


def _rglru_kernel(rec_ref, gg_ref, cw_ref, cb_ref, wax_ref, ba_ref, bx_ref, sp_ref,
                  y_ref, rp_ref, a_ref, b_ref, st_ref, *, steps):
    i = pl.program_id(0)
    rows = steps * NB

    @pl.when(i == 0)
    def _():
        st_ref[...] = jnp.zeros_like(st_ref)
        rp_ref[0:CONV_HALO, :] = jnp.zeros((CONV_HALO, D_RNN), F32)

    row = i * rows + lax.broadcasted_iota(jnp.int32, (rows, 1), 0)
    real = row >= PAD_ROWS
    rp_ref[CONV_HALO:CONV_HALO + rows, :] = jnp.where(real, rec_ref[...], 0.0)
    xc = cb_ref[...] + sum(rp_ref[NB * j:NB * j + rows, :] * cw_ref[j:j + 1, :] for j in range(CONV_WIDTH))
    tail = rp_ref[rows:rows + CONV_HALO, :]
    rp_ref[0:CONV_HALO, :] = tail

    for n in range(LRU_BLOCKS):
        cs = slice(LRU_BLOCK_W * n, LRU_BLOCK_W * (n + 1))
        xb = xc[:, cs]
        ra = jnp.dot(xb.astype(BF16), wax_ref[n], preferred_element_type=F32)
        r = _sigmoid(ra[:, :LRU_BLOCK_W] + ba_ref[:, cs])
        ig = _sigmoid(ra[:, LRU_BLOCK_W:] + bx_ref[:, cs])
        log_a = -LRU_C * r * sp_ref[:, cs]
        a = jnp.exp(log_a)
        bt = jnp.sqrt(1.0 - a * a) * (ig * xb)
        a_ref[:, cs] = a
        b_ref[:, cs] = jnp.where(real, bt, 0.0)

    def step(t, h):
        r0 = pl.multiple_of(t * NB, NB)
        hn = a_ref[pl.ds(r0, NB), :] * h + b_ref[pl.ds(r0, NB), :]
        b_ref[pl.ds(r0, NB), :] = hn
        return hn

    st_ref[...] = lax.fori_loop(0, steps, step, st_ref[...])
    y_ref[...] = (gg_ref[...].astype(F32) * b_ref[...]).astype(BF16)


def _rglru(rec, gg, conv_w, conv_b, w_a, b_a, w_x, b_x, lru_lambda):
    t_rows = rec.shape[0]
    steps = SCAN_STEPS
    rows = steps * NB
    cw = conv_w.astype(F32)
    cb = conv_b.astype(F32).reshape(1, D_RNN)
    wax = jnp.concatenate([w_a.astype(F32), w_x.astype(F32)], axis=2).astype(BF16)
    ba = b_a.astype(F32).reshape(1, D_RNN)
    bx = b_x.astype(F32).reshape(1, D_RNN)
    sp = jax.nn.softplus(-lru_lambda.astype(F32)).reshape(1, D_RNN)
    full = lambda a: pl.BlockSpec(a.shape, lambda i: (0,) * a.ndim)
    row = pl.BlockSpec((rows, D_RNN), lambda i: (i, 0))
    return pl.pallas_call(
        functools.partial(_rglru_kernel, steps=steps),
        grid=(t_rows // rows,),
        in_specs=[row, row, full(cw), full(cb), full(wax), full(ba), full(bx), full(sp)],
        out_specs=row,
        out_shape=jax.ShapeDtypeStruct((t_rows, D_RNN), BF16),
        scratch_shapes=[pltpu.VMEM((rows + CONV_HALO, D_RNN), F32), pltpu.VMEM((rows, D_RNN), F32),
                        pltpu.VMEM((rows, D_RNN), F32), pltpu.VMEM((NB, D_RNN), F32)],
        compiler_params=_cparams(1),
        name="l1_rglru",
    )(rec, gg, cw, cb, wax, ba, bx, sp)


def kernel(x, meta, l0_ln1_g, l0_ln1_b, l0_w_in, l0_s5_lambda_re, l0_s5_lambda_im, l0_s5_log_dt, l0_s5_b_re, l0_s5_b_im, l0_s5_c_re, l0_s5_c_im, l0_s5_d, l0_s5_w_glu, l0_s5_b_glu, l0_da_lq1, l0_da_lk1, l0_da_lq2, l0_da_lk2, l0_da_subln_g, l0_w_out, l0_ln2_g, l0_ln2_b, l0_moe_w_coarse, l0_moe_b_coarse, l0_moe_w_fine, l0_moe_b_fine, l0_moe_w_gate, l0_moe_w_up, l0_moe_w_down, l1_ln1_g, l1_ln1_b, l1_w_in, l1_conv_w, l1_conv_b, l1_lru_w_a, l1_lru_b_a, l1_lru_w_x, l1_lru_b_x, l1_lru_lambda, l1_w_out, l1_ln2_g, l1_ln2_b, l1_moe_w_coarse, l1_moe_b_coarse, l1_moe_w_fine, l1_moe_b_fine, l1_moe_w_gate, l1_moe_w_up, l1_moe_w_down):
    bsz, seq, _ = x.shape
    assert bsz == NB and seq % Q_TILE == 0
    dt = x.dtype
    lp = FRONT_PAD + N_META + seq
    h = jnp.concatenate([
        jnp.zeros((FRONT_PAD, NB, D_MODEL), dt),
        jnp.broadcast_to(meta.astype(dt)[:, None, :], (N_META, NB, D_MODEL)),
        jnp.transpose(x, (1, 0, 2))], axis=0).reshape(lp * NB, D_MODEL)

    col_scale = jnp.concatenate([jnp.ones((S5_WIDTH,), F32),
                                 jnp.full((DA_WIDTH,), DA_HEAD_DIM ** -0.5, F32),
                                 jnp.ones((2 * DA_WIDTH,), F32)])
    w_in0 = (l0_w_in.astype(F32) * col_scale[None, :]).astype(BF16)
    u, q, k, v = _inproj0(h, w_in0, lp)
    s5p = _s5_params(l0_s5_lambda_re, l0_s5_lambda_im, l0_s5_log_dt,
                     l0_s5_b_re, l0_s5_b_im, l0_s5_c_re, l0_s5_c_im)
    y_s5 = _s5(u, *s5p, l0_s5_d, l0_s5_w_glu, l0_s5_b_glu)
    lam_init = 0.8 - 0.6 * math.exp(-0.3 * 0)
    lam = (jnp.exp(jnp.sum(l0_da_lq1.astype(F32) * l0_da_lk1.astype(F32)))
           - jnp.exp(jnp.sum(l0_da_lq2.astype(F32) * l0_da_lk2.astype(F32))) + lam_init)
    y_da = _diffattn(q, k, v, lam, l0_da_subln_g, lam_init)
    router0 = _router_weights(l0_moe_w_coarse, l0_moe_b_coarse, l0_moe_w_fine, l0_moe_b_fine)
    h, hr, route, counts = _outproj(y_s5, y_da, h, l0_w_out, l0_ln1_g, l0_ln1_b, router0)
    h = _moe_block(h, hr, route, counts, l0_moe_w_gate, l0_moe_w_up, l0_moe_w_down, l0_ln2_g, l0_ln2_b, final=False)

    gg, rec = _inproj1(h, l1_w_in.astype(BF16))
    y = _rglru(rec, gg, l1_conv_w, l1_conv_b, l1_lru_w_a, l1_lru_b_a, l1_lru_w_x, l1_lru_b_x, l1_lru_lambda)
    router1 = _router_weights(l1_moe_w_coarse, l1_moe_b_coarse, l1_moe_w_fine, l1_moe_b_fine)
    h, hr, route, counts = _outproj(y, None, h, l1_w_out, l1_ln1_g, l1_ln1_b, router1)
    out = _moe_block(h, hr, route, counts, l1_moe_w_gate, l1_moe_w_up, l1_moe_w_down, l1_ln2_g, l1_ln2_b, final=True)
    return out.astype(dt)
```

```python
import functools
import math

import jax
import jax.numpy as jnp
from jax import lax
from jax.experimental import pallas as pl
from jax.experimental.pallas import tpu as pltpu

F32 = jnp.float32
BF16 = jnp.bfloat16

D_MODEL = 1024
DEPTH = 2
CHUNK = 64
N_META = 16
S5_WIDTH = 512
S5_GROUP = 16
S5_GROUPS = 32
S5_STATE = 64
DA_HEADS = 4
DA_HEAD_DIM = 64
DA_WIDTH = 512
D_RNN = 1280
LRU_BLOCKS = 10
LRU_BLOCK_W = 128
CONV_WIDTH = 4
LRU_C = 8.0
N_GROUPS = 4
EXPERTS_PER_GROUP = 8
N_EXPERTS = 32
TOP_K_FINE = 2
D_EXPERT = 512
ALPHA = (2 * DEPTH) ** 0.25
LN_EPS = 1e-5
NEG_INF = -1e30

NB = 8
LANES = 128
FRONT_PAD = LANES - N_META
PAD_ROWS = FRONT_PAD * NB
Q_TILE = 256
K_TILE = 512
TOK_TILE = 512
SCAN_STEPS = 32
EXPERT_TILE = 256
COMBINE_TILE = 256
ROUTER_LANES = 128
VMEM_LIMIT = 48 * 1024 * 1024
LOG2_E = math.log2(math.e)


def _cparams(n_axes, vmem=VMEM_LIMIT):
    return pltpu.CompilerParams(dimension_semantics=("arbitrary",) * n_axes,
                                vmem_limit_bytes=vmem)


def _gelu(x):
    return 0.5 * x * (1.0 + jnp.tanh(math.sqrt(2.0 / math.pi) * (x + 0.044715 * (x * x * x))))


def _sigmoid(x):
    return 1.0 / (1.0 + jnp.exp(-x))


def _layer_norm_rows(r, g, b):
    mu = jnp.mean(r, axis=-1, keepdims=True)
    c = r - mu
    var = jnp.mean(c * c, axis=-1, keepdims=True)
    return c * lax.rsqrt(var + LN_EPS) * g + b


def _split_bf16(w):
    hi = w.astype(BF16)
    lo = (w - hi.astype(F32)).astype(BF16)
    return hi, lo


def _inproj0_kernel(h_ref, w_ref, u_ref, q_ref, k_ref, v_ref, zs_ref, *, tl):
    x = h_ref[...].astype(BF16)
    z = jnp.dot(x, w_ref[...], preferred_element_type=F32)
    u_ref[...] = z[:, :S5_WIDTH]
    n_slab = 3 * DA_WIDTH // LANES
    for s in range(n_slab):
        zs_ref[s] = z[:, S5_WIDTH + LANES * s:S5_WIDTH + LANES * (s + 1)]
    dsts = (q_ref, k_ref, v_ref)
    per = DA_WIDTH // LANES
    for b in range(NB):
        for s in range(n_slab):
            c = (s % per) * LANES
            blk = zs_ref[s, pl.ds(b, tl, stride=NB), :]
            if s < per:
                blk = blk * LOG2_E
            dsts[s // per][b, :, c:c + LANES] = blk.astype(BF16)


def _inproj0(h, w_bf16, lp):
    t_rows = h.shape[0]
    tm = TOK_TILE
    tl = tm // NB
    qkv_shape = jax.ShapeDtypeStruct((NB, lp, DA_WIDTH), BF16)
    qkv_spec = pl.BlockSpec((NB, tl, DA_WIDTH), lambda i: (0, i, 0))
    return pl.pallas_call(
        functools.partial(_inproj0_kernel, tl=tl),
        grid=(t_rows // tm,),
        in_specs=[pl.BlockSpec((tm, D_MODEL), lambda i: (i, 0)),
                  pl.BlockSpec(w_bf16.shape, lambda i: (0, 0))],
        out_specs=[pl.BlockSpec((tm, S5_WIDTH), lambda i: (i, 0)), qkv_spec, qkv_spec, qkv_spec],
        out_shape=[jax.ShapeDtypeStruct((t_rows, S5_WIDTH), F32), qkv_shape, qkv_shape, qkv_shape],
        scratch_shapes=[pltpu.VMEM((3 * DA_WIDTH // LANES, tm, LANES), F32)],
        compiler_params=_cparams(1),
        name="l0_inproj",
    )(h, w_bf16)


S5_SLABS = S5_WIDTH // LANES
S5_SLAB_STATE = (S5_GROUPS // S5_SLABS) * S5_STATE
S5_NSTATE = S5_GROUPS * S5_STATE


def _s5_kernel(u_ref, bmat_ref, are_ref, aim_ref, cre_ref, cim_ref, d_ref, wglu_ref, bglu_ref,
               y_ref, hre_ref, him_ref, st_ref, *, steps):
    i = pl.program_id(0)
    rows = steps * NB

    @pl.when(i == 0)
    def _():
        st_ref[...] = jnp.zeros_like(st_ref)

    row = i * rows + lax.broadcasted_iota(jnp.int32, (rows, 1), 0)
    u = jnp.where(row >= PAD_ROWS, u_ref[...], 0.0)
    ub = u.astype(BF16)
    for s in range(S5_SLABS):
        bu = jnp.dot(ub[:, LANES * s:LANES * (s + 1)], bmat_ref[s], preferred_element_type=F32)
        hre_ref[:, S5_SLAB_STATE * s:S5_SLAB_STATE * (s + 1)] = bu[:, :S5_SLAB_STATE]
        him_ref[:, S5_SLAB_STATE * s:S5_SLAB_STATE * (s + 1)] = bu[:, S5_SLAB_STATE:]

    cw = 512
    for cg in range(S5_NSTATE // cw):
        c0 = cg * cw
        a_r = are_ref[:, c0:c0 + cw]
        a_i = aim_ref[:, c0:c0 + cw]

        def step(t, carry, c0=c0, a_r=a_r, a_i=a_i):
            sr, si = carry
            r0 = pl.multiple_of(t * NB, NB)
            br = hre_ref[pl.ds(r0, NB), c0:c0 + cw]
            bi = him_ref[pl.ds(r0, NB), c0:c0 + cw]
            nr = a_r * sr - a_i * si + br
            ni = a_r * si + a_i * sr + bi
            hre_ref[pl.ds(r0, NB), c0:c0 + cw] = nr
            him_ref[pl.ds(r0, NB), c0:c0 + cw] = ni
            return nr, ni

        sr, si = lax.fori_loop(0, steps, step, (st_ref[0, :, c0:c0 + cw], st_ref[1, :, c0:c0 + cw]))
        st_ref[0, :, c0:c0 + cw] = sr
        st_ref[1, :, c0:c0 + cw] = si

    ys = []
    for s in range(S5_SLABS):
        hr = hre_ref[:, S5_SLAB_STATE * s:S5_SLAB_STATE * (s + 1)].astype(BF16)
        hi = him_ref[:, S5_SLAB_STATE * s:S5_SLAB_STATE * (s + 1)].astype(BF16)
        ys.append(jnp.dot(hr, cre_ref[s], preferred_element_type=F32)
                  + jnp.dot(hi, cim_ref[s], preferred_element_type=F32))
    y = jnp.concatenate(ys, axis=1) + d_ref[...] * u
    y = _gelu(y)
    gate = _sigmoid(jnp.dot(y.astype(BF16), wglu_ref[...], preferred_element_type=F32) + bglu_ref[...])
    y_ref[...] = (y * gate).astype(BF16)


def _s5_params(lam_re, lam_im, log_dt, b_re, b_im, c_re, c_im):
    dt = jnp.exp(log_dt.astype(F32))[:, None]
    lr = jnp.minimum(lam_re.astype(F32), -1e-4)
    li = lam_im.astype(F32)
    mag = jnp.exp(lr * dt)
    ar = mag * jnp.cos(li * dt)
    ai = mag * jnp.sin(li * dt)
    den = lr * lr + li * li
    nr, ni = ar - 1.0, ai
    fr = ((nr * lr + ni * li) / den)[..., None]
    fi = ((ni * lr - nr * li) / den)[..., None]
    br, bi = b_re.astype(F32), b_im.astype(F32)
    bbr = fr * br - fi * bi
    bbi = fr * bi + fi * br
    gps = S5_GROUPS // S5_SLABS
    eye = jnp.eye(gps, dtype=F32)

    def in_slab(m):
        m4 = m.reshape(S5_SLABS, gps, S5_STATE, S5_GROUP)
        return jnp.einsum('sgph,gk->sghkp', m4, eye).reshape(S5_SLABS, LANES, S5_SLAB_STATE)

    def out_slab(m):
        m4 = m.reshape(S5_SLABS, gps, S5_GROUP, S5_STATE)
        return jnp.einsum('sghp,gk->sgpkh', m4, eye).reshape(S5_SLABS, S5_SLAB_STATE, LANES)

    bmat = jnp.concatenate([in_slab(bbr), in_slab(bbi)], axis=2).astype(BF16)
    cre = out_slab(c_re.astype(F32)).astype(BF16)
    cim = out_slab(-c_im.astype(F32)).astype(BF16)
    a_re = jnp.broadcast_to(ar.reshape(1, S5_NSTATE), (NB, S5_NSTATE))
    a_im = jnp.broadcast_to(ai.reshape(1, S5_NSTATE), (NB, S5_NSTATE))
    return bmat, a_re, a_im, cre, cim


def _s5(u, bmat, a_re, a_im, cre, cim, d, w_glu, b_glu):
    t_rows = u.shape[0]
    steps = SCAN_STEPS
    rows = steps * NB
    full = lambda a: pl.BlockSpec(a.shape, lambda i: (0,) * a.ndim)
    d2 = d.astype(F32).reshape(1, S5_WIDTH)
    bg2 = b_glu.astype(F32).reshape(1, S5_WIDTH)
    wg = w_glu.astype(BF16)
    return pl.pallas_call(
        functools.partial(_s5_kernel, steps=steps),
        grid=(t_rows // rows,),
        in_specs=[pl.BlockSpec((rows, S5_WIDTH), lambda i: (i, 0)),
                  full(bmat), full(a_re), full(a_im), full(cre), full(cim), full(d2), full(wg), full(bg2)],
        out_specs=pl.BlockSpec((rows, S5_WIDTH), lambda i: (i, 0)),
        out_shape=jax.ShapeDtypeStruct((t_rows, S5_WIDTH), BF16),
        scratch_shapes=[pltpu.VMEM((rows, S5_NSTATE), F32), pltpu.VMEM((rows, S5_NSTATE), F32),
                        pltpu.VMEM((2, NB, S5_NSTATE), F32)],
        compiler_params=_cparams(1),
        name="l0_s5",
    )(u, bmat, a_re, a_im, cre, cim, d2, wg, bg2)


def _diffattn_kernel(lam_ref, q_ref, k_ref, v_ref, g_ref, o_ref,
                     s_ref, qm_ref, m_ref, l_ref, a_ref, *, lam_init, lp):
    lam = lam_ref[0]
    tq, tk = Q_TILE, K_TILE
    lane = lax.broadcasted_iota(jnp.int32, (tq, LANES), 1)
    qrow = lax.broadcasted_iota(jnp.int32, (tq, tk), 0)
    kloc = lax.broadcasted_iota(jnp.int32, (tq, tk), 1)
    nt = (((1,), (1,)), ((), ()))
    bf16_rows = 16

    def key_start(j):
        return pl.multiple_of(jnp.minimum(FRONT_PAD + j * tk, lp - tk), bf16_rows)

    def chunk_of(pos):
        return jnp.right_shift(pos - CHUNK, 6)

    def lane_fold(x, op):
        r = x[:, :LANES]
        for c in range(1, x.shape[1] // LANES):
            r = op(r, x[:, LANES * c:LANES * (c + 1)])
        return r

    def q_start(i):
        return pl.multiple_of(jnp.minimum(i * tq, lp - tq), LANES)

    def n_key_tiles(i):
        return (q_start(i) + tq - FRONT_PAD + tk - 1) // tk

    def n_full_tiles(i):
        return jnp.maximum(q_start(i) - FRONT_PAD, 0) // tk


    def prep(i, par):
        q = q_ref[0, pl.ds(q_start(i), tq), :]
        zero = jnp.zeros_like(q)
        qm_ref[par, 0] = jnp.where(lane < DA_HEAD_DIM, q, zero)
        qm_ref[par, 1] = jnp.where(lane >= DA_HEAD_DIM, q, zero)
        m_ref[par] = jnp.full(m_ref.shape[1:], NEG_INF, F32)

    def scores(i, par, j, masked):
        k0 = key_start(j)
        kt = k_ref[0, pl.ds(k0, tk), :]
        if masked:
            kpos = k0 + kloc
            mask = (chunk_of(kpos) <= chunk_of(q_start(i) + qrow)) & (kpos >= FRONT_PAD + j * tk)
        for h in range(2):
            s = lax.dot_general(qm_ref[par, h], kt, nt, preferred_element_type=F32)
            if masked:
                s = jnp.where(mask, s, NEG_INF)
            s_ref[par, h, j] = s
            m_ref[par, h] = jnp.maximum(m_ref[par, h], lane_fold(s, jnp.maximum))

    def values(par, j, m):
        vt = v_ref[0, pl.ds(key_start(j), tk), :]
        for h in range(2):
            p = jnp.exp2(s_ref[par, h, j] - m[h])
            l_ref[h] += lane_fold(p, jnp.add)
            a_ref[h] += jnp.dot(p.astype(BF16), vt, preferred_element_type=F32)

    def row_max(par):
        l_ref[...] = jnp.zeros(l_ref.shape, F32)
        a_ref[...] = jnp.zeros(a_ref.shape, F32)
        return [jnp.max(m_ref[par, h], axis=-1, keepdims=True) for h in range(2)]

    def finish(i):
        l1 = jnp.sum(l_ref[0], axis=-1, keepdims=True)
        l2 = jnp.sum(l_ref[1], axis=-1, keepdims=True)
        o = a_ref[0] / l1 - lam * (a_ref[1] / l2)
        o = o * lax.rsqrt(jnp.mean(o * o, axis=-1, keepdims=True) + LN_EPS) * g_ref[...]
        o_ref[0, pl.ds(q_start(i), tq), :] = o * (1.0 - lam_init)

    def loop(lo, hi, body):
        def wrapped(j, _):
            body(j)
            return 0
        lax.fori_loop(lo, hi, wrapped, 0)

    def step(i, par):
        m = row_max(par)
        prep(i + 1, 1 - par)

        def both(j, masked):
            scores(i + 1, 1 - par, j, masked)
            values(par, j, m)

        loop(0, n_full_tiles(i + 1), functools.partial(both, masked=False))
        loop(n_full_tiles(i + 1), n_key_tiles(i), functools.partial(both, masked=True))
        loop(n_key_tiles(i), n_key_tiles(i + 1), lambda j: scores(i + 1, 1 - par, j, True))
        finish(i)

    def step_pair(i2):
        step(2 * i2, 0)
        step(2 * i2 + 1, 1)

    n_q = pl.cdiv(lp, tq)
    prep(0, 0)
    loop(0, n_key_tiles(0), lambda j: scores(0, 0, j, True))
    loop(0, (n_q - 1) // 2, step_pair)
    if (n_q - 1) % 2:
        step(n_q - 2, 0)
    last = (n_q - 1) % 2
    m = row_max(last)
    loop(0, n_key_tiles(n_q - 1), lambda j: values(last, j, m))
    finish(n_q - 1)


def _diffattn(q, k, v, lam, subln_g, lam_init):
    nb, lp, _ = q.shape
    g2 = subln_g.astype(F32).reshape(1, 2 * DA_HEAD_DIM)
    seq_spec = pl.BlockSpec((1, lp, LANES), lambda b, h: (b, 0, h))
    return pl.pallas_call(
        functools.partial(_diffattn_kernel, lam_init=lam_init, lp=lp),
        grid=(nb, DA_HEADS),
        in_specs=[pl.BlockSpec(memory_space=pltpu.SMEM), seq_spec, seq_spec, seq_spec,
                  pl.BlockSpec((1, LANES), lambda b, h: (0, 0))],
        out_specs=seq_spec,
        out_shape=jax.ShapeDtypeStruct((nb, lp, DA_WIDTH), F32),
        scratch_shapes=[pltpu.VMEM((2, 2, pl.cdiv(lp, K_TILE), Q_TILE, K_TILE), F32),
                        pltpu.VMEM((2, 2, Q_TILE, LANES), BF16),
                        pltpu.VMEM((2, 2, Q_TILE, LANES), F32), pltpu.VMEM((2, Q_TILE, LANES), F32),
                        pltpu.VMEM((2, Q_TILE, LANES), F32)],
        compiler_params=_cparams(2),
        name="l0_diffattn",
    )(lam.reshape(1), q, k, v, g2)


def _router_logits(h1, whi_ref, wlo_ref, rb_ref):
    hi = h1.astype(BF16)
    lo = (h1 - hi.astype(F32)).astype(BF16)
    return (jnp.dot(hi, whi_ref[...], preferred_element_type=F32)
            + jnp.dot(lo, whi_ref[...], preferred_element_type=F32)
            + jnp.dot(hi, wlo_ref[...], preferred_element_type=F32) + rb_ref[...])


ROUTE_E, ROUTE_GATE, ROUTE_RANK = 0, 2, 4


def _route_rows(lg, cnt_ref, first_row):
    rows = lg.shape[0]
    lane = lax.broadcasted_iota(jnp.int32, lg.shape, 1)
    lanef = lane.astype(F32)
    valid = (first_row + lax.broadcasted_iota(jnp.int32, (rows, 1), 0)) >= PAD_ROWS
    ninf = float('-inf')
    first = lambda hit: jnp.min(jnp.where(hit, lanef, float(LANES)), axis=1, keepdims=True)
    cm = jnp.where(lane < N_GROUPS, lg, ninf)
    cmax = jnp.max(cm, axis=1, keepdims=True)
    p_grp = 1.0 / jnp.sum(jnp.exp(cm - cmax), axis=1, keepdims=True)
    lo = N_GROUPS + EXPERTS_PER_GROUP * first(cm == cmax)
    fm = jnp.where((lanef >= lo) & (lanef < lo + EXPERTS_PER_GROUP), lg, ninf)
    v1 = jnp.max(fm, axis=1, keepdims=True)
    i1 = first(fm == v1)
    fm2 = jnp.where(lanef == i1, ninf, fm)
    v2 = jnp.max(fm2, axis=1, keepdims=True)
    i2 = first(fm2 == v2)
    t = jnp.exp(v2 - v1)
    g1 = p_grp / (1.0 + t)
    g2 = p_grp * t / (1.0 + t)
    oh1 = (lanef == i1) & valid
    oh2 = (lanef == i2) & valid
    m = jnp.where(oh1 | oh2, 1.0, 0.0)
    ri = lax.broadcasted_iota(jnp.int32, (rows, rows), 0)
    ci = lax.broadcasted_iota(jnp.int32, (rows, rows), 1)
    earlier = jnp.where(ci < ri, 1.0, 0.0).astype(BF16)
    before = jnp.dot(earlier, m.astype(BF16), preferred_element_type=F32) + cnt_ref[0:1, :]
    r1 = jnp.sum(jnp.where(oh1, before, 0.0), axis=1, keepdims=True)
    r2 = jnp.sum(jnp.where(oh2, before, 0.0), axis=1, keepdims=True)
    cnt_ref[...] = cnt_ref[...] + jnp.sum(m, axis=0, keepdims=True)
    zero = jnp.zeros_like(g1)
    rec = jnp.zeros(lg.shape, F32)
    for k, val in enumerate((i1 - N_GROUPS, i2 - N_GROUPS, jnp.where(valid, g1, zero),
                             jnp.where(valid, g2, zero), r1, r2)):
        rec = jnp.where(lane == k, val, rec)
    return rec


def _route_tile(h1, whi_ref, wlo_ref, rb_ref, rt_ref, cnt_ref):
    i = pl.program_id(0)

    @pl.when(i == 0)
    def _():
        cnt_ref[...] = jnp.zeros_like(cnt_ref)

    rt_ref[...] = _route_rows(_router_logits(h1, whi_ref, wlo_ref, rb_ref), cnt_ref, i * h1.shape[0])


def _outproj0_kernel(ys_ref, yda_ref, h_ref, w_ref, g_ref, b_ref, whi_ref, wlo_ref, rb_ref,
                     h1_ref, h1r_ref, rt_ref, cnt_ref, das_ref, *, tl):
    per = DA_WIDTH // LANES
    for b in range(NB):
        for s in range(per):
            das_ref[s, pl.ds(b, tl, stride=NB), :] = yda_ref[b, :, LANES * s:LANES * (s + 1)]
    da = jnp.concatenate([das_ref[s] for s in range(per)], axis=1).astype(BF16)
    mix = (jnp.dot(ys_ref[...], w_ref[:S5_WIDTH, :], preferred_element_type=F32)
           + jnp.dot(da, w_ref[S5_WIDTH:, :], preferred_element_type=F32))
    h1 = _layer_norm_rows(ALPHA * h_ref[...] + mix, g_ref[...], b_ref[...])
    h1_ref[...] = h1
    _to_row_tiles(h1r_ref, h1)
    _route_tile(h1, whi_ref, wlo_ref, rb_ref, rt_ref, cnt_ref)


def _outproj1_kernel(y_ref, h_ref, w_ref, g_ref, b_ref, whi_ref, wlo_ref, rb_ref,
                     h1_ref, h1r_ref, rt_ref, cnt_ref):
    mix = jnp.dot(y_ref[...], w_ref[...], preferred_element_type=F32)
    h1 = _layer_norm_rows(ALPHA * h_ref[...] + mix, g_ref[...], b_ref[...])
    h1_ref[...] = h1
    _to_row_tiles(h1r_ref, h1)
    _route_tile(h1, whi_ref, wlo_ref, rb_ref, rt_ref, cnt_ref)


def _router_weights(w_coarse, b_coarse, w_fine, b_fine):
    wf = jnp.transpose(w_fine.astype(F32), (1, 0, 2)).reshape(D_MODEL, N_EXPERTS)
    w = jnp.concatenate([w_coarse.astype(F32), wf], axis=1)
    w = jnp.pad(w, ((0, 0), (0, ROUTER_LANES - w.shape[1])))
    b = jnp.concatenate([b_coarse.astype(F32), b_fine.astype(F32).reshape(-1)])
    b = jnp.pad(b, (0, ROUTER_LANES - b.shape[0])).reshape(1, ROUTER_LANES)
    whi, wlo = _split_bf16(w)
    return whi, wlo, b


def _outproj(ys, yda, h, w_out, ln_g, ln_b, router):
    t_rows = h.shape[0]
    tm = TOK_TILE
    tl = tm // NB
    whi, wlo, rb = router
    w = w_out.astype(BF16)
    g2 = ln_g.astype(F32).reshape(1, D_MODEL)
    b2 = ln_b.astype(F32).reshape(1, D_MODEL)
    full = lambda a: pl.BlockSpec(a.shape, lambda i: (0,) * a.ndim)
    row = lambda width: pl.BlockSpec((tm, width), lambda i: (i, 0))
    common_in = [row(D_MODEL), full(w), full(g2), full(b2), full(whi), full(wlo), full(rb)]
    out_specs = [row(D_MODEL), pl.BlockSpec((tm * ROW_TILES, LANES), lambda i: (i, 0)), row(ROUTER_LANES),
                 pl.BlockSpec((NB, ROUTER_LANES), lambda i: (0, 0))]
    out_shape = [jax.ShapeDtypeStruct((t_rows, D_MODEL), F32),
                 jax.ShapeDtypeStruct((t_rows * ROW_TILES, LANES), F32),
                 jax.ShapeDtypeStruct((t_rows, ROUTER_LANES), F32),
                 jax.ShapeDtypeStruct((NB, ROUTER_LANES), F32)]
    if yda is not None:
        return pl.pallas_call(
            functools.partial(_outproj0_kernel, tl=tl),
            grid=(t_rows // tm,),
            in_specs=[row(S5_WIDTH), pl.BlockSpec((NB, tl, DA_WIDTH), lambda i: (0, i, 0))] + common_in,
            out_specs=out_specs, out_shape=out_shape,
            scratch_shapes=[pltpu.VMEM((DA_WIDTH // LANES, tm, LANES), F32)],
            compiler_params=_cparams(1),
            name="l0_outproj",
        )(ys, yda, h, w, g2, b2, whi, wlo, rb)
    return pl.pallas_call(
        _outproj1_kernel,
        grid=(t_rows // tm,),
        in_specs=[row(ys.shape[1])] + common_in,
        out_specs=out_specs, out_shape=out_shape,
        compiler_params=_cparams(1),
        name="l1_outproj",
    )(ys, h, w, g2, b2, whi, wlo, rb)


def _slots(route, counts, n_slots):
    t_rows = route.shape[0]
    experts = route[:, ROUTE_E:ROUTE_E + TOP_K_FINE].astype(jnp.int32)
    gates = route[:, ROUTE_GATE:ROUTE_GATE + TOP_K_FINE]
    rank = route[:, ROUTE_RANK:ROUTE_RANK + TOP_K_FINE].astype(jnp.int32)
    cnt = counts[0, N_GROUPS:N_GROUPS + N_EXPERTS].astype(jnp.int32)
    padded = (cnt + EXPERT_TILE - 1) // EXPERT_TILE * EXPERT_TILE
    pad_end = jnp.cumsum(padded)
    pad_start = pad_end - padded
    valid = (jnp.arange(t_rows) >= PAD_ROWS)[:, None]
    dest = jnp.where(valid, pad_start[experts] + rank, n_slots).astype(jnp.int32)
    tok = jnp.broadcast_to(jnp.arange(t_rows, dtype=jnp.int32)[:, None], dest.shape)
    slot_tok = jnp.zeros((n_slots,), jnp.int32).at[dest.reshape(-1)].set(tok.reshape(-1), mode='drop')
    nblk = n_slots // EXPERT_TILE
    blk_start = jnp.arange(nblk, dtype=jnp.int32) * EXPERT_TILE
    block_expert = jnp.minimum(jnp.sum((pad_end[None, :] <= blk_start[:, None]).astype(jnp.int32), axis=1),
                               N_EXPERTS - 1)
    n_used = (pad_end[-1] // EXPERT_TILE).astype(jnp.int32).reshape(1)
    pos = jnp.where(valid, dest, 0).reshape(-1)
    return slot_tok * ROW_TILES, block_expert, n_used, pos * ROW_TILES, gates


ROW_TILES = D_MODEL // LANES


def _to_row_tiles(dst_ref, val):
    rows = val.shape[0]
    for s in range(ROW_TILES):
        dst_ref[pl.ds(s, rows, stride=ROW_TILES), :] = val[:, LANES * s:LANES * (s + 1)]


def _from_row_tiles(src_ref, rows):
    return jnp.concatenate([src_ref[pl.ds(s, rows, stride=ROW_TILES), :] for s in range(ROW_TILES)], axis=1)


LOOKAHEAD = 2
N_BUF = LOOKAHEAD + 1


def _expert_kernel(be_ref, nused_ref, *refs):
    tok_refs = refs[:N_BUF]
    x_hbm, wg_ref, wu_ref, wd_ref, y_ref = refs[N_BUF:N_BUF + 5]
    bufs = refs[N_BUF + 5:2 * N_BUF + 5]
    sem_ref = refs[2 * N_BUF + 5]
    i = pl.program_id(0)
    n_used = nused_ref[0]
    tb = EXPERT_TILE
    turn = lax.rem(i, N_BUF)

    def gather(tok_ref, p):
        for r in range(tb):
            row = pl.multiple_of(tok_ref[0, 0, r], ROW_TILES)
            pltpu.make_async_copy(x_hbm.at[pl.ds(row, ROW_TILES), :],
                                  bufs[p].at[pl.ds(ROW_TILES * r, ROW_TILES), :],
                                  sem_ref.at[p]).start(priority=r % 2)

    def wait(p):
        pltpu.make_async_copy(x_hbm.at[pl.ds(0, tb * ROW_TILES), :], bufs[p], sem_ref.at[p]).wait()

    @pl.when(i == 0)
    def _():
        for d in range(LOOKAHEAD):
            gather(tok_refs[d], d)

    for p in range(N_BUF):
        @pl.when((turn == p) & (i < n_used))
        def _(p=p):
            wait(p)
            gather(tok_refs[LOOKAHEAD], (p + LOOKAHEAD) % N_BUF)
            x = _from_row_tiles(bufs[p], tb).astype(BF16)
            g = jnp.dot(x, wg_ref[0].astype(BF16), preferred_element_type=F32)
            u = jnp.dot(x, wu_ref[0].astype(BF16), preferred_element_type=F32)
            hid = (g * _sigmoid(g) * u).astype(BF16)
            _to_row_tiles(y_ref, jnp.dot(hid, wd_ref[0].astype(BF16), preferred_element_type=F32))

        @pl.when((turn == p) & (i >= n_used) & (i < n_used + LOOKAHEAD))
        def _(p=p):
            wait(p)

    @pl.when(i >= n_used)
    def _():
        y_ref[...] = jnp.zeros_like(y_ref)


def _experts(h1r, slot_row, block_expert, n_used, w_gate, w_up, w_down):
    n_slots = slot_row.shape[0]
    nblk = n_slots // EXPERT_TILE
    tok3 = slot_row.reshape(nblk, 1, EXPERT_TILE)
    ahead = lambda d: pl.BlockSpec((1, 1, EXPERT_TILE), lambda i, be, nu: (jnp.minimum(i + d, nblk - 1), 0, 0),
                                   memory_space=pltpu.SMEM)
    tile_buf = pltpu.VMEM((EXPERT_TILE * ROW_TILES, LANES), F32)
    grid_spec = pltpu.PrefetchScalarGridSpec(
        num_scalar_prefetch=2,
        grid=(nblk,),
        in_specs=[ahead(d) for d in range(N_BUF)] + [
            pl.BlockSpec(memory_space=pl.ANY),
            pl.BlockSpec((1, D_MODEL, D_EXPERT), lambda i, be, nu: (be[i], 0, 0)),
            pl.BlockSpec((1, D_MODEL, D_EXPERT), lambda i, be, nu: (be[i], 0, 0)),
            pl.BlockSpec((1, D_EXPERT, D_MODEL), lambda i, be, nu: (be[i], 0, 0))],
        out_specs=pl.BlockSpec((EXPERT_TILE * ROW_TILES, LANES), lambda i, be, nu: (i, 0)),
        scratch_shapes=[tile_buf] * N_BUF + [pltpu.SemaphoreType.DMA((N_BUF,))],
    )
    return pl.pallas_call(
        _expert_kernel,
        grid_spec=grid_spec,
        out_shape=jax.ShapeDtypeStruct((n_slots * ROW_TILES, LANES), F32),
        compiler_params=_cparams(1),
        name="moe_experts",
    )(block_expert, n_used, *([tok3] * N_BUF), h1r, w_gate, w_up, w_down)


def _combine_kernel(*refs, first_tile, n_tiles, to_batch_major):
    pos_refs = refs[:N_BUF]
    yb_hbm, gate_ref, h_ref, g_ref, b_ref, o_ref, ybuf_ref, sem_ref = refs[N_BUF:N_BUF + 8]
    rest = refs[N_BUF + 8:]
    i = pl.program_id(0)
    tm = COMBINE_TILE
    turn = lax.rem(i, N_BUF)

    def gather(pos_ref, p):
        for r in range(tm):
            for kk in range(TOP_K_FINE):
                row = pl.multiple_of(pos_ref[0, 0, TOP_K_FINE * r + kk], ROW_TILES)
                pltpu.make_async_copy(yb_hbm.at[pl.ds(row, ROW_TILES), :],
                                      ybuf_ref.at[p, kk, pl.ds(ROW_TILES * r, ROW_TILES), :],
                                      sem_ref.at[p]).start(priority=kk)

    def finish(p, look_ahead):
        for kk in range(TOP_K_FINE):
            pltpu.make_async_copy(yb_hbm.at[pl.ds(0, tm * ROW_TILES), :], ybuf_ref.at[p, kk],
                                  sem_ref.at[p]).wait()
        if look_ahead:
            gather(pos_refs[LOOKAHEAD], (p + LOOKAHEAD) % N_BUF)
        gts = gate_ref[...]
        moe = (_from_row_tiles(ybuf_ref.at[p, 0], tm) * gts[:, 0:1]
               + _from_row_tiles(ybuf_ref.at[p, 1], tm) * gts[:, 1:2])
        h2 = _layer_norm_rows(ALPHA * h_ref[...] + moe, g_ref[...], b_ref[...])
        if not to_batch_major:
            o_ref[...] = h2
        else:
            hs_ref = rest[0]
            tl = tm // NB
            for s in range(D_MODEL // LANES):
                hs_ref[s] = h2[:, LANES * s:LANES * (s + 1)]
            for b in range(NB):
                for s in range(D_MODEL // LANES):
                    o_ref[b, :, LANES * s:LANES * (s + 1)] = hs_ref[s, pl.ds(b, tl, stride=NB), :]

    @pl.when(i == 0)
    def _():
        for d in range(LOOKAHEAD):
            gather(pos_refs[d], d)

    for p in range(N_BUF):
        @pl.when((turn == p) & (i < n_tiles - LOOKAHEAD))
        def _(p=p):
            finish(p, True)

    @pl.when(i >= n_tiles - LOOKAHEAD)
    def _():
        finish(turn, False)


def _combine(yb, pos, gates, h1, ln_g, ln_b, *, final):
    t_rows = h1.shape[0]
    tm = COMBINE_TILE
    first_tile = (LANES * NB) // tm if final else 0
    n_tiles = t_rows // tm - first_tile
    pos3 = pos.reshape(t_rows // tm, 1, TOP_K_FINE * tm)
    g2 = ln_g.astype(F32).reshape(1, D_MODEL)
    b2 = ln_b.astype(F32).reshape(1, D_MODEL)
    full = lambda a: pl.BlockSpec(a.shape, lambda i: (0,) * a.ndim)
    last = t_rows // tm - 1
    ahead = lambda d: pl.BlockSpec((1, 1, TOP_K_FINE * tm),
                                   lambda i: (jnp.minimum(i + first_tile + d, last), 0, 0),
                                   memory_space=pltpu.SMEM)
    in_specs = [ahead(d) for d in range(N_BUF)] + [
                pl.BlockSpec(memory_space=pl.ANY),
                pl.BlockSpec((tm, TOP_K_FINE), lambda i: (i + first_tile, 0)),
                pl.BlockSpec((tm, D_MODEL), lambda i: (i + first_tile, 0)),
                full(g2), full(b2)]
    scratch = [pltpu.VMEM((N_BUF, TOP_K_FINE, tm * ROW_TILES, LANES), F32), pltpu.SemaphoreType.DMA((N_BUF,))]
    if final:
        tl = tm // NB
        seq = t_rows // NB - LANES
        out_specs = pl.BlockSpec((NB, tl, D_MODEL), lambda i: (0, i, 0))
        out_shape = jax.ShapeDtypeStruct((NB, seq, D_MODEL), F32)
        scratch.append(pltpu.VMEM((D_MODEL // LANES, tm, LANES), F32))
    else:
        out_specs = pl.BlockSpec((tm, D_MODEL), lambda i: (i, 0))
        out_shape = jax.ShapeDtypeStruct((t_rows, D_MODEL), F32)
    return pl.pallas_call(
        functools.partial(_combine_kernel, first_tile=first_tile, n_tiles=n_tiles, to_batch_major=final),
        grid=(n_tiles,),
        in_specs=in_specs, out_specs=out_specs, out_shape=out_shape,
        scratch_shapes=scratch,
        compiler_params=_cparams(1),
        name="moe_combine_final" if final else "moe_combine",
    )(*([pos3] * N_BUF), yb, gates, h1, g2, b2)


def _moe_block(h1, h1r, route, counts, w_gate, w_up, w_down, ln_g, ln_b, *, final):
    t_rows = h1.shape[0]
    n_assign = (t_rows - PAD_ROWS) * TOP_K_FINE
    nblk = -(-(n_assign + N_EXPERTS * (EXPERT_TILE - 1)) // EXPERT_TILE) + LOOKAHEAD
    n_slots = nblk * EXPERT_TILE
    slot_row, block_expert, n_used, pos, gates = _slots(route, counts, n_slots)
    yb = _experts(h1r, slot_row, block_expert, n_used, w_gate, w_up, w_down)
    return _combine(yb, pos, gates, h1, ln_g, ln_b, final=final)


def _inproj1_kernel(h_ref, w_ref, gg_ref, rec_ref):
    x = h_ref[...].astype(BF16)
    z = jnp.dot(x, w_ref[...], preferred_element_type=F32)
    gg_ref[...] = _gelu(z[:, :D_RNN]).astype(BF16)
    rec_ref[...] = z[:, D_RNN:]


def _inproj1(h, w_bf16):
    t_rows = h.shape[0]
    tm = TOK_TILE
    row = lambda width: pl.BlockSpec((tm, width), lambda i: (i, 0))
    return pl.pallas_call(
        _inproj1_kernel,
        grid=(t_rows // tm,),
        in_specs=[row(D_MODEL), pl.BlockSpec(w_bf16.shape, lambda i: (0, 0))],
        out_specs=[row(D_RNN), row(D_RNN)],
        out_shape=[jax.ShapeDtypeStruct((t_rows, D_RNN), BF16), jax.ShapeDtypeStruct((t_rows, D_RNN), F32)],
        compiler_params=_cparams(1),
        name="l1_inproj",
    )(h, w_bf16)


CONV_HALO = (CONV_WIDTH - 1) * NB


def _rglru_kernel(rec_ref, gg_ref, cw_ref, cb_ref, wax_ref, ba_ref, bx_ref, sp_ref,
                  y_ref, rp_ref, a_ref, b_ref, st_ref, *, steps):
    i = pl.program_id(0)
    rows = steps * NB

    @pl.when(i == 0)
    def _():
        st_ref[...] = jnp.zeros_like(st_ref)
        rp_ref[0:CONV_HALO, :] = jnp.zeros((CONV_HALO, D_RNN), F32)

    row = i * rows + lax.broadcasted_iota(jnp.int32, (rows, 1), 0)
    real = row >= PAD_ROWS
    rp_ref[CONV_HALO:CONV_HALO + rows, :] = jnp.where(real, rec_ref[...], 0.0)
    xc = cb_ref[...] + sum(rp_ref[NB * j:NB * j + rows, :] * cw_ref[j:j + 1, :] for j in range(CONV_WIDTH))
    tail = rp_ref[rows:rows + CONV_HALO, :]
    rp_ref[0:CONV_HALO, :] = tail

    for n in range(LRU_BLOCKS):
        cs = slice(LRU_BLOCK_W * n, LRU_BLOCK_W * (n + 1))
        xb = xc[:, cs]
        ra = jnp.dot(xb.astype(BF16), wax_ref[n], preferred_element_type=F32)
        r = _sigmoid(ra[:, :LRU_BLOCK_W] + ba_ref[:, cs])
        ig = _sigmoid(ra[:, LRU_BLOCK_W:] + bx_ref[:, cs])
        log_a = -LRU_C * r * sp_ref[:, cs]
        a = jnp.exp(log_a)
        bt = jnp.sqrt(1.0 - a * a) * (ig * xb)
        a_ref[:, cs] = a
        b_ref[:, cs] = jnp.where(real, bt, 0.0)

    def step(t, h):
        r0 = pl.multiple_of(t * NB, NB)
        hn = a_ref[pl.ds(r0, NB), :] * h + b_ref[pl.ds(r0, NB), :]
        b_ref[pl.ds(r0, NB), :] = hn
        return hn

    st_ref[...] = lax.fori_loop(0, steps, step, st_ref[...])
    y_ref[...] = (gg_ref[...].astype(F32) * b_ref[...]).astype(BF16)


def _rglru(rec, gg, conv_w, conv_b, w_a, b_a, w_x, b_x, lru_lambda):
    t_rows = rec.shape[0]
    steps = SCAN_STEPS
    rows = steps * NB
    cw = conv_w.astype(F32)
    cb = conv_b.astype(F32).reshape(1, D_RNN)
    wax = jnp.concatenate([w_a.astype(F32), w_x.astype(F32)], axis=2).astype(BF16)
    ba = b_a.astype(F32).reshape(1, D_RNN)
    bx = b_x.astype(F32).reshape(1, D_RNN)
    sp = jax.nn.softplus(-lru_lambda.astype(F32)).reshape(1, D_RNN)
    full = lambda a: pl.BlockSpec(a.shape, lambda i: (0,) * a.ndim)
    row = pl.BlockSpec((rows, D_RNN), lambda i: (i, 0))
    return pl.pallas_call(
        functools.partial(_rglru_kernel, steps=steps),
        grid=(t_rows // rows,),
        in_specs=[row, row, full(cw), full(cb), full(wax), full(ba), full(bx), full(sp)],
        out_specs=row,
        out_shape=jax.ShapeDtypeStruct((t_rows, D_RNN), BF16),
        scratch_shapes=[pltpu.VMEM((rows + CONV_HALO, D_RNN), F32), pltpu.VMEM((rows, D_RNN), F32),
                        pltpu.VMEM((rows, D_RNN), F32), pltpu.VMEM((NB, D_RNN), F32)],
        compiler_params=_cparams(1),
        name="l1_rglru",
    )(rec, gg, cw, cb, wax, ba, bx, sp)


def kernel(x, meta, l0_ln1_g, l0_ln1_b, l0_w_in, l0_s5_lambda_re, l0_s5_lambda_im, l0_s5_log_dt, l0_s5_b_re, l0_s5_b_im, l0_s5_c_re, l0_s5_c_im, l0_s5_d, l0_s5_w_glu, l0_s5_b_glu, l0_da_lq1, l0_da_lk1, l0_da_lq2, l0_da_lk2, l0_da_subln_g, l0_w_out, l0_ln2_g, l0_ln2_b, l0_moe_w_coarse, l0_moe_b_coarse, l0_moe_w_fine, l0_moe_b_fine, l0_moe_w_gate, l0_moe_w_up, l0_moe_w_down, l1_ln1_g, l1_ln1_b, l1_w_in, l1_conv_w, l1_conv_b, l1_lru_w_a, l1_lru_b_a, l1_lru_w_x, l1_lru_b_x, l1_lru_lambda, l1_w_out, l1_ln2_g, l1_ln2_b, l1_moe_w_coarse, l1_moe_b_coarse, l1_moe_w_fine, l1_moe_b_fine, l1_moe_w_gate, l1_moe_w_up, l1_moe_w_down):
    bsz, seq, _ = x.shape
    assert bsz == NB and seq % Q_TILE == 0
    dt = x.dtype
    lp = FRONT_PAD + N_META + seq
    h = jnp.concatenate([
        jnp.zeros((FRONT_PAD, NB, D_MODEL), dt),
        jnp.broadcast_to(meta.astype(dt)[:, None, :], (N_META, NB, D_MODEL)),
        jnp.transpose(x, (1, 0, 2))], axis=0).reshape(lp * NB, D_MODEL)

    col_scale = jnp.concatenate([jnp.ones((S5_WIDTH,), F32),
                                 jnp.full((DA_WIDTH,), DA_HEAD_DIM ** -0.5, F32),
                                 jnp.ones((2 * DA_WIDTH,), F32)])
    w_in0 = (l0_w_in.astype(F32) * col_scale[None, :]).astype(BF16)
    u, q, k, v = _inproj0(h, w_in0, lp)
    s5p = _s5_params(l0_s5_lambda_re, l0_s5_lambda_im, l0_s5_log_dt,
                     l0_s5_b_re, l0_s5_b_im, l0_s5_c_re, l0_s5_c_im)
    y_s5 = _s5(u, *s5p, l0_s5_d, l0_s5_w_glu, l0_s5_b_glu)
    lam_init = 0.8 - 0.6 * math.exp(-0.3 * 0)
    lam = (jnp.exp(jnp.sum(l0_da_lq1.astype(F32) * l0_da_lk1.astype(F32)))
           - jnp.exp(jnp.sum(l0_da_lq2.astype(F32) * l0_da_lk2.astype(F32))) + lam_init)
    y_da = _diffattn(q, k, v, lam, l0_da_subln_g, lam_init)
    router0 = _router_weights(l0_moe_w_coarse, l0_moe_b_coarse, l0_moe_w_fine, l0_moe_b_fine)
    h, hr, route, counts = _outproj(y_s5, y_da, h, l0_w_out, l0_ln1_g, l0_ln1_b, router0)
    h = _moe_block(h, hr, route, counts, l0_moe_w_gate, l0_moe_w_up, l0_moe_w_down, l0_ln2_g, l0_ln2_b, final=False)

    gg, rec = _inproj1(h, l1_w_in.astype(BF16))
    y = _rglru(rec, gg, l1_conv_w, l1_conv_b, l1_lru_w_a, l1_lru_b_a, l1_lru_w_x, l1_lru_b_x, l1_lru_lambda)
    router1 = _router_weights(l1_moe_w_coarse, l1_moe_b_coarse, l1_moe_w_fine, l1_moe_b_fine)
    h, hr, route, counts = _outproj(y, None, h, l1_w_out, l1_ln1_g, l1_ln1_b, router1)
    out = _moe_block(h, hr, route, counts, l1_moe_w_gate, l1_moe_w_up, l1_moe_w_down, l1_ln2_g, l1_ln2_b, final=True)
    return out.astype(dt)
```

```python
import functools
import math

import jax
import jax.numpy as jnp
from jax import lax
from jax.experimental import pallas as pl
from jax.experimental.pallas import tpu as pltpu

F32 = jnp.float32
BF16 = jnp.bfloat16

D_MODEL = 1024
DEPTH = 2
CHUNK = 64
N_META = 16
S5_WIDTH = 512
S5_GROUP = 16
S5_GROUPS = 32
S5_STATE = 64
DA_HEADS = 4
DA_HEAD_DIM = 64
DA_WIDTH = 512
D_RNN = 1280
LRU_BLOCKS = 10
LRU_BLOCK_W = 128
CONV_WIDTH = 4
LRU_C = 8.0
N_GROUPS = 4
EXPERTS_PER_GROUP = 8
N_EXPERTS = 32
TOP_K_FINE = 2
D_EXPERT = 512
ALPHA = (2 * DEPTH) ** 0.25
LN_EPS = 1e-5
NEG_INF = -1e30

NB = 8
LANES = 128
FRONT_PAD = LANES - N_META
PAD_ROWS = FRONT_PAD * NB
Q_TILE = 256
K_TILE = 512
TOK_TILE = 512
SCAN_STEPS = 32
EXPERT_TILE = 256
COMBINE_TILE = 256
ROUTER_LANES = 128
VMEM_LIMIT = 48 * 1024 * 1024
LOG2_E = math.log2(math.e)


def _cparams(n_axes, vmem=VMEM_LIMIT):
    return pltpu.CompilerParams(dimension_semantics=("arbitrary",) * n_axes,
                                vmem_limit_bytes=vmem)


def _gelu(x):
    return 0.5 * x * (1.0 + jnp.tanh(math.sqrt(2.0 / math.pi) * (x + 0.044715 * (x * x * x))))


def _sigmoid(x):
    return 1.0 / (1.0 + jnp.exp(-x))


def _layer_norm_rows(r, g, b):
    mu = jnp.mean(r, axis=-1, keepdims=True)
    c = r - mu
    var = jnp.mean(c * c, axis=-1, keepdims=True)
    return c * lax.rsqrt(var + LN_EPS) * g + b


def _split_bf16(w):
    hi = w.astype(BF16)
    lo = (w - hi.astype(F32)).astype(BF16)
    return hi, lo


def _inproj0_kernel(h_ref, w_ref, u_ref, q_ref, k_ref, v_ref, zs_ref, *, tl):
    x = h_ref[...].astype(BF16)
    z = jnp.dot(x, w_ref[...], preferred_element_type=F32)
    u_ref[...] = z[:, :S5_WIDTH]
    n_slab = 3 * DA_WIDTH // LANES
    for s in range(n_slab):
        zs_ref[s] = z[:, S5_WIDTH + LANES * s:S5_WIDTH + LANES * (s + 1)]
    dsts = (q_ref, k_ref, v_ref)
    per = DA_WIDTH // LANES
    for b in range(NB):
        for s in range(n_slab):
            c = (s % per) * LANES
            blk = zs_ref[s, pl.ds(b, tl, stride=NB), :]
            if s < per:
                blk = blk * LOG2_E
            dsts[s // per][b, :, c:c + LANES] = blk.astype(BF16)


def _inproj0(h, w_bf16, lp):
    t_rows = h.shape[0]
    tm = TOK_TILE
    tl = tm // NB
    qkv_shape = jax.ShapeDtypeStruct((NB, lp, DA_WIDTH), BF16)
    qkv_spec = pl.BlockSpec((NB, tl, DA_WIDTH), lambda i: (0, i, 0))
    return pl.pallas_call(
        functools.partial(_inproj0_kernel, tl=tl),
        grid=(t_rows // tm,),
        in_specs=[pl.BlockSpec((tm, D_MODEL), lambda i: (i, 0)),
                  pl.BlockSpec(w_bf16.shape, lambda i: (0, 0))],
        out_specs=[pl.BlockSpec((tm, S5_WIDTH), lambda i: (i, 0)), qkv_spec, qkv_spec, qkv_spec],
        out_shape=[jax.ShapeDtypeStruct((t_rows, S5_WIDTH), F32), qkv_shape, qkv_shape, qkv_shape],
        scratch_shapes=[pltpu.VMEM((3 * DA_WIDTH // LANES, tm, LANES), F32)],
        compiler_params=_cparams(1),
        name="l0_inproj",
    )(h, w_bf16)


S5_SLABS = S5_WIDTH // LANES
S5_SLAB_STATE = (S5_GROUPS // S5_SLABS) * S5_STATE
S5_NSTATE = S5_GROUPS * S5_STATE


def _s5_kernel(u_ref, bmat_ref, are_ref, aim_ref, cre_ref, cim_ref, d_ref, wglu_ref, bglu_ref,
               y_ref, hre_ref, him_ref, st_ref, *, steps):
    i = pl.program_id(0)
    rows = steps * NB

    @pl.when(i == 0)
    def _():
        st_ref[...] = jnp.zeros_like(st_ref)

    row = i * rows + lax.broadcasted_iota(jnp.int32, (rows, 1), 0)
    u = jnp.where(row >= PAD_ROWS, u_ref[...], 0.0)
    ub = u.astype(BF16)
    for s in range(S5_SLABS):
        bu = jnp.dot(ub[:, LANES * s:LANES * (s + 1)], bmat_ref[s], preferred_element_type=F32)
        hre_ref[:, S5_SLAB_STATE * s:S5_SLAB_STATE * (s + 1)] = bu[:, :S5_SLAB_STATE]
        him_ref[:, S5_SLAB_STATE * s:S5_SLAB_STATE * (s + 1)] = bu[:, S5_SLAB_STATE:]

    cw = 512
    for cg in range(S5_NSTATE // cw):
        c0 = cg * cw
        a_r = are_ref[:, c0:c0 + cw]
        a_i = aim_ref[:, c0:c0 + cw]

        def step(t, carry, c0=c0, a_r=a_r, a_i=a_i):
            sr, si = carry
            r0 = pl.multiple_of(t * NB, NB)
            br = hre_ref[pl.ds(r0, NB), c0:c0 + cw]
            bi = him_ref[pl.ds(r0, NB), c0:c0 + cw]
            nr = a_r * sr - a_i * si + br
            ni = a_r * si + a_i * sr + bi
            hre_ref[pl.ds(r0, NB), c0:c0 + cw] = nr
            him_ref[pl.ds(r0, NB), c0:c0 + cw] = ni
            return nr, ni

        sr, si = lax.fori_loop(0, steps, step, (st_ref[0, :, c0:c0 + cw], st_ref[1, :, c0:c0 + cw]))
        st_ref[0, :, c0:c0 + cw] = sr
        st_ref[1, :, c0:c0 + cw] = si

    ys = []
    for s in range(S5_SLABS):
        hr = hre_ref[:, S5_SLAB_STATE * s:S5_SLAB_STATE * (s + 1)].astype(BF16)
        hi = him_ref[:, S5_SLAB_STATE * s:S5_SLAB_STATE * (s + 1)].astype(BF16)
        ys.append(jnp.dot(hr, cre_ref[s], preferred_element_type=F32)
                  + jnp.dot(hi, cim_ref[s], preferred_element_type=F32))
    y = jnp.concatenate(ys, axis=1) + d_ref[...] * u
    y = _gelu(y)
    gate = _sigmoid(jnp.dot(y.astype(BF16), wglu_ref[...], preferred_element_type=F32) + bglu_ref[...])
    y_ref[...] = (y * gate).astype(BF16)


def _s5_params(lam_re, lam_im, log_dt, b_re, b_im, c_re, c_im):
    dt = jnp.exp(log_dt.astype(F32))[:, None]
    lr = jnp.minimum(lam_re.astype(F32), -1e-4)
    li = lam_im.astype(F32)
    mag = jnp.exp(lr * dt)
    ar = mag * jnp.cos(li * dt)
    ai = mag * jnp.sin(li * dt)
    den = lr * lr + li * li
    nr, ni = ar - 1.0, ai
    fr = ((nr * lr + ni * li) / den)[..., None]
    fi = ((ni * lr - nr * li) / den)[..., None]
    br, bi = b_re.astype(F32), b_im.astype(F32)
    bbr = fr * br - fi * bi
    bbi = fr * bi + fi * br
    gps = S5_GROUPS // S5_SLABS
    eye = jnp.eye(gps, dtype=F32)

    def in_slab(m):
        m4 = m.reshape(S5_SLABS, gps, S5_STATE, S5_GROUP)
        return jnp.einsum('sgph,gk->sghkp', m4, eye).reshape(S5_SLABS, LANES, S5_SLAB_STATE)

    def out_slab(m):
        m4 = m.reshape(S5_SLABS, gps, S5_GROUP, S5_STATE)
        return jnp.einsum('sghp,gk->sgpkh', m4, eye).reshape(S5_SLABS, S5_SLAB_STATE, LANES)

    bmat = jnp.concatenate([in_slab(bbr), in_slab(bbi)], axis=2).astype(BF16)
    cre = out_slab(c_re.astype(F32)).astype(BF16)
    cim = out_slab(-c_im.astype(F32)).astype(BF16)
    a_re = jnp.broadcast_to(ar.reshape(1, S5_NSTATE), (NB, S5_NSTATE))
    a_im = jnp.broadcast_to(ai.reshape(1, S5_NSTATE), (NB, S5_NSTATE))
    return bmat, a_re, a_im, cre, cim


def _s5(u, bmat, a_re, a_im, cre, cim, d, w_glu, b_glu):
    t_rows = u.shape[0]
    steps = SCAN_STEPS
    rows = steps * NB
    full = lambda a: pl.BlockSpec(a.shape, lambda i: (0,) * a.ndim)
    d2 = d.astype(F32).reshape(1, S5_WIDTH)
    bg2 = b_glu.astype(F32).reshape(1, S5_WIDTH)
    wg = w_glu.astype(BF16)
    return pl.pallas_call(
        functools.partial(_s5_kernel, steps=steps),
        grid=(t_rows // rows,),
        in_specs=[pl.BlockSpec((rows, S5_WIDTH), lambda i: (i, 0)),
                  full(bmat), full(a_re), full(a_im), full(cre), full(cim), full(d2), full(wg), full(bg2)],
        out_specs=pl.BlockSpec((rows, S5_WIDTH), lambda i: (i, 0)),
        out_shape=jax.ShapeDtypeStruct((t_rows, S5_WIDTH), BF16),
        scratch_shapes=[pltpu.VMEM((rows, S5_NSTATE), F32), pltpu.VMEM((rows, S5_NSTATE), F32),
                        pltpu.VMEM((2, NB, S5_NSTATE), F32)],
        compiler_params=_cparams(1),
        name="l0_s5",
    )(u, bmat, a_re, a_im, cre, cim, d2, wg, bg2)


def _diffattn_kernel(lam_ref, q_ref, k_ref, v_ref, g_ref, o_ref,
                     s_ref, qm_ref, m_ref, l_ref, a_ref, *, lam_init, lp):
    lam = lam_ref[0]
    tq, tk = Q_TILE, K_TILE
    lane = lax.broadcasted_iota(jnp.int32, (tq, LANES), 1)
    qrow = lax.broadcasted_iota(jnp.int32, (tq, tk), 0)
    kloc = lax.broadcasted_iota(jnp.int32, (tq, tk), 1)
    nt = (((1,), (1,)), ((), ()))
    bf16_rows = 16

    def key_start(j):
        return pl.multiple_of(jnp.minimum(FRONT_PAD + j * tk, lp - tk), bf16_rows)

    def chunk_of(pos):
        return jnp.right_shift(pos - CHUNK, 6)

    def lane_fold(x, op):
        r = x[:, :LANES]
        for c in range(1, x.shape[1] // LANES):
            r = op(r, x[:, LANES * c:LANES * (c + 1)])
        return r

    def q_start(i):
        return pl.multiple_of(jnp.minimum(i * tq, lp - tq), LANES)

    def n_key_tiles(i):
        return (q_start(i) + tq - FRONT_PAD + tk - 1) // tk

    def n_full_tiles(i):
        return jnp.maximum(q_start(i) - FRONT_PAD, 0) // tk


    def prep(i, par):
        q = q_ref[0, pl.ds(q_start(i), tq), :]
        zero = jnp.zeros_like(q)
        qm_ref[par, 0] = jnp.where(lane < DA_HEAD_DIM, q, zero)
        qm_ref[par, 1] = jnp.where(lane >= DA_HEAD_DIM, q, zero)
        m_ref[par] = jnp.full(m_ref.shape[1:], NEG_INF, F32)

    def scores(i, par, j, masked):
        k0 = key_start(j)
        kt = k_ref[0, pl.ds(k0, tk), :]
        if masked:
            kpos = k0 + kloc
            mask = (chunk_of(kpos) <= chunk_of(q_start(i) + qrow)) & (kpos >= FRONT_PAD + j * tk)
        for h in range(2):
            s = lax.dot_general(qm_ref[par, h], kt, nt, preferred_element_type=F32)
            if masked:
                s = jnp.where(mask, s, NEG_INF)
            s_ref[par, h, j] = s
            m_ref[par, h] = jnp.maximum(m_ref[par, h], lane_fold(s, jnp.maximum))

    def values(par, j, m):
        vt = v_ref[0, pl.ds(key_start(j), tk), :]
        for h in range(2):
            p = jnp.exp2(s_ref[par, h, j] - m[h])
            l_ref[h] += lane_fold(p, jnp.add)
            a_ref[h] += jnp.dot(p.astype(BF16), vt, preferred_element_type=F32)

    def row_max(par):
        l_ref[...] = jnp.zeros(l_ref.shape, F32)
        a_ref[...] = jnp.zeros(a_ref.shape, F32)
        return [jnp.max(m_ref[par, h], axis=-1, keepdims=True) for h in range(2)]

    def finish(i):
        l1 = jnp.sum(l_ref[0], axis=-1, keepdims=True)
        l2 = jnp.sum(l_ref[1], axis=-1, keepdims=True)
        o = a_ref[0] / l1 - lam * (a_ref[1] / l2)
        o = o * lax.rsqrt(jnp.mean(o * o, axis=-1, keepdims=True) + LN_EPS) * g_ref[...]
        o_ref[0, pl.ds(q_start(i), tq), :] = o * (1.0 - lam_init)

    def loop(lo, hi, body):
        def wrapped(j, _):
            body(j)
            return 0
        lax.fori_loop(lo, hi, wrapped, 0)

    def step(i, par):
        m = row_max(par)
        prep(i + 1, 1 - par)

        def both(j, masked):
            scores(i + 1, 1 - par, j, masked)
            values(par, j, m)

        loop(0, n_full_tiles(i + 1), functools.partial(both, masked=False))
        loop(n_full_tiles(i + 1), n_key_tiles(i), functools.partial(both, masked=True))
        loop(n_key_tiles(i), n_key_tiles(i + 1), lambda j: scores(i + 1, 1 - par, j, True))
        finish(i)

    def step_pair(i2):
        step(2 * i2, 0)
        step(2 * i2 + 1, 1)

    n_q = pl.cdiv(lp, tq)
    prep(0, 0)
    loop(0, n_key_tiles(0), lambda j: scores(0, 0, j, True))
    loop(0, (n_q - 1) // 2, step_pair)
    if (n_q - 1) % 2:
        step(n_q - 2, 0)
    last = (n_q - 1) % 2
    m = row_max(last)
    loop(0, n_key_tiles(n_q - 1), lambda j: values(last, j, m))
    finish(n_q - 1)


def _diffattn(q, k, v, lam, subln_g, lam_init):
    nb, lp, _ = q.shape
    g2 = subln_g.astype(F32).reshape(1, 2 * DA_HEAD_DIM)
    seq_spec = pl.BlockSpec((1, lp, LANES), lambda b, h: (b, 0, h))
    return pl.pallas_call(
        functools.partial(_diffattn_kernel, lam_init=lam_init, lp=lp),
        grid=(nb, DA_HEADS),
        in_specs=[pl.BlockSpec(memory_space=pltpu.SMEM), seq_spec, seq_spec, seq_spec,
                  pl.BlockSpec((1, LANES), lambda b, h: (0, 0))],
        out_specs=seq_spec,
        out_shape=jax.ShapeDtypeStruct((nb, lp, DA_WIDTH), F32),
        scratch_shapes=[pltpu.VMEM((2, 2, pl.cdiv(lp, K_TILE), Q_TILE, K_TILE), F32),
                        pltpu.VMEM((2, 2, Q_TILE, LANES), BF16),
                        pltpu.VMEM((2, 2, Q_TILE, LANES), F32), pltpu.VMEM((2, Q_TILE, LANES), F32),
                        pltpu.VMEM((2, Q_TILE, LANES), F32)],
        compiler_params=_cparams(2),
        name="l0_diffattn",
    )(lam.reshape(1), q, k, v, g2)


def _router_logits(h1, whi_ref, wlo_ref, rb_ref):
    hi = h1.astype(BF16)
    lo = (h1 - hi.astype(F32)).astype(BF16)
    return (jnp.dot(hi, whi_ref[...], preferred_element_type=F32)
            + jnp.dot(lo, whi_ref[...], preferred_element_type=F32)
            + jnp.dot(hi, wlo_ref[...], preferred_element_type=F32) + rb_ref[...])


ROUTE_E, ROUTE_GATE, ROUTE_RANK = 0, 2, 4


def _route_rows(lg, cnt_ref, first_row):
    rows = lg.shape[0]
    lane = lax.broadcasted_iota(jnp.int32, lg.shape, 1)
    lanef = lane.astype(F32)
    valid = (first_row + lax.broadcasted_iota(jnp.int32, (rows, 1), 0)) >= PAD_ROWS
    ninf = float('-inf')
    first = lambda hit: jnp.min(jnp.where(hit, lanef, float(LANES)), axis=1, keepdims=True)
    cm = jnp.where(lane < N_GROUPS, lg, ninf)
    cmax = jnp.max(cm, axis=1, keepdims=True)
    p_grp = 1.0 / jnp.sum(jnp.exp(cm - cmax), axis=1, keepdims=True)
    lo = N_GROUPS + EXPERTS_PER_GROUP * first(cm == cmax)
    fm = jnp.where((lanef >= lo) & (lanef < lo + EXPERTS_PER_GROUP), lg, ninf)
    v1 = jnp.max(fm, axis=1, keepdims=True)
    i1 = first(fm == v1)
    fm2 = jnp.where(lanef == i1, ninf, fm)
    v2 = jnp.max(fm2, axis=1, keepdims=True)
    i2 = first(fm2 == v2)
    t = jnp.exp(v2 - v1)
    g1 = p_grp / (1.0 + t)
    g2 = p_grp * t / (1.0 + t)
    oh1 = (lanef == i1) & valid
    oh2 = (lanef == i2) & valid
    m = jnp.where(oh1 | oh2, 1.0, 0.0)
    ri = lax.broadcasted_iota(jnp.int32, (rows, rows), 0)
    ci = lax.broadcasted_iota(jnp.int32, (rows, rows), 1)
    earlier = jnp.where(ci < ri, 1.0, 0.0).astype(BF16)
    before = jnp.dot(earlier, m.astype(BF16), preferred_element_type=F32) + cnt_ref[0:1, :]
    r1 = jnp.sum(jnp.where(oh1, before, 0.0), axis=1, keepdims=True)
    r2 = jnp.sum(jnp.where(oh2, before, 0.0), axis=1, keepdims=True)
    cnt_ref[...] = cnt_ref[...] + jnp.sum(m, axis=0, keepdims=True)
    zero = jnp.zeros_like(g1)
    rec = jnp.zeros(lg.shape, F32)
    for k, val in enumerate((i1 - N_GROUPS, i2 - N_GROUPS, jnp.where(valid, g1, zero),
                             jnp.where(valid, g2, zero), r1, r2)):
        rec = jnp.where(lane == k, val, rec)
    return rec


def _route_tile(h1, whi_ref, wlo_ref, rb_ref, rt_ref, cnt_ref):
    i = pl.program_id(0)

    @pl.when(i == 0)
    def _():
        cnt_ref[...] = jnp.zeros_like(cnt_ref)

    rt_ref[...] = _route_rows(_router_logits(h1, whi_ref, wlo_ref, rb_ref), cnt_ref, i * h1.shape[0])


def _outproj0_kernel(ys_ref, yda_ref, h_ref, w_ref, g_ref, b_ref, whi_ref, wlo_ref, rb_ref,
                     h1_ref, h1r_ref, rt_ref, cnt_ref, das_ref, *, tl):
    per = DA_WIDTH // LANES
    for b in range(NB):
        for s in range(per):
            das_ref[s, pl.ds(b, tl, stride=NB), :] = yda_ref[b, :, LANES * s:LANES * (s + 1)]
    da = jnp.concatenate([das_ref[s] for s in range(per)], axis=1).astype(BF16)
    mix = (jnp.dot(ys_ref[...], w_ref[:S5_WIDTH, :], preferred_element_type=F32)
           + jnp.dot(da, w_ref[S5_WIDTH:, :], preferred_element_type=F32))
    h1 = _layer_norm_rows(ALPHA * h_ref[...] + mix, g_ref[...], b_ref[...])
    h1_ref[...] = h1
    _to_row_tiles(h1r_ref, h1)
    _route_tile(h1, whi_ref, wlo_ref, rb_ref, rt_ref, cnt_ref)


def _outproj1_kernel(y_ref, h_ref, w_ref, g_ref, b_ref, whi_ref, wlo_ref, rb_ref,
                     h1_ref, h1r_ref, rt_ref, cnt_ref):
    mix = jnp.dot(y_ref[...], w_ref[...], preferred_element_type=F32)
    h1 = _layer_norm_rows(ALPHA * h_ref[...] + mix, g_ref[...], b_ref[...])
    h1_ref[...] = h1
    _to_row_tiles(h1r_ref, h1)
    _route_tile(h1, whi_ref, wlo_ref, rb_ref, rt_ref, cnt_ref)


def _router_weights(w_coarse, b_coarse, w_fine, b_fine):
    wf = jnp.transpose(w_fine.astype(F32), (1, 0, 2)).reshape(D_MODEL, N_EXPERTS)
    w = jnp.concatenate([w_coarse.astype(F32), wf], axis=1)
    w = jnp.pad(w, ((0, 0), (0, ROUTER_LANES - w.shape[1])))
    b = jnp.concatenate([b_coarse.astype(F32), b_fine.astype(F32).reshape(-1)])
    b = jnp.pad(b, (0, ROUTER_LANES - b.shape[0])).reshape(1, ROUTER_LANES)
    whi, wlo = _split_bf16(w)
    return whi, wlo, b


def _outproj(ys, yda, h, w_out, ln_g, ln_b, router):
    t_rows = h.shape[0]
    tm = TOK_TILE
    tl = tm // NB
    whi, wlo, rb = router
    w = w_out.astype(BF16)
    g2 = ln_g.astype(F32).reshape(1, D_MODEL)
    b2 = ln_b.astype(F32).reshape(1, D_MODEL)
    full = lambda a: pl.BlockSpec(a.shape, lambda i: (0,) * a.ndim)
    row = lambda width: pl.BlockSpec((tm, width), lambda i: (i, 0))
    common_in = [row(D_MODEL), full(w), full(g2), full(b2), full(whi), full(wlo), full(rb)]
    out_specs = [row(D_MODEL), pl.BlockSpec((tm * ROW_TILES, LANES), lambda i: (i, 0)), row(ROUTER_LANES),
                 pl.BlockSpec((NB, ROUTER_LANES), lambda i: (0, 0))]
    out_shape = [jax.ShapeDtypeStruct((t_rows, D_MODEL), F32),
                 jax.ShapeDtypeStruct((t_rows * ROW_TILES, LANES), F32),
                 jax.ShapeDtypeStruct((t_rows, ROUTER_LANES), F32),
                 jax.ShapeDtypeStruct((NB, ROUTER_LANES), F32)]
    if yda is not None:
        return pl.pallas_call(
            functools.partial(_outproj0_kernel, tl=tl),
            grid=(t_rows // tm,),
            in_specs=[row(S5_WIDTH), pl.BlockSpec((NB, tl, DA_WIDTH), lambda i: (0, i, 0))] + common_in,
            out_specs=out_specs, out_shape=out_shape,
            scratch_shapes=[pltpu.VMEM((DA_WIDTH // LANES, tm, LANES), F32)],
            compiler_params=_cparams(1),
            name="l0_outproj",
        )(ys, yda, h, w, g2, b2, whi, wlo, rb)
    return pl.pallas_call(
        _outproj1_kernel,
        grid=(t_rows // tm,),
        in_specs=[row(ys.shape[1])] + common_in,
        out_specs=out_specs, out_shape=out_shape,
        compiler_params=_cparams(1),
        name="l1_outproj",
    )(ys, h, w, g2, b2, whi, wlo, rb)


def _slots(route, counts, n_slots):
    t_rows = route.shape[0]
    experts = route[:, ROUTE_E:ROUTE_E + TOP_K_FINE].astype(jnp.int32)
    gates = route[:, ROUTE_GATE:ROUTE_GATE + TOP_K_FINE]
    rank = route[:, ROUTE_RANK:ROUTE_RANK + TOP_K_FINE].astype(jnp.int32)
    cnt = counts[0, N_GROUPS:N_GROUPS + N_EXPERTS].astype(jnp.int32)
    padded = (cnt + EXPERT_TILE - 1) // EXPERT_TILE * EXPERT_TILE
    pad_end = jnp.cumsum(padded)
    pad_start = pad_end - padded
    raw_start = jnp.cumsum(cnt) - cnt
    valid = (jnp.arange(t_rows) >= PAD_ROWS)[:, None]
    e_ids = jnp.arange(N_EXPERTS, dtype=jnp.int32)
    start_of = jnp.sum(jnp.where(experts[..., None] == e_ids, pad_start, 0), axis=-1)
    dest = jnp.where(valid, start_of + rank, n_slots).astype(jnp.int32)
    nblk = n_slots // EXPERT_TILE
    blk_start = jnp.arange(nblk, dtype=jnp.int32) * EXPERT_TILE
    block_expert = jnp.minimum(jnp.sum((pad_end[None, :] <= blk_start[:, None]).astype(jnp.int32), axis=1),
                               N_EXPERTS - 1)
    hit = block_expert[:, None] == e_ids[None, :]
    blk_pad_start = jnp.sum(jnp.where(hit, pad_start, 0), axis=1)
    blk_raw_start = jnp.sum(jnp.where(hit, raw_start, 0), axis=1)
    blk_cnt = jnp.sum(jnp.where(hit, cnt, 0), axis=1)
    n_used = (pad_end[-1] // EXPERT_TILE).astype(jnp.int32)
    blk_rows = jnp.where(jnp.arange(nblk) < n_used,
                         jnp.clip(blk_cnt - (blk_start - blk_pad_start), 0, EXPERT_TILE), 0)
    n_fetch = (blk_rows + GATHER_CHUNK - 1) // GATHER_CHUNK * GATHER_CHUNK
    order = jnp.argsort(dest.reshape(-1)).astype(jnp.int32)
    off = (blk_start - blk_pad_start)[:, None] + jnp.arange(EXPERT_TILE, dtype=jnp.int32)[None, :]
    src = jnp.clip(blk_raw_start[:, None] + off, 0, order.shape[0] - 1)
    slot_tok = jnp.where(off < blk_cnt[:, None], jnp.right_shift(order[src], 1), 0).reshape(-1)
    pos = jnp.where(valid, dest, 0).reshape(-1)
    return (slot_tok * ROW_TILES, block_expert, n_used.reshape(1), n_fetch.astype(jnp.int32),
            pos * ROW_TILES, gates)


ROW_TILES = D_MODEL // LANES


def _to_row_tiles(dst_ref, val):
    rows = val.shape[0]
    for s in range(ROW_TILES):
        dst_ref[pl.ds(s, rows, stride=ROW_TILES), :] = val[:, LANES * s:LANES * (s + 1)]


def _from_row_tiles(src_ref, rows):
    return jnp.concatenate([src_ref[pl.ds(s, rows, stride=ROW_TILES), :] for s in range(ROW_TILES)], axis=1)


LOOKAHEAD = 2
N_BUF = LOOKAHEAD + 1


GATHER_CHUNK = 64


def _expert_kernel(be_ref, nused_ref, nfetch_ref, *refs):
    tok_refs = refs[:N_BUF]
    x_hbm, wg_ref, wu_ref, wd_ref, y_ref = refs[N_BUF:N_BUF + 5]
    bufs = refs[N_BUF + 5:2 * N_BUF + 5]
    sem_ref = refs[2 * N_BUF + 5]
    i = pl.program_id(0)
    n_used = nused_ref[0]
    n_tiles = pl.num_programs(0)
    tb = EXPERT_TILE
    turn = lax.rem(i, N_BUF)

    def gather(tok_ref, p, tile):
        nfetch = nfetch_ref[jnp.minimum(tile, n_tiles - 1)]
        for c in range(tb // GATHER_CHUNK):
            @pl.when((tile < n_tiles) & (c * GATHER_CHUNK < nfetch))
            def _(c=c):
                for r in range(c * GATHER_CHUNK, (c + 1) * GATHER_CHUNK):
                    row = pl.multiple_of(tok_ref[0, 0, r], ROW_TILES)
                    pltpu.make_async_copy(x_hbm.at[pl.ds(row, ROW_TILES), :],
                                          bufs[p].at[pl.ds(ROW_TILES * r, ROW_TILES), :],
                                          sem_ref.at[p]).start(priority=r % 2)

    def wait(p, tile):
        rows = pl.multiple_of(nfetch_ref[tile] * ROW_TILES, ROW_TILES)
        pltpu.make_async_copy(x_hbm.at[pl.ds(0, rows), :], bufs[p].at[pl.ds(0, rows), :],
                              sem_ref.at[p]).wait()

    @pl.when(i == 0)
    def _():
        for p in range(N_BUF):
            bufs[p][...] = jnp.zeros_like(bufs[p])
        for d in range(LOOKAHEAD):
            gather(tok_refs[d], d, d)

    for p in range(N_BUF):
        @pl.when((turn == p) & (i < n_used))
        def _(p=p):
            wait(p, i)
            gather(tok_refs[LOOKAHEAD], (p + LOOKAHEAD) % N_BUF, i + LOOKAHEAD)
            x = _from_row_tiles(bufs[p], tb).astype(BF16)
            g = jnp.dot(x, wg_ref[0].astype(BF16), preferred_element_type=F32)
            u = jnp.dot(x, wu_ref[0].astype(BF16), preferred_element_type=F32)
            hid = (g * _sigmoid(g) * u).astype(BF16)
            _to_row_tiles(y_ref, jnp.dot(hid, wd_ref[0].astype(BF16), preferred_element_type=F32))

    @pl.when(i >= n_used)
    def _():
        y_ref[...] = jnp.zeros_like(y_ref)


def _experts(h1r, slot_row, block_expert, n_used, n_fetch, w_gate, w_up, w_down):
    n_slots = slot_row.shape[0]
    nblk = n_slots // EXPERT_TILE
    tok3 = slot_row.reshape(nblk, 1, EXPERT_TILE)
    ahead = lambda d: pl.BlockSpec((1, 1, EXPERT_TILE), lambda i, be, nu, nf: (jnp.minimum(i + d, nblk - 1), 0, 0),
                                   memory_space=pltpu.SMEM)
    tile_buf = pltpu.VMEM((EXPERT_TILE * ROW_TILES, LANES), F32)
    grid_spec = pltpu.PrefetchScalarGridSpec(
        num_scalar_prefetch=3,
        grid=(nblk,),
        in_specs=[ahead(d) for d in range(N_BUF)] + [
            pl.BlockSpec(memory_space=pl.ANY),
            pl.BlockSpec((1, D_MODEL, D_EXPERT), lambda i, be, nu, nf: (be[i], 0, 0)),
            pl.BlockSpec((1, D_MODEL, D_EXPERT), lambda i, be, nu, nf: (be[i], 0, 0)),
            pl.BlockSpec((1, D_EXPERT, D_MODEL), lambda i, be, nu, nf: (be[i], 0, 0))],
        out_specs=pl.BlockSpec((EXPERT_TILE * ROW_TILES, LANES), lambda i, be, nu, nf: (i, 0)),
        scratch_shapes=[tile_buf] * N_BUF + [pltpu.SemaphoreType.DMA((N_BUF,))],
    )
    return pl.pallas_call(
        _expert_kernel,
        grid_spec=grid_spec,
        out_shape=jax.ShapeDtypeStruct((n_slots * ROW_TILES, LANES), F32),
        compiler_params=_cparams(1),
        name="moe_experts",
    )(block_expert, n_used, n_fetch, *([tok3] * N_BUF), h1r, w_gate, w_up, w_down)


def _combine_kernel(*refs, first_tile, n_tiles, to_batch_major):
    pos_refs = refs[:N_BUF]
    yb_hbm, gate_ref, h_ref, g_ref, b_ref, o_ref, ybuf_ref, sem_ref = refs[N_BUF:N_BUF + 8]
    rest = refs[N_BUF + 8:]
    i = pl.program_id(0)
    tm = COMBINE_TILE
    turn = lax.rem(i, N_BUF)

    def gather(pos_ref, p):
        for r in range(tm):
            for kk in range(TOP_K_FINE):
                row = pl.multiple_of(pos_ref[0, 0, TOP_K_FINE * r + kk], ROW_TILES)
                pltpu.make_async_copy(yb_hbm.at[pl.ds(row, ROW_TILES), :],
                                      ybuf_ref.at[p, kk, pl.ds(ROW_TILES * r, ROW_TILES), :],
                                      sem_ref.at[p]).start(priority=kk)

    def finish(p, look_ahead):
        for kk in range(TOP_K_FINE):
            pltpu.make_async_copy(yb_hbm.at[pl.ds(0, tm * ROW_TILES), :], ybuf_ref.at[p, kk],
                                  sem_ref.at[p]).wait()
        if look_ahead:
            gather(pos_refs[LOOKAHEAD], (p + LOOKAHEAD) % N_BUF)
        gts = gate_ref[...]
        moe = (_from_row_tiles(ybuf_ref.at[p, 0], tm) * gts[:, 0:1]
               + _from_row_tiles(ybuf_ref.at[p, 1], tm) * gts[:, 1:2])
        h2 = _layer_norm_rows(ALPHA * h_ref[...] + moe, g_ref[...], b_ref[...])
        if not to_batch_major:
            o_ref[...] = h2
        else:
            hs_ref = rest[0]
            tl = tm // NB
            for s in range(D_MODEL // LANES):
                hs_ref[s] = h2[:, LANES * s:LANES * (s + 1)]
            for b in range(NB):
                for s in range(D_MODEL // LANES):
                    o_ref[b, :, LANES * s:LANES * (s + 1)] = hs_ref[s, pl.ds(b, tl, stride=NB), :]

    @pl.when(i == 0)
    def _():
        for d in range(LOOKAHEAD):
            gather(pos_refs[d], d)

    for p in range(N_BUF):
        @pl.when((turn == p) & (i < n_tiles - LOOKAHEAD))
        def _(p=p):
            finish(p, True)

    @pl.when(i >= n_tiles - LOOKAHEAD)
    def _():
        finish(turn, False)


def _combine(yb, pos, gates, h1, ln_g, ln_b, *, final):
    t_rows = h1.shape[0]
    tm = COMBINE_TILE
    first_tile = (LANES * NB) // tm if final else 0
    n_tiles = t_rows // tm - first_tile
    pos3 = pos.reshape(t_rows // tm, 1, TOP_K_FINE * tm)
    g2 = ln_g.astype(F32).reshape(1, D_MODEL)
    b2 = ln_b.astype(F32).reshape(1, D_MODEL)
    full = lambda a: pl.BlockSpec(a.shape, lambda i: (0,) * a.ndim)
    last = t_rows // tm - 1
    ahead = lambda d: pl.BlockSpec((1, 1, TOP_K_FINE * tm),
                                   lambda i: (jnp.minimum(i + first_tile + d, last), 0, 0),
                                   memory_space=pltpu.SMEM)
    in_specs = [ahead(d) for d in range(N_BUF)] + [
                pl.BlockSpec(memory_space=pl.ANY),
                pl.BlockSpec((tm, TOP_K_FINE), lambda i: (i + first_tile, 0)),
                pl.BlockSpec((tm, D_MODEL), lambda i: (i + first_tile, 0)),
                full(g2), full(b2)]
    scratch = [pltpu.VMEM((N_BUF, TOP_K_FINE, tm * ROW_TILES, LANES), F32), pltpu.SemaphoreType.DMA((N_BUF,))]
    if final:
        tl = tm // NB
        seq = t_rows // NB - LANES
        out_specs = pl.BlockSpec((NB, tl, D_MODEL), lambda i: (0, i, 0))
        out_shape = jax.ShapeDtypeStruct((NB, seq, D_MODEL), F32)
        scratch.append(pltpu.VMEM((D_MODEL // LANES, tm, LANES), F32))
    else:
        out_specs = pl.BlockSpec((tm, D_MODEL), lambda i: (i, 0))
        out_shape = jax.ShapeDtypeStruct((t_rows, D_MODEL), F32)
    return pl.pallas_call(
        functools.partial(_combine_kernel, first_tile=first_tile, n_tiles=n_tiles, to_batch_major=final),
        grid=(n_tiles,),
        in_specs=in_specs, out_specs=out_specs, out_shape=out_shape,
        scratch_shapes=scratch,
        compiler_params=_cparams(1),
        name="moe_combine_final" if final else "moe_combine",
    )(*([pos3] * N_BUF), yb, gates, h1, g2, b2)


def _moe_block(h1, h1r, route, counts, w_gate, w_up, w_down, ln_g, ln_b, *, final):
    t_rows = h1.shape[0]
    n_assign = (t_rows - PAD_ROWS) * TOP_K_FINE
    nblk = -(-(n_assign + N_EXPERTS * (EXPERT_TILE - 1)) // EXPERT_TILE)
    n_slots = nblk * EXPERT_TILE
    slot_row, block_expert, n_used, n_fetch, pos, gates = _slots(route, counts, n_slots)
    yb = _experts(h1r, slot_row, block_expert, n_used, n_fetch, w_gate, w_up, w_down)
    return _combine(yb, pos, gates, h1, ln_g, ln_b, final=final)


def _inproj1_kernel(h_ref, w_ref, gg_ref, rec_ref):
    x = h_ref[...].astype(BF16)
    z = jnp.dot(x, w_ref[...], preferred_element_type=F32)
    gg_ref[...] = _gelu(z[:, :D_RNN]).astype(BF16)
    rec_ref[...] = z[:, D_RNN:]


def _inproj1(h, w_bf16):
    t_rows = h.shape[0]
    tm = TOK_TILE
    row = lambda width: pl.BlockSpec((tm, width), lambda i: (i, 0))
    return pl.pallas_call(
        _inproj1_kernel,
        grid=(t_rows // tm,),
        in_specs=[row(D_MODEL), pl.BlockSpec(w_bf16.shape, lambda i: (0, 0))],
        out_specs=[row(D_RNN), row(D_RNN)],
        out_shape=[jax.ShapeDtypeStruct((t_rows, D_RNN), BF16), jax.ShapeDtypeStruct((t_rows, D_RNN), F32)],
        compiler_params=_cparams(1),
        name="l1_inproj",
    )(h, w_bf16)


CONV_HALO = (CONV_WIDTH - 1) * NB


def _rglru_kernel(rec_ref, gg_ref, cw_ref, cb_ref, wax_ref, ba_ref, bx_ref, sp_ref,
                  y_ref, rp_ref, a_ref, b_ref, st_ref, *, steps):
    i = pl.program_id(0)
    rows = steps * NB

    @pl.when(i == 0)
    def _():
        st_ref[...] = jnp.zeros_like(st_ref)
        rp_ref[0:CONV_HALO, :] = jnp.zeros((CONV_HALO, D_RNN), F32)

    row = i * rows + lax.broadcasted_iota(jnp.int32, (rows, 1), 0)
    real = row >= PAD_ROWS
    rp_ref[CONV_HALO:CONV_HALO + rows, :] = jnp.where(real, rec_ref[...], 0.0)
    xc = cb_ref[...] + sum(rp_ref[NB * j:NB * j + rows, :] * cw_ref[j:j + 1, :] for j in range(CONV_WIDTH))
    tail = rp_ref[rows:rows + CONV_HALO, :]
    rp_ref[0:CONV_HALO, :] = tail

    for n in range(LRU_BLOCKS):
        cs = slice(LRU_BLOCK_W * n, LRU_BLOCK_W * (n + 1))
        xb = xc[:, cs]
        ra = jnp.dot(xb.astype(BF16), wax_ref[n], preferred_element_type=F32)
        r = _sigmoid(ra[:, :LRU_BLOCK_W] + ba_ref[:, cs])
        ig = _sigmoid(ra[:, LRU_BLOCK_W:] + bx_ref[:, cs])
        log_a = -LRU_C * r * sp_ref[:, cs]
        a = jnp.exp(log_a)
        bt = jnp.sqrt(1.0 - a * a) * (ig * xb)
        a_ref[:, cs] = a
        b_ref[:, cs] = jnp.where(real, bt, 0.0)

    def step(t, h):
        r0 = pl.multiple_of(t * NB, NB)
        hn = a_ref[pl.ds(r0, NB), :] * h + b_ref[pl.ds(r0, NB), :]
        b_ref[pl.ds(r0, NB), :] = hn
        return hn

    st_ref[...] = lax.fori_loop(0, steps, step, st_ref[...])
    y_ref[...] = (gg_ref[...].astype(F32) * b_ref[...]).astype(BF16)


def _rglru(rec, gg, conv_w, conv_b, w_a, b_a, w_x, b_x, lru_lambda):
    t_rows = rec.shape[0]
    steps = SCAN_STEPS
    rows = steps * NB
    cw = conv_w.astype(F32)
    cb = conv_b.astype(F32).reshape(1, D_RNN)
    wax = jnp.concatenate([w_a.astype(F32), w_x.astype(F32)], axis=2).astype(BF16)
    ba = b_a.astype(F32).reshape(1, D_RNN)
    bx = b_x.astype(F32).reshape(1, D_RNN)
    sp = jax.nn.softplus(-lru_lambda.astype(F32)).reshape(1, D_RNN)
    full = lambda a: pl.BlockSpec(a.shape, lambda i: (0,) * a.ndim)
    row = pl.BlockSpec((rows, D_RNN), lambda i: (i, 0))
    return pl.pallas_call(
        functools.partial(_rglru_kernel, steps=steps),
        grid=(t_rows // rows,),
        in_specs=[row, row, full(cw), full(cb), full(wax), full(ba), full(bx), full(sp)],
        out_specs=row,
        out_shape=jax.ShapeDtypeStruct((t_rows, D_RNN), BF16),
        scratch_shapes=[pltpu.VMEM((rows + CONV_HALO, D_RNN), F32), pltpu.VMEM((rows, D_RNN), F32),
                        pltpu.VMEM((rows, D_RNN), F32), pltpu.VMEM((NB, D_RNN), F32)],
        compiler_params=_cparams(1),
        name="l1_rglru",
    )(rec, gg, cw, cb, wax, ba, bx, sp)


def kernel(x, meta, l0_ln1_g, l0_ln1_b, l0_w_in, l0_s5_lambda_re, l0_s5_lambda_im, l0_s5_log_dt, l0_s5_b_re, l0_s5_b_im, l0_s5_c_re, l0_s5_c_im, l0_s5_d, l0_s5_w_glu, l0_s5_b_glu, l0_da_lq1, l0_da_lk1, l0_da_lq2, l0_da_lk2, l0_da_subln_g, l0_w_out, l0_ln2_g, l0_ln2_b, l0_moe_w_coarse, l0_moe_b_coarse, l0_moe_w_fine, l0_moe_b_fine, l0_moe_w_gate, l0_moe_w_up, l0_moe_w_down, l1_ln1_g, l1_ln1_b, l1_w_in, l1_conv_w, l1_conv_b, l1_lru_w_a, l1_lru_b_a, l1_lru_w_x, l1_lru_b_x, l1_lru_lambda, l1_w_out, l1_ln2_g, l1_ln2_b, l1_moe_w_coarse, l1_moe_b_coarse, l1_moe_w_fine, l1_moe_b_fine, l1_moe_w_gate, l1_moe_w_up, l1_moe_w_down):
    bsz, seq, _ = x.shape
    assert bsz == NB and seq % Q_TILE == 0
    dt = x.dtype
    lp = FRONT_PAD + N_META + seq
    h = jnp.concatenate([
        jnp.zeros((FRONT_PAD, NB, D_MODEL), dt),
        jnp.broadcast_to(meta.astype(dt)[:, None, :], (N_META, NB, D_MODEL)),
        jnp.transpose(x, (1, 0, 2))], axis=0).reshape(lp * NB, D_MODEL)

    col_scale = jnp.concatenate([jnp.ones((S5_WIDTH,), F32),
                                 jnp.full((DA_WIDTH,), DA_HEAD_DIM ** -0.5, F32),
                                 jnp.ones((2 * DA_WIDTH,), F32)])
    w_in0 = (l0_w_in.astype(F32) * col_scale[None, :]).astype(BF16)
    u, q, k, v = _inproj0(h, w_in0, lp)
    s5p = _s5_params(l0_s5_lambda_re, l0_s5_lambda_im, l0_s5_log_dt,
                     l0_s5_b_re, l0_s5_b_im, l0_s5_c_re, l0_s5_c_im)
    y_s5 = _s5(u, *s5p, l0_s5_d, l0_s5_w_glu, l0_s5_b_glu)
    lam_init = 0.8 - 0.6 * math.exp(-0.3 * 0)
    lam = (jnp.exp(jnp.sum(l0_da_lq1.astype(F32) * l0_da_lk1.astype(F32)))
           - jnp.exp(jnp.sum(l0_da_lq2.astype(F32) * l0_da_lk2.astype(F32))) + lam_init)
    y_da = _diffattn(q, k, v, lam, l0_da_subln_g, lam_init)
    router0 = _router_weights(l0_moe_w_coarse, l0_moe_b_coarse, l0_moe_w_fine, l0_moe_b_fine)
    h, hr, route, counts = _outproj(y_s5, y_da, h, l0_w_out, l0_ln1_g, l0_ln1_b, router0)
    h = _moe_block(h, hr, route, counts, l0_moe_w_gate, l0_moe_w_up, l0_moe_w_down, l0_ln2_g, l0_ln2_b, final=False)

    gg, rec = _inproj1(h, l1_w_in.astype(BF16))
    y = _rglru(rec, gg, l1_conv_w, l1_conv_b, l1_lru_w_a, l1_lru_b_a, l1_lru_w_x, l1_lru_b_x, l1_lru_lambda)
    router1 = _router_weights(l1_moe_w_coarse, l1_moe_b_coarse, l1_moe_w_fine, l1_moe_b_fine)
    h, hr, route, counts = _outproj(y, None, h, l1_w_out, l1_ln1_g, l1_ln1_b, router1)
    out = _moe_block(h, hr, route, counts, l1_moe_w_gate, l1_moe_w_up, l1_moe_w_down, l1_ln2_g, l1_ln2_b, final=True)
    return out.astype(dt)
```

```python
import functools
import math

import jax
import jax.numpy as jnp
from jax import lax
from jax.experimental import pallas as pl
from jax.experimental.pallas import tpu as pltpu

F32 = jnp.float32
BF16 = jnp.bfloat16

D_MODEL = 1024
DEPTH = 2
CHUNK = 64
N_META = 16
S5_WIDTH = 512
S5_GROUP = 16
S5_GROUPS = 32
S5_STATE = 64
DA_HEADS = 4
DA_HEAD_DIM = 64
DA_WIDTH = 512
D_RNN = 1280
LRU_BLOCKS = 10
LRU_BLOCK_W = 128
CONV_WIDTH = 4
LRU_C = 8.0
N_GROUPS = 4
EXPERTS_PER_GROUP = 8
N_EXPERTS = 32
TOP_K_FINE = 2
D_EXPERT = 512
ALPHA = (2 * DEPTH) ** 0.25
LN_EPS = 1e-5
NEG_INF = -1e30

NB = 8
LANES = 128
FRONT_PAD = LANES - N_META
PAD_ROWS = FRONT_PAD * NB
Q_TILE = 256
K_TILE = 512
TOK_TILE = 512
SCAN_STEPS = 32
EXPERT_TILE = 256
COMBINE_TILE = 256
ROUTER_LANES = 128
VMEM_LIMIT = 48 * 1024 * 1024
LOG2_E = math.log2(math.e)


def _cparams(n_axes, vmem=VMEM_LIMIT):
    return pltpu.CompilerParams(dimension_semantics=("arbitrary",) * n_axes,
                                vmem_limit_bytes=vmem)


def _gelu(x):
    return 0.5 * x * (1.0 + jnp.tanh(math.sqrt(2.0 / math.pi) * (x + 0.044715 * (x * x * x))))


def _sigmoid(x):
    return 1.0 / (1.0 + jnp.exp(-x))


def _layer_norm_rows(r, g, b):
    mu = jnp.mean(r, axis=-1, keepdims=True)
    c = r - mu
    var = jnp.mean(c * c, axis=-1, keepdims=True)
    return c * lax.rsqrt(var + LN_EPS) * g + b


def _split_bf16(w):
    hi = w.astype(BF16)
    lo = (w - hi.astype(F32)).astype(BF16)
    return hi, lo


def _inproj0_kernel(h_ref, w_ref, u_ref, q_ref, k_ref, v_ref, zs_ref, *, tl):
    x = h_ref[...].astype(BF16)
    z = jnp.dot(x, w_ref[...], preferred_element_type=F32)
    u_ref[...] = z[:, :S5_WIDTH]
    n_slab = 3 * DA_WIDTH // LANES
    for s in range(n_slab):
        zs_ref[s] = z[:, S5_WIDTH + LANES * s:S5_WIDTH + LANES * (s + 1)]
    dsts = (q_ref, k_ref, v_ref)
    per = DA_WIDTH // LANES
    for b in range(NB):
        for s in range(n_slab):
            c = (s % per) * LANES
            blk = zs_ref[s, pl.ds(b, tl, stride=NB), :]
            if s < per:
                blk = blk * LOG2_E
            dsts[s // per][b, :, c:c + LANES] = blk.astype(BF16)


def _inproj0(h, w_bf16, lp):
    t_rows = h.shape[0]
    tm = TOK_TILE
    tl = tm // NB
    qkv_shape = jax.ShapeDtypeStruct((NB, lp, DA_WIDTH), BF16)
    qkv_spec = pl.BlockSpec((NB, tl, DA_WIDTH), lambda i: (0, i, 0))
    return pl.pallas_call(
        functools.partial(_inproj0_kernel, tl=tl),
        grid=(t_rows // tm,),
        in_specs=[pl.BlockSpec((tm, D_MODEL), lambda i: (i, 0)),
                  pl.BlockSpec(w_bf16.shape, lambda i: (0, 0))],
        out_specs=[pl.BlockSpec((tm, S5_WIDTH), lambda i: (i, 0)), qkv_spec, qkv_spec, qkv_spec],
        out_shape=[jax.ShapeDtypeStruct((t_rows, S5_WIDTH), F32), qkv_shape, qkv_shape, qkv_shape],
        scratch_shapes=[pltpu.VMEM((3 * DA_WIDTH // LANES, tm, LANES), F32)],
        compiler_params=_cparams(1),
        name="l0_inproj",
    )(h, w_bf16)


S5_SLABS = S5_WIDTH // LANES
S5_SLAB_STATE = (S5_GROUPS // S5_SLABS) * S5_STATE
S5_NSTATE = S5_GROUPS * S5_STATE


def _s5_kernel(u_ref, bmat_ref, are_ref, aim_ref, cre_ref, cim_ref, d_ref, wglu_ref, bglu_ref,
               y_ref, hre_ref, him_ref, st_ref, *, steps):
    i = pl.program_id(0)
    rows = steps * NB

    @pl.when(i == 0)
    def _():
        st_ref[...] = jnp.zeros_like(st_ref)

    row = i * rows + lax.broadcasted_iota(jnp.int32, (rows, 1), 0)
    u = jnp.where(row >= PAD_ROWS, u_ref[...], 0.0)
    ub = u.astype(BF16)
    for s in range(S5_SLABS):
        bu = jnp.dot(ub[:, LANES * s:LANES * (s + 1)], bmat_ref[s], preferred_element_type=F32)
        hre_ref[:, S5_SLAB_STATE * s:S5_SLAB_STATE * (s + 1)] = bu[:, :S5_SLAB_STATE]
        him_ref[:, S5_SLAB_STATE * s:S5_SLAB_STATE * (s + 1)] = bu[:, S5_SLAB_STATE:]

    cw = 512
    for cg in range(S5_NSTATE // cw):
        c0 = cg * cw
        a_r = are_ref[:, c0:c0 + cw]
        a_i = aim_ref[:, c0:c0 + cw]

        def step(t, carry, c0=c0, a_r=a_r, a_i=a_i):
            sr, si = carry
            r0 = pl.multiple_of(t * NB, NB)
            br = hre_ref[pl.ds(r0, NB), c0:c0 + cw]
            bi = him_ref[pl.ds(r0, NB), c0:c0 + cw]
            nr = a_r * sr - a_i * si + br
            ni = a_r * si + a_i * sr + bi
            hre_ref[pl.ds(r0, NB), c0:c0 + cw] = nr
            him_ref[pl.ds(r0, NB), c0:c0 + cw] = ni
            return nr, ni

        sr, si = lax.fori_loop(0, steps, step, (st_ref[0, :, c0:c0 + cw], st_ref[1, :, c0:c0 + cw]))
        st_ref[0, :, c0:c0 + cw] = sr
        st_ref[1, :, c0:c0 + cw] = si

    ys = []
    for s in range(S5_SLABS):
        hr = hre_ref[:, S5_SLAB_STATE * s:S5_SLAB_STATE * (s + 1)].astype(BF16)
        hi = him_ref[:, S5_SLAB_STATE * s:S5_SLAB_STATE * (s + 1)].astype(BF16)
        ys.append(jnp.dot(hr, cre_ref[s], preferred_element_type=F32)
                  + jnp.dot(hi, cim_ref[s], preferred_element_type=F32))
    y = jnp.concatenate(ys, axis=1) + d_ref[...] * u
    y = _gelu(y)
    gate = _sigmoid(jnp.dot(y.astype(BF16), wglu_ref[...], preferred_element_type=F32) + bglu_ref[...])
    y_ref[...] = (y * gate).astype(BF16)


def _s5_params(lam_re, lam_im, log_dt, b_re, b_im, c_re, c_im):
    dt = jnp.exp(log_dt.astype(F32))[:, None]
    lr = jnp.minimum(lam_re.astype(F32), -1e-4)
    li = lam_im.astype(F32)
    mag = jnp.exp(lr * dt)
    ar = mag * jnp.cos(li * dt)
    ai = mag * jnp.sin(li * dt)
    den = lr * lr + li * li
    nr, ni = ar - 1.0, ai
    fr = ((nr * lr + ni * li) / den)[..., None]
    fi = ((ni * lr - nr * li) / den)[..., None]
    br, bi = b_re.astype(F32), b_im.astype(F32)
    bbr = fr * br - fi * bi
    bbi = fr * bi + fi * br
    gps = S5_GROUPS // S5_SLABS
    eye = jnp.eye(gps, dtype=F32)

    def in_slab(m):
        m4 = m.reshape(S5_SLABS, gps, S5_STATE, S5_GROUP)
        return jnp.einsum('sgph,gk->sghkp', m4, eye).reshape(S5_SLABS, LANES, S5_SLAB_STATE)

    def out_slab(m):
        m4 = m.reshape(S5_SLABS, gps, S5_GROUP, S5_STATE)
        return jnp.einsum('sghp,gk->sgpkh', m4, eye).reshape(S5_SLABS, S5_SLAB_STATE, LANES)

    bmat = jnp.concatenate([in_slab(bbr), in_slab(bbi)], axis=2).astype(BF16)
    cre = out_slab(c_re.astype(F32)).astype(BF16)
    cim = out_slab(-c_im.astype(F32)).astype(BF16)
    a_re = jnp.broadcast_to(ar.reshape(1, S5_NSTATE), (NB, S5_NSTATE))
    a_im = jnp.broadcast_to(ai.reshape(1, S5_NSTATE), (NB, S5_NSTATE))
    return bmat, a_re, a_im, cre, cim


def _s5(u, bmat, a_re, a_im, cre, cim, d, w_glu, b_glu):
    t_rows = u.shape[0]
    steps = SCAN_STEPS
    rows = steps * NB
    full = lambda a: pl.BlockSpec(a.shape, lambda i: (0,) * a.ndim)
    d2 = d.astype(F32).reshape(1, S5_WIDTH)
    bg2 = b_glu.astype(F32).reshape(1, S5_WIDTH)
    wg = w_glu.astype(BF16)
    return pl.pallas_call(
        functools.partial(_s5_kernel, steps=steps),
        grid=(t_rows // rows,),
        in_specs=[pl.BlockSpec((rows, S5_WIDTH), lambda i: (i, 0)),
                  full(bmat), full(a_re), full(a_im), full(cre), full(cim), full(d2), full(wg), full(bg2)],
        out_specs=pl.BlockSpec((rows, S5_WIDTH), lambda i: (i, 0)),
        out_shape=jax.ShapeDtypeStruct((t_rows, S5_WIDTH), BF16),
        scratch_shapes=[pltpu.VMEM((rows, S5_NSTATE), F32), pltpu.VMEM((rows, S5_NSTATE), F32),
                        pltpu.VMEM((2, NB, S5_NSTATE), F32)],
        compiler_params=_cparams(1),
        name="l0_s5",
    )(u, bmat, a_re, a_im, cre, cim, d2, wg, bg2)


def _diffattn_kernel(lam_ref, q_ref, k_ref, v_ref, g_ref, o_ref,
                     s_ref, qm_ref, m_ref, a_ref, *, lam_init, lp):
    lam = lam_ref[0]
    tq, tk = Q_TILE, K_TILE
    lane = lax.broadcasted_iota(jnp.int32, (tq, LANES), 1)
    qrow = lax.broadcasted_iota(jnp.int32, (tq, tk), 0)
    kloc = lax.broadcasted_iota(jnp.int32, (tq, tk), 1)
    nt = (((1,), (1,)), ((), ()))
    bf16_rows = 16

    def key_start(j):
        return pl.multiple_of(jnp.minimum(FRONT_PAD + j * tk, lp - tk), bf16_rows)

    def chunk_of(pos):
        return jnp.right_shift(pos - CHUNK, 6)

    def lane_fold(x, op):
        r = x[:, :LANES]
        for c in range(1, x.shape[1] // LANES):
            r = op(r, x[:, LANES * c:LANES * (c + 1)])
        return r

    def q_start(i):
        return pl.multiple_of(jnp.minimum(i * tq, lp - tq), LANES)

    def n_key_tiles(i):
        return (q_start(i) + tq - FRONT_PAD + tk - 1) // tk

    def n_full_tiles(i):
        return jnp.maximum(q_start(i) - FRONT_PAD, 0) // tk


    def prep(i, par):
        q = q_ref[0, pl.ds(q_start(i), tq), :]
        zero = jnp.zeros_like(q)
        qm_ref[par, 0] = jnp.where(lane < DA_HEAD_DIM, q, zero)
        qm_ref[par, 1] = jnp.where(lane >= DA_HEAD_DIM, q, zero)
        m_ref[par] = jnp.full(m_ref.shape[1:], NEG_INF, F32)

    def scores(i, par, j, masked):
        k0 = key_start(j)
        kt = k_ref[0, pl.ds(k0, tk), :]
        if masked:
            kpos = k0 + kloc
            mask = (chunk_of(kpos) <= chunk_of(q_start(i) + qrow)) & (kpos >= FRONT_PAD + j * tk)
        for h in range(2):
            s = lax.dot_general(qm_ref[par, h], kt, nt, preferred_element_type=F32)
            if masked:
                s = jnp.where(mask, s, NEG_INF)
            s_ref[par, h, j] = s
            m_ref[par, h] = jnp.maximum(m_ref[par, h], lane_fold(s, jnp.maximum))

    ones_col = jnp.where(lax.broadcasted_iota(jnp.int32, (tk, LANES), 1) == 0, 1.0, 0.0).astype(BF16)

    def values(par, j, m):
        vt = jnp.concatenate([v_ref[0, pl.ds(key_start(j), tk), :], ones_col], axis=1)
        for h in range(2):
            p = jnp.exp2((s_ref[par, h, j] - m[h]).astype(BF16))
            a_ref[h] += jnp.dot(p, vt, preferred_element_type=F32)

    def row_max(par):
        a_ref[...] = jnp.zeros(a_ref.shape, F32)
        return [jnp.max(m_ref[par, h], axis=-1, keepdims=True) for h in range(2)]

    def finish(i):
        l1 = a_ref[0, :, LANES:LANES + 1]
        l2 = a_ref[1, :, LANES:LANES + 1]
        o = a_ref[0, :, :LANES] / l1 - lam * (a_ref[1, :, :LANES] / l2)
        o = o * lax.rsqrt(jnp.mean(o * o, axis=-1, keepdims=True) + LN_EPS) * g_ref[...]
        o_ref[0, pl.ds(q_start(i), tq), :] = o * (1.0 - lam_init)

    def loop(lo, hi, body):
        def wrapped(j, _):
            body(j)
            return 0
        lax.fori_loop(lo, hi, wrapped, 0)

    def step(i, par):
        m = row_max(par)
        prep(i + 1, 1 - par)

        def both(j, masked):
            scores(i + 1, 1 - par, j, masked)
            values(par, j, m)

        loop(0, n_full_tiles(i + 1), functools.partial(both, masked=False))
        loop(n_full_tiles(i + 1), n_key_tiles(i), functools.partial(both, masked=True))
        loop(n_key_tiles(i), n_key_tiles(i + 1), lambda j: scores(i + 1, 1 - par, j, True))
        finish(i)

    def step_pair(i2):
        step(2 * i2, 0)
        step(2 * i2 + 1, 1)

    n_q = pl.cdiv(lp, tq)
    prep(0, 0)
    loop(0, n_key_tiles(0), lambda j: scores(0, 0, j, True))
    loop(0, (n_q - 1) // 2, step_pair)
    if (n_q - 1) % 2:
        step(n_q - 2, 0)
    last = (n_q - 1) % 2
    m = row_max(last)
    loop(0, n_key_tiles(n_q - 1), lambda j: values(last, j, m))
    finish(n_q - 1)


def _diffattn(q, k, v, lam, subln_g, lam_init):
    nb, lp, _ = q.shape
    g2 = subln_g.astype(F32).reshape(1, 2 * DA_HEAD_DIM)
    seq_spec = pl.BlockSpec((1, lp, LANES), lambda b, h: (b, 0, h))
    return pl.pallas_call(
        functools.partial(_diffattn_kernel, lam_init=lam_init, lp=lp),
        grid=(nb, DA_HEADS),
        in_specs=[pl.BlockSpec(memory_space=pltpu.SMEM), seq_spec, seq_spec, seq_spec,
                  pl.BlockSpec((1, LANES), lambda b, h: (0, 0))],
        out_specs=seq_spec,
        out_shape=jax.ShapeDtypeStruct((nb, lp, DA_WIDTH), F32),
        scratch_shapes=[pltpu.VMEM((2, 2, pl.cdiv(lp, K_TILE), Q_TILE, K_TILE), F32),
                        pltpu.VMEM((2, 2, Q_TILE, LANES), BF16),
                        pltpu.VMEM((2, 2, Q_TILE, LANES), F32),
                        pltpu.VMEM((2, Q_TILE, 2 * LANES), F32)],
        compiler_params=_cparams(2),
        name="l0_diffattn",
    )(lam.reshape(1), q, k, v, g2)


def _router_logits(h1, whi_ref, wlo_ref, rb_ref):
    hi = h1.astype(BF16)
    lo = (h1 - hi.astype(F32)).astype(BF16)
    return (jnp.dot(hi, whi_ref[...], preferred_element_type=F32)
            + jnp.dot(lo, whi_ref[...], preferred_element_type=F32)
            + jnp.dot(hi, wlo_ref[...], preferred_element_type=F32) + rb_ref[...])


ROUTE_E, ROUTE_GATE, ROUTE_RANK = 0, 2, 4


def _route_rows(lg, cnt_ref, first_row):
    rows = lg.shape[0]
    lane = lax.broadcasted_iota(jnp.int32, lg.shape, 1)
    lanef = lane.astype(F32)
    valid = (first_row + lax.broadcasted_iota(jnp.int32, (rows, 1), 0)) >= PAD_ROWS
    ninf = float('-inf')
    first = lambda hit: jnp.min(jnp.where(hit, lanef, float(LANES)), axis=1, keepdims=True)
    cm = jnp.where(lane < N_GROUPS, lg, ninf)
    cmax = jnp.max(cm, axis=1, keepdims=True)
    p_grp = 1.0 / jnp.sum(jnp.exp(cm - cmax), axis=1, keepdims=True)
    lo = N_GROUPS + EXPERTS_PER_GROUP * first(cm == cmax)
    fm = jnp.where((lanef >= lo) & (lanef < lo + EXPERTS_PER_GROUP), lg, ninf)
    v1 = jnp.max(fm, axis=1, keepdims=True)
    i1 = first(fm == v1)
    fm2 = jnp.where(lanef == i1, ninf, fm)
    v2 = jnp.max(fm2, axis=1, keepdims=True)
    i2 = first(fm2 == v2)
    t = jnp.exp(v2 - v1)
    g1 = p_grp / (1.0 + t)
    g2 = p_grp * t / (1.0 + t)
    oh1 = (lanef == i1) & valid
    oh2 = (lanef == i2) & valid
    m = jnp.where(oh1 | oh2, 1.0, 0.0)
    ri = lax.broadcasted_iota(jnp.int32, (rows, rows), 0)
    ci = lax.broadcasted_iota(jnp.int32, (rows, rows), 1)
    earlier = jnp.where(ci < ri, 1.0, 0.0).astype(BF16)
    before = jnp.dot(earlier, m.astype(BF16), preferred_element_type=F32) + cnt_ref[0:1, :]
    r1 = jnp.sum(jnp.where(oh1, before, 0.0), axis=1, keepdims=True)
    r2 = jnp.sum(jnp.where(oh2, before, 0.0), axis=1, keepdims=True)
    cnt_ref[...] = cnt_ref[...] + jnp.sum(m, axis=0, keepdims=True)
    zero = jnp.zeros_like(g1)
    rec = jnp.zeros(lg.shape, F32)
    for k, val in enumerate((i1 - N_GROUPS, i2 - N_GROUPS, jnp.where(valid, g1, zero),
                             jnp.where(valid, g2, zero), r1, r2)):
        rec = jnp.where(lane == k, val, rec)
    return rec


def _route_tile(h1, whi_ref, wlo_ref, rb_ref, rt_ref, cnt_ref):
    i = pl.program_id(0)

    @pl.when(i == 0)
    def _():
        cnt_ref[...] = jnp.zeros_like(cnt_ref)

    rt_ref[...] = _route_rows(_router_logits(h1, whi_ref, wlo_ref, rb_ref), cnt_ref, i * h1.shape[0])


def _outproj0_kernel(ys_ref, yda_ref, h_ref, w_ref, g_ref, b_ref, whi_ref, wlo_ref, rb_ref,
                     h1_ref, h1r_ref, rt_ref, cnt_ref, das_ref, *, tl):
    per = DA_WIDTH // LANES
    for b in range(NB):
        for s in range(per):
            das_ref[s, pl.ds(b, tl, stride=NB), :] = yda_ref[b, :, LANES * s:LANES * (s + 1)]
    da = jnp.concatenate([das_ref[s] for s in range(per)], axis=1).astype(BF16)
    mix = (jnp.dot(ys_ref[...], w_ref[:S5_WIDTH, :], preferred_element_type=F32)
           + jnp.dot(da, w_ref[S5_WIDTH:, :], preferred_element_type=F32))
    h1 = _layer_norm_rows(ALPHA * h_ref[...] + mix, g_ref[...], b_ref[...])
    h1_ref[...] = h1
    _to_row_tiles(h1r_ref, h1)
    _route_tile(h1, whi_ref, wlo_ref, rb_ref, rt_ref, cnt_ref)


def _outproj1_kernel(y_ref, h_ref, w_ref, g_ref, b_ref, whi_ref, wlo_ref, rb_ref,
                     h1_ref, h1r_ref, rt_ref, cnt_ref):
    mix = jnp.dot(y_ref[...], w_ref[...], preferred_element_type=F32)
    h1 = _layer_norm_rows(ALPHA * h_ref[...] + mix, g_ref[...], b_ref[...])
    h1_ref[...] = h1
    _to_row_tiles(h1r_ref, h1)
    _route_tile(h1, whi_ref, wlo_ref, rb_ref, rt_ref, cnt_ref)


def _router_weights(w_coarse, b_coarse, w_fine, b_fine):
    wf = jnp.transpose(w_fine.astype(F32), (1, 0, 2)).reshape(D_MODEL, N_EXPERTS)
    w = jnp.concatenate([w_coarse.astype(F32), wf], axis=1)
    w = jnp.pad(w, ((0, 0), (0, ROUTER_LANES - w.shape[1])))
    b = jnp.concatenate([b_coarse.astype(F32), b_fine.astype(F32).reshape(-1)])
    b = jnp.pad(b, (0, ROUTER_LANES - b.shape[0])).reshape(1, ROUTER_LANES)
    whi, wlo = _split_bf16(w)
    return whi, wlo, b


def _outproj(ys, yda, h, w_out, ln_g, ln_b, router):
    t_rows = h.shape[0]
    tm = TOK_TILE
    tl = tm // NB
    whi, wlo, rb = router
    w = w_out.astype(BF16)
    g2 = ln_g.astype(F32).reshape(1, D_MODEL)
    b2 = ln_b.astype(F32).reshape(1, D_MODEL)
    full = lambda a: pl.BlockSpec(a.shape, lambda i: (0,) * a.ndim)
    row = lambda width: pl.BlockSpec((tm, width), lambda i: (i, 0))
    common_in = [row(D_MODEL), full(w), full(g2), full(b2), full(whi), full(wlo), full(rb)]
    out_specs = [row(D_MODEL), pl.BlockSpec((tm * ROW_TILES, LANES), lambda i: (i, 0)), row(ROUTER_LANES),
                 pl.BlockSpec((NB, ROUTER_LANES), lambda i: (0, 0))]
    out_shape = [jax.ShapeDtypeStruct((t_rows, D_MODEL), F32),
                 jax.ShapeDtypeStruct((t_rows * ROW_TILES, LANES), F32),
                 jax.ShapeDtypeStruct((t_rows, ROUTER_LANES), F32),
                 jax.ShapeDtypeStruct((NB, ROUTER_LANES), F32)]
    if yda is not None:
        return pl.pallas_call(
            functools.partial(_outproj0_kernel, tl=tl),
            grid=(t_rows // tm,),
            in_specs=[row(S5_WIDTH), pl.BlockSpec((NB, tl, DA_WIDTH), lambda i: (0, i, 0))] + common_in,
            out_specs=out_specs, out_shape=out_shape,
            scratch_shapes=[pltpu.VMEM((DA_WIDTH // LANES, tm, LANES), F32)],
            compiler_params=_cparams(1),
            name="l0_outproj",
        )(ys, yda, h, w, g2, b2, whi, wlo, rb)
    return pl.pallas_call(
        _outproj1_kernel,
        grid=(t_rows // tm,),
        in_specs=[row(ys.shape[1])] + common_in,
        out_specs=out_specs, out_shape=out_shape,
        compiler_params=_cparams(1),
        name="l1_outproj",
    )(ys, h, w, g2, b2, whi, wlo, rb)


def _slots(route, counts, n_slots):
    t_rows = route.shape[0]
    experts = route[:, ROUTE_E:ROUTE_E + TOP_K_FINE].astype(jnp.int32)
    gates = route[:, ROUTE_GATE:ROUTE_GATE + TOP_K_FINE]
    rank = route[:, ROUTE_RANK:ROUTE_RANK + TOP_K_FINE].astype(jnp.int32)
    cnt = counts[0, N_GROUPS:N_GROUPS + N_EXPERTS].astype(jnp.int32)
    padded = (cnt + EXPERT_TILE - 1) // EXPERT_TILE * EXPERT_TILE
    pad_end = jnp.cumsum(padded)
    pad_start = pad_end - padded
    raw_start = jnp.cumsum(cnt) - cnt
    valid = (jnp.arange(t_rows) >= PAD_ROWS)[:, None]
    e_ids = jnp.arange(N_EXPERTS, dtype=jnp.int32)
    start_of = jnp.sum(jnp.where(experts[..., None] == e_ids, pad_start, 0), axis=-1)
    dest = jnp.where(valid, start_of + rank, n_slots).astype(jnp.int32)
    nblk = n_slots // EXPERT_TILE
    blk_start = jnp.arange(nblk, dtype=jnp.int32) * EXPERT_TILE
    block_expert = jnp.minimum(jnp.sum((pad_end[None, :] <= blk_start[:, None]).astype(jnp.int32), axis=1),
                               N_EXPERTS - 1)
    hit = block_expert[:, None] == e_ids[None, :]
    blk_pad_start = jnp.sum(jnp.where(hit, pad_start, 0), axis=1)
    blk_raw_start = jnp.sum(jnp.where(hit, raw_start, 0), axis=1)
    blk_cnt = jnp.sum(jnp.where(hit, cnt, 0), axis=1)
    n_used = (pad_end[-1] // EXPERT_TILE).astype(jnp.int32)
    blk_rows = jnp.where(jnp.arange(nblk) < n_used,
                         jnp.clip(blk_cnt - (blk_start - blk_pad_start), 0, EXPERT_TILE), 0)
    n_fetch = (blk_rows + GATHER_CHUNK - 1) // GATHER_CHUNK * GATHER_CHUNK
    order = jnp.argsort(dest.reshape(-1)).astype(jnp.int32)
    off = (blk_start - blk_pad_start)[:, None] + jnp.arange(EXPERT_TILE, dtype=jnp.int32)[None, :]
    src = jnp.clip(blk_raw_start[:, None] + off, 0, order.shape[0] - 1)
    slot_tok = jnp.where(off < blk_cnt[:, None], jnp.right_shift(order[src], 1), 0).reshape(-1)
    pos = jnp.where(valid, dest, 0).reshape(-1)
    return (slot_tok * ROW_TILES, block_expert, n_used.reshape(1), n_fetch.astype(jnp.int32),
            pos * ROW_TILES, gates)


ROW_TILES = D_MODEL // LANES


def _to_row_tiles(dst_ref, val):
    rows = val.shape[0]
    for s in range(ROW_TILES):
        dst_ref[pl.ds(s, rows, stride=ROW_TILES), :] = val[:, LANES * s:LANES * (s + 1)]


def _from_row_tiles(src_ref, rows):
    return jnp.concatenate([src_ref[pl.ds(s, rows, stride=ROW_TILES), :] for s in range(ROW_TILES)], axis=1)


LOOKAHEAD = 2
N_BUF = LOOKAHEAD + 1


GATHER_CHUNK = 1


def _expert_kernel(be_ref, nused_ref, nfetch_ref, *refs):
    tok_refs = refs[:N_BUF]
    x_hbm, wg_ref, wu_ref, wd_ref, y_ref = refs[N_BUF:N_BUF + 5]
    bufs = refs[N_BUF + 5:2 * N_BUF + 5]
    sem_ref = refs[2 * N_BUF + 5]
    i = pl.program_id(0)
    n_used = nused_ref[0]
    n_tiles = pl.num_programs(0)
    tb = EXPERT_TILE
    turn = lax.rem(i, N_BUF)

    def gather(tok_ref, p, tile):
        nfetch = jnp.where(tile < n_tiles, nfetch_ref[jnp.minimum(tile, n_tiles - 1)], 0)
        for c in range(tb // GATHER_CHUNK):
            @pl.when(c * GATHER_CHUNK < nfetch)
            def _(c=c):
                for r in range(c * GATHER_CHUNK, (c + 1) * GATHER_CHUNK):
                    row = pl.multiple_of(tok_ref[0, 0, r], ROW_TILES)
                    pltpu.make_async_copy(x_hbm.at[pl.ds(row, ROW_TILES), :],
                                          bufs[p].at[pl.ds(ROW_TILES * r, ROW_TILES), :],
                                          sem_ref.at[p]).start(priority=r % 2)

    def wait(p, tile):
        rows = pl.multiple_of(nfetch_ref[tile] * ROW_TILES, ROW_TILES)
        pltpu.make_async_copy(x_hbm.at[pl.ds(0, rows), :], bufs[p].at[pl.ds(0, rows), :],
                              sem_ref.at[p]).wait()

    @pl.when(i == 0)
    def _():
        for p in range(N_BUF):
            bufs[p][...] = jnp.zeros_like(bufs[p])
        for d in range(LOOKAHEAD):
            gather(tok_refs[d], d, d)

    for p in range(N_BUF):
        @pl.when((turn == p) & (i < n_used))
        def _(p=p):
            wait(p, i)
            gather(tok_refs[LOOKAHEAD], (p + LOOKAHEAD) % N_BUF, i + LOOKAHEAD)
            x = _from_row_tiles(bufs[p], tb).astype(BF16)
            g = jnp.dot(x, wg_ref[0].astype(BF16), preferred_element_type=F32)
            u = jnp.dot(x, wu_ref[0].astype(BF16), preferred_element_type=F32)
            hid = (g * _sigmoid(g) * u).astype(BF16)
            _to_row_tiles(y_ref, jnp.dot(hid, wd_ref[0].astype(BF16), preferred_element_type=F32))

    @pl.when(i >= n_used)
    def _():
        y_ref[...] = jnp.zeros_like(y_ref)


def _experts(h1r, slot_row, block_expert, n_used, n_fetch, w_gate, w_up, w_down):
    n_slots = slot_row.shape[0]
    nblk = n_slots // EXPERT_TILE
    tok3 = slot_row.reshape(nblk, 1, EXPERT_TILE)
    ahead = lambda d: pl.BlockSpec((1, 1, EXPERT_TILE), lambda i, be, nu, nf: (jnp.minimum(i + d, nblk - 1), 0, 0),
                                   memory_space=pltpu.SMEM)
    tile_buf = pltpu.VMEM((EXPERT_TILE * ROW_TILES, LANES), F32)
    grid_spec = pltpu.PrefetchScalarGridSpec(
        num_scalar_prefetch=3,
        grid=(nblk,),
        in_specs=[ahead(d) for d in range(N_BUF)] + [
            pl.BlockSpec(memory_space=pl.ANY),
            pl.BlockSpec((1, D_MODEL, D_EXPERT), lambda i, be, nu, nf: (be[i], 0, 0)),
            pl.BlockSpec((1, D_MODEL, D_EXPERT), lambda i, be, nu, nf: (be[i], 0, 0)),
            pl.BlockSpec((1, D_EXPERT, D_MODEL), lambda i, be, nu, nf: (be[i], 0, 0))],
        out_specs=pl.BlockSpec((EXPERT_TILE * ROW_TILES, LANES), lambda i, be, nu, nf: (i, 0)),
        scratch_shapes=[tile_buf] * N_BUF + [pltpu.SemaphoreType.DMA((N_BUF,))],
    )
    return pl.pallas_call(
        _expert_kernel,
        grid_spec=grid_spec,
        out_shape=jax.ShapeDtypeStruct((n_slots * ROW_TILES, LANES), F32),
        compiler_params=_cparams(1),
        name="moe_experts",
    )(block_expert, n_used, n_fetch, *([tok3] * N_BUF), h1r, w_gate, w_up, w_down)


def _combine_kernel(*refs, first_tile, n_tiles, to_batch_major):
    pos_refs = refs[:N_BUF]
    yb_hbm, gate_ref, h_ref, g_ref, b_ref, o_ref, ybuf_ref, sem_ref = refs[N_BUF:N_BUF + 8]
    rest = refs[N_BUF + 8:]
    i = pl.program_id(0)
    tm = COMBINE_TILE
    turn = lax.rem(i, N_BUF)

    def gather(pos_ref, p):
        for r in range(tm):
            for kk in range(TOP_K_FINE):
                row = pl.multiple_of(pos_ref[0, 0, TOP_K_FINE * r + kk], ROW_TILES)
                pltpu.make_async_copy(yb_hbm.at[pl.ds(row, ROW_TILES), :],
                                      ybuf_ref.at[p, kk, pl.ds(ROW_TILES * r, ROW_TILES), :],
                                      sem_ref.at[p]).start(priority=kk)

    def finish(p, look_ahead):
        for kk in range(TOP_K_FINE):
            pltpu.make_async_copy(yb_hbm.at[pl.ds(0, tm * ROW_TILES), :], ybuf_ref.at[p, kk],
                                  sem_ref.at[p]).wait()
        if look_ahead:
            gather(pos_refs[LOOKAHEAD], (p + LOOKAHEAD) % N_BUF)
        gts = gate_ref[...]
        moe = (_from_row_tiles(ybuf_ref.at[p, 0], tm) * gts[:, 0:1]
               + _from_row_tiles(ybuf_ref.at[p, 1], tm) * gts[:, 1:2])
        h2 = _layer_norm_rows(ALPHA * h_ref[...] + moe, g_ref[...], b_ref[...])
        if not to_batch_major:
            o_ref[...] = h2
        else:
            hs_ref = rest[0]
            tl = tm // NB
            for s in range(D_MODEL // LANES):
                hs_ref[s] = h2[:, LANES * s:LANES * (s + 1)]
            for b in range(NB):
                for s in range(D_MODEL // LANES):
                    o_ref[b, :, LANES * s:LANES * (s + 1)] = hs_ref[s, pl.ds(b, tl, stride=NB), :]

    @pl.when(i == 0)
    def _():
        for d in range(LOOKAHEAD):
            gather(pos_refs[d], d)

    for p in range(N_BUF):
        @pl.when((turn == p) & (i < n_tiles - LOOKAHEAD))
        def _(p=p):
            finish(p, True)

    @pl.when(i >= n_tiles - LOOKAHEAD)
    def _():
        finish(turn, False)


def _combine(yb, pos, gates, h1, ln_g, ln_b, *, final):
    t_rows = h1.shape[0]
    tm = COMBINE_TILE
    first_tile = (LANES * NB) // tm if final else 0
    n_tiles = t_rows // tm - first_tile
    pos3 = pos.reshape(t_rows // tm, 1, TOP_K_FINE * tm)
    g2 = ln_g.astype(F32).reshape(1, D_MODEL)
    b2 = ln_b.astype(F32).reshape(1, D_MODEL)
    full = lambda a: pl.BlockSpec(a.shape, lambda i: (0,) * a.ndim)
    last = t_rows // tm - 1
    ahead = lambda d: pl.BlockSpec((1, 1, TOP_K_FINE * tm),
                                   lambda i: (jnp.minimum(i + first_tile + d, last), 0, 0),
                                   memory_space=pltpu.SMEM)
    in_specs = [ahead(d) for d in range(N_BUF)] + [
                pl.BlockSpec(memory_space=pl.ANY),
                pl.BlockSpec((tm, TOP_K_FINE), lambda i: (i + first_tile, 0)),
                pl.BlockSpec((tm, D_MODEL), lambda i: (i + first_tile, 0)),
                full(g2), full(b2)]
    scratch = [pltpu.VMEM((N_BUF, TOP_K_FINE, tm * ROW_TILES, LANES), F32), pltpu.SemaphoreType.DMA((N_BUF,))]
    if final:
        tl = tm // NB
        seq = t_rows // NB - LANES
        out_specs = pl.BlockSpec((NB, tl, D_MODEL), lambda i: (0, i, 0))
        out_shape = jax.ShapeDtypeStruct((NB, seq, D_MODEL), F32)
        scratch.append(pltpu.VMEM((D_MODEL // LANES, tm, LANES), F32))
    else:
        out_specs = pl.BlockSpec((tm, D_MODEL), lambda i: (i, 0))
        out_shape = jax.ShapeDtypeStruct((t_rows, D_MODEL), F32)
    return pl.pallas_call(
        functools.partial(_combine_kernel, first_tile=first_tile, n_tiles=n_tiles, to_batch_major=final),
        grid=(n_tiles,),
        in_specs=in_specs, out_specs=out_specs, out_shape=out_shape,
        scratch_shapes=scratch,
        compiler_params=_cparams(1),
        name="moe_combine_final" if final else "moe_combine",
    )(*([pos3] * N_BUF), yb, gates, h1, g2, b2)


def _moe_block(h1, h1r, route, counts, w_gate, w_up, w_down, ln_g, ln_b, *, final):
    t_rows = h1.shape[0]
    n_assign = (t_rows - PAD_ROWS) * TOP_K_FINE
    nblk = -(-(n_assign + N_EXPERTS * (EXPERT_TILE - 1)) // EXPERT_TILE)
    n_slots = nblk * EXPERT_TILE
    slot_row, block_expert, n_used, n_fetch, pos, gates = _slots(route, counts, n_slots)
    yb = _experts(h1r, slot_row, block_expert, n_used, n_fetch, w_gate, w_up, w_down)
    return _combine(yb, pos, gates, h1, ln_g, ln_b, final=final)


def _inproj1_kernel(h_ref, w_ref, gg_ref, rec_ref):
    x = h_ref[...].astype(BF16)
    z = jnp.dot(x, w_ref[...], preferred_element_type=F32)
    gg_ref[...] = _gelu(z[:, :D_RNN]).astype(BF16)
    rec_ref[...] = z[:, D_RNN:]


def _inproj1(h, w_bf16):
    t_rows = h.shape[0]
    tm = TOK_TILE
    row = lambda width: pl.BlockSpec((tm, width), lambda i: (i, 0))
    return pl.pallas_call(
        _inproj1_kernel,
        grid=(t_rows // tm,),
        in_specs=[row(D_MODEL), pl.BlockSpec(w_bf16.shape, lambda i: (0, 0))],
        out_specs=[row(D_RNN), row(D_RNN)],
        out_shape=[jax.ShapeDtypeStruct((t_rows, D_RNN), BF16), jax.ShapeDtypeStruct((t_rows, D_RNN), F32)],
        compiler_params=_cparams(1),
        name="l1_inproj",
    )(h, w_bf16)


CONV_HALO = (CONV_WIDTH - 1) * NB


def _rglru_kernel(rec_ref, gg_ref, cw_ref, cb_ref, wax_ref, ba_ref, bx_ref, sp_ref,
                  y_ref, rp_ref, a_ref, b_ref, st_ref, *, steps):
    i = pl.program_id(0)
    rows = steps * NB

    @pl.when(i == 0)
    def _():
        st_ref[...] = jnp.zeros_like(st_ref)
        rp_ref[0:CONV_HALO, :] = jnp.zeros((CONV_HALO, D_RNN), F32)

    row = i * rows + lax.broadcasted_iota(jnp.int32, (rows, 1), 0)
    real = row >= PAD_ROWS
    rp_ref[CONV_HALO:CONV_HALO + rows, :] = jnp.where(real, rec_ref[...], 0.0)
    xc = cb_ref[...] + sum(rp_ref[NB * j:NB * j + rows, :] * cw_ref[j:j + 1, :] for j in range(CONV_WIDTH))
    tail = rp_ref[rows:rows + CONV_HALO, :]
    rp_ref[0:CONV_HALO, :] = tail

    for n in range(LRU_BLOCKS):
        cs = slice(LRU_BLOCK_W * n, LRU_BLOCK_W * (n + 1))
        xb = xc[:, cs]
        ra = jnp.dot(xb.astype(BF16), wax_ref[n], preferred_element_type=F32)
        r = _sigmoid(ra[:, :LRU_BLOCK_W] + ba_ref[:, cs])
        ig = _sigmoid(ra[:, LRU_BLOCK_W:] + bx_ref[:, cs])
        log_a = -LRU_C * r * sp_ref[:, cs]
        a = jnp.exp(log_a)
        bt = jnp.sqrt(1.0 - a * a) * (ig * xb)
        a_ref[:, cs] = a
        b_ref[:, cs] = jnp.where(real, bt, 0.0)

    def step(t, h):
        r0 = pl.multiple_of(t * NB, NB)
        hn = a_ref[pl.ds(r0, NB), :] * h + b_ref[pl.ds(r0, NB), :]
        b_ref[pl.ds(r0, NB), :] = hn
        return hn

    st_ref[...] = lax.fori_loop(0, steps, step, st_ref[...])
    y_ref[...] = (gg_ref[...].astype(F32) * b_ref[...]).astype(BF16)


def _rglru(rec, gg, conv_w, conv_b, w_a, b_a, w_x, b_x, lru_lambda):
    t_rows = rec.shape[0]
    steps = SCAN_STEPS
    rows = steps * NB
    cw = conv_w.astype(F32)
    cb = conv_b.astype(F32).reshape(1, D_RNN)
    wax = jnp.concatenate([w_a.astype(F32), w_x.astype(F32)], axis=2).astype(BF16)
    ba = b_a.astype(F32).reshape(1, D_RNN)
    bx = b_x.astype(F32).reshape(1, D_RNN)
    sp = jax.nn.softplus(-lru_lambda.astype(F32)).reshape(1, D_RNN)
    full = lambda a: pl.BlockSpec(a.shape, lambda i: (0,) * a.ndim)
    row = pl.BlockSpec((rows, D_RNN), lambda i: (i, 0))
    return pl.pallas_call(
        functools.partial(_rglru_kernel, steps=steps),
        grid=(t_rows // rows,),
        in_specs=[row, row, full(cw), full(cb), full(wax), full(ba), full(bx), full(sp)],
        out_specs=row,
        out_shape=jax.ShapeDtypeStruct((t_rows, D_RNN), BF16),
        scratch_shapes=[pltpu.VMEM((rows + CONV_HALO, D_RNN), F32), pltpu.VMEM((rows, D_RNN), F32),
                        pltpu.VMEM((rows, D_RNN), F32), pltpu.VMEM((NB, D_RNN), F32)],
        compiler_params=_cparams(1),
        name="l1_rglru",
    )(rec, gg, cw, cb, wax, ba, bx, sp)


def kernel(x, meta, l0_ln1_g, l0_ln1_b, l0_w_in, l0_s5_lambda_re, l0_s5_lambda_im, l0_s5_log_dt, l0_s5_b_re, l0_s5_b_im, l0_s5_c_re, l0_s5_c_im, l0_s5_d, l0_s5_w_glu, l0_s5_b_glu, l0_da_lq1, l0_da_lk1, l0_da_lq2, l0_da_lk2, l0_da_subln_g, l0_w_out, l0_ln2_g, l0_ln2_b, l0_moe_w_coarse, l0_moe_b_coarse, l0_moe_w_fine, l0_moe_b_fine, l0_moe_w_gate, l0_moe_w_up, l0_moe_w_down, l1_ln1_g, l1_ln1_b, l1_w_in, l1_conv_w, l1_conv_b, l1_lru_w_a, l1_lru_b_a, l1_lru_w_x, l1_lru_b_x, l1_lru_lambda, l1_w_out, l1_ln2_g, l1_ln2_b, l1_moe_w_coarse, l1_moe_b_coarse, l1_moe_w_fine, l1_moe_b_fine, l1_moe_w_gate, l1_moe_w_up, l1_moe_w_down):
    bsz, seq, _ = x.shape
    assert bsz == NB and seq % Q_TILE == 0
    dt = x.dtype
    lp = FRONT_PAD + N_META + seq
    h = jnp.concatenate([
        jnp.zeros((FRONT_PAD, NB, D_MODEL), dt),
        jnp.broadcast_to(meta.astype(dt)[:, None, :], (N_META, NB, D_MODEL)),
        jnp.transpose(x, (1, 0, 2))], axis=0).reshape(lp * NB, D_MODEL)

    col_scale = jnp.concatenate([jnp.ones((S5_WIDTH,), F32),
                                 jnp.full((DA_WIDTH,), DA_HEAD_DIM ** -0.5, F32),
                                 jnp.ones((2 * DA_WIDTH,), F32)])
    w_in0 = (l0_w_in.astype(F32) * col_scale[None, :]).astype(BF16)
    u, q, k, v = _inproj0(h, w_in0, lp)
    s5p = _s5_params(l0_s5_lambda_re, l0_s5_lambda_im, l0_s5_log_dt,
                     l0_s5_b_re, l0_s5_b_im, l0_s5_c_re, l0_s5_c_im)
    y_s5 = _s5(u, *s5p, l0_s5_d, l0_s5_w_glu, l0_s5_b_glu)
    lam_init = 0.8 - 0.6 * math.exp(-0.3 * 0)
    lam = (jnp.exp(jnp.sum(l0_da_lq1.astype(F32) * l0_da_lk1.astype(F32)))
           - jnp.exp(jnp.sum(l0_da_lq2.astype(F32) * l0_da_lk2.astype(F32))) + lam_init)
    y_da = _diffattn(q, k, v, lam, l0_da_subln_g, lam_init)
    router0 = _router_weights(l0_moe_w_coarse, l0_moe_b_coarse, l0_moe_w_fine, l0_moe_b_fine)
    h, hr, route, counts = _outproj(y_s5, y_da, h, l0_w_out, l0_ln1_g, l0_ln1_b, router0)
    h = _moe_block(h, hr, route, counts, l0_moe_w_gate, l0_moe_w_up, l0_moe_w_down, l0_ln2_g, l0_ln2_b, final=False)

    gg, rec = _inproj1(h, l1_w_in.astype(BF16))
    y = _rglru(rec, gg, l1_conv_w, l1_conv_b, l1_lru_w_a, l1_lru_b_a, l1_lru_w_x, l1_lru_b_x, l1_lru_lambda)
    router1 = _router_weights(l1_moe_w_coarse, l1_moe_b_coarse, l1_moe_w_fine, l1_moe_b_fine)
    h, hr, route, counts = _outproj(y, None, h, l1_w_out, l1_ln1_g, l1_ln1_b, router1)
    out = _moe_block(h, hr, route, counts, l1_moe_w_gate, l1_moe_w_up, l1_moe_w_down, l1_ln2_g, l1_ln2_b, final=True)
    return out.astype(dt)
```

```python
import functools
import math

import jax
import jax.numpy as jnp
from jax import lax
from jax.experimental import pallas as pl
from jax.experimental.pallas import tpu as pltpu

F32 = jnp.float32
BF16 = jnp.bfloat16

D_MODEL = 1024
DEPTH = 2
CHUNK = 64
N_META = 16
S5_WIDTH = 512
S5_GROUP = 16
S5_GROUPS = 32
S5_STATE = 64
DA_HEADS = 4
DA_HEAD_DIM = 64
DA_WIDTH = 512
D_RNN = 1280
LRU_BLOCKS = 10
LRU_BLOCK_W = 128
CONV_WIDTH = 4
LRU_C = 8.0
N_GROUPS = 4
EXPERTS_PER_GROUP = 8
N_EXPERTS = 32
TOP_K_FINE = 2
D_EXPERT = 512
ALPHA = (2 * DEPTH) ** 0.25
LN_EPS = 1e-5
NEG_INF = -1e30

NB = 8
LANES = 128
FRONT_PAD = LANES - N_META
PAD_ROWS = FRONT_PAD * NB
Q_TILE = 256
K_TILE = 512
TOK_TILE = 512
SCAN_STEPS = 32
EXPERT_TILE = 256
COMBINE_TILE = 256
ROUTER_LANES = 128
VMEM_LIMIT = 48 * 1024 * 1024
LOG2_E = math.log2(math.e)


def _cparams(n_axes, vmem=VMEM_LIMIT):
    return pltpu.CompilerParams(dimension_semantics=("arbitrary",) * n_axes,
                                vmem_limit_bytes=vmem)


def _gelu(x):
    return 0.5 * x * (1.0 + jnp.tanh(math.sqrt(2.0 / math.pi) * (x + 0.044715 * (x * x * x))))


def _sigmoid(x):
    return 1.0 / (1.0 + jnp.exp(-x))


def _layer_norm_rows(r, g, b):
    mu = jnp.mean(r, axis=-1, keepdims=True)
    c = r - mu
    var = jnp.mean(c * c, axis=-1, keepdims=True)
    return c * lax.rsqrt(var + LN_EPS) * g + b


def _split_bf16(w):
    hi = w.astype(BF16)
    lo = (w - hi.astype(F32)).astype(BF16)
    return hi, lo


def _inproj0_kernel(head_ref, x_ref, w_ref, h_ref, u_ref, q_ref, k_ref, v_ref, zs_ref, *, tl, head_tiles):
    i = pl.program_id(0)

    @pl.when(i < head_tiles)
    def _():
        h_ref[...] = head_ref[...]

    @pl.when(i >= head_tiles)
    def _():
        for s in range(D_MODEL // LANES):
            for b in range(NB):
                zs_ref[s, pl.ds(b, tl, stride=NB), :] = x_ref[b, :, LANES * s:LANES * (s + 1)]
        for s in range(D_MODEL // LANES):
            h_ref[:, LANES * s:LANES * (s + 1)] = zs_ref[s]

    x = h_ref[...].astype(BF16)
    z = jnp.dot(x, w_ref[...], preferred_element_type=F32)
    u_ref[...] = z[:, :S5_WIDTH]
    n_slab = 3 * DA_WIDTH // LANES
    for s in range(n_slab):
        zs_ref[s] = z[:, S5_WIDTH + LANES * s:S5_WIDTH + LANES * (s + 1)]
    dsts = (q_ref, k_ref, v_ref)
    per = DA_WIDTH // LANES
    for b in range(NB):
        for s in range(n_slab):
            c = (s % per) * LANES
            blk = zs_ref[s, pl.ds(b, tl, stride=NB), :]
            if s < per:
                blk = blk * LOG2_E
            dsts[s // per][b, :, c:c + LANES] = blk.astype(BF16)


def _inproj0(x, meta, w_bf16, lp):
    t_rows = lp * NB
    tm = TOK_TILE
    tl = tm // NB
    head_tiles = LANES // tl
    head = jnp.concatenate([
        jnp.zeros((FRONT_PAD, NB, D_MODEL), x.dtype),
        jnp.broadcast_to(meta.astype(x.dtype)[:, None, :], (N_META, NB, D_MODEL))], axis=0)
    head = head.reshape(LANES * NB, D_MODEL)
    qkv_shape = jax.ShapeDtypeStruct((NB, lp, DA_WIDTH), BF16)
    qkv_spec = pl.BlockSpec((NB, tl, DA_WIDTH), lambda i: (0, i, 0))
    n_slab = max(3 * DA_WIDTH, D_MODEL) // LANES
    return pl.pallas_call(
        functools.partial(_inproj0_kernel, tl=tl, head_tiles=head_tiles),
        grid=(t_rows // tm,),
        in_specs=[pl.BlockSpec((tm, D_MODEL), lambda i: (jnp.minimum(i, head_tiles - 1), 0)),
                  pl.BlockSpec((NB, tl, D_MODEL), lambda i: (0, jnp.maximum(i - head_tiles, 0), 0)),
                  pl.BlockSpec(w_bf16.shape, lambda i: (0, 0))],
        out_specs=[pl.BlockSpec((tm, D_MODEL), lambda i: (i, 0)),
                   pl.BlockSpec((tm, S5_WIDTH), lambda i: (i, 0)), qkv_spec, qkv_spec, qkv_spec],
        out_shape=[jax.ShapeDtypeStruct((t_rows, D_MODEL), F32),
                   jax.ShapeDtypeStruct((t_rows, S5_WIDTH), F32), qkv_shape, qkv_shape, qkv_shape],
        scratch_shapes=[pltpu.VMEM((n_slab, tm, LANES), F32)],
        compiler_params=_cparams(1),
        name="l0_inproj",
    )(head, x, w_bf16)


S5_SLABS = S5_WIDTH // LANES
S5_SLAB_STATE = (S5_GROUPS // S5_SLABS) * S5_STATE
S5_NSTATE = S5_GROUPS * S5_STATE


def _s5_kernel(u_ref, bmat_ref, are_ref, aim_ref, cre_ref, cim_ref, d_ref, wglu_ref, bglu_ref,
               y_ref, hre_ref, him_ref, st_ref, *, steps):
    i = pl.program_id(0)
    rows = steps * NB

    @pl.when(i == 0)
    def _():
        st_ref[...] = jnp.zeros_like(st_ref)

    row = i * rows + lax.broadcasted_iota(jnp.int32, (rows, 1), 0)
    u = jnp.where(row >= PAD_ROWS, u_ref[...], 0.0)
    ub = u.astype(BF16)
    for s in range(S5_SLABS):
        bu = jnp.dot(ub[:, LANES * s:LANES * (s + 1)], bmat_ref[s], preferred_element_type=F32)
        hre_ref[:, S5_SLAB_STATE * s:S5_SLAB_STATE * (s + 1)] = bu[:, :S5_SLAB_STATE]
        him_ref[:, S5_SLAB_STATE * s:S5_SLAB_STATE * (s + 1)] = bu[:, S5_SLAB_STATE:]

    cw = 512
    for cg in range(S5_NSTATE // cw):
        c0 = cg * cw
        a_r = are_ref[:, c0:c0 + cw]
        a_i = aim_ref[:, c0:c0 + cw]

        def step(t, carry, c0=c0, a_r=a_r, a_i=a_i):
            sr, si = carry
            r0 = pl.multiple_of(t * NB, NB)
            br = hre_ref[pl.ds(r0, NB), c0:c0 + cw]
            bi = him_ref[pl.ds(r0, NB), c0:c0 + cw]
            nr = a_r * sr - a_i * si + br
            ni = a_r * si + a_i * sr + bi
            hre_ref[pl.ds(r0, NB), c0:c0 + cw] = nr
            him_ref[pl.ds(r0, NB), c0:c0 + cw] = ni
            return nr, ni

        sr, si = lax.fori_loop(0, steps, step, (st_ref[0, :, c0:c0 + cw], st_ref[1, :, c0:c0 + cw]))
        st_ref[0, :, c0:c0 + cw] = sr
        st_ref[1, :, c0:c0 + cw] = si

    ys = []
    for s in range(S5_SLABS):
        hr = hre_ref[:, S5_SLAB_STATE * s:S5_SLAB_STATE * (s + 1)].astype(BF16)
        hi = him_ref[:, S5_SLAB_STATE * s:S5_SLAB_STATE * (s + 1)].astype(BF16)
        ys.append(jnp.dot(hr, cre_ref[s], preferred_element_type=F32)
                  + jnp.dot(hi, cim_ref[s], preferred_element_type=F32))
    y = jnp.concatenate(ys, axis=1) + d_ref[...] * u
    y = _gelu(y)
    gate = _sigmoid(jnp.dot(y.astype(BF16), wglu_ref[...], preferred_element_type=F32) + bglu_ref[...])
    y_ref[...] = (y * gate).astype(BF16)


def _s5_params(lam_re, lam_im, log_dt, b_re, b_im, c_re, c_im):
    dt = jnp.exp(log_dt.astype(F32))[:, None]
    lr = jnp.minimum(lam_re.astype(F32), -1e-4)
    li = lam_im.astype(F32)
    mag = jnp.exp(lr * dt)
    ar = mag * jnp.cos(li * dt)
    ai = mag * jnp.sin(li * dt)
    den = lr * lr + li * li
    nr, ni = ar - 1.0, ai
    fr = ((nr * lr + ni * li) / den)[..., None]
    fi = ((ni * lr - nr * li) / den)[..., None]
    br, bi = b_re.astype(F32), b_im.astype(F32)
    bbr = fr * br - fi * bi
    bbi = fr * bi + fi * br
    gps = S5_GROUPS // S5_SLABS
    eye = jnp.eye(gps, dtype=F32)

    def in_slab(m):
        m4 = m.reshape(S5_SLABS, gps, S5_STATE, S5_GROUP)
        return jnp.einsum('sgph,gk->sghkp', m4, eye).reshape(S5_SLABS, LANES, S5_SLAB_STATE)

    def out_slab(m):
        m4 = m.reshape(S5_SLABS, gps, S5_GROUP, S5_STATE)
        return jnp.einsum('sghp,gk->sgpkh', m4, eye).reshape(S5_SLABS, S5_SLAB_STATE, LANES)

    bmat = jnp.concatenate([in_slab(bbr), in_slab(bbi)], axis=2).astype(BF16)
    cre = out_slab(c_re.astype(F32)).astype(BF16)
    cim = out_slab(-c_im.astype(F32)).astype(BF16)
    a_re = jnp.broadcast_to(ar.reshape(1, S5_NSTATE), (NB, S5_NSTATE))
    a_im = jnp.broadcast_to(ai.reshape(1, S5_NSTATE), (NB, S5_NSTATE))
    return bmat, a_re, a_im, cre, cim


def _s5(u, bmat, a_re, a_im, cre, cim, d, w_glu, b_glu):
    t_rows = u.shape[0]
    steps = SCAN_STEPS
    rows = steps * NB
    full = lambda a: pl.BlockSpec(a.shape, lambda i: (0,) * a.ndim)
    d2 = d.astype(F32).reshape(1, S5_WIDTH)
    bg2 = b_glu.astype(F32).reshape(1, S5_WIDTH)
    wg = w_glu.astype(BF16)
    return pl.pallas_call(
        functools.partial(_s5_kernel, steps=steps),
        grid=(t_rows // rows,),
        in_specs=[pl.BlockSpec((rows, S5_WIDTH), lambda i: (i, 0)),
                  full(bmat), full(a_re), full(a_im), full(cre), full(cim), full(d2), full(wg), full(bg2)],
        out_specs=pl.BlockSpec((rows, S5_WIDTH), lambda i: (i, 0)),
        out_shape=jax.ShapeDtypeStruct((t_rows, S5_WIDTH), BF16),
        scratch_shapes=[pltpu.VMEM((rows, S5_NSTATE), F32), pltpu.VMEM((rows, S5_NSTATE), F32),
                        pltpu.VMEM((2, NB, S5_NSTATE), F32)],
        compiler_params=_cparams(1),
        name="l0_s5",
    )(u, bmat, a_re, a_im, cre, cim, d2, wg, bg2)


def _diffattn_kernel(lam_ref, q_ref, k_ref, v_ref, g_ref, o_ref,
                     s_ref, qm_ref, m_ref, a_ref, *, lam_init, lp):
    lam = lam_ref[0]
    tq, tk = Q_TILE, K_TILE
    lane = lax.broadcasted_iota(jnp.int32, (tq, LANES), 1)
    qrow = lax.broadcasted_iota(jnp.int32, (tq, tk), 0)
    kloc = lax.broadcasted_iota(jnp.int32, (tq, tk), 1)
    nt = (((1,), (1,)), ((), ()))
    bf16_rows = 16

    def key_start(j):
        return pl.multiple_of(jnp.minimum(FRONT_PAD + j * tk, lp - tk), bf16_rows)

    def chunk_of(pos):
        return jnp.right_shift(pos - CHUNK, 6)

    def lane_fold(x, op):
        r = x[:, :LANES]
        for c in range(1, x.shape[1] // LANES):
            r = op(r, x[:, LANES * c:LANES * (c + 1)])
        return r

    def q_start(i):
        return pl.multiple_of(jnp.minimum(i * tq, lp - tq), LANES)

    def n_key_tiles(i):
        return (q_start(i) + tq - FRONT_PAD + tk - 1) // tk

    def n_full_tiles(i):
        return jnp.maximum(q_start(i) - FRONT_PAD, 0) // tk


    def prep(i, par):
        q = q_ref[0, pl.ds(q_start(i), tq), :]
        zero = jnp.zeros_like(q)
        qm_ref[par, 0] = jnp.where(lane < DA_HEAD_DIM, q, zero)
        qm_ref[par, 1] = jnp.where(lane >= DA_HEAD_DIM, q, zero)
        m_ref[par] = jnp.full(m_ref.shape[1:], NEG_INF, F32)

    def scores(i, par, j, masked):
        k0 = key_start(j)
        kt = k_ref[0, pl.ds(k0, tk), :]
        if masked:
            kpos = k0 + kloc
            mask = (chunk_of(kpos) <= chunk_of(q_start(i) + qrow)) & (kpos >= FRONT_PAD + j * tk)
        for h in range(2):
            s = lax.dot_general(qm_ref[par, h], kt, nt, preferred_element_type=F32)
            if masked:
                s = jnp.where(mask, s, NEG_INF)
            s_ref[par, h, j] = s
            m_ref[par, h] = jnp.maximum(m_ref[par, h], lane_fold(s, jnp.maximum))

    ones_col = jnp.where(lax.broadcasted_iota(jnp.int32, (tk, LANES), 1) == 0, 1.0, 0.0).astype(BF16)

    def values(par, j, m):
        vt = jnp.concatenate([v_ref[0, pl.ds(key_start(j), tk), :], ones_col], axis=1)
        for h in range(2):
            p = jnp.exp2((s_ref[par, h, j] - m[h]).astype(BF16))
            a_ref[h] += jnp.dot(p, vt, preferred_element_type=F32)

    def row_max(par):
        a_ref[...] = jnp.zeros(a_ref.shape, F32)
        return [jnp.max(m_ref[par, h], axis=-1, keepdims=True) for h in range(2)]

    def finish(i):
        l1 = a_ref[0, :, LANES:LANES + 1]
        l2 = a_ref[1, :, LANES:LANES + 1]
        o = a_ref[0, :, :LANES] / l1 - lam * (a_ref[1, :, :LANES] / l2)
        o = o * lax.rsqrt(jnp.mean(o * o, axis=-1, keepdims=True) + LN_EPS) * g_ref[...]
        o_ref[0, pl.ds(q_start(i), tq), :] = o * (1.0 - lam_init)

    def loop(lo, hi, body):
        def wrapped(j, _):
            body(j)
            return 0
        lax.fori_loop(lo, hi, wrapped, 0)

    def step(i, par):
        m = row_max(par)
        prep(i + 1, 1 - par)

        def both(j, masked):
            scores(i + 1, 1 - par, j, masked)
            values(par, j, m)

        loop(0, n_full_tiles(i + 1), functools.partial(both, masked=False))
        loop(n_full_tiles(i + 1), n_key_tiles(i), functools.partial(both, masked=True))
        loop(n_key_tiles(i), n_key_tiles(i + 1), lambda j: scores(i + 1, 1 - par, j, True))
        finish(i)

    def step_pair(i2):
        step(2 * i2, 0)
        step(2 * i2 + 1, 1)

    n_q = pl.cdiv(lp, tq)
    prep(0, 0)
    loop(0, n_key_tiles(0), lambda j: scores(0, 0, j, True))
    loop(0, (n_q - 1) // 2, step_pair)
    if (n_q - 1) % 2:
        step(n_q - 2, 0)
    last = (n_q - 1) % 2
    m = row_max(last)
    loop(0, n_key_tiles(n_q - 1), lambda j: values(last, j, m))
    finish(n_q - 1)


def _diffattn(q, k, v, lam, subln_g, lam_init):
    nb, lp, _ = q.shape
    g2 = subln_g.astype(F32).reshape(1, 2 * DA_HEAD_DIM)
    seq_spec = pl.BlockSpec((1, lp, LANES), lambda b, h: (b, 0, h))
    return pl.pallas_call(
        functools.partial(_diffattn_kernel, lam_init=lam_init, lp=lp),
        grid=(nb, DA_HEADS),
        in_specs=[pl.BlockSpec(memory_space=pltpu.SMEM), seq_spec, seq_spec, seq_spec,
                  pl.BlockSpec((1, LANES), lambda b, h: (0, 0))],
        out_specs=seq_spec,
        out_shape=jax.ShapeDtypeStruct((nb, lp, DA_WIDTH), F32),
        scratch_shapes=[pltpu.VMEM((2, 2, pl.cdiv(lp, K_TILE), Q_TILE, K_TILE), F32),
                        pltpu.VMEM((2, 2, Q_TILE, LANES), BF16),
                        pltpu.VMEM((2, 2, Q_TILE, LANES), F32),
                        pltpu.VMEM((2, Q_TILE, 2 * LANES), F32)],
        compiler_params=_cparams(2),
        name="l0_diffattn",
    )(lam.reshape(1), q, k, v, g2)


def _router_logits(h1, whi_ref, wlo_ref, rb_ref):
    hi = h1.astype(BF16)
    lo = (h1 - hi.astype(F32)).astype(BF16)
    return (jnp.dot(hi, whi_ref[...], preferred_element_type=F32)
            + jnp.dot(lo, whi_ref[...], preferred_element_type=F32)
            + jnp.dot(hi, wlo_ref[...], preferred_element_type=F32) + rb_ref[...])


ROUTE_E, ROUTE_GATE, ROUTE_RANK = 0, 2, 4


def _route_rows(lg, cnt_ref, first_row):
    rows = lg.shape[0]
    lane = lax.broadcasted_iota(jnp.int32, lg.shape, 1)
    lanef = lane.astype(F32)
    valid = (first_row + lax.broadcasted_iota(jnp.int32, (rows, 1), 0)) >= PAD_ROWS
    ninf = float('-inf')
    first = lambda hit: jnp.min(jnp.where(hit, lanef, float(LANES)), axis=1, keepdims=True)
    cm = jnp.where(lane < N_GROUPS, lg, ninf)
    cmax = jnp.max(cm, axis=1, keepdims=True)
    p_grp = 1.0 / jnp.sum(jnp.exp(cm - cmax), axis=1, keepdims=True)
    lo = N_GROUPS + EXPERTS_PER_GROUP * first(cm == cmax)
    fm = jnp.where((lanef >= lo) & (lanef < lo + EXPERTS_PER_GROUP), lg, ninf)
    v1 = jnp.max(fm, axis=1, keepdims=True)
    i1 = first(fm == v1)
    fm2 = jnp.where(lanef == i1, ninf, fm)
    v2 = jnp.max(fm2, axis=1, keepdims=True)
    i2 = first(fm2 == v2)
    t = jnp.exp(v2 - v1)
    g1 = p_grp / (1.0 + t)
    g2 = p_grp * t / (1.0 + t)
    oh1 = (lanef == i1) & valid
    oh2 = (lanef == i2) & valid
    m = jnp.where(oh1 | oh2, 1.0, 0.0)
    ri = lax.broadcasted_iota(jnp.int32, (rows, rows), 0)
    ci = lax.broadcasted_iota(jnp.int32, (rows, rows), 1)
    earlier = jnp.where(ci < ri, 1.0, 0.0).astype(BF16)
    before = jnp.dot(earlier, m.astype(BF16), preferred_element_type=F32) + cnt_ref[0:1, :]
    r1 = jnp.sum(jnp.where(oh1, before, 0.0), axis=1, keepdims=True)
    r2 = jnp.sum(jnp.where(oh2, before, 0.0), axis=1, keepdims=True)
    cnt_ref[...] = cnt_ref[...] + jnp.sum(m, axis=0, keepdims=True)
    zero = jnp.zeros_like(g1)
    rec = jnp.zeros(lg.shape, F32)
    for k, val in enumerate((i1 - N_GROUPS, i2 - N_GROUPS, jnp.where(valid, g1, zero),
                             jnp.where(valid, g2, zero), r1, r2)):
        rec = jnp.where(lane == k, val, rec)
    return rec


def _route_tile(h1, whi_ref, wlo_ref, rb_ref, rt_ref, cnt_ref):
    i = pl.program_id(0)

    @pl.when(i == 0)
    def _():
        cnt_ref[...] = jnp.zeros_like(cnt_ref)

    rt_ref[...] = _route_rows(_router_logits(h1, whi_ref, wlo_ref, rb_ref), cnt_ref, i * h1.shape[0])


def _outproj0_kernel(ys_ref, yda_ref, h_ref, w_ref, g_ref, b_ref, whi_ref, wlo_ref, rb_ref,
                     h1_ref, h1r_ref, rt_ref, cnt_ref, das_ref, *, tl):
    per = DA_WIDTH // LANES
    for b in range(NB):
        for s in range(per):
            das_ref[s, pl.ds(b, tl, stride=NB), :] = yda_ref[b, :, LANES * s:LANES * (s + 1)]
    da = jnp.concatenate([das_ref[s] for s in range(per)], axis=1).astype(BF16)
    mix = (jnp.dot(ys_ref[...], w_ref[:S5_WIDTH, :], preferred_element_type=F32)
           + jnp.dot(da, w_ref[S5_WIDTH:, :], preferred_element_type=F32))
    h1 = _layer_norm_rows(ALPHA * h_ref[...] + mix, g_ref[...], b_ref[...])
    h1_ref[...] = h1
    _to_row_tiles(h1r_ref, h1)
    _route_tile(h1, whi_ref, wlo_ref, rb_ref, rt_ref, cnt_ref)


def _outproj1_kernel(y_ref, h_ref, w_ref, g_ref, b_ref, whi_ref, wlo_ref, rb_ref,
                     h1_ref, h1r_ref, rt_ref, cnt_ref):
    mix = jnp.dot(y_ref[...], w_ref[...], preferred_element_type=F32)
    h1 = _layer_norm_rows(ALPHA * h_ref[...] + mix, g_ref[...], b_ref[...])
    h1_ref[...] = h1
    _to_row_tiles(h1r_ref, h1)
    _route_tile(h1, whi_ref, wlo_ref, rb_ref, rt_ref, cnt_ref)


def _router_weights(w_coarse, b_coarse, w_fine, b_fine):
    wf = jnp.transpose(w_fine.astype(F32), (1, 0, 2)).reshape(D_MODEL, N_EXPERTS)
    w = jnp.concatenate([w_coarse.astype(F32), wf], axis=1)
    w = jnp.pad(w, ((0, 0), (0, ROUTER_LANES - w.shape[1])))
    b = jnp.concatenate([b_coarse.astype(F32), b_fine.astype(F32).reshape(-1)])
    b = jnp.pad(b, (0, ROUTER_LANES - b.shape[0])).reshape(1, ROUTER_LANES)
    whi, wlo = _split_bf16(w)
    return whi, wlo, b


def _outproj(ys, yda, h, w_out, ln_g, ln_b, router):
    t_rows = h.shape[0]
    tm = TOK_TILE
    tl = tm // NB
    whi, wlo, rb = router
    w = w_out.astype(BF16)
    g2 = ln_g.astype(F32).reshape(1, D_MODEL)
    b2 = ln_b.astype(F32).reshape(1, D_MODEL)
    full = lambda a: pl.BlockSpec(a.shape, lambda i: (0,) * a.ndim)
    row = lambda width: pl.BlockSpec((tm, width), lambda i: (i, 0))
    common_in = [row(D_MODEL), full(w), full(g2), full(b2), full(whi), full(wlo), full(rb)]
    out_specs = [row(D_MODEL), pl.BlockSpec((tm * ROW_TILES, LANES), lambda i: (i, 0)), row(ROUTER_LANES),
                 pl.BlockSpec((NB, ROUTER_LANES), lambda i: (0, 0))]
    out_shape = [jax.ShapeDtypeStruct((t_rows, D_MODEL), F32),
                 jax.ShapeDtypeStruct((t_rows * ROW_TILES, LANES), F32),
                 jax.ShapeDtypeStruct((t_rows, ROUTER_LANES), F32),
                 jax.ShapeDtypeStruct((NB, ROUTER_LANES), F32)]
    if yda is not None:
        return pl.pallas_call(
            functools.partial(_outproj0_kernel, tl=tl),
            grid=(t_rows // tm,),
            in_specs=[row(S5_WIDTH), pl.BlockSpec((NB, tl, DA_WIDTH), lambda i: (0, i, 0))] + common_in,
            out_specs=out_specs, out_shape=out_shape,
            scratch_shapes=[pltpu.VMEM((DA_WIDTH // LANES, tm, LANES), F32)],
            compiler_params=_cparams(1),
            name="l0_outproj",
        )(ys, yda, h, w, g2, b2, whi, wlo, rb)
    return pl.pallas_call(
        _outproj1_kernel,
        grid=(t_rows // tm,),
        in_specs=[row(ys.shape[1])] + common_in,
        out_specs=out_specs, out_shape=out_shape,
        compiler_params=_cparams(1),
        name="l1_outproj",
    )(ys, h, w, g2, b2, whi, wlo, rb)


def _slots(route, counts, n_slots):
    t_rows = route.shape[0]
    experts = route[:, ROUTE_E:ROUTE_E + TOP_K_FINE].astype(jnp.int32)
    gates = route[:, ROUTE_GATE:ROUTE_GATE + TOP_K_FINE]
    rank = route[:, ROUTE_RANK:ROUTE_RANK + TOP_K_FINE].astype(jnp.int32)
    cnt = counts[0, N_GROUPS:N_GROUPS + N_EXPERTS].astype(jnp.int32)
    padded = (cnt + EXPERT_TILE - 1) // EXPERT_TILE * EXPERT_TILE
    pad_end = jnp.cumsum(padded)
    pad_start = pad_end - padded
    raw_start = jnp.cumsum(cnt) - cnt
    valid = (jnp.arange(t_rows) >= PAD_ROWS)[:, None]
    e_ids = jnp.arange(N_EXPERTS, dtype=jnp.int32)
    start_of = jnp.sum(jnp.where(experts[..., None] == e_ids, pad_start, 0), axis=-1)
    dest = jnp.where(valid, start_of + rank, n_slots).astype(jnp.int32)
    nblk = n_slots // EXPERT_TILE
    blk_start = jnp.arange(nblk, dtype=jnp.int32) * EXPERT_TILE
    block_expert = jnp.minimum(jnp.sum((pad_end[None, :] <= blk_start[:, None]).astype(jnp.int32), axis=1),
                               N_EXPERTS - 1)
    hit = block_expert[:, None] == e_ids[None, :]
    blk_pad_start = jnp.sum(jnp.where(hit, pad_start, 0), axis=1)
    blk_raw_start = jnp.sum(jnp.where(hit, raw_start, 0), axis=1)
    blk_cnt = jnp.sum(jnp.where(hit, cnt, 0), axis=1)
    n_used = (pad_end[-1] // EXPERT_TILE).astype(jnp.int32)
    blk_rows = jnp.where(jnp.arange(nblk) < n_used,
                         jnp.clip(blk_cnt - (blk_start - blk_pad_start), 0, EXPERT_TILE), 0)
    n_fetch = (blk_rows + GATHER_CHUNK - 1) // GATHER_CHUNK * GATHER_CHUNK
    order = jnp.argsort(dest.reshape(-1)).astype(jnp.int32)
    off = (blk_start - blk_pad_start)[:, None] + jnp.arange(EXPERT_TILE, dtype=jnp.int32)[None, :]
    src = jnp.clip(blk_raw_start[:, None] + off, 0, order.shape[0] - 1)
    slot_tok = jnp.where(off < blk_cnt[:, None], jnp.right_shift(order[src], 1), 0).reshape(-1)
    pos = jnp.where(valid, dest, 0).reshape(-1)
    return (slot_tok * ROW_TILES, block_expert, n_used.reshape(1), n_fetch.astype(jnp.int32),
            pos * ROW_TILES, gates)


ROW_TILES = D_MODEL // LANES


def _to_row_tiles(dst_ref, val):
    rows = val.shape[0]
    for s in range(ROW_TILES):
        dst_ref[pl.ds(s, rows, stride=ROW_TILES), :] = val[:, LANES * s:LANES * (s + 1)]


def _from_row_tiles(src_ref, rows):
    return jnp.concatenate([src_ref[pl.ds(s, rows, stride=ROW_TILES), :] for s in range(ROW_TILES)], axis=1)


LOOKAHEAD = 2
N_BUF = LOOKAHEAD + 1


GATHER_CHUNK = 1


def _expert_kernel(be_ref, nused_ref, nfetch_ref, *refs):
    tok_refs = refs[:N_BUF]
    x_hbm, wg_ref, wu_ref, wd_ref, y_ref = refs[N_BUF:N_BUF + 5]
    bufs = refs[N_BUF + 5:2 * N_BUF + 5]
    sem_ref, wgb_ref, wub_ref, wdb_ref = refs[2 * N_BUF + 5:2 * N_BUF + 9]
    i = pl.program_id(0)
    n_used = nused_ref[0]
    n_tiles = pl.num_programs(0)
    tb = EXPERT_TILE
    turn = lax.rem(i, N_BUF)

    def gather(tok_ref, p, tile):
        nfetch = jnp.where(tile < n_tiles, nfetch_ref[jnp.minimum(tile, n_tiles - 1)], 0)
        for c in range(tb // GATHER_CHUNK):
            @pl.when(c * GATHER_CHUNK < nfetch)
            def _(c=c):
                for r in range(c * GATHER_CHUNK, (c + 1) * GATHER_CHUNK):
                    row = pl.multiple_of(tok_ref[0, 0, r], ROW_TILES)
                    pltpu.make_async_copy(x_hbm.at[pl.ds(row, ROW_TILES), :],
                                          bufs[p].at[pl.ds(ROW_TILES * r, ROW_TILES), :],
                                          sem_ref.at[p]).start(priority=r % 2)

    def wait(p, tile):
        rows = pl.multiple_of(nfetch_ref[tile] * ROW_TILES, ROW_TILES)
        pltpu.make_async_copy(x_hbm.at[pl.ds(0, rows), :], bufs[p].at[pl.ds(0, rows), :],
                              sem_ref.at[p]).wait()

    @pl.when(i == 0)
    def _():
        for p in range(N_BUF):
            bufs[p][...] = jnp.zeros_like(bufs[p])
        for d in range(LOOKAHEAD):
            gather(tok_refs[d], d, d)

    @pl.when((i < n_used) & ((i == 0) | (be_ref[i] != be_ref[jnp.maximum(i - 1, 0)])))
    def _():
        wgb_ref[...] = wg_ref[0].astype(BF16)
        wub_ref[...] = wu_ref[0].astype(BF16)
        wdb_ref[...] = wd_ref[0].astype(BF16)

    for p in range(N_BUF):
        @pl.when((turn == p) & (i < n_used))
        def _(p=p):
            wait(p, i)
            gather(tok_refs[LOOKAHEAD], (p + LOOKAHEAD) % N_BUF, i + LOOKAHEAD)
            x = _from_row_tiles(bufs[p], tb).astype(BF16)
            g = jnp.dot(x, wgb_ref[...], preferred_element_type=F32)
            u = jnp.dot(x, wub_ref[...], preferred_element_type=F32)
            hid = (g * _sigmoid(g) * u).astype(BF16)
            _to_row_tiles(y_ref, jnp.dot(hid, wdb_ref[...], preferred_element_type=F32))

    @pl.when(i >= n_used)
    def _():
        y_ref[...] = jnp.zeros_like(y_ref)


def _experts(h1r, slot_row, block_expert, n_used, n_fetch, w_gate, w_up, w_down):
    n_slots = slot_row.shape[0]
    nblk = n_slots // EXPERT_TILE
    tok3 = slot_row.reshape(nblk, 1, EXPERT_TILE)
    ahead = lambda d: pl.BlockSpec((1, 1, EXPERT_TILE), lambda i, be, nu, nf: (jnp.minimum(i + d, nblk - 1), 0, 0),
                                   memory_space=pltpu.SMEM)
    tile_buf = pltpu.VMEM((EXPERT_TILE * ROW_TILES, LANES), F32)
    grid_spec = pltpu.PrefetchScalarGridSpec(
        num_scalar_prefetch=3,
        grid=(nblk,),
        in_specs=[ahead(d) for d in range(N_BUF)] + [
            pl.BlockSpec(memory_space=pl.ANY),
            pl.BlockSpec((1, D_MODEL, D_EXPERT), lambda i, be, nu, nf: (be[i], 0, 0)),
            pl.BlockSpec((1, D_MODEL, D_EXPERT), lambda i, be, nu, nf: (be[i], 0, 0)),
            pl.BlockSpec((1, D_EXPERT, D_MODEL), lambda i, be, nu, nf: (be[i], 0, 0))],
        out_specs=pl.BlockSpec((EXPERT_TILE * ROW_TILES, LANES), lambda i, be, nu, nf: (i, 0)),
        scratch_shapes=[tile_buf] * N_BUF + [
            pltpu.SemaphoreType.DMA((N_BUF,)),
            pltpu.VMEM((D_MODEL, D_EXPERT), BF16), pltpu.VMEM((D_MODEL, D_EXPERT), BF16),
            pltpu.VMEM((D_EXPERT, D_MODEL), BF16)],
    )
    return pl.pallas_call(
        _expert_kernel,
        grid_spec=grid_spec,
        out_shape=jax.ShapeDtypeStruct((n_slots * ROW_TILES, LANES), F32),
        compiler_params=_cparams(1),
        name="moe_experts",
    )(block_expert, n_used, n_fetch, *([tok3] * N_BUF), h1r, w_gate, w_up, w_down)


def _combine_kernel(*refs, first_tile, n_tiles, to_batch_major):
    pos_refs = refs[:N_BUF]
    yb_hbm, gate_ref, h_ref, g_ref, b_ref, o_ref, ybuf_ref, sem_ref = refs[N_BUF:N_BUF + 8]
    rest = refs[N_BUF + 8:]
    i = pl.program_id(0)
    tm = COMBINE_TILE
    turn = lax.rem(i, N_BUF)

    def gather(pos_ref, p):
        for r in range(tm):
            for kk in range(TOP_K_FINE):
                row = pl.multiple_of(pos_ref[0, 0, TOP_K_FINE * r + kk], ROW_TILES)
                pltpu.make_async_copy(yb_hbm.at[pl.ds(row, ROW_TILES), :],
                                      ybuf_ref.at[p, kk, pl.ds(ROW_TILES * r, ROW_TILES), :],
                                      sem_ref.at[p]).start(priority=kk)

    def finish(p, look_ahead):
        for kk in range(TOP_K_FINE):
            pltpu.make_async_copy(yb_hbm.at[pl.ds(0, tm * ROW_TILES), :], ybuf_ref.at[p, kk],
                                  sem_ref.at[p]).wait()
        if look_ahead:
            gather(pos_refs[LOOKAHEAD], (p + LOOKAHEAD) % N_BUF)
        gts = gate_ref[...]
        moe = (_from_row_tiles(ybuf_ref.at[p, 0], tm) * gts[:, 0:1]
               + _from_row_tiles(ybuf_ref.at[p, 1], tm) * gts[:, 1:2])
        h2 = _layer_norm_rows(ALPHA * h_ref[...] + moe, g_ref[...], b_ref[...])
        if not to_batch_major:
            o_ref[...] = h2
        else:
            hs_ref = rest[0]
            tl = tm // NB
            for s in range(D_MODEL // LANES):
                hs_ref[s] = h2[:, LANES * s:LANES * (s + 1)]
            for b in range(NB):
                for s in range(D_MODEL // LANES):
                    o_ref[b, :, LANES * s:LANES * (s + 1)] = hs_ref[s, pl.ds(b, tl, stride=NB), :]

    @pl.when(i == 0)
    def _():
        for d in range(LOOKAHEAD):
            gather(pos_refs[d], d)

    for p in range(N_BUF):
        @pl.when((turn == p) & (i < n_tiles - LOOKAHEAD))
        def _(p=p):
            finish(p, True)

    @pl.when(i >= n_tiles - LOOKAHEAD)
    def _():
        finish(turn, False)


def _combine(yb, pos, gates, h1, ln_g, ln_b, *, final):
    t_rows = h1.shape[0]
    tm = COMBINE_TILE
    first_tile = (LANES * NB) // tm if final else 0
    n_tiles = t_rows // tm - first_tile
    pos3 = pos.reshape(t_rows // tm, 1, TOP_K_FINE * tm)
    g2 = ln_g.astype(F32).reshape(1, D_MODEL)
    b2 = ln_b.astype(F32).reshape(1, D_MODEL)
    full = lambda a: pl.BlockSpec(a.shape, lambda i: (0,) * a.ndim)
    last = t_rows // tm - 1
    ahead = lambda d: pl.BlockSpec((1, 1, TOP_K_FINE * tm),
                                   lambda i: (jnp.minimum(i + first_tile + d, last), 0, 0),
                                   memory_space=pltpu.SMEM)
    in_specs = [ahead(d) for d in range(N_BUF)] + [
                pl.BlockSpec(memory_space=pl.ANY),
                pl.BlockSpec((tm, TOP_K_FINE), lambda i: (i + first_tile, 0)),
                pl.BlockSpec((tm, D_MODEL), lambda i: (i + first_tile, 0)),
                full(g2), full(b2)]
    scratch = [pltpu.VMEM((N_BUF, TOP_K_FINE, tm * ROW_TILES, LANES), F32), pltpu.SemaphoreType.DMA((N_BUF,))]
    if final:
        tl = tm // NB
        seq = t_rows // NB - LANES
        out_specs = pl.BlockSpec((NB, tl, D_MODEL), lambda i: (0, i, 0))
        out_shape = jax.ShapeDtypeStruct((NB, seq, D_MODEL), F32)
        scratch.append(pltpu.VMEM((D_MODEL // LANES, tm, LANES), F32))
    else:
        out_specs = pl.BlockSpec((tm, D_MODEL), lambda i: (i, 0))
        out_shape = jax.ShapeDtypeStruct((t_rows, D_MODEL), F32)
    return pl.pallas_call(
        functools.partial(_combine_kernel, first_tile=first_tile, n_tiles=n_tiles, to_batch_major=final),
        grid=(n_tiles,),
        in_specs=in_specs, out_specs=out_specs, out_shape=out_shape,
        scratch_shapes=scratch,
        compiler_params=_cparams(1),
        name="moe_combine_final" if final else "moe_combine",
    )(*([pos3] * N_BUF), yb, gates, h1, g2, b2)


def _moe_block(h1, h1r, route, counts, w_gate, w_up, w_down, ln_g, ln_b, *, final):
    t_rows = h1.shape[0]
    n_assign = (t_rows - PAD_ROWS) * TOP_K_FINE
    nblk = -(-(n_assign + N_EXPERTS * (EXPERT_TILE - 1)) // EXPERT_TILE)
    n_slots = nblk * EXPERT_TILE
    slot_row, block_expert, n_used, n_fetch, pos, gates = _slots(route, counts, n_slots)
    yb = _experts(h1r, slot_row, block_expert, n_used, n_fetch, w_gate, w_up, w_down)
    return _combine(yb, pos, gates, h1, ln_g, ln_b, final=final)


def _inproj1_kernel(h_ref, w_ref, gg_ref, rec_ref):
    x = h_ref[...].astype(BF16)
    z = jnp.dot(x, w_ref[...], preferred_element_type=F32)
    gg_ref[...] = _gelu(z[:, :D_RNN]).astype(BF16)
    rec_ref[...] = z[:, D_RNN:]


def _inproj1(h, w_bf16):
    t_rows = h.shape[0]
    tm = TOK_TILE
    row = lambda width: pl.BlockSpec((tm, width), lambda i: (i, 0))
    return pl.pallas_call(
        _inproj1_kernel,
        grid=(t_rows // tm,),
        in_specs=[row(D_MODEL), pl.BlockSpec(w_bf16.shape, lambda i: (0, 0))],
        out_specs=[row(D_RNN), row(D_RNN)],
        out_shape=[jax.ShapeDtypeStruct((t_rows, D_RNN), BF16), jax.ShapeDtypeStruct((t_rows, D_RNN), F32)],
        compiler_params=_cparams(1),
        name="l1_inproj",
    )(h, w_bf16)


CONV_HALO = (CONV_WIDTH - 1) * NB


def _rglru_kernel(rec_ref, gg_ref, cw_ref, cb_ref, wax_ref, ba_ref, bx_ref, sp_ref,
                  y_ref, rp_ref, a_ref, b_ref, st_ref, *, steps):
    i = pl.program_id(0)
    rows = steps * NB

    @pl.when(i == 0)
    def _():
        st_ref[...] = jnp.zeros_like(st_ref)
        rp_ref[0:CONV_HALO, :] = jnp.zeros((CONV_HALO, D_RNN), F32)

    row = i * rows + lax.broadcasted_iota(jnp.int32, (rows, 1), 0)
    real = row >= PAD_ROWS
    rp_ref[CONV_HALO:CONV_HALO + rows, :] = jnp.where(real, rec_ref[...], 0.0)
    xc = cb_ref[...] + sum(rp_ref[NB * j:NB * j + rows, :] * cw_ref[j:j + 1, :] for j in range(CONV_WIDTH))
    tail = rp_ref[rows:rows + CONV_HALO, :]
    rp_ref[0:CONV_HALO, :] = tail

    for n in range(LRU_BLOCKS):
        cs = slice(LRU_BLOCK_W * n, LRU_BLOCK_W * (n + 1))
        xb = xc[:, cs]
        ra = jnp.dot(xb.astype(BF16), wax_ref[n], preferred_element_type=F32)
        r = _sigmoid(ra[:, :LRU_BLOCK_W] + ba_ref[:, cs])
        ig = _sigmoid(ra[:, LRU_BLOCK_W:] + bx_ref[:, cs])
        log_a = -LRU_C * r * sp_ref[:, cs]
        a = jnp.exp(log_a)
        bt = jnp.sqrt(1.0 - a * a) * (ig * xb)
        a_ref[:, cs] = a
        b_ref[:, cs] = jnp.where(real, bt, 0.0)

    def step(t, h):
        r0 = pl.multiple_of(t * NB, NB)
        hn = a_ref[pl.ds(r0, NB), :] * h + b_ref[pl.ds(r0, NB), :]
        b_ref[pl.ds(r0, NB), :] = hn
        return hn

    st_ref[...] = lax.fori_loop(0, steps, step, st_ref[...])
    y_ref[...] = (gg_ref[...].astype(F32) * b_ref[...]).astype(BF16)


def _rglru(rec, gg, conv_w, conv_b, w_a, b_a, w_x, b_x, lru_lambda):
    t_rows = rec.shape[0]
    steps = SCAN_STEPS
    rows = steps * NB
    cw = conv_w.astype(F32)
    cb = conv_b.astype(F32).reshape(1, D_RNN)
    wax = jnp.concatenate([w_a.astype(F32), w_x.astype(F32)], axis=2).astype(BF16)
    ba = b_a.astype(F32).reshape(1, D_RNN)
    bx = b_x.astype(F32).reshape(1, D_RNN)
    sp = jax.nn.softplus(-lru_lambda.astype(F32)).reshape(1, D_RNN)
    full = lambda a: pl.BlockSpec(a.shape, lambda i: (0,) * a.ndim)
    row = pl.BlockSpec((rows, D_RNN), lambda i: (i, 0))
    return pl.pallas_call(
        functools.partial(_rglru_kernel, steps=steps),
        grid=(t_rows // rows,),
        in_specs=[row, row, full(cw), full(cb), full(wax), full(ba), full(bx), full(sp)],
        out_specs=row,
        out_shape=jax.ShapeDtypeStruct((t_rows, D_RNN), BF16),
        scratch_shapes=[pltpu.VMEM((rows + CONV_HALO, D_RNN), F32), pltpu.VMEM((rows, D_RNN), F32),
                        pltpu.VMEM((rows, D_RNN), F32), pltpu.VMEM((NB, D_RNN), F32)],
        compiler_params=_cparams(1),
        name="l1_rglru",
    )(rec, gg, cw, cb, wax, ba, bx, sp)


def kernel(x, meta, l0_ln1_g, l0_ln1_b, l0_w_in, l0_s5_lambda_re, l0_s5_lambda_im, l0_s5_log_dt, l0_s5_b_re, l0_s5_b_im, l0_s5_c_re, l0_s5_c_im, l0_s5_d, l0_s5_w_glu, l0_s5_b_glu, l0_da_lq1, l0_da_lk1, l0_da_lq2, l0_da_lk2, l0_da_subln_g, l0_w_out, l0_ln2_g, l0_ln2_b, l0_moe_w_coarse, l0_moe_b_coarse, l0_moe_w_fine, l0_moe_b_fine, l0_moe_w_gate, l0_moe_w_up, l0_moe_w_down, l1_ln1_g, l1_ln1_b, l1_w_in, l1_conv_w, l1_conv_b, l1_lru_w_a, l1_lru_b_a, l1_lru_w_x, l1_lru_b_x, l1_lru_lambda, l1_w_out, l1_ln2_g, l1_ln2_b, l1_moe_w_coarse, l1_moe_b_coarse, l1_moe_w_fine, l1_moe_b_fine, l1_moe_w_gate, l1_moe_w_up, l1_moe_w_down):
    bsz, seq, _ = x.shape
    assert bsz == NB and seq % Q_TILE == 0
    dt = x.dtype
    lp = FRONT_PAD + N_META + seq

    col_scale = jnp.concatenate([jnp.ones((S5_WIDTH,), F32),
                                 jnp.full((DA_WIDTH,), DA_HEAD_DIM ** -0.5, F32),
                                 jnp.ones((2 * DA_WIDTH,), F32)])
    w_in0 = (l0_w_in.astype(F32) * col_scale[None, :]).astype(BF16)
    h, u, q, k, v = _inproj0(x, meta, w_in0, lp)
    s5p = _s5_params(l0_s5_lambda_re, l0_s5_lambda_im, l0_s5_log_dt,
                     l0_s5_b_re, l0_s5_b_im, l0_s5_c_re, l0_s5_c_im)
    y_s5 = _s5(u, *s5p, l0_s5_d, l0_s5_w_glu, l0_s5_b_glu)
    lam_init = 0.8 - 0.6 * math.exp(-0.3 * 0)
    lam = (jnp.exp(jnp.sum(l0_da_lq1.astype(F32) * l0_da_lk1.astype(F32)))
           - jnp.exp(jnp.sum(l0_da_lq2.astype(F32) * l0_da_lk2.astype(F32))) + lam_init)
    y_da = _diffattn(q, k, v, lam, l0_da_subln_g, lam_init)
    router0 = _router_weights(l0_moe_w_coarse, l0_moe_b_coarse, l0_moe_w_fine, l0_moe_b_fine)
    h, hr, route, counts = _outproj(y_s5, y_da, h, l0_w_out, l0_ln1_g, l0_ln1_b, router0)
    h = _moe_block(h, hr, route, counts, l0_moe_w_gate, l0_moe_w_up, l0_moe_w_down, l0_ln2_g, l0_ln2_b, final=False)

    gg, rec = _inproj1(h, l1_w_in.astype(BF16))
    y = _rglru(rec, gg, l1_conv_w, l1_conv_b, l1_lru_w_a, l1_lru_b_a, l1_lru_w_x, l1_lru_b_x, l1_lru_lambda)
    router1 = _router_weights(l1_moe_w_coarse, l1_moe_b_coarse, l1_moe_w_fine, l1_moe_b_fine)
    h, hr, route, counts = _outproj(y, None, h, l1_w_out, l1_ln1_g, l1_ln1_b, router1)
    out = _moe_block(h, hr, route, counts, l1_moe_w_gate, l1_moe_w_up, l1_moe_w_down, l1_ln2_g, l1_ln2_b, final=True)
    return out.astype(dt)
```

```python
import functools
import math

import jax
import jax.numpy as jnp
from jax import lax
from jax.experimental import pallas as pl
from jax.experimental.pallas import tpu as pltpu

F32 = jnp.float32
BF16 = jnp.bfloat16

D_MODEL = 1024
DEPTH = 2
CHUNK = 64
N_META = 16
S5_WIDTH = 512
S5_GROUP = 16
S5_GROUPS = 32
S5_STATE = 64
DA_HEADS = 4
DA_HEAD_DIM = 64
DA_WIDTH = 512
D_RNN = 1280
LRU_BLOCKS = 10
LRU_BLOCK_W = 128
CONV_WIDTH = 4
LRU_C = 8.0
N_GROUPS = 4
EXPERTS_PER_GROUP = 8
N_EXPERTS = 32
TOP_K_FINE = 2
D_EXPERT = 512
ALPHA = (2 * DEPTH) ** 0.25
LN_EPS = 1e-5
NEG_INF = -1e30

NB = 8
LANES = 128
FRONT_PAD = LANES - N_META
PAD_ROWS = FRONT_PAD * NB
Q_TILE = 256
K_TILE = 512
TOK_TILE = 512
SCAN_STEPS = 64
EXPERT_TILE = 256
COMBINE_TILE = 256
VMEM_LIMIT = 48 * 1024 * 1024
LOG2_E = math.log2(math.e)


def _cparams(n_axes, vmem=VMEM_LIMIT):
    return pltpu.CompilerParams(dimension_semantics=("arbitrary",) * n_axes,
                                vmem_limit_bytes=vmem)


def _gelu(x):
    return 0.5 * x * (1.0 + jnp.tanh(math.sqrt(2.0 / math.pi) * (x + 0.044715 * (x * x * x))))


def _sigmoid(x):
    return 1.0 / (1.0 + jnp.exp(-x))


def _layer_norm_rows(r, g, b):
    mu = jnp.mean(r, axis=-1, keepdims=True)
    c = r - mu
    var = jnp.mean(c * c, axis=-1, keepdims=True)
    return c * lax.rsqrt(var + LN_EPS) * g + b


def _split_bf16(w):
    hi = w.astype(BF16)
    lo = (w - hi.astype(F32)).astype(BF16)
    return hi, lo


def _inproj0_kernel(head_ref, x_ref, w_ref, h_ref, u_ref, q_ref, k_ref, v_ref, zs_ref, *, tl, head_tiles):
    i = pl.program_id(0)

    @pl.when(i < head_tiles)
    def _():
        h_ref[...] = head_ref[...]

    @pl.when(i >= head_tiles)
    def _():
        for s in range(D_MODEL // LANES):
            for b in range(NB):
                zs_ref[s, pl.ds(b, tl, stride=NB), :] = x_ref[b, :, LANES * s:LANES * (s + 1)]
        for s in range(D_MODEL // LANES):
            h_ref[:, LANES * s:LANES * (s + 1)] = zs_ref[s]

    x = h_ref[...].astype(BF16)
    z = jnp.dot(x, w_ref[...], preferred_element_type=F32)
    u_ref[...] = z[:, :S5_WIDTH]
    n_slab = 3 * DA_WIDTH // LANES
    for s in range(n_slab):
        zs_ref[s] = z[:, S5_WIDTH + LANES * s:S5_WIDTH + LANES * (s + 1)]
    dsts = (q_ref, k_ref, v_ref)
    per = DA_WIDTH // LANES
    for b in range(NB):
        for s in range(n_slab):
            c = (s % per) * LANES
            blk = zs_ref[s, pl.ds(b, tl, stride=NB), :]
            if s < per:
                blk = blk * LOG2_E
            dsts[s // per][b, :, c:c + LANES] = blk.astype(BF16)


def _inproj0(x, meta, w_bf16, lp):
    t_rows = lp * NB
    tm = TOK_TILE
    tl = tm // NB
    head_tiles = LANES // tl
    head = jnp.concatenate([
        jnp.zeros((FRONT_PAD, NB, D_MODEL), x.dtype),
        jnp.broadcast_to(meta.astype(x.dtype)[:, None, :], (N_META, NB, D_MODEL))], axis=0)
    head = head.reshape(LANES * NB, D_MODEL)
    qkv_shape = jax.ShapeDtypeStruct((NB, lp, DA_WIDTH), BF16)
    qkv_spec = pl.BlockSpec((NB, tl, DA_WIDTH), lambda i: (0, i, 0))
    n_slab = max(3 * DA_WIDTH, D_MODEL) // LANES
    return pl.pallas_call(
        functools.partial(_inproj0_kernel, tl=tl, head_tiles=head_tiles),
        grid=(t_rows // tm,),
        in_specs=[pl.BlockSpec((tm, D_MODEL), lambda i: (jnp.minimum(i, head_tiles - 1), 0)),
                  pl.BlockSpec((NB, tl, D_MODEL), lambda i: (0, jnp.maximum(i - head_tiles, 0), 0)),
                  pl.BlockSpec(w_bf16.shape, lambda i: (0, 0))],
        out_specs=[pl.BlockSpec((tm, D_MODEL), lambda i: (i, 0)),
                   pl.BlockSpec((tm, S5_WIDTH), lambda i: (i, 0)), qkv_spec, qkv_spec, qkv_spec],
        out_shape=[jax.ShapeDtypeStruct((t_rows, D_MODEL), F32),
                   jax.ShapeDtypeStruct((t_rows, S5_WIDTH), F32), qkv_shape, qkv_shape, qkv_shape],
        scratch_shapes=[pltpu.VMEM((n_slab, tm, LANES), F32)],
        compiler_params=_cparams(1),
        name="l0_inproj",
    )(head, x, w_bf16)


S5_SLABS = S5_WIDTH // LANES
S5_SLAB_STATE = (S5_GROUPS // S5_SLABS) * S5_STATE
S5_NSTATE = S5_GROUPS * S5_STATE


def _s5_kernel(u_ref, bmat_ref, are_ref, aim_ref, cre_ref, cim_ref, d_ref, wglu_ref, bglu_ref,
               y_ref, hre_ref, him_ref, st_ref, *, steps):
    i = pl.program_id(0)
    rows = steps * NB

    @pl.when(i == 0)
    def _():
        st_ref[...] = jnp.zeros_like(st_ref)

    row = i * rows + lax.broadcasted_iota(jnp.int32, (rows, 1), 0)
    u = jnp.where(row >= PAD_ROWS, u_ref[...], 0.0)
    ub = u.astype(BF16)
    for s in range(S5_SLABS):
        bu = jnp.dot(ub[:, LANES * s:LANES * (s + 1)], bmat_ref[s], preferred_element_type=F32)
        hre_ref[:, S5_SLAB_STATE * s:S5_SLAB_STATE * (s + 1)] = bu[:, :S5_SLAB_STATE]
        him_ref[:, S5_SLAB_STATE * s:S5_SLAB_STATE * (s + 1)] = bu[:, S5_SLAB_STATE:]

    cw = 512
    for cg in range(S5_NSTATE // cw):
        c0 = cg * cw
        a_r = are_ref[:, c0:c0 + cw]
        a_i = aim_ref[:, c0:c0 + cw]

        def step(t, carry, c0=c0, a_r=a_r, a_i=a_i):
            sr, si = carry
            r0 = pl.multiple_of(t * NB, NB)
            br = hre_ref[pl.ds(r0, NB), c0:c0 + cw]
            bi = him_ref[pl.ds(r0, NB), c0:c0 + cw]
            nr = a_r * sr - a_i * si + br
            ni = a_r * si + a_i * sr + bi
            hre_ref[pl.ds(r0, NB), c0:c0 + cw] = nr
            him_ref[pl.ds(r0, NB), c0:c0 + cw] = ni
            return nr, ni

        sr, si = lax.fori_loop(0, steps, step, (st_ref[0, :, c0:c0 + cw], st_ref[1, :, c0:c0 + cw]))
        st_ref[0, :, c0:c0 + cw] = sr
        st_ref[1, :, c0:c0 + cw] = si

    ys = []
    for s in range(S5_SLABS):
        hr = hre_ref[:, S5_SLAB_STATE * s:S5_SLAB_STATE * (s + 1)].astype(BF16)
        hi = him_ref[:, S5_SLAB_STATE * s:S5_SLAB_STATE * (s + 1)].astype(BF16)
        ys.append(jnp.dot(hr, cre_ref[s], preferred_element_type=F32)
                  + jnp.dot(hi, cim_ref[s], preferred_element_type=F32))
    y = jnp.concatenate(ys, axis=1) + d_ref[...] * u
    y = _gelu(y)
    gate = _sigmoid(jnp.dot(y.astype(BF16), wglu_ref[...], preferred_element_type=F32) + bglu_ref[...])
    y_ref[...] = (y * gate).astype(BF16)


def _s5_params(lam_re, lam_im, log_dt, b_re, b_im, c_re, c_im):
    dt = jnp.exp(log_dt.astype(F32))[:, None]
    lr = jnp.minimum(lam_re.astype(F32), -1e-4)
    li = lam_im.astype(F32)
    mag = jnp.exp(lr * dt)
    ar = mag * jnp.cos(li * dt)
    ai = mag * jnp.sin(li * dt)
    den = lr * lr + li * li
    nr, ni = ar - 1.0, ai
    fr = ((nr * lr + ni * li) / den)[..., None]
    fi = ((ni * lr - nr * li) / den)[..., None]
    br, bi = b_re.astype(F32), b_im.astype(F32)
    bbr = fr * br - fi * bi
    bbi = fr * bi + fi * br
    gps = S5_GROUPS // S5_SLABS
    eye = jnp.eye(gps, dtype=F32)

    def in_slab(m):
        m4 = m.reshape(S5_SLABS, gps, S5_STATE, S5_GROUP)
        return jnp.einsum('sgph,gk->sghkp', m4, eye).reshape(S5_SLABS, LANES, S5_SLAB_STATE)

    def out_slab(m):
        m4 = m.reshape(S5_SLABS, gps, S5_GROUP, S5_STATE)
        return jnp.einsum('sghp,gk->sgpkh', m4, eye).reshape(S5_SLABS, S5_SLAB_STATE, LANES)

    bmat = jnp.concatenate([in_slab(bbr), in_slab(bbi)], axis=2).astype(BF16)
    cre = out_slab(c_re.astype(F32)).astype(BF16)
    cim = out_slab(-c_im.astype(F32)).astype(BF16)
    a_re = jnp.broadcast_to(ar.reshape(1, S5_NSTATE), (NB, S5_NSTATE))
    a_im = jnp.broadcast_to(ai.reshape(1, S5_NSTATE), (NB, S5_NSTATE))
    return bmat, a_re, a_im, cre, cim


def _s5(u, bmat, a_re, a_im, cre, cim, d, w_glu, b_glu):
    t_rows = u.shape[0]
    steps = SCAN_STEPS
    rows = steps * NB
    full = lambda a: pl.BlockSpec(a.shape, lambda i: (0,) * a.ndim)
    d2 = d.astype(F32).reshape(1, S5_WIDTH)
    bg2 = b_glu.astype(F32).reshape(1, S5_WIDTH)
    wg = w_glu.astype(BF16)
    return pl.pallas_call(
        functools.partial(_s5_kernel, steps=steps),
        grid=(t_rows // rows,),
        in_specs=[pl.BlockSpec((rows, S5_WIDTH), lambda i: (i, 0)),
                  full(bmat), full(a_re), full(a_im), full(cre), full(cim), full(d2), full(wg), full(bg2)],
        out_specs=pl.BlockSpec((rows, S5_WIDTH), lambda i: (i, 0)),
        out_shape=jax.ShapeDtypeStruct((t_rows, S5_WIDTH), BF16),
        scratch_shapes=[pltpu.VMEM((rows, S5_NSTATE), F32), pltpu.VMEM((rows, S5_NSTATE), F32),
                        pltpu.VMEM((2, NB, S5_NSTATE), F32)],
        compiler_params=_cparams(1),
        name="l0_s5",
    )(u, bmat, a_re, a_im, cre, cim, d2, wg, bg2)


def _diffattn_kernel(lam_ref, q_ref, k_ref, v_ref, g_ref, o_ref,
                     s_ref, qm_ref, m_ref, a_ref, *, lam_init, lp):
    lam = lam_ref[0]
    tq, tk = Q_TILE, K_TILE
    lane = lax.broadcasted_iota(jnp.int32, (tq, LANES), 1)
    qrow = lax.broadcasted_iota(jnp.int32, (tq, tk), 0)
    kloc = lax.broadcasted_iota(jnp.int32, (tq, tk), 1)
    nt = (((1,), (1,)), ((), ()))
    bf16_rows = 16

    def key_start(j):
        return pl.multiple_of(jnp.minimum(FRONT_PAD + j * tk, lp - tk), bf16_rows)

    def chunk_of(pos):
        return jnp.right_shift(pos - CHUNK, 6)

    def lane_fold(x, op):
        r = x[:, :LANES]
        for c in range(1, x.shape[1] // LANES):
            r = op(r, x[:, LANES * c:LANES * (c + 1)])
        return r

    def q_start(i):
        return pl.multiple_of(jnp.minimum(i * tq, lp - tq), LANES)

    def n_key_tiles(i):
        return (q_start(i) + tq - FRONT_PAD + tk - 1) // tk

    def n_full_tiles(i):
        return jnp.maximum(q_start(i) - FRONT_PAD, 0) // tk


    def prep(i, par):
        q = q_ref[0, pl.ds(q_start(i), tq), :]
        zero = jnp.zeros_like(q)
        qm_ref[par, 0] = jnp.where(lane < DA_HEAD_DIM, q, zero)
        qm_ref[par, 1] = jnp.where(lane >= DA_HEAD_DIM, q, zero)
        m_ref[par] = jnp.full(m_ref.shape[1:], NEG_INF, F32)

    def scores(i, par, j, masked):
        k0 = key_start(j)
        kt = k_ref[0, pl.ds(k0, tk), :]
        if masked:
            kpos = k0 + kloc
            mask = (chunk_of(kpos) <= chunk_of(q_start(i) + qrow)) & (kpos >= FRONT_PAD + j * tk)
        for h in range(2):
            s = lax.dot_general(qm_ref[par, h], kt, nt, preferred_element_type=F32)
            if masked:
                s = jnp.where(mask, s, NEG_INF)
            s_ref[par, h, j] = s
            m_ref[par, h] = jnp.maximum(m_ref[par, h], lane_fold(s, jnp.maximum))

    ones_col = jnp.where(lax.broadcasted_iota(jnp.int32, (tk, LANES), 1) == 0, 1.0, 0.0).astype(BF16)

    def values(par, j, m):
        vt = jnp.concatenate([v_ref[0, pl.ds(key_start(j), tk), :], ones_col], axis=1)
        for h in range(2):
            p = jnp.exp2((s_ref[par, h, j] - m[h]).astype(BF16))
            a_ref[h] += jnp.dot(p, vt, preferred_element_type=F32)

    def row_max(par):
        a_ref[...] = jnp.zeros(a_ref.shape, F32)
        return [jnp.max(m_ref[par, h], axis=-1, keepdims=True) for h in range(2)]

    def finish(i):
        l1 = a_ref[0, :, LANES:LANES + 1]
        l2 = a_ref[1, :, LANES:LANES + 1]
        o = a_ref[0, :, :LANES] / l1 - lam * (a_ref[1, :, :LANES] / l2)
        o = o * lax.rsqrt(jnp.mean(o * o, axis=-1, keepdims=True) + LN_EPS) * g_ref[...]
        o_ref[0, pl.ds(q_start(i), tq), :] = o * (1.0 - lam_init)

    def loop(lo, hi, body):
        def wrapped(j, _):
            body(j)
            return 0
        lax.fori_loop(lo, hi, wrapped, 0)

    def step(i, par):
        m = row_max(par)
        prep(i + 1, 1 - par)

        def both(j, masked):
            scores(i + 1, 1 - par, j, masked)
            values(par, j, m)

        loop(0, n_full_tiles(i + 1), functools.partial(both, masked=False))
        loop(n_full_tiles(i + 1), n_key_tiles(i), functools.partial(both, masked=True))
        loop(n_key_tiles(i), n_key_tiles(i + 1), lambda j: scores(i + 1, 1 - par, j, True))
        finish(i)

    def step_pair(i2):
        step(2 * i2, 0)
        step(2 * i2 + 1, 1)

    n_q = pl.cdiv(lp, tq)
    prep(0, 0)
    loop(0, n_key_tiles(0), lambda j: scores(0, 0, j, True))
    loop(0, (n_q - 1) // 2, step_pair)
    if (n_q - 1) % 2:
        step(n_q - 2, 0)
    last = (n_q - 1) % 2
    m = row_max(last)
    loop(0, n_key_tiles(n_q - 1), lambda j: values(last, j, m))
    finish(n_q - 1)


def _diffattn(q, k, v, lam, subln_g, lam_init):
    nb, lp, _ = q.shape
    g2 = subln_g.astype(F32).reshape(1, 2 * DA_HEAD_DIM)
    seq_spec = pl.BlockSpec((1, lp, LANES), lambda b, h: (b, 0, h))
    return pl.pallas_call(
        functools.partial(_diffattn_kernel, lam_init=lam_init, lp=lp),
        grid=(nb, DA_HEADS),
        in_specs=[pl.BlockSpec(memory_space=pltpu.SMEM), seq_spec, seq_spec, seq_spec,
                  pl.BlockSpec((1, LANES), lambda b, h: (0, 0))],
        out_specs=seq_spec,
        out_shape=jax.ShapeDtypeStruct((nb, lp, DA_WIDTH), F32),
        scratch_shapes=[pltpu.VMEM((2, 2, pl.cdiv(lp, K_TILE), Q_TILE, K_TILE), F32),
                        pltpu.VMEM((2, 2, Q_TILE, LANES), BF16),
                        pltpu.VMEM((2, 2, Q_TILE, LANES), F32),
                        pltpu.VMEM((2, Q_TILE, 2 * LANES), F32)],
        compiler_params=_cparams(2),
        name="l0_diffattn",
    )(lam.reshape(1), q, k, v, g2)


ROUTE_ROWS = 64
ROUTE_E, ROUTE_GATE, ROUTE_RANK = 0, 2, 4


def _router_logits_t(h1, whi_ref, wlo_ref, rb_ref):
    hi = h1.astype(BF16)
    lo = (h1 - hi.astype(F32)).astype(BF16)
    nt = (((1,), (1,)), ((), ()))
    return (lax.dot_general(whi_ref[...], hi, nt, preferred_element_type=F32)
            + lax.dot_general(whi_ref[...], lo, nt, preferred_element_type=F32)
            + lax.dot_general(wlo_ref[...], hi, nt, preferred_element_type=F32) + rb_ref[...])


def _route_cols(lg, cnt_ref, tri_ref, first_tok):
    toks = lg.shape[1]
    row = lax.broadcasted_iota(jnp.int32, lg.shape, 0)
    rowf = row.astype(F32)
    valid = (first_tok + lax.broadcasted_iota(jnp.int32, (1, toks), 1)) >= PAD_ROWS
    ninf = float('-inf')
    first = lambda hit: jnp.min(jnp.where(hit, rowf, float(ROUTE_ROWS)), axis=0, keepdims=True)
    cm = jnp.where(row < N_GROUPS, lg, ninf)
    cmax = jnp.max(cm, axis=0, keepdims=True)
    p_grp = 1.0 / jnp.sum(jnp.exp(cm - cmax), axis=0, keepdims=True)
    lo = N_GROUPS + EXPERTS_PER_GROUP * first(cm == cmax)
    fm = jnp.where((rowf >= lo) & (rowf < lo + EXPERTS_PER_GROUP), lg, ninf)
    v1 = jnp.max(fm, axis=0, keepdims=True)
    i1 = first(fm == v1)
    fm2 = jnp.where(rowf == i1, ninf, fm)
    v2 = jnp.max(fm2, axis=0, keepdims=True)
    i2 = first(fm2 == v2)
    t = jnp.exp(v2 - v1)
    g1 = p_grp / (1.0 + t)
    g2 = p_grp * t / (1.0 + t)
    oh1 = (rowf == i1) & valid
    oh2 = (rowf == i2) & valid
    m = jnp.where(oh1 | oh2, 1.0, 0.0)
    before = jnp.dot(m.astype(BF16), tri_ref[...], preferred_element_type=F32) + cnt_ref[:, 0:1]
    r1 = jnp.sum(jnp.where(oh1, before, 0.0), axis=0, keepdims=True)
    r2 = jnp.sum(jnp.where(oh2, before, 0.0), axis=0, keepdims=True)
    cnt_ref[...] = cnt_ref[...] + jnp.sum(m, axis=1, keepdims=True)
    zero = jnp.zeros_like(g1)
    rec_row = lax.broadcasted_iota(jnp.int32, (NB, toks), 0)
    rec = jnp.zeros((NB, toks), F32)
    for k, val in enumerate((i1 - N_GROUPS, i2 - N_GROUPS, jnp.where(valid, g1, zero),
                             jnp.where(valid, g2, zero), r1, r2)):
        rec = jnp.where(rec_row == k, val, rec)
    return rec


def _route_tile(h1, whi_ref, wlo_ref, rb_ref, rt_ref, cnt_ref, tri_ref):
    i = pl.program_id(0)
    toks = h1.shape[0]

    @pl.when(i == 0)
    def _():
        cnt_ref[...] = jnp.zeros_like(cnt_ref)
        ri = lax.broadcasted_iota(jnp.int32, (toks, toks), 0)
        ci = lax.broadcasted_iota(jnp.int32, (toks, toks), 1)
        tri_ref[...] = jnp.where(ri < ci, 1.0, 0.0).astype(BF16)

    lg = _router_logits_t(h1, whi_ref, wlo_ref, rb_ref)
    rt_ref[...] = _route_cols(lg, cnt_ref, tri_ref, i * toks)


def _outproj0_kernel(ys_ref, yda_ref, h_ref, w_ref, g_ref, b_ref, whi_ref, wlo_ref, rb_ref,
                     h1_ref, h1r_ref, rt_ref, cnt_ref, das_ref, tri_ref, *, tl):
    per = DA_WIDTH // LANES
    for b in range(NB):
        for s in range(per):
            das_ref[s, pl.ds(b, tl, stride=NB), :] = yda_ref[b, :, LANES * s:LANES * (s + 1)]
    da = jnp.concatenate([das_ref[s] for s in range(per)], axis=1).astype(BF16)
    mix = (jnp.dot(ys_ref[...], w_ref[:S5_WIDTH, :], preferred_element_type=F32)
           + jnp.dot(da, w_ref[S5_WIDTH:, :], preferred_element_type=F32))
    h1 = _layer_norm_rows(ALPHA * h_ref[...] + mix, g_ref[...], b_ref[...])
    h1_ref[...] = h1
    _to_row_tiles(h1r_ref, h1)
    _route_tile(h1, whi_ref, wlo_ref, rb_ref, rt_ref, cnt_ref, tri_ref)


def _outproj1_kernel(y_ref, h_ref, w_ref, g_ref, b_ref, whi_ref, wlo_ref, rb_ref,
                     h1_ref, h1r_ref, rt_ref, cnt_ref, tri_ref):
    mix = jnp.dot(y_ref[...], w_ref[...], preferred_element_type=F32)
    h1 = _layer_norm_rows(ALPHA * h_ref[...] + mix, g_ref[...], b_ref[...])
    h1_ref[...] = h1
    _to_row_tiles(h1r_ref, h1)
    _route_tile(h1, whi_ref, wlo_ref, rb_ref, rt_ref, cnt_ref, tri_ref)


def _router_weights(w_coarse, b_coarse, w_fine, b_fine):
    wf = jnp.transpose(w_fine.astype(F32), (1, 0, 2)).reshape(D_MODEL, N_EXPERTS)
    w = jnp.concatenate([w_coarse.astype(F32), wf], axis=1).T
    w = jnp.pad(w, ((0, ROUTE_ROWS - w.shape[0]), (0, 0)))
    b = jnp.concatenate([b_coarse.astype(F32), b_fine.astype(F32).reshape(-1)])
    b = jnp.pad(b, (0, ROUTE_ROWS - b.shape[0])).reshape(ROUTE_ROWS, 1)
    whi, wlo = _split_bf16(w)
    return whi, wlo, b


def _outproj(ys, yda, h, w_out, ln_g, ln_b, router):
    t_rows = h.shape[0]
    tm = TOK_TILE
    tl = tm // NB
    whi, wlo, rb = router
    w = w_out.astype(BF16)
    g2 = ln_g.astype(F32).reshape(1, D_MODEL)
    b2 = ln_b.astype(F32).reshape(1, D_MODEL)
    full = lambda a: pl.BlockSpec(a.shape, lambda i: (0,) * a.ndim)
    row = lambda width: pl.BlockSpec((tm, width), lambda i: (i, 0))
    common_in = [row(D_MODEL), full(w), full(g2), full(b2), full(whi), full(wlo), full(rb)]
    out_specs = [row(D_MODEL), pl.BlockSpec((tm * ROW_TILES, LANES), lambda i: (i, 0)),
                 pl.BlockSpec((NB, tm), lambda i: (0, i)),
                 pl.BlockSpec((ROUTE_ROWS, LANES), lambda i: (0, 0))]
    out_shape = [jax.ShapeDtypeStruct((t_rows, D_MODEL), F32),
                 jax.ShapeDtypeStruct((t_rows * ROW_TILES, LANES), F32),
                 jax.ShapeDtypeStruct((NB, t_rows), F32),
                 jax.ShapeDtypeStruct((ROUTE_ROWS, LANES), F32)]
    tri = pltpu.VMEM((tm, tm), BF16)
    if yda is not None:
        return pl.pallas_call(
            functools.partial(_outproj0_kernel, tl=tl),
            grid=(t_rows // tm,),
            in_specs=[row(S5_WIDTH), pl.BlockSpec((NB, tl, DA_WIDTH), lambda i: (0, i, 0))] + common_in,
            out_specs=out_specs, out_shape=out_shape,
            scratch_shapes=[pltpu.VMEM((DA_WIDTH // LANES, tm, LANES), F32), tri],
            compiler_params=_cparams(1),
            name="l0_outproj",
        )(ys, yda, h, w, g2, b2, whi, wlo, rb)
    return pl.pallas_call(
        _outproj1_kernel,
        grid=(t_rows // tm,),
        in_specs=[row(ys.shape[1])] + common_in,
        out_specs=out_specs, out_shape=out_shape,
        scratch_shapes=[tri],
        compiler_params=_cparams(1),
        name="l1_outproj",
    )(ys, h, w, g2, b2, whi, wlo, rb)


def _slots(route, counts, n_slots):
    t_rows = route.shape[1]
    experts = route[ROUTE_E:ROUTE_E + TOP_K_FINE].T.astype(jnp.int32)
    gates = route[ROUTE_GATE:ROUTE_GATE + TOP_K_FINE].T
    rank = route[ROUTE_RANK:ROUTE_RANK + TOP_K_FINE].T.astype(jnp.int32)
    cnt = counts[N_GROUPS:N_GROUPS + N_EXPERTS, 0].astype(jnp.int32)
    padded = (cnt + EXPERT_TILE - 1) // EXPERT_TILE * EXPERT_TILE
    pad_end = jnp.cumsum(padded)
    pad_start = pad_end - padded
    raw_start = jnp.cumsum(cnt) - cnt
    valid = (jnp.arange(t_rows) >= PAD_ROWS)[:, None]
    e_ids = jnp.arange(N_EXPERTS, dtype=jnp.int32)
    start_of = jnp.sum(jnp.where(experts[..., None] == e_ids, pad_start, 0), axis=-1)
    dest = jnp.where(valid, start_of + rank, n_slots).astype(jnp.int32)
    nblk = n_slots // EXPERT_TILE
    blk_start = jnp.arange(nblk, dtype=jnp.int32) * EXPERT_TILE
    block_expert = jnp.minimum(jnp.sum((pad_end[None, :] <= blk_start[:, None]).astype(jnp.int32), axis=1),
                               N_EXPERTS - 1)
    hit = block_expert[:, None] == e_ids[None, :]
    blk_pad_start = jnp.sum(jnp.where(hit, pad_start, 0), axis=1)
    blk_raw_start = jnp.sum(jnp.where(hit, raw_start, 0), axis=1)
    blk_cnt = jnp.sum(jnp.where(hit, cnt, 0), axis=1)
    n_used = (pad_end[-1] // EXPERT_TILE).astype(jnp.int32)
    blk_rows = jnp.where(jnp.arange(nblk) < n_used,
                         jnp.clip(blk_cnt - (blk_start - blk_pad_start), 0, EXPERT_TILE), 0)
    n_fetch = (blk_rows + GATHER_CHUNK - 1) // GATHER_CHUNK * GATHER_CHUNK
    order = jnp.argsort(dest.reshape(-1)).astype(jnp.int32)
    off = (blk_start - blk_pad_start)[:, None] + jnp.arange(EXPERT_TILE, dtype=jnp.int32)[None, :]
    src = jnp.clip(blk_raw_start[:, None] + off, 0, order.shape[0] - 1)
    slot_tok = jnp.where(off < blk_cnt[:, None], jnp.right_shift(order[src], 1), 0).reshape(-1)
    pos = jnp.where(valid, dest, 0).reshape(-1)
    return (slot_tok * ROW_TILES, block_expert, n_used.reshape(1), n_fetch.astype(jnp.int32),
            pos * ROW_TILES, gates)


ROW_TILES = D_MODEL // LANES


def _to_row_tiles(dst_ref, val):
    rows = val.shape[0]
    for s in range(ROW_TILES):
        dst_ref[pl.ds(s, rows, stride=ROW_TILES), :] = val[:, LANES * s:LANES * (s + 1)]


def _from_row_tiles(src_ref, rows):
    return jnp.concatenate([src_ref[pl.ds(s, rows, stride=ROW_TILES), :] for s in range(ROW_TILES)], axis=1)


LOOKAHEAD = 2
N_BUF = LOOKAHEAD + 1


GATHER_CHUNK = 1


def _expert_kernel(be_ref, nused_ref, nfetch_ref, *refs):
    tok_refs = refs[:N_BUF]
    x_hbm, wg_ref, wu_ref, wd_ref, y_ref = refs[N_BUF:N_BUF + 5]
    bufs = refs[N_BUF + 5:2 * N_BUF + 5]
    sem_ref, wgb_ref, wub_ref, wdb_ref = refs[2 * N_BUF + 5:2 * N_BUF + 9]
    i = pl.program_id(0)
    n_used = nused_ref[0]
    n_tiles = pl.num_programs(0)
    tb = EXPERT_TILE
    turn = lax.rem(i, N_BUF)

    def gather(tok_ref, p, tile):
        nfetch = jnp.where(tile < n_tiles, nfetch_ref[jnp.minimum(tile, n_tiles - 1)], 0)
        for c in range(tb // GATHER_CHUNK):
            @pl.when(c * GATHER_CHUNK < nfetch)
            def _(c=c):
                for r in range(c * GATHER_CHUNK, (c + 1) * GATHER_CHUNK):
                    row = pl.multiple_of(tok_ref[0, 0, r], ROW_TILES)
                    pltpu.make_async_copy(x_hbm.at[pl.ds(row, ROW_TILES), :],
                                          bufs[p].at[pl.ds(ROW_TILES * r, ROW_TILES), :],
                                          sem_ref.at[p]).start(priority=r % 2)

    def wait(p, tile):
        rows = pl.multiple_of(nfetch_ref[tile] * ROW_TILES, ROW_TILES)
        pltpu.make_async_copy(x_hbm.at[pl.ds(0, rows), :], bufs[p].at[pl.ds(0, rows), :],
                              sem_ref.at[p]).wait()

    @pl.when(i == 0)
    def _():
        for p in range(N_BUF):
            bufs[p][...] = jnp.zeros_like(bufs[p])
        for d in range(LOOKAHEAD):
            gather(tok_refs[d], d, d)

    @pl.when((i < n_used) & ((i == 0) | (be_ref[i] != be_ref[jnp.maximum(i - 1, 0)])))
    def _():
        wgb_ref[...] = wg_ref[0].astype(BF16)
        wub_ref[...] = wu_ref[0].astype(BF16)
        wdb_ref[...] = wd_ref[0].astype(BF16)

    for p in range(N_BUF):
        @pl.when((turn == p) & (i < n_used))
        def _(p=p):
            wait(p, i)
            gather(tok_refs[LOOKAHEAD], (p + LOOKAHEAD) % N_BUF, i + LOOKAHEAD)
            x = _from_row_tiles(bufs[p], tb).astype(BF16)
            g = jnp.dot(x, wgb_ref[...], preferred_element_type=F32)
            u = jnp.dot(x, wub_ref[...], preferred_element_type=F32)
            hid = (g * _sigmoid(g) * u).astype(BF16)
            _to_row_tiles(y_ref, jnp.dot(hid, wdb_ref[...], preferred_element_type=F32))

    @pl.when(i >= n_used)
    def _():
        y_ref[...] = jnp.zeros_like(y_ref)


def _experts(h1r, slot_row, block_expert, n_used, n_fetch, w_gate, w_up, w_down):
    n_slots = slot_row.shape[0]
    nblk = n_slots // EXPERT_TILE
    tok3 = slot_row.reshape(nblk, 1, EXPERT_TILE)
    ahead = lambda d: pl.BlockSpec((1, 1, EXPERT_TILE), lambda i, be, nu, nf: (jnp.minimum(i + d, nblk - 1), 0, 0),
                                   memory_space=pltpu.SMEM)
    tile_buf = pltpu.VMEM((EXPERT_TILE * ROW_TILES, LANES), F32)
    grid_spec = pltpu.PrefetchScalarGridSpec(
        num_scalar_prefetch=3,
        grid=(nblk,),
        in_specs=[ahead(d) for d in range(N_BUF)] + [
            pl.BlockSpec(memory_space=pl.ANY),
            pl.BlockSpec((1, D_MODEL, D_EXPERT), lambda i, be, nu, nf: (be[i], 0, 0)),
            pl.BlockSpec((1, D_MODEL, D_EXPERT), lambda i, be, nu, nf: (be[i], 0, 0)),
            pl.BlockSpec((1, D_EXPERT, D_MODEL), lambda i, be, nu, nf: (be[i], 0, 0))],
        out_specs=pl.BlockSpec((EXPERT_TILE * ROW_TILES, LANES), lambda i, be, nu, nf: (i, 0)),
        scratch_shapes=[tile_buf] * N_BUF + [
            pltpu.SemaphoreType.DMA((N_BUF,)),
            pltpu.VMEM((D_MODEL, D_EXPERT), BF16), pltpu.VMEM((D_MODEL, D_EXPERT), BF16),
            pltpu.VMEM((D_EXPERT, D_MODEL), BF16)],
    )
    return pl.pallas_call(
        _expert_kernel,
        grid_spec=grid_spec,
        out_shape=jax.ShapeDtypeStruct((n_slots * ROW_TILES, LANES), F32),
        compiler_params=_cparams(1),
        name="moe_experts",
    )(block_expert, n_used, n_fetch, *([tok3] * N_BUF), h1r, w_gate, w_up, w_down)


def _combine_kernel(*refs, first_tile, n_tiles, to_batch_major):
    pos_refs = refs[:N_BUF]
    yb_hbm, gate_ref, h_ref, g_ref, b_ref, o_ref, ybuf_ref, sem_ref = refs[N_BUF:N_BUF + 8]
    rest = refs[N_BUF + 8:]
    i = pl.program_id(0)
    tm = COMBINE_TILE
    turn = lax.rem(i, N_BUF)

    def gather(pos_ref, p):
        for r in range(tm):
            for kk in range(TOP_K_FINE):
                row = pl.multiple_of(pos_ref[0, 0, TOP_K_FINE * r + kk], ROW_TILES)
                pltpu.make_async_copy(yb_hbm.at[pl.ds(row, ROW_TILES), :],
                                      ybuf_ref.at[p, kk, pl.ds(ROW_TILES * r, ROW_TILES), :],
                                      sem_ref.at[p]).start(priority=kk)

    def finish(p, look_ahead):
        for kk in range(TOP_K_FINE):
            pltpu.make_async_copy(yb_hbm.at[pl.ds(0, tm * ROW_TILES), :], ybuf_ref.at[p, kk],
                                  sem_ref.at[p]).wait()
        if look_ahead:
            gather(pos_refs[LOOKAHEAD], (p + LOOKAHEAD) % N_BUF)
        gts = gate_ref[...]
        moe = (_from_row_tiles(ybuf_ref.at[p, 0], tm) * gts[:, 0:1]
               + _from_row_tiles(ybuf_ref.at[p, 1], tm) * gts[:, 1:2])
        h2 = _layer_norm_rows(ALPHA * h_ref[...] + moe, g_ref[...], b_ref[...])
        if not to_batch_major:
            o_ref[...] = h2
        else:
            hs_ref = rest[0]
            tl = tm // NB
            for s in range(D_MODEL // LANES):
                hs_ref[s] = h2[:, LANES * s:LANES * (s + 1)]
            for b in range(NB):
                for s in range(D_MODEL // LANES):
                    o_ref[b, :, LANES * s:LANES * (s + 1)] = hs_ref[s, pl.ds(b, tl, stride=NB), :]

    @pl.when(i == 0)
    def _():
        for d in range(LOOKAHEAD):
            gather(pos_refs[d], d)

    for p in range(N_BUF):
        @pl.when((turn == p) & (i < n_tiles - LOOKAHEAD))
        def _(p=p):
            finish(p, True)

    @pl.when(i >= n_tiles - LOOKAHEAD)
    def _():
        finish(turn, False)


def _combine(yb, pos, gates, h1, ln_g, ln_b, *, final):
    t_rows = h1.shape[0]
    tm = COMBINE_TILE
    first_tile = (LANES * NB) // tm if final else 0
    n_tiles = t_rows // tm - first_tile
    pos3 = pos.reshape(t_rows // tm, 1, TOP_K_FINE * tm)
    g2 = ln_g.astype(F32).reshape(1, D_MODEL)
    b2 = ln_b.astype(F32).reshape(1, D_MODEL)
    full = lambda a: pl.BlockSpec(a.shape, lambda i: (0,) * a.ndim)
    last = t_rows // tm - 1
    ahead = lambda d: pl.BlockSpec((1, 1, TOP_K_FINE * tm),
                                   lambda i: (jnp.minimum(i + first_tile + d, last), 0, 0),
                                   memory_space=pltpu.SMEM)
    in_specs = [ahead(d) for d in range(N_BUF)] + [
                pl.BlockSpec(memory_space=pl.ANY),
                pl.BlockSpec((tm, TOP_K_FINE), lambda i: (i + first_tile, 0)),
                pl.BlockSpec((tm, D_MODEL), lambda i: (i + first_tile, 0)),
                full(g2), full(b2)]
    scratch = [pltpu.VMEM((N_BUF, TOP_K_FINE, tm * ROW_TILES, LANES), F32), pltpu.SemaphoreType.DMA((N_BUF,))]
    if final:
        tl = tm // NB
        seq = t_rows // NB - LANES
        out_specs = pl.BlockSpec((NB, tl, D_MODEL), lambda i: (0, i, 0))
        out_shape = jax.ShapeDtypeStruct((NB, seq, D_MODEL), F32)
        scratch.append(pltpu.VMEM((D_MODEL // LANES, tm, LANES), F32))
    else:
        out_specs = pl.BlockSpec((tm, D_MODEL), lambda i: (i, 0))
        out_shape = jax.ShapeDtypeStruct((t_rows, D_MODEL), F32)
    return pl.pallas_call(
        functools.partial(_combine_kernel, first_tile=first_tile, n_tiles=n_tiles, to_batch_major=final),
        grid=(n_tiles,),
        in_specs=in_specs, out_specs=out_specs, out_shape=out_shape,
        scratch_shapes=scratch,
        compiler_params=_cparams(1),
        name="moe_combine_final" if final else "moe_combine",
    )(*([pos3] * N_BUF), yb, gates, h1, g2, b2)


def _moe_block(h1, h1r, route, counts, w_gate, w_up, w_down, ln_g, ln_b, *, final):
    t_rows = h1.shape[0]
    n_assign = (t_rows - PAD_ROWS) * TOP_K_FINE
    nblk = -(-(n_assign + N_EXPERTS * (EXPERT_TILE - 1)) // EXPERT_TILE)
    n_slots = nblk * EXPERT_TILE
    slot_row, block_expert, n_used, n_fetch, pos, gates = _slots(route, counts, n_slots)
    yb = _experts(h1r, slot_row, block_expert, n_used, n_fetch, w_gate, w_up, w_down)
    return _combine(yb, pos, gates, h1, ln_g, ln_b, final=final)


def _inproj1_kernel(h_ref, w_ref, gg_ref, rec_ref):
    x = h_ref[...].astype(BF16)
    z = jnp.dot(x, w_ref[...], preferred_element_type=F32)
    gg_ref[...] = _gelu(z[:, :D_RNN]).astype(BF16)
    rec_ref[...] = z[:, D_RNN:]


def _inproj1(h, w_bf16):
    t_rows = h.shape[0]
    tm = TOK_TILE
    row = lambda width: pl.BlockSpec((tm, width), lambda i: (i, 0))
    return pl.pallas_call(
        _inproj1_kernel,
        grid=(t_rows // tm,),
        in_specs=[row(D_MODEL), pl.BlockSpec(w_bf16.shape, lambda i: (0, 0))],
        out_specs=[row(D_RNN), row(D_RNN)],
        out_shape=[jax.ShapeDtypeStruct((t_rows, D_RNN), BF16), jax.ShapeDtypeStruct((t_rows, D_RNN), F32)],
        compiler_params=_cparams(1),
        name="l1_inproj",
    )(h, w_bf16)


CONV_HALO = (CONV_WIDTH - 1) * NB


def _rglru_kernel(rec_ref, gg_ref, cw_ref, cb_ref, wax_ref, ba_ref, bx_ref, sp_ref,
                  y_ref, rp_ref, a_ref, b_ref, st_ref, *, steps):
    i = pl.program_id(0)
    rows = steps * NB

    @pl.when(i == 0)
    def _():
        st_ref[...] = jnp.zeros_like(st_ref)
        rp_ref[0:CONV_HALO, :] = jnp.zeros((CONV_HALO, D_RNN), F32)

    row = i * rows + lax.broadcasted_iota(jnp.int32, (rows, 1), 0)
    real = row >= PAD_ROWS
    rp_ref[CONV_HALO:CONV_HALO + rows, :] = jnp.where(real, rec_ref[...], 0.0)
    xc = cb_ref[...] + sum(rp_ref[NB * j:NB * j + rows, :] * cw_ref[j:j + 1, :] for j in range(CONV_WIDTH))
    tail = rp_ref[rows:rows + CONV_HALO, :]
    rp_ref[0:CONV_HALO, :] = tail

    for n in range(LRU_BLOCKS):
        cs = slice(LRU_BLOCK_W * n, LRU_BLOCK_W * (n + 1))
        xb = xc[:, cs]
        ra = jnp.dot(xb.astype(BF16), wax_ref[n], preferred_element_type=F32)
        r = _sigmoid(ra[:, :LRU_BLOCK_W] + ba_ref[:, cs])
        ig = _sigmoid(ra[:, LRU_BLOCK_W:] + bx_ref[:, cs])
        log_a = -LRU_C * r * sp_ref[:, cs]
        a = jnp.exp(log_a)
        bt = jnp.sqrt(1.0 - a * a) * (ig * xb)
        a_ref[:, cs] = a
        b_ref[:, cs] = jnp.where(real, bt, 0.0)

    def step(t, h):
        r0 = pl.multiple_of(t * NB, NB)
        hn = a_ref[pl.ds(r0, NB), :] * h + b_ref[pl.ds(r0, NB), :]
        b_ref[pl.ds(r0, NB), :] = hn
        return hn

    st_ref[...] = lax.fori_loop(0, steps, step, st_ref[...])
    y_ref[...] = (gg_ref[...].astype(F32) * b_ref[...]).astype(BF16)


def _rglru(rec, gg, conv_w, conv_b, w_a, b_a, w_x, b_x, lru_lambda):
    t_rows = rec.shape[0]
    steps = SCAN_STEPS
    rows = steps * NB
    cw = conv_w.astype(F32)
    cb = conv_b.astype(F32).reshape(1, D_RNN)
    wax = jnp.concatenate([w_a.astype(F32), w_x.astype(F32)], axis=2).astype(BF16)
    ba = b_a.astype(F32).reshape(1, D_RNN)
    bx = b_x.astype(F32).reshape(1, D_RNN)
    sp = jax.nn.softplus(-lru_lambda.astype(F32)).reshape(1, D_RNN)
    full = lambda a: pl.BlockSpec(a.shape, lambda i: (0,) * a.ndim)
    row = pl.BlockSpec((rows, D_RNN), lambda i: (i, 0))
    return pl.pallas_call(
        functools.partial(_rglru_kernel, steps=steps),
        grid=(t_rows // rows,),
        in_specs=[row, row, full(cw), full(cb), full(wax), full(ba), full(bx), full(sp)],
        out_specs=row,
        out_shape=jax.ShapeDtypeStruct((t_rows, D_RNN), BF16),
        scratch_shapes=[pltpu.VMEM((rows + CONV_HALO, D_RNN), F32), pltpu.VMEM((rows, D_RNN), F32),
                        pltpu.VMEM((rows, D_RNN), F32), pltpu.VMEM((NB, D_RNN), F32)],
        compiler_params=_cparams(1),
        name="l1_rglru",
    )(rec, gg, cw, cb, wax, ba, bx, sp)


def kernel(x, meta, l0_ln1_g, l0_ln1_b, l0_w_in, l0_s5_lambda_re, l0_s5_lambda_im, l0_s5_log_dt, l0_s5_b_re, l0_s5_b_im, l0_s5_c_re, l0_s5_c_im, l0_s5_d, l0_s5_w_glu, l0_s5_b_glu, l0_da_lq1, l0_da_lk1, l0_da_lq2, l0_da_lk2, l0_da_subln_g, l0_w_out, l0_ln2_g, l0_ln2_b, l0_moe_w_coarse, l0_moe_b_coarse, l0_moe_w_fine, l0_moe_b_fine, l0_moe_w_gate, l0_moe_w_up, l0_moe_w_down, l1_ln1_g, l1_ln1_b, l1_w_in, l1_conv_w, l1_conv_b, l1_lru_w_a, l1_lru_b_a, l1_lru_w_x, l1_lru_b_x, l1_lru_lambda, l1_w_out, l1_ln2_g, l1_ln2_b, l1_moe_w_coarse, l1_moe_b_coarse, l1_moe_w_fine, l1_moe_b_fine, l1_moe_w_gate, l1_moe_w_up, l1_moe_w_down):
    bsz, seq, _ = x.shape
    assert bsz == NB and seq % Q_TILE == 0
    dt = x.dtype
    lp = FRONT_PAD + N_META + seq

    col_scale = jnp.concatenate([jnp.ones((S5_WIDTH,), F32),
                                 jnp.full((DA_WIDTH,), DA_HEAD_DIM ** -0.5, F32),
                                 jnp.ones((2 * DA_WIDTH,), F32)])
    w_in0 = (l0_w_in.astype(F32) * col_scale[None, :]).astype(BF16)
    h, u, q, k, v = _inproj0(x, meta, w_in0, lp)
    s5p = _s5_params(l0_s5_lambda_re, l0_s5_lambda_im, l0_s5_log_dt,
                     l0_s5_b_re, l0_s5_b_im, l0_s5_c_re, l0_s5_c_im)
    y_s5 = _s5(u, *s5p, l0_s5_d, l0_s5_w_glu, l0_s5_b_glu)
    lam_init = 0.8 - 0.6 * math.exp(-0.3 * 0)
    lam = (jnp.exp(jnp.sum(l0_da_lq1.astype(F32) * l0_da_lk1.astype(F32)))
           - jnp.exp(jnp.sum(l0_da_lq2.astype(F32) * l0_da_lk2.astype(F32))) + lam_init)
    y_da = _diffattn(q, k, v, lam, l0_da_subln_g, lam_init)
    router0 = _router_weights(l0_moe_w_coarse, l0_moe_b_coarse, l0_moe_w_fine, l0_moe_b_fine)
    h, hr, route, counts = _outproj(y_s5, y_da, h, l0_w_out, l0_ln1_g, l0_ln1_b, router0)
    h = _moe_block(h, hr, route, counts, l0_moe_w_gate, l0_moe_w_up, l0_moe_w_down, l0_ln2_g, l0_ln2_b, final=False)

    gg, rec = _inproj1(h, l1_w_in.astype(BF16))
    y = _rglru(rec, gg, l1_conv_w, l1_conv_b, l1_lru_w_a, l1_lru_b_a, l1_lru_w_x, l1_lru_b_x, l1_lru_lambda)
    router1 = _router_weights(l1_moe_w_coarse, l1_moe_b_coarse, l1_moe_w_fine, l1_moe_b_fine)
    h, hr, route, counts = _outproj(y, None, h, l1_w_out, l1_ln1_g, l1_ln1_b, router1)
    out = _moe_block(h, hr, route, counts, l1_moe_w_gate, l1_moe_w_up, l1_moe_w_down, l1_ln2_g, l1_ln2_b, final=True)
    return out.astype(dt)
```

```python
import functools
import math

import jax
import jax.numpy as jnp
from jax import lax
from jax.experimental import pallas as pl
from jax.experimental.pallas import tpu as pltpu

F32 = jnp.float32
BF16 = jnp.bfloat16

D_MODEL = 1024
DEPTH = 2
CHUNK = 64
N_META = 16
S5_WIDTH = 512
S5_GROUP = 16
S5_GROUPS = 32
S5_STATE = 64
DA_HEADS = 4
DA_HEAD_DIM = 64
DA_WIDTH = 512
D_RNN = 1280
LRU_BLOCKS = 10
LRU_BLOCK_W = 128
CONV_WIDTH = 4
LRU_C = 8.0
N_GROUPS = 4
EXPERTS_PER_GROUP = 8
N_EXPERTS = 32
TOP_K_FINE = 2
D_EXPERT = 512
ALPHA = (2 * DEPTH) ** 0.25
LN_EPS = 1e-5
NEG_INF = -1e30

NB = 8
LANES = 128
FRONT_PAD = LANES - N_META
PAD_ROWS = FRONT_PAD * NB
Q_TILE = 256
K_TILE = 512
TOK_TILE = 512
SCAN_STEPS = 64
EXPERT_TILE = 256
COMBINE_TILE = 256
VMEM_LIMIT = 48 * 1024 * 1024
LOG2_E = math.log2(math.e)


def _cparams(n_axes, vmem=VMEM_LIMIT):
    return pltpu.CompilerParams(dimension_semantics=("arbitrary",) * n_axes,
                                vmem_limit_bytes=vmem)


def _gelu(x):
    return 0.5 * x * (1.0 + jnp.tanh(math.sqrt(2.0 / math.pi) * (x + 0.044715 * (x * x * x))))


def _sigmoid(x):
    return 1.0 / (1.0 + jnp.exp(-x))


def _layer_norm_rows(r, g, b):
    mu = jnp.mean(r, axis=-1, keepdims=True)
    c = r - mu
    var = jnp.mean(c * c, axis=-1, keepdims=True)
    return c * lax.rsqrt(var + LN_EPS) * g + b


def _split_bf16(w):
    hi = w.astype(BF16)
    lo = (w - hi.astype(F32)).astype(BF16)
    return hi, lo


def _inproj0_kernel(head_ref, x_ref, w_ref, h_ref, u_ref, q_ref, k_ref, v_ref, zs_ref, *, tl, head_tiles):
    i = pl.program_id(0)

    @pl.when(i < head_tiles)
    def _():
        h_ref[...] = head_ref[...]

    @pl.when(i >= head_tiles)
    def _():
        for s in range(D_MODEL // LANES):
            for b in range(NB):
                zs_ref[s, pl.ds(b, tl, stride=NB), :] = x_ref[b, :, LANES * s:LANES * (s + 1)]
        for s in range(D_MODEL // LANES):
            h_ref[:, LANES * s:LANES * (s + 1)] = zs_ref[s]

    x = h_ref[...].astype(BF16)
    z = jnp.dot(x, w_ref[...], preferred_element_type=F32)
    u_ref[...] = z[:, :S5_WIDTH]
    n_slab = 3 * DA_WIDTH // LANES
    for s in range(n_slab):
        zs_ref[s] = z[:, S5_WIDTH + LANES * s:S5_WIDTH + LANES * (s + 1)]
    dsts = (q_ref, k_ref, v_ref)
    per = DA_WIDTH // LANES
    for b in range(NB):
        for s in range(n_slab):
            c = (s % per) * LANES
            blk = zs_ref[s, pl.ds(b, tl, stride=NB), :]
            if s < per:
                blk = blk * LOG2_E
            dsts[s // per][b, :, c:c + LANES] = blk.astype(BF16)


def _inproj0(x, meta, w_bf16, lp):
    t_rows = lp * NB
    tm = TOK_TILE
    tl = tm // NB
    head_tiles = LANES // tl
    head = jnp.concatenate([
        jnp.zeros((FRONT_PAD, NB, D_MODEL), x.dtype),
        jnp.broadcast_to(meta.astype(x.dtype)[:, None, :], (N_META, NB, D_MODEL))], axis=0)
    head = head.reshape(LANES * NB, D_MODEL)
    qkv_shape = jax.ShapeDtypeStruct((NB, lp, DA_WIDTH), BF16)
    qkv_spec = pl.BlockSpec((NB, tl, DA_WIDTH), lambda i: (0, i, 0))
    n_slab = max(3 * DA_WIDTH, D_MODEL) // LANES
    return pl.pallas_call(
        functools.partial(_inproj0_kernel, tl=tl, head_tiles=head_tiles),
        grid=(t_rows // tm,),
        in_specs=[pl.BlockSpec((tm, D_MODEL), lambda i: (jnp.minimum(i, head_tiles - 1), 0)),
                  pl.BlockSpec((NB, tl, D_MODEL), lambda i: (0, jnp.maximum(i - head_tiles, 0), 0)),
                  pl.BlockSpec(w_bf16.shape, lambda i: (0, 0))],
        out_specs=[pl.BlockSpec((tm, D_MODEL), lambda i: (i, 0)),
                   pl.BlockSpec((tm, S5_WIDTH), lambda i: (i, 0)), qkv_spec, qkv_spec, qkv_spec],
        out_shape=[jax.ShapeDtypeStruct((t_rows, D_MODEL), F32),
                   jax.ShapeDtypeStruct((t_rows, S5_WIDTH), F32), qkv_shape, qkv_shape, qkv_shape],
        scratch_shapes=[pltpu.VMEM((n_slab, tm, LANES), F32)],
        compiler_params=_cparams(1),
        name="l0_inproj",
    )(head, x, w_bf16)


S5_SLABS = S5_WIDTH // LANES
S5_SLAB_STATE = (S5_GROUPS // S5_SLABS) * S5_STATE
S5_NSTATE = S5_GROUPS * S5_STATE


def _s5_kernel(u_ref, bmat_ref, are_ref, aim_ref, cre_ref, cim_ref, d_ref, wglu_ref, bglu_ref,
               y_ref, hre_ref, him_ref, st_ref, *, steps):
    i = pl.program_id(0)
    rows = steps * NB

    @pl.when(i == 0)
    def _():
        st_ref[...] = jnp.zeros_like(st_ref)

    row = i * rows + lax.broadcasted_iota(jnp.int32, (rows, 1), 0)
    u = jnp.where(row >= PAD_ROWS, u_ref[...], 0.0)
    ub = u.astype(BF16)
    for s in range(S5_SLABS):
        bu = jnp.dot(ub[:, LANES * s:LANES * (s + 1)], bmat_ref[s], preferred_element_type=F32)
        hre_ref[:, S5_SLAB_STATE * s:S5_SLAB_STATE * (s + 1)] = bu[:, :S5_SLAB_STATE]
        him_ref[:, S5_SLAB_STATE * s:S5_SLAB_STATE * (s + 1)] = bu[:, S5_SLAB_STATE:]

    cw = 512
    for cg in range(S5_NSTATE // cw):
        c0 = cg * cw
        a_r = are_ref[:, c0:c0 + cw]
        a_i = aim_ref[:, c0:c0 + cw]

        def step(t, carry, c0=c0, a_r=a_r, a_i=a_i):
            sr, si = carry
            r0 = pl.multiple_of(t * NB, NB)
            br = hre_ref[pl.ds(r0, NB), c0:c0 + cw]
            bi = him_ref[pl.ds(r0, NB), c0:c0 + cw]
            nr = a_r * sr - a_i * si + br
            ni = a_r * si + a_i * sr + bi
            hre_ref[pl.ds(r0, NB), c0:c0 + cw] = nr
            him_ref[pl.ds(r0, NB), c0:c0 + cw] = ni
            return nr, ni

        sr, si = lax.fori_loop(0, steps, step, (st_ref[0, :, c0:c0 + cw], st_ref[1, :, c0:c0 + cw]))
        st_ref[0, :, c0:c0 + cw] = sr
        st_ref[1, :, c0:c0 + cw] = si

    ys = []
    for s in range(S5_SLABS):
        hr = hre_ref[:, S5_SLAB_STATE * s:S5_SLAB_STATE * (s + 1)].astype(BF16)
        hi = him_ref[:, S5_SLAB_STATE * s:S5_SLAB_STATE * (s + 1)].astype(BF16)
        ys.append(jnp.dot(hr, cre_ref[s], preferred_element_type=F32)
                  + jnp.dot(hi, cim_ref[s], preferred_element_type=F32))
    y = jnp.concatenate(ys, axis=1) + d_ref[...] * u
    y = _gelu(y)
    gate = _sigmoid(jnp.dot(y.astype(BF16), wglu_ref[...], preferred_element_type=F32) + bglu_ref[...])
    y_ref[...] = (y * gate).astype(BF16)


def _s5_params(lam_re, lam_im, log_dt, b_re, b_im, c_re, c_im):
    dt = jnp.exp(log_dt.astype(F32))[:, None]
    lr = jnp.minimum(lam_re.astype(F32), -1e-4)
    li = lam_im.astype(F32)
    mag = jnp.exp(lr * dt)
    ar = mag * jnp.cos(li * dt)
    ai = mag * jnp.sin(li * dt)
    den = lr * lr + li * li
    nr, ni = ar - 1.0, ai
    fr = ((nr * lr + ni * li) / den)[..., None]
    fi = ((ni * lr - nr * li) / den)[..., None]
    br, bi = b_re.astype(F32), b_im.astype(F32)
    bbr = fr * br - fi * bi
    bbi = fr * bi + fi * br
    gps = S5_GROUPS // S5_SLABS
    eye = jnp.eye(gps, dtype=F32)

    def in_slab(m):
        m4 = m.reshape(S5_SLABS, gps, S5_STATE, S5_GROUP)
        return jnp.einsum('sgph,gk->sghkp', m4, eye).reshape(S5_SLABS, LANES, S5_SLAB_STATE)

    def out_slab(m):
        m4 = m.reshape(S5_SLABS, gps, S5_GROUP, S5_STATE)
        return jnp.einsum('sghp,gk->sgpkh', m4, eye).reshape(S5_SLABS, S5_SLAB_STATE, LANES)

    bmat = jnp.concatenate([in_slab(bbr), in_slab(bbi)], axis=2).astype(BF16)
    cre = out_slab(c_re.astype(F32)).astype(BF16)
    cim = out_slab(-c_im.astype(F32)).astype(BF16)
    a_re = jnp.broadcast_to(ar.reshape(1, S5_NSTATE), (NB, S5_NSTATE))
    a_im = jnp.broadcast_to(ai.reshape(1, S5_NSTATE), (NB, S5_NSTATE))
    return bmat, a_re, a_im, cre, cim


def _s5(u, bmat, a_re, a_im, cre, cim, d, w_glu, b_glu):
    t_rows = u.shape[0]
    steps = SCAN_STEPS
    rows = steps * NB
    full = lambda a: pl.BlockSpec(a.shape, lambda i: (0,) * a.ndim)
    d2 = d.astype(F32).reshape(1, S5_WIDTH)
    bg2 = b_glu.astype(F32).reshape(1, S5_WIDTH)
    wg = w_glu.astype(BF16)
    return pl.pallas_call(
        functools.partial(_s5_kernel, steps=steps),
        grid=(t_rows // rows,),
        in_specs=[pl.BlockSpec((rows, S5_WIDTH), lambda i: (i, 0)),
                  full(bmat), full(a_re), full(a_im), full(cre), full(cim), full(d2), full(wg), full(bg2)],
        out_specs=pl.BlockSpec((rows, S5_WIDTH), lambda i: (i, 0)),
        out_shape=jax.ShapeDtypeStruct((t_rows, S5_WIDTH), BF16),
        scratch_shapes=[pltpu.VMEM((rows, S5_NSTATE), F32), pltpu.VMEM((rows, S5_NSTATE), F32),
                        pltpu.VMEM((2, NB, S5_NSTATE), F32)],
        compiler_params=_cparams(1),
        name="l0_s5",
    )(u, bmat, a_re, a_im, cre, cim, d2, wg, bg2)


def _diffattn_kernel(lam_ref, q_ref, k_ref, v_ref, g_ref, o_ref,
                     s_ref, qm_ref, m_ref, a_ref, *, lam_init, lp):
    lam = lam_ref[0]
    tq, tk = Q_TILE, K_TILE
    lane = lax.broadcasted_iota(jnp.int32, (tq, LANES), 1)
    qrow = jnp.bitwise_and(lax.broadcasted_iota(jnp.int32, (2 * tq, tk), 0), tq - 1)
    kloc = lax.broadcasted_iota(jnp.int32, (2 * tq, tk), 1)
    nt = (((1,), (1,)), ((), ()))
    bf16_rows = 16

    def key_start(j):
        return pl.multiple_of(jnp.minimum(FRONT_PAD + j * tk, lp - tk), bf16_rows)

    def chunk_of(pos):
        return jnp.right_shift(pos - CHUNK, 6)

    def lane_fold(x, op):
        r = x[:, :LANES]
        for c in range(1, x.shape[1] // LANES):
            r = op(r, x[:, LANES * c:LANES * (c + 1)])
        return r

    def q_start(i):
        return pl.multiple_of(jnp.minimum(i * tq, lp - tq), LANES)

    def n_key_tiles(i):
        return (q_start(i) + tq - FRONT_PAD + tk - 1) // tk

    def n_full_tiles(i):
        return jnp.maximum(q_start(i) - FRONT_PAD, 0) // tk


    def prep(i, par):
        q = q_ref[0, pl.ds(q_start(i), tq), :]
        zero = jnp.zeros_like(q)
        qm_ref[par, :tq] = jnp.where(lane < DA_HEAD_DIM, q, zero)
        qm_ref[par, tq:] = jnp.where(lane >= DA_HEAD_DIM, q, zero)
        m_ref[par] = jnp.full(m_ref.shape[1:], NEG_INF, F32)

    def scores(i, par, j, masked):
        k0 = key_start(j)
        kt = k_ref[0, pl.ds(k0, tk), :]
        if masked:
            kpos = k0 + kloc
            mask = (chunk_of(kpos) <= chunk_of(q_start(i) + qrow)) & (kpos >= FRONT_PAD + j * tk)
        s = lax.dot_general(qm_ref[par], kt, nt, preferred_element_type=F32)
        if masked:
            s = jnp.where(mask, s, NEG_INF)
        s_ref[par, j] = s
        m_ref[par] = jnp.maximum(m_ref[par], lane_fold(s, jnp.maximum))

    ones_col = jnp.where(lax.broadcasted_iota(jnp.int32, (tk, LANES), 1) == 0, 1.0, 0.0).astype(BF16)

    def values(par, j, m):
        vt = jnp.concatenate([v_ref[0, pl.ds(key_start(j), tk), :], ones_col], axis=1)
        p = jnp.exp2((s_ref[par, j] - m).astype(BF16))
        a_ref[...] += jnp.dot(p, vt, preferred_element_type=F32)

    def row_max(par):
        a_ref[...] = jnp.zeros(a_ref.shape, F32)
        return jnp.max(m_ref[par], axis=-1, keepdims=True)

    def finish(i):
        l1 = a_ref[:tq, LANES:LANES + 1]
        l2 = a_ref[tq:, LANES:LANES + 1]
        o = a_ref[:tq, :LANES] / l1 - lam * (a_ref[tq:, :LANES] / l2)
        o = o * lax.rsqrt(jnp.mean(o * o, axis=-1, keepdims=True) + LN_EPS) * g_ref[...]
        o_ref[0, pl.ds(q_start(i), tq), :] = o * (1.0 - lam_init)

    def loop(lo, hi, body):
        def wrapped(j, _):
            body(j)
            return 0
        lax.fori_loop(lo, hi, wrapped, 0)

    def step(i, par):
        m = row_max(par)
        prep(i + 1, 1 - par)

        def both(j, masked):
            scores(i + 1, 1 - par, j, masked)
            values(par, j, m)

        loop(0, n_full_tiles(i + 1), functools.partial(both, masked=False))
        loop(n_full_tiles(i + 1), n_key_tiles(i), functools.partial(both, masked=True))
        loop(n_key_tiles(i), n_key_tiles(i + 1), lambda j: scores(i + 1, 1 - par, j, True))
        finish(i)

    def step_pair(i2):
        step(2 * i2, 0)
        step(2 * i2 + 1, 1)

    n_q = pl.cdiv(lp, tq)
    prep(0, 0)
    loop(0, n_key_tiles(0), lambda j: scores(0, 0, j, True))
    loop(0, (n_q - 1) // 2, step_pair)
    if (n_q - 1) % 2:
        step(n_q - 2, 0)
    last = (n_q - 1) % 2
    m = row_max(last)
    loop(0, n_key_tiles(n_q - 1), lambda j: values(last, j, m))
    finish(n_q - 1)


def _diffattn(q, k, v, lam, subln_g, lam_init):
    nb, lp, _ = q.shape
    g2 = subln_g.astype(F32).reshape(1, 2 * DA_HEAD_DIM)
    seq_spec = pl.BlockSpec((1, lp, LANES), lambda b, h: (b, 0, h))
    return pl.pallas_call(
        functools.partial(_diffattn_kernel, lam_init=lam_init, lp=lp),
        grid=(nb, DA_HEADS),
        in_specs=[pl.BlockSpec(memory_space=pltpu.SMEM), seq_spec, seq_spec, seq_spec,
                  pl.BlockSpec((1, LANES), lambda b, h: (0, 0))],
        out_specs=seq_spec,
        out_shape=jax.ShapeDtypeStruct((nb, lp, DA_WIDTH), F32),
        scratch_shapes=[pltpu.VMEM((2, pl.cdiv(lp, K_TILE), 2 * Q_TILE, K_TILE), F32),
                        pltpu.VMEM((2, 2 * Q_TILE, LANES), BF16),
                        pltpu.VMEM((2, 2 * Q_TILE, LANES), F32),
                        pltpu.VMEM((2 * Q_TILE, 2 * LANES), F32)],
        compiler_params=_cparams(2),
        name="l0_diffattn",
    )(lam.reshape(1), q, k, v, g2)


ROUTE_ROWS = 64
ROUTE_E, ROUTE_GATE, ROUTE_RANK = 0, 2, 4


def _router_logits_t(h1, whi_ref, wlo_ref, rb_ref):
    hi = h1.astype(BF16)
    lo = (h1 - hi.astype(F32)).astype(BF16)
    nt = (((1,), (1,)), ((), ()))
    return (lax.dot_general(whi_ref[...], hi, nt, preferred_element_type=F32)
            + lax.dot_general(whi_ref[...], lo, nt, preferred_element_type=F32)
            + lax.dot_general(wlo_ref[...], hi, nt, preferred_element_type=F32) + rb_ref[...])


def _route_cols(lg, cnt_ref, tri_ref, first_tok):
    toks = lg.shape[1]
    row = lax.broadcasted_iota(jnp.int32, lg.shape, 0)
    rowf = row.astype(F32)
    valid = (first_tok + lax.broadcasted_iota(jnp.int32, (1, toks), 1)) >= PAD_ROWS
    ninf = float('-inf')
    first = lambda hit: jnp.min(jnp.where(hit, rowf, float(ROUTE_ROWS)), axis=0, keepdims=True)
    cm = jnp.where(row < N_GROUPS, lg, ninf)
    cmax = jnp.max(cm, axis=0, keepdims=True)
    p_grp = 1.0 / jnp.sum(jnp.exp(cm - cmax), axis=0, keepdims=True)
    lo = N_GROUPS + EXPERTS_PER_GROUP * first(cm == cmax)
    fm = jnp.where((rowf >= lo) & (rowf < lo + EXPERTS_PER_GROUP), lg, ninf)
    v1 = jnp.max(fm, axis=0, keepdims=True)
    i1 = first(fm == v1)
    fm2 = jnp.where(rowf == i1, ninf, fm)
    v2 = jnp.max(fm2, axis=0, keepdims=True)
    i2 = first(fm2 == v2)
    t = jnp.exp(v2 - v1)
    g1 = p_grp / (1.0 + t)
    g2 = p_grp * t / (1.0 + t)
    oh1 = (rowf == i1) & valid
    oh2 = (rowf == i2) & valid
    m = jnp.where(oh1 | oh2, 1.0, 0.0)
    before = jnp.dot(m.astype(BF16), tri_ref[...], preferred_element_type=F32) + cnt_ref[:, 0:1]
    r1 = jnp.sum(jnp.where(oh1, before, 0.0), axis=0, keepdims=True)
    r2 = jnp.sum(jnp.where(oh2, before, 0.0), axis=0, keepdims=True)
    cnt_ref[...] = cnt_ref[...] + jnp.sum(m, axis=1, keepdims=True)
    zero = jnp.zeros_like(g1)
    rec_row = lax.broadcasted_iota(jnp.int32, (NB, toks), 0)
    rec = jnp.zeros((NB, toks), F32)
    for k, val in enumerate((i1 - N_GROUPS, i2 - N_GROUPS, jnp.where(valid, g1, zero),
                             jnp.where(valid, g2, zero), r1, r2)):
        rec = jnp.where(rec_row == k, val, rec)
    return rec


def _route_tile(h1, whi_ref, wlo_ref, rb_ref, rt_ref, cnt_ref, tri_ref):
    i = pl.program_id(0)
    toks = h1.shape[0]

    @pl.when(i == 0)
    def _():
        cnt_ref[...] = jnp.zeros_like(cnt_ref)
        ri = lax.broadcasted_iota(jnp.int32, (toks, toks), 0)
        ci = lax.broadcasted_iota(jnp.int32, (toks, toks), 1)
        tri_ref[...] = jnp.where(ri < ci, 1.0, 0.0).astype(BF16)

    lg = _router_logits_t(h1, whi_ref, wlo_ref, rb_ref)
    rt_ref[...] = _route_cols(lg, cnt_ref, tri_ref, i * toks)


def _outproj0_kernel(ys_ref, yda_ref, h_ref, w_ref, g_ref, b_ref, whi_ref, wlo_ref, rb_ref,
                     h1_ref, h1r_ref, rt_ref, cnt_ref, das_ref, tri_ref, *, tl):
    per = DA_WIDTH // LANES
    for b in range(NB):
        for s in range(per):
            das_ref[s, pl.ds(b, tl, stride=NB), :] = yda_ref[b, :, LANES * s:LANES * (s + 1)]
    da = jnp.concatenate([das_ref[s] for s in range(per)], axis=1).astype(BF16)
    mix = (jnp.dot(ys_ref[...], w_ref[:S5_WIDTH, :], preferred_element_type=F32)
           + jnp.dot(da, w_ref[S5_WIDTH:, :], preferred_element_type=F32))
    h1 = _layer_norm_rows(ALPHA * h_ref[...] + mix, g_ref[...], b_ref[...])
    h1_ref[...] = h1
    _to_row_tiles(h1r_ref, h1)
    _route_tile(h1, whi_ref, wlo_ref, rb_ref, rt_ref, cnt_ref, tri_ref)


def _outproj1_kernel(y_ref, h_ref, w_ref, g_ref, b_ref, whi_ref, wlo_ref, rb_ref,
                     h1_ref, h1r_ref, rt_ref, cnt_ref, tri_ref):
    mix = jnp.dot(y_ref[...], w_ref[...], preferred_element_type=F32)
    h1 = _layer_norm_rows(ALPHA * h_ref[...] + mix, g_ref[...], b_ref[...])
    h1_ref[...] = h1
    _to_row_tiles(h1r_ref, h1)
    _route_tile(h1, whi_ref, wlo_ref, rb_ref, rt_ref, cnt_ref, tri_ref)


def _router_weights(w_coarse, b_coarse, w_fine, b_fine):
    wf = jnp.transpose(w_fine.astype(F32), (1, 0, 2)).reshape(D_MODEL, N_EXPERTS)
    w = jnp.concatenate([w_coarse.astype(F32), wf], axis=1).T
    w = jnp.pad(w, ((0, ROUTE_ROWS - w.shape[0]), (0, 0)))
    b = jnp.concatenate([b_coarse.astype(F32), b_fine.astype(F32).reshape(-1)])
    b = jnp.pad(b, (0, ROUTE_ROWS - b.shape[0])).reshape(ROUTE_ROWS, 1)
    whi, wlo = _split_bf16(w)
    return whi, wlo, b


def _outproj(ys, yda, h, w_out, ln_g, ln_b, router):
    t_rows = h.shape[0]
    tm = TOK_TILE
    tl = tm // NB
    whi, wlo, rb = router
    w = w_out.astype(BF16)
    g2 = ln_g.astype(F32).reshape(1, D_MODEL)
    b2 = ln_b.astype(F32).reshape(1, D_MODEL)
    full = lambda a: pl.BlockSpec(a.shape, lambda i: (0,) * a.ndim)
    row = lambda width: pl.BlockSpec((tm, width), lambda i: (i, 0))
    common_in = [row(D_MODEL), full(w), full(g2), full(b2), full(whi), full(wlo), full(rb)]
    out_specs = [row(D_MODEL), pl.BlockSpec((tm * ROW_TILES, LANES), lambda i: (i, 0)),
                 pl.BlockSpec((NB, tm), lambda i: (0, i)),
                 pl.BlockSpec((ROUTE_ROWS, LANES), lambda i: (0, 0))]
    out_shape = [jax.ShapeDtypeStruct((t_rows, D_MODEL), F32),
                 jax.ShapeDtypeStruct((t_rows * ROW_TILES, LANES), F32),
                 jax.ShapeDtypeStruct((NB, t_rows), F32),
                 jax.ShapeDtypeStruct((ROUTE_ROWS, LANES), F32)]
    tri = pltpu.VMEM((tm, tm), BF16)
    if yda is not None:
        return pl.pallas_call(
            functools.partial(_outproj0_kernel, tl=tl),
            grid=(t_rows // tm,),
            in_specs=[row(S5_WIDTH), pl.BlockSpec((NB, tl, DA_WIDTH), lambda i: (0, i, 0))] + common_in,
            out_specs=out_specs, out_shape=out_shape,
            scratch_shapes=[pltpu.VMEM((DA_WIDTH // LANES, tm, LANES), F32), tri],
            compiler_params=_cparams(1),
            name="l0_outproj",
        )(ys, yda, h, w, g2, b2, whi, wlo, rb)
    return pl.pallas_call(
        _outproj1_kernel,
        grid=(t_rows // tm,),
        in_specs=[row(ys.shape[1])] + common_in,
        out_specs=out_specs, out_shape=out_shape,
        scratch_shapes=[tri],
        compiler_params=_cparams(1),
        name="l1_outproj",
    )(ys, h, w, g2, b2, whi, wlo, rb)


def _slots(route, counts, n_slots):
    t_rows = route.shape[1]
    experts = route[ROUTE_E:ROUTE_E + TOP_K_FINE].T.astype(jnp.int32)
    gates = route[ROUTE_GATE:ROUTE_GATE + TOP_K_FINE].T
    rank = route[ROUTE_RANK:ROUTE_RANK + TOP_K_FINE].T.astype(jnp.int32)
    cnt = counts[N_GROUPS:N_GROUPS + N_EXPERTS, 0].astype(jnp.int32)
    padded = (cnt + EXPERT_TILE - 1) // EXPERT_TILE * EXPERT_TILE
    pad_end = jnp.cumsum(padded)
    pad_start = pad_end - padded
    raw_start = jnp.cumsum(cnt) - cnt
    valid = (jnp.arange(t_rows) >= PAD_ROWS)[:, None]
    e_ids = jnp.arange(N_EXPERTS, dtype=jnp.int32)
    start_of = jnp.sum(jnp.where(experts[..., None] == e_ids, pad_start, 0), axis=-1)
    dest = jnp.where(valid, start_of + rank, n_slots).astype(jnp.int32)
    nblk = n_slots // EXPERT_TILE
    blk_start = jnp.arange(nblk, dtype=jnp.int32) * EXPERT_TILE
    block_expert = jnp.minimum(jnp.sum((pad_end[None, :] <= blk_start[:, None]).astype(jnp.int32), axis=1),
                               N_EXPERTS - 1)
    hit = block_expert[:, None] == e_ids[None, :]
    blk_pad_start = jnp.sum(jnp.where(hit, pad_start, 0), axis=1)
    blk_raw_start = jnp.sum(jnp.where(hit, raw_start, 0), axis=1)
    blk_cnt = jnp.sum(jnp.where(hit, cnt, 0), axis=1)
    n_used = (pad_end[-1] // EXPERT_TILE).astype(jnp.int32)
    blk_rows = jnp.where(jnp.arange(nblk) < n_used,
                         jnp.clip(blk_cnt - (blk_start - blk_pad_start), 0, EXPERT_TILE), 0)
    n_fetch = (blk_rows + GATHER_CHUNK - 1) // GATHER_CHUNK * GATHER_CHUNK
    order = jnp.argsort(dest.reshape(-1)).astype(jnp.int32)
    off = (blk_start - blk_pad_start)[:, None] + jnp.arange(EXPERT_TILE, dtype=jnp.int32)[None, :]
    src = jnp.clip(blk_raw_start[:, None] + off, 0, order.shape[0] - 1)
    slot_tok = jnp.where(off < blk_cnt[:, None], jnp.right_shift(order[src], 1), 0).reshape(-1)
    pos = jnp.where(valid, dest, 0).reshape(-1)
    return (slot_tok * ROW_TILES, block_expert, n_used.reshape(1), n_fetch.astype(jnp.int32),
            pos * ROW_TILES, gates)


ROW_TILES = D_MODEL // LANES


def _to_row_tiles(dst_ref, val):
    rows = val.shape[0]
    for s in range(ROW_TILES):
        dst_ref[pl.ds(s, rows, stride=ROW_TILES), :] = val[:, LANES * s:LANES * (s + 1)]


def _from_row_tiles(src_ref, rows):
    return jnp.concatenate([src_ref[pl.ds(s, rows, stride=ROW_TILES), :] for s in range(ROW_TILES)], axis=1)


LOOKAHEAD = 2
N_BUF = LOOKAHEAD + 1


GATHER_CHUNK = 1


def _expert_kernel(be_ref, nused_ref, nfetch_ref, *refs):
    tok_refs = refs[:N_BUF]
    x_hbm, wg_ref, wu_ref, wd_ref, y_ref = refs[N_BUF:N_BUF + 5]
    bufs = refs[N_BUF + 5:2 * N_BUF + 5]
    sem_ref, wgb_ref, wub_ref, wdb_ref = refs[2 * N_BUF + 5:2 * N_BUF + 9]
    i = pl.program_id(0)
    n_used = nused_ref[0]
    n_tiles = pl.num_programs(0)
    tb = EXPERT_TILE
    turn = lax.rem(i, N_BUF)

    def gather(tok_ref, p, tile):
        nfetch = jnp.where(tile < n_tiles, nfetch_ref[jnp.minimum(tile, n_tiles - 1)], 0)
        for c in range(tb // GATHER_CHUNK):
            @pl.when(c * GATHER_CHUNK < nfetch)
            def _(c=c):
                for r in range(c * GATHER_CHUNK, (c + 1) * GATHER_CHUNK):
                    row = pl.multiple_of(tok_ref[0, 0, r], ROW_TILES)
                    pltpu.make_async_copy(x_hbm.at[pl.ds(row, ROW_TILES), :],
                                          bufs[p].at[pl.ds(ROW_TILES * r, ROW_TILES), :],
                                          sem_ref.at[p]).start(priority=r % 2)

    def wait(p, tile):
        rows = pl.multiple_of(nfetch_ref[tile] * ROW_TILES, ROW_TILES)
        pltpu.make_async_copy(x_hbm.at[pl.ds(0, rows), :], bufs[p].at[pl.ds(0, rows), :],
                              sem_ref.at[p]).wait()

    @pl.when(i == 0)
    def _():
        for p in range(N_BUF):
            bufs[p][...] = jnp.zeros_like(bufs[p])
        for d in range(LOOKAHEAD):
            gather(tok_refs[d], d, d)

    @pl.when((i < n_used) & ((i == 0) | (be_ref[i] != be_ref[jnp.maximum(i - 1, 0)])))
    def _():
        wgb_ref[...] = wg_ref[0].astype(BF16)
        wub_ref[...] = wu_ref[0].astype(BF16)
        wdb_ref[...] = wd_ref[0].astype(BF16)

    for p in range(N_BUF):
        @pl.when((turn == p) & (i < n_used))
        def _(p=p):
            wait(p, i)
            gather(tok_refs[LOOKAHEAD], (p + LOOKAHEAD) % N_BUF, i + LOOKAHEAD)
            x = _from_row_tiles(bufs[p], tb).astype(BF16)
            g = jnp.dot(x, wgb_ref[...], preferred_element_type=F32)
            u = jnp.dot(x, wub_ref[...], preferred_element_type=F32)
            hid = (g * _sigmoid(g) * u).astype(BF16)
            _to_row_tiles(y_ref, jnp.dot(hid, wdb_ref[...], preferred_element_type=F32))

    @pl.when(i >= n_used)
    def _():
        y_ref[...] = jnp.zeros_like(y_ref)


def _experts(h1r, slot_row, block_expert, n_used, n_fetch, w_gate, w_up, w_down):
    n_slots = slot_row.shape[0]
    nblk = n_slots // EXPERT_TILE
    tok3 = slot_row.reshape(nblk, 1, EXPERT_TILE)
    ahead = lambda d: pl.BlockSpec((1, 1, EXPERT_TILE), lambda i, be, nu, nf: (jnp.minimum(i + d, nblk - 1), 0, 0),
                                   memory_space=pltpu.SMEM)
    tile_buf = pltpu.VMEM((EXPERT_TILE * ROW_TILES, LANES), F32)
    grid_spec = pltpu.PrefetchScalarGridSpec(
        num_scalar_prefetch=3,
        grid=(nblk,),
        in_specs=[ahead(d) for d in range(N_BUF)] + [
            pl.BlockSpec(memory_space=pl.ANY),
            pl.BlockSpec((1, D_MODEL, D_EXPERT), lambda i, be, nu, nf: (be[i], 0, 0)),
            pl.BlockSpec((1, D_MODEL, D_EXPERT), lambda i, be, nu, nf: (be[i], 0, 0)),
            pl.BlockSpec((1, D_EXPERT, D_MODEL), lambda i, be, nu, nf: (be[i], 0, 0))],
        out_specs=pl.BlockSpec((EXPERT_TILE * ROW_TILES, LANES), lambda i, be, nu, nf: (i, 0)),
        scratch_shapes=[tile_buf] * N_BUF + [
            pltpu.SemaphoreType.DMA((N_BUF,)),
            pltpu.VMEM((D_MODEL, D_EXPERT), BF16), pltpu.VMEM((D_MODEL, D_EXPERT), BF16),
            pltpu.VMEM((D_EXPERT, D_MODEL), BF16)],
    )
    return pl.pallas_call(
        _expert_kernel,
        grid_spec=grid_spec,
        out_shape=jax.ShapeDtypeStruct((n_slots * ROW_TILES, LANES), F32),
        compiler_params=_cparams(1),
        name="moe_experts",
    )(block_expert, n_used, n_fetch, *([tok3] * N_BUF), h1r, w_gate, w_up, w_down)


def _combine_kernel(*refs, n_tiles, to_batch_major):
    pos_refs = refs[:N_BUF]
    yb_hbm, gate_ref, h_ref, g_ref, b_ref = refs[N_BUF:N_BUF + 5]
    if to_batch_major:
        o_ref, ybuf_ref, sem_ref, hs_ref = refs[N_BUF + 5:]
    else:
        w_ref, o_ref, gg_ref, rec_ref, ybuf_ref, sem_ref = refs[N_BUF + 5:]
    i = pl.program_id(0)
    tm = COMBINE_TILE
    turn = lax.rem(i, N_BUF)

    def gather(pos_ref, p):
        for r in range(tm):
            for kk in range(TOP_K_FINE):
                row = pl.multiple_of(pos_ref[0, 0, TOP_K_FINE * r + kk], ROW_TILES)
                pltpu.make_async_copy(yb_hbm.at[pl.ds(row, ROW_TILES), :],
                                      ybuf_ref.at[p, kk, pl.ds(ROW_TILES * r, ROW_TILES), :],
                                      sem_ref.at[p]).start(priority=kk)

    def finish(p, look_ahead):
        for kk in range(TOP_K_FINE):
            pltpu.make_async_copy(yb_hbm.at[pl.ds(0, tm * ROW_TILES), :], ybuf_ref.at[p, kk],
                                  sem_ref.at[p]).wait()
        if look_ahead:
            gather(pos_refs[LOOKAHEAD], (p + LOOKAHEAD) % N_BUF)
        gts = gate_ref[...]
        moe = (_from_row_tiles(ybuf_ref.at[p, 0], tm) * gts[:, 0:1]
               + _from_row_tiles(ybuf_ref.at[p, 1], tm) * gts[:, 1:2])
        h2 = _layer_norm_rows(ALPHA * h_ref[...] + moe, g_ref[...], b_ref[...])
        if not to_batch_major:
            o_ref[...] = h2
            z = jnp.dot(h2.astype(BF16), w_ref[...], preferred_element_type=F32)
            gg_ref[...] = _gelu(z[:, :D_RNN]).astype(BF16)
            rec_ref[...] = z[:, D_RNN:]
        else:
            tl = tm // NB
            for s in range(D_MODEL // LANES):
                hs_ref[s] = h2[:, LANES * s:LANES * (s + 1)]
            for b in range(NB):
                for s in range(D_MODEL // LANES):
                    o_ref[b, :, LANES * s:LANES * (s + 1)] = hs_ref[s, pl.ds(b, tl, stride=NB), :]

    @pl.when(i == 0)
    def _():
        for d in range(LOOKAHEAD):
            gather(pos_refs[d], d)

    for p in range(N_BUF):
        @pl.when((turn == p) & (i < n_tiles - LOOKAHEAD))
        def _(p=p):
            finish(p, True)

    @pl.when(i >= n_tiles - LOOKAHEAD)
    def _():
        finish(turn, False)


def _combine(yb, pos, gates, h1, ln_g, ln_b, w_next=None):
    final = w_next is None
    t_rows = h1.shape[0]
    tm = COMBINE_TILE
    first_tile = (LANES * NB) // tm if final else 0
    n_tiles = t_rows // tm - first_tile
    pos3 = pos.reshape(t_rows // tm, 1, TOP_K_FINE * tm)
    g2 = ln_g.astype(F32).reshape(1, D_MODEL)
    b2 = ln_b.astype(F32).reshape(1, D_MODEL)
    full = lambda a: pl.BlockSpec(a.shape, lambda i: (0,) * a.ndim)
    last = t_rows // tm - 1
    ahead = lambda d: pl.BlockSpec((1, 1, TOP_K_FINE * tm),
                                   lambda i: (jnp.minimum(i + first_tile + d, last), 0, 0),
                                   memory_space=pltpu.SMEM)
    in_specs = [ahead(d) for d in range(N_BUF)] + [
                pl.BlockSpec(memory_space=pl.ANY),
                pl.BlockSpec((tm, TOP_K_FINE), lambda i: (i + first_tile, 0)),
                pl.BlockSpec((tm, D_MODEL), lambda i: (i + first_tile, 0)),
                full(g2), full(b2)]
    scratch = [pltpu.VMEM((N_BUF, TOP_K_FINE, tm * ROW_TILES, LANES), F32), pltpu.SemaphoreType.DMA((N_BUF,))]
    if final:
        tl = tm // NB
        seq = t_rows // NB - LANES
        out_specs = pl.BlockSpec((NB, tl, D_MODEL), lambda i: (0, i, 0))
        out_shape = jax.ShapeDtypeStruct((NB, seq, D_MODEL), F32)
        scratch.append(pltpu.VMEM((D_MODEL // LANES, tm, LANES), F32))
        operands = (yb, gates, h1, g2, b2)
    else:
        row = lambda width: pl.BlockSpec((tm, width), lambda i: (i, 0))
        in_specs.append(full(w_next))
        out_specs = [row(D_MODEL), row(D_RNN), row(D_RNN)]
        out_shape = [jax.ShapeDtypeStruct((t_rows, D_MODEL), F32),
                     jax.ShapeDtypeStruct((t_rows, D_RNN), BF16),
                     jax.ShapeDtypeStruct((t_rows, D_RNN), F32)]
        operands = (yb, gates, h1, g2, b2, w_next)
    return pl.pallas_call(
        functools.partial(_combine_kernel, n_tiles=n_tiles, to_batch_major=final),
        grid=(n_tiles,),
        in_specs=in_specs, out_specs=out_specs, out_shape=out_shape,
        scratch_shapes=scratch,
        compiler_params=_cparams(1),
        name="moe_combine_final" if final else "moe_combine_inproj",
    )(*([pos3] * N_BUF), *operands)


def _moe_block(h1, h1r, route, counts, w_gate, w_up, w_down, ln_g, ln_b, w_next=None):
    t_rows = h1.shape[0]
    n_assign = (t_rows - PAD_ROWS) * TOP_K_FINE
    nblk = -(-(n_assign + N_EXPERTS * (EXPERT_TILE - 1)) // EXPERT_TILE)
    n_slots = nblk * EXPERT_TILE
    slot_row, block_expert, n_used, n_fetch, pos, gates = _slots(route, counts, n_slots)
    yb = _experts(h1r, slot_row, block_expert, n_used, n_fetch, w_gate, w_up, w_down)
    return _combine(yb, pos, gates, h1, ln_g, ln_b, w_next)


CONV_HALO = (CONV_WIDTH - 1) * NB


def _rglru_kernel(rec_ref, gg_ref, cw_ref, cb_ref, wax_ref, ba_ref, bx_ref, sp_ref,
                  y_ref, rp_ref, a_ref, b_ref, st_ref, *, steps):
    i = pl.program_id(0)
    rows = steps * NB

    @pl.when(i == 0)
    def _():
        st_ref[...] = jnp.zeros_like(st_ref)
        rp_ref[0:CONV_HALO, :] = jnp.zeros((CONV_HALO, D_RNN), F32)

    row = i * rows + lax.broadcasted_iota(jnp.int32, (rows, 1), 0)
    real = row >= PAD_ROWS
    rp_ref[CONV_HALO:CONV_HALO + rows, :] = jnp.where(real, rec_ref[...], 0.0)
    xc = cb_ref[...] + sum(rp_ref[NB * j:NB * j + rows, :] * cw_ref[j:j + 1, :] for j in range(CONV_WIDTH))
    tail = rp_ref[rows:rows + CONV_HALO, :]
    rp_ref[0:CONV_HALO, :] = tail

    for n in range(LRU_BLOCKS):
        cs = slice(LRU_BLOCK_W * n, LRU_BLOCK_W * (n + 1))
        xb = xc[:, cs]
        ra = jnp.dot(xb.astype(BF16), wax_ref[n], preferred_element_type=F32)
        r = _sigmoid(ra[:, :LRU_BLOCK_W] + ba_ref[:, cs])
        ig = _sigmoid(ra[:, LRU_BLOCK_W:] + bx_ref[:, cs])
        log_a = -LRU_C * r * sp_ref[:, cs]
        a = jnp.exp(log_a)
        bt = jnp.sqrt(1.0 - a * a) * (ig * xb)
        a_ref[:, cs] = a
        b_ref[:, cs] = jnp.where(real, bt, 0.0)

    def step(t, h):
        r0 = pl.multiple_of(t * NB, NB)
        hn = a_ref[pl.ds(r0, NB), :] * h + b_ref[pl.ds(r0, NB), :]
        b_ref[pl.ds(r0, NB), :] = hn
        return hn

    st_ref[...] = lax.fori_loop(0, steps, step, st_ref[...])
    y_ref[...] = (gg_ref[...].astype(F32) * b_ref[...]).astype(BF16)


def _rglru(rec, gg, conv_w, conv_b, w_a, b_a, w_x, b_x, lru_lambda):
    t_rows = rec.shape[0]
    steps = SCAN_STEPS
    rows = steps * NB
    cw = conv_w.astype(F32)
    cb = conv_b.astype(F32).reshape(1, D_RNN)
    wax = jnp.concatenate([w_a.astype(F32), w_x.astype(F32)], axis=2).astype(BF16)
    ba = b_a.astype(F32).reshape(1, D_RNN)
    bx = b_x.astype(F32).reshape(1, D_RNN)
    sp = jax.nn.softplus(-lru_lambda.astype(F32)).reshape(1, D_RNN)
    full = lambda a: pl.BlockSpec(a.shape, lambda i: (0,) * a.ndim)
    row = pl.BlockSpec((rows, D_RNN), lambda i: (i, 0))
    return pl.pallas_call(
        functools.partial(_rglru_kernel, steps=steps),
        grid=(t_rows // rows,),
        in_specs=[row, row, full(cw), full(cb), full(wax), full(ba), full(bx), full(sp)],
        out_specs=row,
        out_shape=jax.ShapeDtypeStruct((t_rows, D_RNN), BF16),
        scratch_shapes=[pltpu.VMEM((rows + CONV_HALO, D_RNN), F32), pltpu.VMEM((rows, D_RNN), F32),
                        pltpu.VMEM((rows, D_RNN), F32), pltpu.VMEM((NB, D_RNN), F32)],
        compiler_params=_cparams(1),
        name="l1_rglru",
    )(rec, gg, cw, cb, wax, ba, bx, sp)


def kernel(x, meta, l0_ln1_g, l0_ln1_b, l0_w_in, l0_s5_lambda_re, l0_s5_lambda_im, l0_s5_log_dt, l0_s5_b_re, l0_s5_b_im, l0_s5_c_re, l0_s5_c_im, l0_s5_d, l0_s5_w_glu, l0_s5_b_glu, l0_da_lq1, l0_da_lk1, l0_da_lq2, l0_da_lk2, l0_da_subln_g, l0_w_out, l0_ln2_g, l0_ln2_b, l0_moe_w_coarse, l0_moe_b_coarse, l0_moe_w_fine, l0_moe_b_fine, l0_moe_w_gate, l0_moe_w_up, l0_moe_w_down, l1_ln1_g, l1_ln1_b, l1_w_in, l1_conv_w, l1_conv_b, l1_lru_w_a, l1_lru_b_a, l1_lru_w_x, l1_lru_b_x, l1_lru_lambda, l1_w_out, l1_ln2_g, l1_ln2_b, l1_moe_w_coarse, l1_moe_b_coarse, l1_moe_w_fine, l1_moe_b_fine, l1_moe_w_gate, l1_moe_w_up, l1_moe_w_down):
    bsz, seq, _ = x.shape
    assert bsz == NB and seq % Q_TILE == 0
    dt = x.dtype
    lp = FRONT_PAD + N_META + seq

    col_scale = jnp.concatenate([jnp.ones((S5_WIDTH,), F32),
                                 jnp.full((DA_WIDTH,), DA_HEAD_DIM ** -0.5, F32),
                                 jnp.ones((2 * DA_WIDTH,), F32)])
    w_in0 = (l0_w_in.astype(F32) * col_scale[None, :]).astype(BF16)
    h, u, q, k, v = _inproj0(x, meta, w_in0, lp)
    s5p = _s5_params(l0_s5_lambda_re, l0_s5_lambda_im, l0_s5_log_dt,
                     l0_s5_b_re, l0_s5_b_im, l0_s5_c_re, l0_s5_c_im)
    y_s5 = _s5(u, *s5p, l0_s5_d, l0_s5_w_glu, l0_s5_b_glu)
    lam_init = 0.8 - 0.6 * math.exp(-0.3 * 0)
    lam = (jnp.exp(jnp.sum(l0_da_lq1.astype(F32) * l0_da_lk1.astype(F32)))
           - jnp.exp(jnp.sum(l0_da_lq2.astype(F32) * l0_da_lk2.astype(F32))) + lam_init)
    y_da = _diffattn(q, k, v, lam, l0_da_subln_g, lam_init)
    router0 = _router_weights(l0_moe_w_coarse, l0_moe_b_coarse, l0_moe_w_fine, l0_moe_b_fine)
    h, hr, route, counts = _outproj(y_s5, y_da, h, l0_w_out, l0_ln1_g, l0_ln1_b, router0)
    h, gg, rec = _moe_block(h, hr, route, counts, l0_moe_w_gate, l0_moe_w_up, l0_moe_w_down,
                            l0_ln2_g, l0_ln2_b, l1_w_in.astype(BF16))

    y = _rglru(rec, gg, l1_conv_w, l1_conv_b, l1_lru_w_a, l1_lru_b_a, l1_lru_w_x, l1_lru_b_x, l1_lru_lambda)
    router1 = _router_weights(l1_moe_w_coarse, l1_moe_b_coarse, l1_moe_w_fine, l1_moe_b_fine)
    h, hr, route, counts = _outproj(y, None, h, l1_w_out, l1_ln1_g, l1_ln1_b, router1)
    out = _moe_block(h, hr, route, counts, l1_moe_w_gate, l1_moe_w_up, l1_moe_w_down, l1_ln2_g, l1_ln2_b)
    return out.astype(dt)
```

```python
import functools
import math

import jax
import jax.numpy as jnp
from jax import lax
from jax.experimental import pallas as pl
from jax.experimental.pallas import tpu as pltpu

F32 = jnp.float32
BF16 = jnp.bfloat16

D_MODEL = 1024
DEPTH = 2
CHUNK = 64
N_META = 16
S5_WIDTH = 512
S5_GROUP = 16
S5_GROUPS = 32
S5_STATE = 64
DA_HEADS = 4
DA_HEAD_DIM = 64
DA_WIDTH = 512
D_RNN = 1280
LRU_BLOCKS = 10
LRU_BLOCK_W = 128
CONV_WIDTH = 4
LRU_C = 8.0
N_GROUPS = 4
EXPERTS_PER_GROUP = 8
N_EXPERTS = 32
TOP_K_FINE = 2
D_EXPERT = 512
ALPHA = (2 * DEPTH) ** 0.25
LN_EPS = 1e-5
NEG_INF = -1e30

NB = 8
LANES = 128
FRONT_PAD = LANES - N_META
PAD_ROWS = FRONT_PAD * NB
Q_TILE = 256
K_TILE = 512
TOK_TILE = 512
SCAN_STEPS = 64
EXPERT_TILE = 512
COMBINE_TILE = 256
VMEM_LIMIT = 48 * 1024 * 1024
LOG2_E = math.log2(math.e)


def _cparams(n_axes, vmem=VMEM_LIMIT):
    return pltpu.CompilerParams(dimension_semantics=("arbitrary",) * n_axes,
                                vmem_limit_bytes=vmem)


def _gelu(x):
    return 0.5 * x * (1.0 + jnp.tanh(math.sqrt(2.0 / math.pi) * (x + 0.044715 * (x * x * x))))


def _sigmoid(x):
    return 1.0 / (1.0 + jnp.exp(-x))


def _layer_norm_rows(r, g, b):
    mu = jnp.mean(r, axis=-1, keepdims=True)
    c = r - mu
    var = jnp.mean(c * c, axis=-1, keepdims=True)
    return c * lax.rsqrt(var + LN_EPS) * g + b


def _split_bf16(w):
    hi = w.astype(BF16)
    lo = (w - hi.astype(F32)).astype(BF16)
    return hi, lo


def _inproj0_kernel(head_ref, x_ref, w_ref, h_ref, u_ref, q_ref, k_ref, v_ref, zs_ref, *, tl, head_tiles):
    i = pl.program_id(0)

    @pl.when(i < head_tiles)
    def _():
        h_ref[...] = head_ref[...]

    @pl.when(i >= head_tiles)
    def _():
        for s in range(D_MODEL // LANES):
            for b in range(NB):
                zs_ref[s, pl.ds(b, tl, stride=NB), :] = x_ref[b, :, LANES * s:LANES * (s + 1)]
        for s in range(D_MODEL // LANES):
            h_ref[:, LANES * s:LANES * (s + 1)] = zs_ref[s]

    x = h_ref[...].astype(BF16)
    z = jnp.dot(x, w_ref[...], preferred_element_type=F32)
    u_ref[...] = z[:, :S5_WIDTH]
    n_slab = 3 * DA_WIDTH // LANES
    for s in range(n_slab):
        zs_ref[s] = z[:, S5_WIDTH + LANES * s:S5_WIDTH + LANES * (s + 1)]
    dsts = (q_ref, k_ref, v_ref)
    per = DA_WIDTH // LANES
    for b in range(NB):
        for s in range(n_slab):
            c = (s % per) * LANES
            blk = zs_ref[s, pl.ds(b, tl, stride=NB), :]
            if s < per:
                blk = blk * LOG2_E
            dsts[s // per][b, :, c:c + LANES] = blk.astype(BF16)


def _inproj0(x, meta, w_bf16, lp):
    t_rows = lp * NB
    tm = TOK_TILE
    tl = tm // NB
    head_tiles = LANES // tl
    head = jnp.concatenate([
        jnp.zeros((FRONT_PAD, NB, D_MODEL), x.dtype),
        jnp.broadcast_to(meta.astype(x.dtype)[:, None, :], (N_META, NB, D_MODEL))], axis=0)
    head = head.reshape(LANES * NB, D_MODEL)
    qkv_shape = jax.ShapeDtypeStruct((NB, lp, DA_WIDTH), BF16)
    qkv_spec = pl.BlockSpec((NB, tl, DA_WIDTH), lambda i: (0, i, 0))
    n_slab = max(3 * DA_WIDTH, D_MODEL) // LANES
    return pl.pallas_call(
        functools.partial(_inproj0_kernel, tl=tl, head_tiles=head_tiles),
        grid=(t_rows // tm,),
        in_specs=[pl.BlockSpec((tm, D_MODEL), lambda i: (jnp.minimum(i, head_tiles - 1), 0)),
                  pl.BlockSpec((NB, tl, D_MODEL), lambda i: (0, jnp.maximum(i - head_tiles, 0), 0)),
                  pl.BlockSpec(w_bf16.shape, lambda i: (0, 0))],
        out_specs=[pl.BlockSpec((tm, D_MODEL), lambda i: (i, 0)),
                   pl.BlockSpec((tm, S5_WIDTH), lambda i: (i, 0)), qkv_spec, qkv_spec, qkv_spec],
        out_shape=[jax.ShapeDtypeStruct((t_rows, D_MODEL), F32),
                   jax.ShapeDtypeStruct((t_rows, S5_WIDTH), F32), qkv_shape, qkv_shape, qkv_shape],
        scratch_shapes=[pltpu.VMEM((n_slab, tm, LANES), F32)],
        compiler_params=_cparams(1),
        name="l0_inproj",
    )(head, x, w_bf16)


S5_SLABS = S5_WIDTH // LANES
S5_SLAB_STATE = (S5_GROUPS // S5_SLABS) * S5_STATE
S5_NSTATE = S5_GROUPS * S5_STATE


def _s5_kernel(u_ref, bmat_ref, are_ref, aim_ref, cre_ref, cim_ref, d_ref, wglu_ref, bglu_ref,
               y_ref, hre_ref, him_ref, st_ref, *, steps):
    i = pl.program_id(0)
    rows = steps * NB

    @pl.when(i == 0)
    def _():
        st_ref[...] = jnp.zeros_like(st_ref)

    row = i * rows + lax.broadcasted_iota(jnp.int32, (rows, 1), 0)
    u = jnp.where(row >= PAD_ROWS, u_ref[...], 0.0)
    ub = u.astype(BF16)
    for s in range(S5_SLABS):
        bu = jnp.dot(ub[:, LANES * s:LANES * (s + 1)], bmat_ref[s], preferred_element_type=F32)
        hre_ref[:, S5_SLAB_STATE * s:S5_SLAB_STATE * (s + 1)] = bu[:, :S5_SLAB_STATE]
        him_ref[:, S5_SLAB_STATE * s:S5_SLAB_STATE * (s + 1)] = bu[:, S5_SLAB_STATE:]

    cw = 512
    for cg in range(S5_NSTATE // cw):
        c0 = cg * cw
        a_r = are_ref[:, c0:c0 + cw]
        a_i = aim_ref[:, c0:c0 + cw]

        def step(t, carry, c0=c0, a_r=a_r, a_i=a_i):
            sr, si = carry
            r0 = pl.multiple_of(t * NB, NB)
            br = hre_ref[pl.ds(r0, NB), c0:c0 + cw]
            bi = him_ref[pl.ds(r0, NB), c0:c0 + cw]
            nr = a_r * sr - a_i * si + br
            ni = a_r * si + a_i * sr + bi
            hre_ref[pl.ds(r0, NB), c0:c0 + cw] = nr
            him_ref[pl.ds(r0, NB), c0:c0 + cw] = ni
            return nr, ni

        sr, si = lax.fori_loop(0, steps, step, (st_ref[0, :, c0:c0 + cw], st_ref[1, :, c0:c0 + cw]))
        st_ref[0, :, c0:c0 + cw] = sr
        st_ref[1, :, c0:c0 + cw] = si

    ys = []
    for s in range(S5_SLABS):
        hr = hre_ref[:, S5_SLAB_STATE * s:S5_SLAB_STATE * (s + 1)].astype(BF16)
        hi = him_ref[:, S5_SLAB_STATE * s:S5_SLAB_STATE * (s + 1)].astype(BF16)
        ys.append(jnp.dot(hr, cre_ref[s], preferred_element_type=F32)
                  + jnp.dot(hi, cim_ref[s], preferred_element_type=F32))
    y = jnp.concatenate(ys, axis=1) + d_ref[...] * u
    y = _gelu(y)
    gate = _sigmoid(jnp.dot(y.astype(BF16), wglu_ref[...], preferred_element_type=F32) + bglu_ref[...])
    y_ref[...] = (y * gate).astype(BF16)


def _s5_params(lam_re, lam_im, log_dt, b_re, b_im, c_re, c_im):
    dt = jnp.exp(log_dt.astype(F32))[:, None]
    lr = jnp.minimum(lam_re.astype(F32), -1e-4)
    li = lam_im.astype(F32)
    mag = jnp.exp(lr * dt)
    ar = mag * jnp.cos(li * dt)
    ai = mag * jnp.sin(li * dt)
    den = lr * lr + li * li
    nr, ni = ar - 1.0, ai
    fr = ((nr * lr + ni * li) / den)[..., None]
    fi = ((ni * lr - nr * li) / den)[..., None]
    br, bi = b_re.astype(F32), b_im.astype(F32)
    bbr = fr * br - fi * bi
    bbi = fr * bi + fi * br
    gps = S5_GROUPS // S5_SLABS
    eye = jnp.eye(gps, dtype=F32)

    def in_slab(m):
        m4 = m.reshape(S5_SLABS, gps, S5_STATE, S5_GROUP)
        return jnp.einsum('sgph,gk->sghkp', m4, eye).reshape(S5_SLABS, LANES, S5_SLAB_STATE)

    def out_slab(m):
        m4 = m.reshape(S5_SLABS, gps, S5_GROUP, S5_STATE)
        return jnp.einsum('sghp,gk->sgpkh', m4, eye).reshape(S5_SLABS, S5_SLAB_STATE, LANES)

    bmat = jnp.concatenate([in_slab(bbr), in_slab(bbi)], axis=2).astype(BF16)
    cre = out_slab(c_re.astype(F32)).astype(BF16)
    cim = out_slab(-c_im.astype(F32)).astype(BF16)
    a_re = jnp.broadcast_to(ar.reshape(1, S5_NSTATE), (NB, S5_NSTATE))
    a_im = jnp.broadcast_to(ai.reshape(1, S5_NSTATE), (NB, S5_NSTATE))
    return bmat, a_re, a_im, cre, cim


def _s5(u, bmat, a_re, a_im, cre, cim, d, w_glu, b_glu):
    t_rows = u.shape[0]
    steps = SCAN_STEPS
    rows = steps * NB
    full = lambda a: pl.BlockSpec(a.shape, lambda i: (0,) * a.ndim)
    d2 = d.astype(F32).reshape(1, S5_WIDTH)
    bg2 = b_glu.astype(F32).reshape(1, S5_WIDTH)
    wg = w_glu.astype(BF16)
    return pl.pallas_call(
        functools.partial(_s5_kernel, steps=steps),
        grid=(t_rows // rows,),
        in_specs=[pl.BlockSpec((rows, S5_WIDTH), lambda i: (i, 0)),
                  full(bmat), full(a_re), full(a_im), full(cre), full(cim), full(d2), full(wg), full(bg2)],
        out_specs=pl.BlockSpec((rows, S5_WIDTH), lambda i: (i, 0)),
        out_shape=jax.ShapeDtypeStruct((t_rows, S5_WIDTH), BF16),
        scratch_shapes=[pltpu.VMEM((rows, S5_NSTATE), F32), pltpu.VMEM((rows, S5_NSTATE), F32),
                        pltpu.VMEM((2, NB, S5_NSTATE), F32)],
        compiler_params=_cparams(1),
        name="l0_s5",
    )(u, bmat, a_re, a_im, cre, cim, d2, wg, bg2)


def _diffattn_kernel(lam_ref, q_ref, k_ref, v_ref, g_ref, o_ref,
                     s_ref, qm_ref, m_ref, a_ref, *, lam_init, lp):
    lam = lam_ref[0]
    tq, tk = Q_TILE, K_TILE
    lane = lax.broadcasted_iota(jnp.int32, (tq, LANES), 1)
    qrow = jnp.bitwise_and(lax.broadcasted_iota(jnp.int32, (2 * tq, tk), 0), tq - 1)
    kloc = lax.broadcasted_iota(jnp.int32, (2 * tq, tk), 1)
    nt = (((1,), (1,)), ((), ()))
    bf16_rows = 16

    def key_start(j):
        return pl.multiple_of(jnp.minimum(FRONT_PAD + j * tk, lp - tk), bf16_rows)

    def chunk_of(pos):
        return jnp.right_shift(pos - CHUNK, 6)

    def lane_fold(x, op):
        r = x[:, :LANES]
        for c in range(1, x.shape[1] // LANES):
            r = op(r, x[:, LANES * c:LANES * (c + 1)])
        return r

    def q_start(i):
        return pl.multiple_of(jnp.minimum(i * tq, lp - tq), LANES)

    def n_key_tiles(i):
        return (q_start(i) + tq - FRONT_PAD + tk - 1) // tk

    def n_full_tiles(i):
        return jnp.maximum(q_start(i) - FRONT_PAD, 0) // tk


    def prep(i, par):
        q = q_ref[0, pl.ds(q_start(i), tq), :]
        zero = jnp.zeros_like(q)
        qm_ref[par, :tq] = jnp.where(lane < DA_HEAD_DIM, q, zero)
        qm_ref[par, tq:] = jnp.where(lane >= DA_HEAD_DIM, q, zero)
        m_ref[par] = jnp.full(m_ref.shape[1:], NEG_INF, F32)

    def scores(i, par, j, masked):
        k0 = key_start(j)
        kt = k_ref[0, pl.ds(k0, tk), :]
        if masked:
            kpos = k0 + kloc
            mask = (chunk_of(kpos) <= chunk_of(q_start(i) + qrow)) & (kpos >= FRONT_PAD + j * tk)
        s = lax.dot_general(qm_ref[par], kt, nt, preferred_element_type=F32)
        if masked:
            s = jnp.where(mask, s, NEG_INF)
        s_ref[par, j] = s
        m_ref[par] = jnp.maximum(m_ref[par], lane_fold(s, jnp.maximum))

    ones_col = jnp.where(lax.broadcasted_iota(jnp.int32, (tk, LANES), 1) == 0, 1.0, 0.0).astype(BF16)

    def values(par, j, m):
        vt = jnp.concatenate([v_ref[0, pl.ds(key_start(j), tk), :], ones_col], axis=1)
        p = jnp.exp2((s_ref[par, j] - m).astype(BF16))
        a_ref[...] += jnp.dot(p, vt, preferred_element_type=F32)

    def row_max(par):
        a_ref[...] = jnp.zeros(a_ref.shape, F32)
        return jnp.max(m_ref[par], axis=-1, keepdims=True)

    def finish(i):
        l1 = a_ref[:tq, LANES:LANES + 1]
        l2 = a_ref[tq:, LANES:LANES + 1]
        o = a_ref[:tq, :LANES] / l1 - lam * (a_ref[tq:, :LANES] / l2)
        o = o * lax.rsqrt(jnp.mean(o * o, axis=-1, keepdims=True) + LN_EPS) * g_ref[...]
        o_ref[0, pl.ds(q_start(i), tq), :] = o * (1.0 - lam_init)

    def loop(lo, hi, body):
        def wrapped(j, _):
            body(j)
            return 0
        lax.fori_loop(lo, hi, wrapped, 0)

    def step(i, par):
        m = row_max(par)
        prep(i + 1, 1 - par)

        def both(j, masked):
            scores(i + 1, 1 - par, j, masked)
            values(par, j, m)

        loop(0, n_full_tiles(i + 1), functools.partial(both, masked=False))
        loop(n_full_tiles(i + 1), n_key_tiles(i), functools.partial(both, masked=True))
        loop(n_key_tiles(i), n_key_tiles(i + 1), lambda j: scores(i + 1, 1 - par, j, True))
        finish(i)

    def step_pair(i2):
        step(2 * i2, 0)
        step(2 * i2 + 1, 1)

    n_q = pl.cdiv(lp, tq)
    prep(0, 0)
    loop(0, n_key_tiles(0), lambda j: scores(0, 0, j, True))
    loop(0, (n_q - 1) // 2, step_pair)
    if (n_q - 1) % 2:
        step(n_q - 2, 0)
    last = (n_q - 1) % 2
    m = row_max(last)
    loop(0, n_key_tiles(n_q - 1), lambda j: values(last, j, m))
    finish(n_q - 1)


def _diffattn(q, k, v, lam, subln_g, lam_init):
    nb, lp, _ = q.shape
    g2 = subln_g.astype(F32).reshape(1, 2 * DA_HEAD_DIM)
    seq_spec = pl.BlockSpec((1, lp, LANES), lambda b, h: (b, 0, h))
    return pl.pallas_call(
        functools.partial(_diffattn_kernel, lam_init=lam_init, lp=lp),
        grid=(nb, DA_HEADS),
        in_specs=[pl.BlockSpec(memory_space=pltpu.SMEM), seq_spec, seq_spec, seq_spec,
                  pl.BlockSpec((1, LANES), lambda b, h: (0, 0))],
        out_specs=seq_spec,
        out_shape=jax.ShapeDtypeStruct((nb, lp, DA_WIDTH), F32),
        scratch_shapes=[pltpu.VMEM((2, pl.cdiv(lp, K_TILE), 2 * Q_TILE, K_TILE), F32),
                        pltpu.VMEM((2, 2 * Q_TILE, LANES), BF16),
                        pltpu.VMEM((2, 2 * Q_TILE, LANES), F32),
                        pltpu.VMEM((2 * Q_TILE, 2 * LANES), F32)],
        compiler_params=_cparams(2),
        name="l0_diffattn",
    )(lam.reshape(1), q, k, v, g2)


ROUTE_ROWS = 64
ROUTE_E, ROUTE_GATE, ROUTE_RANK = 0, 2, 4


def _router_logits_t(h1, whi_ref, wlo_ref, rb_ref):
    hi = h1.astype(BF16)
    lo = (h1 - hi.astype(F32)).astype(BF16)
    nt = (((1,), (1,)), ((), ()))
    return (lax.dot_general(whi_ref[...], hi, nt, preferred_element_type=F32)
            + lax.dot_general(whi_ref[...], lo, nt, preferred_element_type=F32)
            + lax.dot_general(wlo_ref[...], hi, nt, preferred_element_type=F32) + rb_ref[...])


def _route_cols(lg, cnt_ref, tri_ref, first_tok):
    toks = lg.shape[1]
    row = lax.broadcasted_iota(jnp.int32, lg.shape, 0)
    rowf = row.astype(F32)
    valid = (first_tok + lax.broadcasted_iota(jnp.int32, (1, toks), 1)) >= PAD_ROWS
    ninf = float('-inf')
    first = lambda hit: jnp.min(jnp.where(hit, rowf, float(ROUTE_ROWS)), axis=0, keepdims=True)
    cm = jnp.where(row < N_GROUPS, lg, ninf)
    cmax = jnp.max(cm, axis=0, keepdims=True)
    p_grp = 1.0 / jnp.sum(jnp.exp(cm - cmax), axis=0, keepdims=True)
    lo = N_GROUPS + EXPERTS_PER_GROUP * first(cm == cmax)
    fm = jnp.where((rowf >= lo) & (rowf < lo + EXPERTS_PER_GROUP), lg, ninf)
    v1 = jnp.max(fm, axis=0, keepdims=True)
    i1 = first(fm == v1)
    fm2 = jnp.where(rowf == i1, ninf, fm)
    v2 = jnp.max(fm2, axis=0, keepdims=True)
    i2 = first(fm2 == v2)
    t = jnp.exp(v2 - v1)
    g1 = p_grp / (1.0 + t)
    g2 = p_grp * t / (1.0 + t)
    oh1 = (rowf == i1) & valid
    oh2 = (rowf == i2) & valid
    m = jnp.where(oh1 | oh2, 1.0, 0.0)
    before = jnp.dot(m.astype(BF16), tri_ref[...], preferred_element_type=F32) + cnt_ref[:, 0:1]
    r1 = jnp.sum(jnp.where(oh1, before, 0.0), axis=0, keepdims=True)
    r2 = jnp.sum(jnp.where(oh2, before, 0.0), axis=0, keepdims=True)
    cnt_ref[...] = cnt_ref[...] + jnp.sum(m, axis=1, keepdims=True)
    zero = jnp.zeros_like(g1)
    rec_row = lax.broadcasted_iota(jnp.int32, (NB, toks), 0)
    rec = jnp.zeros((NB, toks), F32)
    for k, val in enumerate((i1 - N_GROUPS, i2 - N_GROUPS, jnp.where(valid, g1, zero),
                             jnp.where(valid, g2, zero), r1, r2)):
        rec = jnp.where(rec_row == k, val, rec)
    return rec


def _route_tile(h1, whi_ref, wlo_ref, rb_ref, rt_ref, cnt_ref, tri_ref):
    i = pl.program_id(0)
    toks = h1.shape[0]

    @pl.when(i == 0)
    def _():
        cnt_ref[...] = jnp.zeros_like(cnt_ref)
        ri = lax.broadcasted_iota(jnp.int32, (toks, toks), 0)
        ci = lax.broadcasted_iota(jnp.int32, (toks, toks), 1)
        tri_ref[...] = jnp.where(ri < ci, 1.0, 0.0).astype(BF16)

    lg = _router_logits_t(h1, whi_ref, wlo_ref, rb_ref)
    rt_ref[...] = _route_cols(lg, cnt_ref, tri_ref, i * toks)


def _outproj0_kernel(ys_ref, yda_ref, h_ref, w_ref, g_ref, b_ref, whi_ref, wlo_ref, rb_ref,
                     h1_ref, h1r_ref, rt_ref, cnt_ref, das_ref, tri_ref, *, tl):
    per = DA_WIDTH // LANES
    for b in range(NB):
        for s in range(per):
            das_ref[s, pl.ds(b, tl, stride=NB), :] = yda_ref[b, :, LANES * s:LANES * (s + 1)]
    da = jnp.concatenate([das_ref[s] for s in range(per)], axis=1).astype(BF16)
    mix = (jnp.dot(ys_ref[...], w_ref[:S5_WIDTH, :], preferred_element_type=F32)
           + jnp.dot(da, w_ref[S5_WIDTH:, :], preferred_element_type=F32))
    h1 = _layer_norm_rows(ALPHA * h_ref[...] + mix, g_ref[...], b_ref[...])
    h1_ref[...] = h1
    _to_row_tiles(h1r_ref, h1)
    _route_tile(h1, whi_ref, wlo_ref, rb_ref, rt_ref, cnt_ref, tri_ref)


def _outproj1_kernel(y_ref, h_ref, w_ref, g_ref, b_ref, whi_ref, wlo_ref, rb_ref,
                     h1_ref, h1r_ref, rt_ref, cnt_ref, tri_ref):
    mix = jnp.dot(y_ref[...], w_ref[...], preferred_element_type=F32)
    h1 = _layer_norm_rows(ALPHA * h_ref[...] + mix, g_ref[...], b_ref[...])
    h1_ref[...] = h1
    _to_row_tiles(h1r_ref, h1)
    _route_tile(h1, whi_ref, wlo_ref, rb_ref, rt_ref, cnt_ref, tri_ref)


def _router_weights(w_coarse, b_coarse, w_fine, b_fine):
    wf = jnp.transpose(w_fine.astype(F32), (1, 0, 2)).reshape(D_MODEL, N_EXPERTS)
    w = jnp.concatenate([w_coarse.astype(F32), wf], axis=1).T
    w = jnp.pad(w, ((0, ROUTE_ROWS - w.shape[0]), (0, 0)))
    b = jnp.concatenate([b_coarse.astype(F32), b_fine.astype(F32).reshape(-1)])
    b = jnp.pad(b, (0, ROUTE_ROWS - b.shape[0])).reshape(ROUTE_ROWS, 1)
    whi, wlo = _split_bf16(w)
    return whi, wlo, b


def _outproj(ys, yda, h, w_out, ln_g, ln_b, router):
    t_rows = h.shape[0]
    tm = TOK_TILE
    tl = tm // NB
    whi, wlo, rb = router
    w = w_out.astype(BF16)
    g2 = ln_g.astype(F32).reshape(1, D_MODEL)
    b2 = ln_b.astype(F32).reshape(1, D_MODEL)
    full = lambda a: pl.BlockSpec(a.shape, lambda i: (0,) * a.ndim)
    row = lambda width: pl.BlockSpec((tm, width), lambda i: (i, 0))
    common_in = [row(D_MODEL), full(w), full(g2), full(b2), full(whi), full(wlo), full(rb)]
    out_specs = [row(D_MODEL), pl.BlockSpec((tm * ROW_TILES, LANES), lambda i: (i, 0)),
                 pl.BlockSpec((NB, tm), lambda i: (0, i)),
                 pl.BlockSpec((ROUTE_ROWS, LANES), lambda i: (0, 0))]
    out_shape = [jax.ShapeDtypeStruct((t_rows, D_MODEL), F32),
                 jax.ShapeDtypeStruct((t_rows * ROW_TILES, LANES), F32),
                 jax.ShapeDtypeStruct((NB, t_rows), F32),
                 jax.ShapeDtypeStruct((ROUTE_ROWS, LANES), F32)]
    tri = pltpu.VMEM((tm, tm), BF16)
    if yda is not None:
        return pl.pallas_call(
            functools.partial(_outproj0_kernel, tl=tl),
            grid=(t_rows // tm,),
            in_specs=[row(S5_WIDTH), pl.BlockSpec((NB, tl, DA_WIDTH), lambda i: (0, i, 0))] + common_in,
            out_specs=out_specs, out_shape=out_shape,
            scratch_shapes=[pltpu.VMEM((DA_WIDTH // LANES, tm, LANES), F32), tri],
            compiler_params=_cparams(1),
            name="l0_outproj",
        )(ys, yda, h, w, g2, b2, whi, wlo, rb)
    return pl.pallas_call(
        _outproj1_kernel,
        grid=(t_rows // tm,),
        in_specs=[row(ys.shape[1])] + common_in,
        out_specs=out_specs, out_shape=out_shape,
        scratch_shapes=[tri],
        compiler_params=_cparams(1),
        name="l1_outproj",
    )(ys, h, w, g2, b2, whi, wlo, rb)


def _slots(route, counts, n_slots):
    t_rows = route.shape[1]
    experts = route[ROUTE_E:ROUTE_E + TOP_K_FINE].T.astype(jnp.int32)
    gates = route[ROUTE_GATE:ROUTE_GATE + TOP_K_FINE].T
    rank = route[ROUTE_RANK:ROUTE_RANK + TOP_K_FINE].T.astype(jnp.int32)
    cnt = counts[N_GROUPS:N_GROUPS + N_EXPERTS, 0].astype(jnp.int32)
    padded = (cnt + EXPERT_TILE - 1) // EXPERT_TILE * EXPERT_TILE
    pad_end = jnp.cumsum(padded)
    pad_start = pad_end - padded
    raw_start = jnp.cumsum(cnt) - cnt
    valid = (jnp.arange(t_rows) >= PAD_ROWS)[:, None]
    e_ids = jnp.arange(N_EXPERTS, dtype=jnp.int32)
    start_of = jnp.sum(jnp.where(experts[..., None] == e_ids, pad_start, 0), axis=-1)
    dest = jnp.where(valid, start_of + rank, n_slots).astype(jnp.int32)
    nblk = n_slots // EXPERT_TILE
    blk_start = jnp.arange(nblk, dtype=jnp.int32) * EXPERT_TILE
    block_expert = jnp.minimum(jnp.sum((pad_end[None, :] <= blk_start[:, None]).astype(jnp.int32), axis=1),
                               N_EXPERTS - 1)
    hit = block_expert[:, None] == e_ids[None, :]
    blk_pad_start = jnp.sum(jnp.where(hit, pad_start, 0), axis=1)
    blk_raw_start = jnp.sum(jnp.where(hit, raw_start, 0), axis=1)
    blk_cnt = jnp.sum(jnp.where(hit, cnt, 0), axis=1)
    n_used = (pad_end[-1] // EXPERT_TILE).astype(jnp.int32)
    blk_rows = jnp.where(jnp.arange(nblk) < n_used,
                         jnp.clip(blk_cnt - (blk_start - blk_pad_start), 0, EXPERT_TILE), 0)
    n_fetch = (blk_rows + GATHER_CHUNK - 1) // GATHER_CHUNK * GATHER_CHUNK
    order = jnp.argsort(dest.reshape(-1)).astype(jnp.int32)
    off = (blk_start - blk_pad_start)[:, None] + jnp.arange(EXPERT_TILE, dtype=jnp.int32)[None, :]
    src = jnp.clip(blk_raw_start[:, None] + off, 0, order.shape[0] - 1)
    slot_tok = jnp.where(off < blk_cnt[:, None], jnp.right_shift(order[src], 1), 0).reshape(-1)
    pos = jnp.where(valid, dest, 0).reshape(-1)
    return (slot_tok * ROW_TILES, block_expert, n_used.reshape(1), n_fetch.astype(jnp.int32),
            pos * ROW_TILES, gates)


ROW_TILES = D_MODEL // LANES


def _to_row_tiles(dst_ref, val):
    rows = val.shape[0]
    for s in range(ROW_TILES):
        dst_ref[pl.ds(s, rows, stride=ROW_TILES), :] = val[:, LANES * s:LANES * (s + 1)]


def _from_row_tiles(src_ref, rows):
    return jnp.concatenate([src_ref[pl.ds(s, rows, stride=ROW_TILES), :] for s in range(ROW_TILES)], axis=1)


LOOKAHEAD = 2
N_BUF = LOOKAHEAD + 1


GATHER_CHUNK = 1


def _expert_kernel(be_ref, nused_ref, nfetch_ref, *refs):
    tok_refs = refs[:N_BUF]
    x_hbm, wg_ref, wu_ref, wd_ref, y_ref = refs[N_BUF:N_BUF + 5]
    bufs = refs[N_BUF + 5:2 * N_BUF + 5]
    sem_ref, wgb_ref, wub_ref, wdb_ref = refs[2 * N_BUF + 5:2 * N_BUF + 9]
    i = pl.program_id(0)
    n_used = nused_ref[0]
    n_tiles = pl.num_programs(0)
    tb = EXPERT_TILE
    turn = lax.rem(i, N_BUF)

    def gather(tok_ref, p, tile):
        nfetch = jnp.where(tile < n_tiles, nfetch_ref[jnp.minimum(tile, n_tiles - 1)], 0)
        for c in range(tb // GATHER_CHUNK):
            @pl.when(c * GATHER_CHUNK < nfetch)
            def _(c=c):
                for r in range(c * GATHER_CHUNK, (c + 1) * GATHER_CHUNK):
                    row = pl.multiple_of(tok_ref[0, 0, r], ROW_TILES)
                    pltpu.make_async_copy(x_hbm.at[pl.ds(row, ROW_TILES), :],
                                          bufs[p].at[pl.ds(ROW_TILES * r, ROW_TILES), :],
                                          sem_ref.at[p]).start(priority=r % 2)

    def wait(p, tile):
        rows = pl.multiple_of(nfetch_ref[tile] * ROW_TILES, ROW_TILES)
        pltpu.make_async_copy(x_hbm.at[pl.ds(0, rows), :], bufs[p].at[pl.ds(0, rows), :],
                              sem_ref.at[p]).wait()

    @pl.when(i == 0)
    def _():
        for p in range(N_BUF):
            bufs[p][...] = jnp.zeros_like(bufs[p])
        for d in range(LOOKAHEAD):
            gather(tok_refs[d], d, d)

    @pl.when((i < n_used) & ((i == 0) | (be_ref[i] != be_ref[jnp.maximum(i - 1, 0)])))
    def _():
        wgb_ref[...] = wg_ref[0].astype(BF16)
        wub_ref[...] = wu_ref[0].astype(BF16)
        wdb_ref[...] = wd_ref[0].astype(BF16)

    for p in range(N_BUF):
        @pl.when((turn == p) & (i < n_used))
        def _(p=p):
            wait(p, i)
            gather(tok_refs[LOOKAHEAD], (p + LOOKAHEAD) % N_BUF, i + LOOKAHEAD)
            x = _from_row_tiles(bufs[p], tb).astype(BF16)
            g = jnp.dot(x, wgb_ref[...], preferred_element_type=F32)
            u = jnp.dot(x, wub_ref[...], preferred_element_type=F32)
            hid = (g * _sigmoid(g) * u).astype(BF16)
            _to_row_tiles(y_ref, jnp.dot(hid, wdb_ref[...], preferred_element_type=F32))

    @pl.when(i >= n_used)
    def _():
        y_ref[...] = jnp.zeros_like(y_ref)


def _experts(h1r, slot_row, block_expert, n_used, n_fetch, w_gate, w_up, w_down):
    n_slots = slot_row.shape[0]
    nblk = n_slots // EXPERT_TILE
    tok3 = slot_row.reshape(nblk, 1, EXPERT_TILE)
    ahead = lambda d: pl.BlockSpec((1, 1, EXPERT_TILE), lambda i, be, nu, nf: (jnp.minimum(i + d, nblk - 1), 0, 0),
                                   memory_space=pltpu.SMEM)
    tile_buf = pltpu.VMEM((EXPERT_TILE * ROW_TILES, LANES), F32)
    grid_spec = pltpu.PrefetchScalarGridSpec(
        num_scalar_prefetch=3,
        grid=(nblk,),
        in_specs=[ahead(d) for d in range(N_BUF)] + [
            pl.BlockSpec(memory_space=pl.ANY),
            pl.BlockSpec((1, D_MODEL, D_EXPERT), lambda i, be, nu, nf: (be[i], 0, 0)),
            pl.BlockSpec((1, D_MODEL, D_EXPERT), lambda i, be, nu, nf: (be[i], 0, 0)),
            pl.BlockSpec((1, D_EXPERT, D_MODEL), lambda i, be, nu, nf: (be[i], 0, 0))],
        out_specs=pl.BlockSpec((EXPERT_TILE * ROW_TILES, LANES), lambda i, be, nu, nf: (i, 0)),
        scratch_shapes=[tile_buf] * N_BUF + [
            pltpu.SemaphoreType.DMA((N_BUF,)),
            pltpu.VMEM((D_MODEL, D_EXPERT), BF16), pltpu.VMEM((D_MODEL, D_EXPERT), BF16),
            pltpu.VMEM((D_EXPERT, D_MODEL), BF16)],
    )
    return pl.pallas_call(
        _expert_kernel,
        grid_spec=grid_spec,
        out_shape=jax.ShapeDtypeStruct((n_slots * ROW_TILES, LANES), F32),
        compiler_params=_cparams(1),
        name="moe_experts",
    )(block_expert, n_used, n_fetch, *([tok3] * N_BUF), h1r, w_gate, w_up, w_down)


def _combine_kernel(*refs, n_tiles, to_batch_major):
    pos_refs = refs[:N_BUF]
    yb_hbm, gate_ref, h_ref, g_ref, b_ref = refs[N_BUF:N_BUF + 5]
    if to_batch_major:
        o_ref, ybuf_ref, sem_ref, hs_ref = refs[N_BUF + 5:]
    else:
        w_ref, o_ref, gg_ref, rec_ref, ybuf_ref, sem_ref = refs[N_BUF + 5:]
    i = pl.program_id(0)
    tm = COMBINE_TILE
    turn = lax.rem(i, N_BUF)

    def gather(pos_ref, p):
        for r in range(tm):
            for kk in range(TOP_K_FINE):
                row = pl.multiple_of(pos_ref[0, 0, TOP_K_FINE * r + kk], ROW_TILES)
                pltpu.make_async_copy(yb_hbm.at[pl.ds(row, ROW_TILES), :],
                                      ybuf_ref.at[p, kk, pl.ds(ROW_TILES * r, ROW_TILES), :],
                                      sem_ref.at[p]).start(priority=kk)

    def finish(p, look_ahead):
        for kk in range(TOP_K_FINE):
            pltpu.make_async_copy(yb_hbm.at[pl.ds(0, tm * ROW_TILES), :], ybuf_ref.at[p, kk],
                                  sem_ref.at[p]).wait()
        if look_ahead:
            gather(pos_refs[LOOKAHEAD], (p + LOOKAHEAD) % N_BUF)
        gts = gate_ref[...]
        moe = (_from_row_tiles(ybuf_ref.at[p, 0], tm) * gts[:, 0:1]
               + _from_row_tiles(ybuf_ref.at[p, 1], tm) * gts[:, 1:2])
        h2 = _layer_norm_rows(ALPHA * h_ref[...] + moe, g_ref[...], b_ref[...])
        if not to_batch_major:
            o_ref[...] = h2
            z = jnp.dot(h2.astype(BF16), w_ref[...], preferred_element_type=F32)
            gg_ref[...] = _gelu(z[:, :D_RNN]).astype(BF16)
            rec_ref[...] = z[:, D_RNN:]
        else:
            tl = tm // NB
            for s in range(D_MODEL // LANES):
                hs_ref[s] = h2[:, LANES * s:LANES * (s + 1)]
            for b in range(NB):
                for s in range(D_MODEL // LANES):
                    o_ref[b, :, LANES * s:LANES * (s + 1)] = hs_ref[s, pl.ds(b, tl, stride=NB), :]

    @pl.when(i == 0)
    def _():
        for d in range(LOOKAHEAD):
            gather(pos_refs[d], d)

    for p in range(N_BUF):
        @pl.when((turn == p) & (i < n_tiles - LOOKAHEAD))
        def _(p=p):
            finish(p, True)

    @pl.when(i >= n_tiles - LOOKAHEAD)
    def _():
        finish(turn, False)


def _combine(yb, pos, gates, h1, ln_g, ln_b, w_next=None):
    final = w_next is None
    t_rows = h1.shape[0]
    tm = COMBINE_TILE
    first_tile = (LANES * NB) // tm if final else 0
    n_tiles = t_rows // tm - first_tile
    pos3 = pos.reshape(t_rows // tm, 1, TOP_K_FINE * tm)
    g2 = ln_g.astype(F32).reshape(1, D_MODEL)
    b2 = ln_b.astype(F32).reshape(1, D_MODEL)
    full = lambda a: pl.BlockSpec(a.shape, lambda i: (0,) * a.ndim)
    last = t_rows // tm - 1
    ahead = lambda d: pl.BlockSpec((1, 1, TOP_K_FINE * tm),
                                   lambda i: (jnp.minimum(i + first_tile + d, last), 0, 0),
                                   memory_space=pltpu.SMEM)
    in_specs = [ahead(d) for d in range(N_BUF)] + [
                pl.BlockSpec(memory_space=pl.ANY),
                pl.BlockSpec((tm, TOP_K_FINE), lambda i: (i + first_tile, 0)),
                pl.BlockSpec((tm, D_MODEL), lambda i: (i + first_tile, 0)),
                full(g2), full(b2)]
    scratch = [pltpu.VMEM((N_BUF, TOP_K_FINE, tm * ROW_TILES, LANES), F32), pltpu.SemaphoreType.DMA((N_BUF,))]
    if final:
        tl = tm // NB
        seq = t_rows // NB - LANES
        out_specs = pl.BlockSpec((NB, tl, D_MODEL), lambda i: (0, i, 0))
        out_shape = jax.ShapeDtypeStruct((NB, seq, D_MODEL), F32)
        scratch.append(pltpu.VMEM((D_MODEL // LANES, tm, LANES), F32))
        operands = (yb, gates, h1, g2, b2)
    else:
        row = lambda width: pl.BlockSpec((tm, width), lambda i: (i, 0))
        in_specs.append(full(w_next))
        out_specs = [row(D_MODEL), row(D_RNN), row(D_RNN)]
        out_shape = [jax.ShapeDtypeStruct((t_rows, D_MODEL), F32),
                     jax.ShapeDtypeStruct((t_rows, D_RNN), BF16),
                     jax.ShapeDtypeStruct((t_rows, D_RNN), F32)]
        operands = (yb, gates, h1, g2, b2, w_next)
    return pl.pallas_call(
        functools.partial(_combine_kernel, n_tiles=n_tiles, to_batch_major=final),
        grid=(n_tiles,),
        in_specs=in_specs, out_specs=out_specs, out_shape=out_shape,
        scratch_shapes=scratch,
        compiler_params=_cparams(1),
        name="moe_combine_final" if final else "moe_combine_inproj",
    )(*([pos3] * N_BUF), *operands)


def _moe_block(h1, h1r, route, counts, w_gate, w_up, w_down, ln_g, ln_b, w_next=None):
    t_rows = h1.shape[0]
    n_assign = (t_rows - PAD_ROWS) * TOP_K_FINE
    nblk = -(-(n_assign + N_EXPERTS * (EXPERT_TILE - 1)) // EXPERT_TILE)
    n_slots = nblk * EXPERT_TILE
    slot_row, block_expert, n_used, n_fetch, pos, gates = _slots(route, counts, n_slots)
    yb = _experts(h1r, slot_row, block_expert, n_used, n_fetch, w_gate, w_up, w_down)
    return _combine(yb, pos, gates, h1, ln_g, ln_b, w_next)


CONV_HALO = (CONV_WIDTH - 1) * NB


def _rglru_kernel(rec_ref, gg_ref, cw_ref, cb_ref, wax_ref, ba_ref, bx_ref, ca_ref,
                  y_ref, rp_ref, a_ref, b_ref, st_ref, *, steps):
    i = pl.program_id(0)
    rows = steps * NB
    sigmoid = lambda x: 1.0 / (1.0 + jnp.exp2(x * (-LOG2_E)))

    @pl.when(i == 0)
    def _():
        st_ref[...] = jnp.zeros_like(st_ref)
        rp_ref[0:CONV_HALO, :] = jnp.zeros((CONV_HALO, D_RNN), F32)

    rp_ref[CONV_HALO:CONV_HALO + rows, :] = rec_ref[...]
    has_pad = i * rows < PAD_ROWS
    real = (i * rows + lax.broadcasted_iota(jnp.int32, (rows, 1), 0)) >= PAD_ROWS

    @pl.when(has_pad)
    def _():
        rp_ref[CONV_HALO:CONV_HALO + rows, :] = jnp.where(real, rp_ref[CONV_HALO:CONV_HALO + rows, :], 0.0)

    xc = cb_ref[...] + sum(rp_ref[NB * j:NB * j + rows, :] * cw_ref[j:j + 1, :] for j in range(CONV_WIDTH))
    tail = rp_ref[rows:rows + CONV_HALO, :]
    rp_ref[0:CONV_HALO, :] = tail

    for n in range(LRU_BLOCKS):
        cs = slice(LRU_BLOCK_W * n, LRU_BLOCK_W * (n + 1))
        xb = xc[:, cs]
        ra = jnp.dot(xb.astype(BF16), wax_ref[n], preferred_element_type=F32)
        r = sigmoid(ra[:, :LRU_BLOCK_W] + ba_ref[:, cs])
        ig = sigmoid(ra[:, LRU_BLOCK_W:] + bx_ref[:, cs])
        a = jnp.exp2(r * ca_ref[:, cs])
        a_ref[:, cs] = a
        b_ref[:, cs] = jnp.sqrt(1.0 - a * a) * (ig * xb)

    @pl.when(has_pad)
    def _():
        b_ref[...] = jnp.where(real, b_ref[...], 0.0)

    def step(t, h):
        r0 = pl.multiple_of(t * NB, NB)
        hn = a_ref[pl.ds(r0, NB), :] * h + b_ref[pl.ds(r0, NB), :]
        b_ref[pl.ds(r0, NB), :] = hn
        return hn

    st_ref[...] = lax.fori_loop(0, steps, step, st_ref[...])
    y_ref[...] = (gg_ref[...].astype(F32) * b_ref[...]).astype(BF16)


def _rglru(rec, gg, conv_w, conv_b, w_a, b_a, w_x, b_x, lru_lambda):
    t_rows = rec.shape[0]
    steps = SCAN_STEPS
    rows = steps * NB
    cw = conv_w.astype(F32)
    cb = conv_b.astype(F32).reshape(1, D_RNN)
    wax = jnp.concatenate([w_a.astype(F32), w_x.astype(F32)], axis=2).astype(BF16)
    ba = b_a.astype(F32).reshape(1, D_RNN)
    bx = b_x.astype(F32).reshape(1, D_RNN)
    ca = (-LRU_C * LOG2_E) * jax.nn.softplus(-lru_lambda.astype(F32)).reshape(1, D_RNN)
    full = lambda a: pl.BlockSpec(a.shape, lambda i: (0,) * a.ndim)
    row = pl.BlockSpec((rows, D_RNN), lambda i: (i, 0))
    return pl.pallas_call(
        functools.partial(_rglru_kernel, steps=steps),
        grid=(t_rows // rows,),
        in_specs=[row, row, full(cw), full(cb), full(wax), full(ba), full(bx), full(ca)],
        out_specs=row,
        out_shape=jax.ShapeDtypeStruct((t_rows, D_RNN), BF16),
        scratch_shapes=[pltpu.VMEM((rows + CONV_HALO, D_RNN), F32), pltpu.VMEM((rows, D_RNN), F32),
                        pltpu.VMEM((rows, D_RNN), F32), pltpu.VMEM((NB, D_RNN), F32)],
        compiler_params=_cparams(1),
        name="l1_rglru",
    )(rec, gg, cw, cb, wax, ba, bx, ca)


def kernel(x, meta, l0_ln1_g, l0_ln1_b, l0_w_in, l0_s5_lambda_re, l0_s5_lambda_im, l0_s5_log_dt, l0_s5_b_re, l0_s5_b_im, l0_s5_c_re, l0_s5_c_im, l0_s5_d, l0_s5_w_glu, l0_s5_b_glu, l0_da_lq1, l0_da_lk1, l0_da_lq2, l0_da_lk2, l0_da_subln_g, l0_w_out, l0_ln2_g, l0_ln2_b, l0_moe_w_coarse, l0_moe_b_coarse, l0_moe_w_fine, l0_moe_b_fine, l0_moe_w_gate, l0_moe_w_up, l0_moe_w_down, l1_ln1_g, l1_ln1_b, l1_w_in, l1_conv_w, l1_conv_b, l1_lru_w_a, l1_lru_b_a, l1_lru_w_x, l1_lru_b_x, l1_lru_lambda, l1_w_out, l1_ln2_g, l1_ln2_b, l1_moe_w_coarse, l1_moe_b_coarse, l1_moe_w_fine, l1_moe_b_fine, l1_moe_w_gate, l1_moe_w_up, l1_moe_w_down):
    bsz, seq, _ = x.shape
    assert bsz == NB and seq % Q_TILE == 0
    dt = x.dtype
    lp = FRONT_PAD + N_META + seq

    col_scale = jnp.concatenate([jnp.ones((S5_WIDTH,), F32),
                                 jnp.full((DA_WIDTH,), DA_HEAD_DIM ** -0.5, F32),
                                 jnp.ones((2 * DA_WIDTH,), F32)])
    w_in0 = (l0_w_in.astype(F32) * col_scale[None, :]).astype(BF16)
    h, u, q, k, v = _inproj0(x, meta, w_in0, lp)
    s5p = _s5_params(l0_s5_lambda_re, l0_s5_lambda_im, l0_s5_log_dt,
                     l0_s5_b_re, l0_s5_b_im, l0_s5_c_re, l0_s5_c_im)
    y_s5 = _s5(u, *s5p, l0_s5_d, l0_s5_w_glu, l0_s5_b_glu)
    lam_init = 0.8 - 0.6 * math.exp(-0.3 * 0)
    lam = (jnp.exp(jnp.sum(l0_da_lq1.astype(F32) * l0_da_lk1.astype(F32)))
           - jnp.exp(jnp.sum(l0_da_lq2.astype(F32) * l0_da_lk2.astype(F32))) + lam_init)
    y_da = _diffattn(q, k, v, lam, l0_da_subln_g, lam_init)
    router0 = _router_weights(l0_moe_w_coarse, l0_moe_b_coarse, l0_moe_w_fine, l0_moe_b_fine)
    h, hr, route, counts = _outproj(y_s5, y_da, h, l0_w_out, l0_ln1_g, l0_ln1_b, router0)
    h, gg, rec = _moe_block(h, hr, route, counts, l0_moe_w_gate, l0_moe_w_up, l0_moe_w_down,
                            l0_ln2_g, l0_ln2_b, l1_w_in.astype(BF16))

    y = _rglru(rec, gg, l1_conv_w, l1_conv_b, l1_lru_w_a, l1_lru_b_a, l1_lru_w_x, l1_lru_b_x, l1_lru_lambda)
    router1 = _router_weights(l1_moe_w_coarse, l1_moe_b_coarse, l1_moe_w_fine, l1_moe_b_fine)
    h, hr, route, counts = _outproj(y, None, h, l1_w_out, l1_ln1_g, l1_ln1_b, router1)
    out = _moe_block(h, hr, route, counts, l1_moe_w_gate, l1_moe_w_up, l1_moe_w_down, l1_ln2_g, l1_ln2_b)
    return out.astype(dt)
```

```python
import functools
import math

import jax
import jax.numpy as jnp
from jax import lax
from jax.experimental import pallas as pl
from jax.experimental.pallas import tpu as pltpu

F32 = jnp.float32
BF16 = jnp.bfloat16

D_MODEL = 1024
DEPTH = 2
CHUNK = 64
N_META = 16
S5_WIDTH = 512
S5_GROUP = 16
S5_GROUPS = 32
S5_STATE = 64
DA_HEADS = 4
DA_HEAD_DIM = 64
DA_WIDTH = 512
D_RNN = 1280
LRU_BLOCKS = 10
LRU_BLOCK_W = 128
CONV_WIDTH = 4
LRU_C = 8.0
N_GROUPS = 4
EXPERTS_PER_GROUP = 8
N_EXPERTS = 32
TOP_K_FINE = 2
D_EXPERT = 512
ALPHA = (2 * DEPTH) ** 0.25
LN_EPS = 1e-5
NEG_INF = -1e30

NB = 8
LANES = 128
FRONT_PAD = LANES - N_META
PAD_ROWS = FRONT_PAD * NB
Q_TILE = 256
K_TILE = 512
TOK_TILE = 512
SCAN_STEPS = 64
EXPERT_TILE = 512
COMBINE_TILE = 256
VMEM_LIMIT = 48 * 1024 * 1024
LOG2_E = math.log2(math.e)


def _cparams(n_axes, vmem=VMEM_LIMIT):
    return pltpu.CompilerParams(dimension_semantics=("arbitrary",) * n_axes,
                                vmem_limit_bytes=vmem)


def _gelu(x):
    return 0.5 * x * (1.0 + jnp.tanh(math.sqrt(2.0 / math.pi) * (x + 0.044715 * (x * x * x))))


def _sigmoid(x):
    return 1.0 / (1.0 + jnp.exp(-x))


def _layer_norm_rows(r, g, b):
    mu = jnp.mean(r, axis=-1, keepdims=True)
    c = r - mu
    var = jnp.mean(c * c, axis=-1, keepdims=True)
    return c * lax.rsqrt(var + LN_EPS) * g + b


def _split_bf16(w):
    hi = w.astype(BF16)
    lo = (w - hi.astype(F32)).astype(BF16)
    return hi, lo


def _inproj0_kernel(head_ref, x_ref, w_ref, h_ref, u_ref, q_ref, k_ref, v_ref, zs_ref, *, tl, head_tiles):
    i = pl.program_id(0)

    @pl.when(i < head_tiles)
    def _():
        h_ref[...] = head_ref[...]

    @pl.when(i >= head_tiles)
    def _():
        for s in range(D_MODEL // LANES):
            for b in range(NB):
                zs_ref[s, pl.ds(b, tl, stride=NB), :] = x_ref[b, :, LANES * s:LANES * (s + 1)]
        for s in range(D_MODEL // LANES):
            h_ref[:, LANES * s:LANES * (s + 1)] = zs_ref[s]

    x = h_ref[...].astype(BF16)
    z = jnp.dot(x, w_ref[...], preferred_element_type=F32)
    u_ref[...] = z[:, :S5_WIDTH]
    n_slab = 3 * DA_WIDTH // LANES
    for s in range(n_slab):
        zs_ref[s] = z[:, S5_WIDTH + LANES * s:S5_WIDTH + LANES * (s + 1)]
    dsts = (q_ref, k_ref, v_ref)
    per = DA_WIDTH // LANES
    for b in range(NB):
        for s in range(n_slab):
            c = (s % per) * LANES
            blk = zs_ref[s, pl.ds(b, tl, stride=NB), :]
            if s < per:
                blk = blk * LOG2_E
            dsts[s // per][b, :, c:c + LANES] = blk.astype(BF16)


def _inproj0(x, meta, w_bf16, lp):
    t_rows = lp * NB
    tm = TOK_TILE
    tl = tm // NB
    head_tiles = LANES // tl
    head = jnp.concatenate([
        jnp.zeros((FRONT_PAD, NB, D_MODEL), x.dtype),
        jnp.broadcast_to(meta.astype(x.dtype)[:, None, :], (N_META, NB, D_MODEL))], axis=0)
    head = head.reshape(LANES * NB, D_MODEL)
    qkv_shape = jax.ShapeDtypeStruct((NB, lp, DA_WIDTH), BF16)
    qkv_spec = pl.BlockSpec((NB, tl, DA_WIDTH), lambda i: (0, i, 0))
    n_slab = max(3 * DA_WIDTH, D_MODEL) // LANES
    return pl.pallas_call(
        functools.partial(_inproj0_kernel, tl=tl, head_tiles=head_tiles),
        grid=(t_rows // tm,),
        in_specs=[pl.BlockSpec((tm, D_MODEL), lambda i: (jnp.minimum(i, head_tiles - 1), 0)),
                  pl.BlockSpec((NB, tl, D_MODEL), lambda i: (0, jnp.maximum(i - head_tiles, 0), 0)),
                  pl.BlockSpec(w_bf16.shape, lambda i: (0, 0))],
        out_specs=[pl.BlockSpec((tm, D_MODEL), lambda i: (i, 0)),
                   pl.BlockSpec((tm, S5_WIDTH), lambda i: (i, 0)), qkv_spec, qkv_spec, qkv_spec],
        out_shape=[jax.ShapeDtypeStruct((t_rows, D_MODEL), F32),
                   jax.ShapeDtypeStruct((t_rows, S5_WIDTH), F32), qkv_shape, qkv_shape, qkv_shape],
        scratch_shapes=[pltpu.VMEM((n_slab, tm, LANES), F32)],
        compiler_params=_cparams(1),
        name="l0_inproj",
    )(head, x, w_bf16)


S5_SLABS = S5_WIDTH // LANES
S5_SLAB_STATE = (S5_GROUPS // S5_SLABS) * S5_STATE
S5_NSTATE = S5_GROUPS * S5_STATE


def _s5_kernel(u_ref, bmat_ref, are_ref, aim_ref, cre_ref, cim_ref, d_ref, wglu_ref, bglu_ref,
               y_ref, hre_ref, him_ref, st_ref, *, steps):
    i = pl.program_id(0)
    rows = steps * NB

    @pl.when(i == 0)
    def _():
        st_ref[...] = jnp.zeros_like(st_ref)

    row = i * rows + lax.broadcasted_iota(jnp.int32, (rows, 1), 0)
    u = jnp.where(row >= PAD_ROWS, u_ref[...], 0.0)
    ub = u.astype(BF16)
    for s in range(S5_SLABS):
        bu = jnp.dot(ub[:, LANES * s:LANES * (s + 1)], bmat_ref[s], preferred_element_type=F32)
        hre_ref[:, S5_SLAB_STATE * s:S5_SLAB_STATE * (s + 1)] = bu[:, :S5_SLAB_STATE]
        him_ref[:, S5_SLAB_STATE * s:S5_SLAB_STATE * (s + 1)] = bu[:, S5_SLAB_STATE:]

    cw = 512
    for cg in range(S5_NSTATE // cw):
        c0 = cg * cw
        a_r = are_ref[:, c0:c0 + cw]
        a_i = aim_ref[:, c0:c0 + cw]

        def step(t, carry, c0=c0, a_r=a_r, a_i=a_i):
            sr, si = carry
            r0 = pl.multiple_of(t * NB, NB)
            br = hre_ref[pl.ds(r0, NB), c0:c0 + cw]
            bi = him_ref[pl.ds(r0, NB), c0:c0 + cw]
            nr = a_r * sr - a_i * si + br
            ni = a_r * si + a_i * sr + bi
            hre_ref[pl.ds(r0, NB), c0:c0 + cw] = nr
            him_ref[pl.ds(r0, NB), c0:c0 + cw] = ni
            return nr, ni

        sr, si = lax.fori_loop(0, steps, step, (st_ref[0, :, c0:c0 + cw], st_ref[1, :, c0:c0 + cw]))
        st_ref[0, :, c0:c0 + cw] = sr
        st_ref[1, :, c0:c0 + cw] = si

    ys = []
    for s in range(S5_SLABS):
        hr = hre_ref[:, S5_SLAB_STATE * s:S5_SLAB_STATE * (s + 1)].astype(BF16)
        hi = him_ref[:, S5_SLAB_STATE * s:S5_SLAB_STATE * (s + 1)].astype(BF16)
        ys.append(jnp.dot(hr, cre_ref[s], preferred_element_type=F32)
                  + jnp.dot(hi, cim_ref[s], preferred_element_type=F32))
    y = jnp.concatenate(ys, axis=1) + d_ref[...] * u
    y = _gelu(y)
    gate = _sigmoid(jnp.dot(y.astype(BF16), wglu_ref[...], preferred_element_type=F32) + bglu_ref[...])
    y_ref[...] = (y * gate).astype(BF16)


def _s5_params(lam_re, lam_im, log_dt, b_re, b_im, c_re, c_im):
    dt = jnp.exp(log_dt.astype(F32))[:, None]
    lr = jnp.minimum(lam_re.astype(F32), -1e-4)
    li = lam_im.astype(F32)
    mag = jnp.exp(lr * dt)
    ar = mag * jnp.cos(li * dt)
    ai = mag * jnp.sin(li * dt)
    den = lr * lr + li * li
    nr, ni = ar - 1.0, ai
    fr = ((nr * lr + ni * li) / den)[..., None]
    fi = ((ni * lr - nr * li) / den)[..., None]
    br, bi = b_re.astype(F32), b_im.astype(F32)
    bbr = fr * br - fi * bi
    bbi = fr * bi + fi * br
    gps = S5_GROUPS // S5_SLABS
    eye = jnp.eye(gps, dtype=F32)

    def in_slab(m):
        m4 = m.reshape(S5_SLABS, gps, S5_STATE, S5_GROUP)
        return jnp.einsum('sgph,gk->sghkp', m4, eye).reshape(S5_SLABS, LANES, S5_SLAB_STATE)

    def out_slab(m):
        m4 = m.reshape(S5_SLABS, gps, S5_GROUP, S5_STATE)
        return jnp.einsum('sghp,gk->sgpkh', m4, eye).reshape(S5_SLABS, S5_SLAB_STATE, LANES)

    bmat = jnp.concatenate([in_slab(bbr), in_slab(bbi)], axis=2).astype(BF16)
    cre = out_slab(c_re.astype(F32)).astype(BF16)
    cim = out_slab(-c_im.astype(F32)).astype(BF16)
    a_re = jnp.broadcast_to(ar.reshape(1, S5_NSTATE), (NB, S5_NSTATE))
    a_im = jnp.broadcast_to(ai.reshape(1, S5_NSTATE), (NB, S5_NSTATE))
    return bmat, a_re, a_im, cre, cim


def _s5(u, bmat, a_re, a_im, cre, cim, d, w_glu, b_glu):
    t_rows = u.shape[0]
    steps = SCAN_STEPS
    rows = steps * NB
    full = lambda a: pl.BlockSpec(a.shape, lambda i: (0,) * a.ndim)
    d2 = d.astype(F32).reshape(1, S5_WIDTH)
    bg2 = b_glu.astype(F32).reshape(1, S5_WIDTH)
    wg = w_glu.astype(BF16)
    return pl.pallas_call(
        functools.partial(_s5_kernel, steps=steps),
        grid=(t_rows // rows,),
        in_specs=[pl.BlockSpec((rows, S5_WIDTH), lambda i: (i, 0)),
                  full(bmat), full(a_re), full(a_im), full(cre), full(cim), full(d2), full(wg), full(bg2)],
        out_specs=pl.BlockSpec((rows, S5_WIDTH), lambda i: (i, 0)),
        out_shape=jax.ShapeDtypeStruct((t_rows, S5_WIDTH), BF16),
        scratch_shapes=[pltpu.VMEM((rows, S5_NSTATE), F32), pltpu.VMEM((rows, S5_NSTATE), F32),
                        pltpu.VMEM((2, NB, S5_NSTATE), F32)],
        compiler_params=_cparams(1),
        name="l0_s5",
    )(u, bmat, a_re, a_im, cre, cim, d2, wg, bg2)


def _diffattn_kernel(lam_ref, q_ref, k_ref, v_ref, g_ref, o_ref,
                     s_ref, qm_ref, m_ref, a_ref, *, lam_init, lp):
    lam = lam_ref[0]
    tq, tk = Q_TILE, K_TILE
    lane = lax.broadcasted_iota(jnp.int32, (tq, LANES), 1)
    qrow = jnp.bitwise_and(lax.broadcasted_iota(jnp.int32, (2 * tq, tk), 0), tq - 1)
    kloc = lax.broadcasted_iota(jnp.int32, (2 * tq, tk), 1)
    nt = (((1,), (1,)), ((), ()))
    bf16_rows = 16

    def key_start(j):
        return pl.multiple_of(jnp.minimum(FRONT_PAD + j * tk, lp - tk), bf16_rows)

    def chunk_of(pos):
        return jnp.right_shift(pos - CHUNK, 6)

    def lane_fold(x, op):
        r = x[:, :LANES]
        for c in range(1, x.shape[1] // LANES):
            r = op(r, x[:, LANES * c:LANES * (c + 1)])
        return r

    def q_start(i):
        return pl.multiple_of(jnp.minimum(i * tq, lp - tq), LANES)

    def n_key_tiles(i):
        return (q_start(i) + tq - FRONT_PAD + tk - 1) // tk

    def n_full_tiles(i):
        return jnp.maximum(q_start(i) - FRONT_PAD, 0) // tk


    def prep(i, par):
        q = q_ref[0, pl.ds(q_start(i), tq), :]
        zero = jnp.zeros_like(q)
        qm_ref[par, :tq] = jnp.where(lane < DA_HEAD_DIM, q, zero)
        qm_ref[par, tq:] = jnp.where(lane >= DA_HEAD_DIM, q, zero)
        m_ref[par] = jnp.full(m_ref.shape[1:], NEG_INF, F32)

    def scores(i, par, j, masked):
        k0 = key_start(j)
        kt = k_ref[0, pl.ds(k0, tk), :]
        if masked:
            kpos = k0 + kloc
            mask = (chunk_of(kpos) <= chunk_of(q_start(i) + qrow)) & (kpos >= FRONT_PAD + j * tk)
        s = lax.dot_general(qm_ref[par], kt, nt, preferred_element_type=F32)
        if masked:
            s = jnp.where(mask, s, NEG_INF)
        s_ref[par, j] = s
        m_ref[par] = jnp.maximum(m_ref[par], lane_fold(s, jnp.maximum))

    ones_col = jnp.where(lax.broadcasted_iota(jnp.int32, (tk, LANES), 1) == 0, 1.0, 0.0).astype(BF16)

    def values(par, j, m):
        vt = jnp.concatenate([v_ref[0, pl.ds(key_start(j), tk), :], ones_col], axis=1)
        p = jnp.exp2((s_ref[par, j] - m).astype(BF16))
        a_ref[...] += jnp.dot(p, vt, preferred_element_type=F32)

    def row_max(par):
        a_ref[...] = jnp.zeros(a_ref.shape, F32)
        return jnp.max(m_ref[par], axis=-1, keepdims=True)

    def finish(i):
        l1 = a_ref[:tq, LANES:LANES + 1]
        l2 = a_ref[tq:, LANES:LANES + 1]
        o = a_ref[:tq, :LANES] / l1 - lam * (a_ref[tq:, :LANES] / l2)
        o = o * lax.rsqrt(jnp.mean(o * o, axis=-1, keepdims=True) + LN_EPS) * g_ref[...]
        o_ref[0, pl.ds(q_start(i), tq), :] = o * (1.0 - lam_init)

    def loop(lo, hi, body):
        def wrapped(j, _):
            body(j)
            return 0
        lax.fori_loop(lo, hi, wrapped, 0)

    def step(i, par):
        m = row_max(par)
        prep(i + 1, 1 - par)

        def both(j, masked):
            scores(i + 1, 1 - par, j, masked)
            values(par, j, m)

        loop(0, n_full_tiles(i + 1), functools.partial(both, masked=False))
        loop(n_full_tiles(i + 1), n_key_tiles(i), functools.partial(both, masked=True))
        loop(n_key_tiles(i), n_key_tiles(i + 1), lambda j: scores(i + 1, 1 - par, j, True))
        finish(i)

    def step_pair(i2):
        step(2 * i2, 0)
        step(2 * i2 + 1, 1)

    n_q = pl.cdiv(lp, tq)
    prep(0, 0)
    loop(0, n_key_tiles(0), lambda j: scores(0, 0, j, True))
    loop(0, (n_q - 1) // 2, step_pair)
    if (n_q - 1) % 2:
        step(n_q - 2, 0)
    last = (n_q - 1) % 2
    m = row_max(last)
    loop(0, n_key_tiles(n_q - 1), lambda j: values(last, j, m))
    finish(n_q - 1)


def _diffattn(q, k, v, lam, subln_g, lam_init):
    nb, lp, _ = q.shape
    g2 = subln_g.astype(F32).reshape(1, 2 * DA_HEAD_DIM)
    seq_spec = pl.BlockSpec((1, lp, LANES), lambda b, h: (b, 0, h))
    return pl.pallas_call(
        functools.partial(_diffattn_kernel, lam_init=lam_init, lp=lp),
        grid=(nb, DA_HEADS),
        in_specs=[pl.BlockSpec(memory_space=pltpu.SMEM), seq_spec, seq_spec, seq_spec,
                  pl.BlockSpec((1, LANES), lambda b, h: (0, 0))],
        out_specs=seq_spec,
        out_shape=jax.ShapeDtypeStruct((nb, lp, DA_WIDTH), F32),
        scratch_shapes=[pltpu.VMEM((2, pl.cdiv(lp, K_TILE), 2 * Q_TILE, K_TILE), F32),
                        pltpu.VMEM((2, 2 * Q_TILE, LANES), BF16),
                        pltpu.VMEM((2, 2 * Q_TILE, LANES), F32),
                        pltpu.VMEM((2 * Q_TILE, 2 * LANES), F32)],
        compiler_params=_cparams(2),
        name="l0_diffattn",
    )(lam.reshape(1), q, k, v, g2)


ROUTE_ROWS = 64
ROUTE_E, ROUTE_GATE, ROUTE_RANK = 0, 2, 4


def _router_logits_t(h1, whi_ref, wlo_ref, rb_ref):
    hi = h1.astype(BF16)
    lo = (h1 - hi.astype(F32)).astype(BF16)
    nt = (((1,), (1,)), ((), ()))
    return (lax.dot_general(whi_ref[...], hi, nt, preferred_element_type=F32)
            + lax.dot_general(whi_ref[...], lo, nt, preferred_element_type=F32)
            + lax.dot_general(wlo_ref[...], hi, nt, preferred_element_type=F32) + rb_ref[...])


def _route_cols(lg, cnt_ref, tri_ref, first_tok):
    toks = lg.shape[1]
    row = lax.broadcasted_iota(jnp.int32, lg.shape, 0)
    rowf = row.astype(F32)
    valid = (first_tok + lax.broadcasted_iota(jnp.int32, (1, toks), 1)) >= PAD_ROWS
    ninf = float('-inf')
    first = lambda hit: jnp.min(jnp.where(hit, rowf, float(ROUTE_ROWS)), axis=0, keepdims=True)
    cm = jnp.where(row < N_GROUPS, lg, ninf)
    cmax = jnp.max(cm, axis=0, keepdims=True)
    p_grp = 1.0 / jnp.sum(jnp.exp(cm - cmax), axis=0, keepdims=True)
    lo = N_GROUPS + EXPERTS_PER_GROUP * first(cm == cmax)
    fm = jnp.where((rowf >= lo) & (rowf < lo + EXPERTS_PER_GROUP), lg, ninf)
    v1 = jnp.max(fm, axis=0, keepdims=True)
    i1 = first(fm == v1)
    fm2 = jnp.where(rowf == i1, ninf, fm)
    v2 = jnp.max(fm2, axis=0, keepdims=True)
    i2 = first(fm2 == v2)
    t = jnp.exp(v2 - v1)
    g1 = p_grp / (1.0 + t)
    g2 = p_grp * t / (1.0 + t)
    oh1 = (rowf == i1) & valid
    oh2 = (rowf == i2) & valid
    m = jnp.where(oh1 | oh2, 1.0, 0.0)
    before = jnp.dot(m.astype(BF16), tri_ref[...], preferred_element_type=F32) + cnt_ref[:, 0:1]
    r1 = jnp.sum(jnp.where(oh1, before, 0.0), axis=0, keepdims=True)
    r2 = jnp.sum(jnp.where(oh2, before, 0.0), axis=0, keepdims=True)
    cnt_ref[...] = cnt_ref[...] + jnp.sum(m, axis=1, keepdims=True)
    zero = jnp.zeros_like(g1)
    rec_row = lax.broadcasted_iota(jnp.int32, (NB, toks), 0)
    rec = jnp.zeros((NB, toks), F32)
    for k, val in enumerate((i1 - N_GROUPS, i2 - N_GROUPS, jnp.where(valid, g1, zero),
                             jnp.where(valid, g2, zero), r1, r2)):
        rec = jnp.where(rec_row == k, val, rec)
    return rec


def _route_tile(h1, whi_ref, wlo_ref, rb_ref, rt_ref, cnt_ref, tri_ref):
    i = pl.program_id(0)
    toks = h1.shape[0]

    @pl.when(i == 0)
    def _():
        cnt_ref[...] = jnp.zeros_like(cnt_ref)
        ri = lax.broadcasted_iota(jnp.int32, (toks, toks), 0)
        ci = lax.broadcasted_iota(jnp.int32, (toks, toks), 1)
        tri_ref[...] = jnp.where(ri < ci, 1.0, 0.0).astype(BF16)

    lg = _router_logits_t(h1, whi_ref, wlo_ref, rb_ref)
    rt_ref[...] = _route_cols(lg, cnt_ref, tri_ref, i * toks)


def _outproj0_kernel(ys_ref, yda_ref, h_ref, w_ref, g_ref, b_ref, whi_ref, wlo_ref, rb_ref,
                     h1_ref, h1r_ref, rt_ref, cnt_ref, das_ref, tri_ref, *, tl):
    per = DA_WIDTH // LANES
    for b in range(NB):
        for s in range(per):
            das_ref[s, pl.ds(b, tl, stride=NB), :] = yda_ref[b, :, LANES * s:LANES * (s + 1)]
    da = jnp.concatenate([das_ref[s] for s in range(per)], axis=1).astype(BF16)
    mix = (jnp.dot(ys_ref[...], w_ref[:S5_WIDTH, :], preferred_element_type=F32)
           + jnp.dot(da, w_ref[S5_WIDTH:, :], preferred_element_type=F32))
    h1 = _layer_norm_rows(ALPHA * h_ref[...] + mix, g_ref[...], b_ref[...])
    h1_ref[...] = h1
    _to_row_tiles(h1r_ref, h1)
    _route_tile(h1, whi_ref, wlo_ref, rb_ref, rt_ref, cnt_ref, tri_ref)


def _outproj1_kernel(y_ref, h_ref, w_ref, g_ref, b_ref, whi_ref, wlo_ref, rb_ref,
                     h1_ref, h1r_ref, rt_ref, cnt_ref, tri_ref):
    mix = jnp.dot(y_ref[...], w_ref[...], preferred_element_type=F32)
    h1 = _layer_norm_rows(ALPHA * h_ref[...] + mix, g_ref[...], b_ref[...])
    h1_ref[...] = h1
    _to_row_tiles(h1r_ref, h1)
    _route_tile(h1, whi_ref, wlo_ref, rb_ref, rt_ref, cnt_ref, tri_ref)


def _router_weights(w_coarse, b_coarse, w_fine, b_fine):
    wf = jnp.transpose(w_fine.astype(F32), (1, 0, 2)).reshape(D_MODEL, N_EXPERTS)
    w = jnp.concatenate([w_coarse.astype(F32), wf], axis=1).T
    w = jnp.pad(w, ((0, ROUTE_ROWS - w.shape[0]), (0, 0)))
    b = jnp.concatenate([b_coarse.astype(F32), b_fine.astype(F32).reshape(-1)])
    b = jnp.pad(b, (0, ROUTE_ROWS - b.shape[0])).reshape(ROUTE_ROWS, 1)
    whi, wlo = _split_bf16(w)
    return whi, wlo, b


def _outproj(ys, yda, h, w_out, ln_g, ln_b, router):
    t_rows = h.shape[0]
    tm = TOK_TILE
    tl = tm // NB
    whi, wlo, rb = router
    w = w_out.astype(BF16)
    g2 = ln_g.astype(F32).reshape(1, D_MODEL)
    b2 = ln_b.astype(F32).reshape(1, D_MODEL)
    full = lambda a: pl.BlockSpec(a.shape, lambda i: (0,) * a.ndim)
    row = lambda width: pl.BlockSpec((tm, width), lambda i: (i, 0))
    common_in = [row(D_MODEL), full(w), full(g2), full(b2), full(whi), full(wlo), full(rb)]
    out_specs = [row(D_MODEL), pl.BlockSpec((tm * ROW_TILES, LANES), lambda i: (i, 0)),
                 pl.BlockSpec((NB, tm), lambda i: (0, i)),
                 pl.BlockSpec((ROUTE_ROWS, LANES), lambda i: (0, 0))]
    out_shape = [jax.ShapeDtypeStruct((t_rows, D_MODEL), F32),
                 jax.ShapeDtypeStruct((t_rows * ROW_TILES, LANES), F32),
                 jax.ShapeDtypeStruct((NB, t_rows), F32),
                 jax.ShapeDtypeStruct((ROUTE_ROWS, LANES), F32)]
    tri = pltpu.VMEM((tm, tm), BF16)
    if yda is not None:
        return pl.pallas_call(
            functools.partial(_outproj0_kernel, tl=tl),
            grid=(t_rows // tm,),
            in_specs=[row(S5_WIDTH), pl.BlockSpec((NB, tl, DA_WIDTH), lambda i: (0, i, 0))] + common_in,
            out_specs=out_specs, out_shape=out_shape,
            scratch_shapes=[pltpu.VMEM((DA_WIDTH // LANES, tm, LANES), F32), tri],
            compiler_params=_cparams(1),
            name="l0_outproj",
        )(ys, yda, h, w, g2, b2, whi, wlo, rb)
    return pl.pallas_call(
        _outproj1_kernel,
        grid=(t_rows // tm,),
        in_specs=[row(ys.shape[1])] + common_in,
        out_specs=out_specs, out_shape=out_shape,
        scratch_shapes=[tri],
        compiler_params=_cparams(1),
        name="l1_outproj",
    )(ys, h, w, g2, b2, whi, wlo, rb)


def _slots(route, counts, n_slots):
    t_rows = route.shape[1]
    experts = route[ROUTE_E:ROUTE_E + TOP_K_FINE].T.astype(jnp.int32)
    gates = route[ROUTE_GATE:ROUTE_GATE + TOP_K_FINE].T
    rank = route[ROUTE_RANK:ROUTE_RANK + TOP_K_FINE].T.astype(jnp.int32)
    cnt = counts[N_GROUPS:N_GROUPS + N_EXPERTS, 0].astype(jnp.int32)
    padded = (cnt + EXPERT_TILE - 1) // EXPERT_TILE * EXPERT_TILE
    pad_end = jnp.cumsum(padded)
    pad_start = pad_end - padded
    raw_start = jnp.cumsum(cnt) - cnt
    valid = (jnp.arange(t_rows) >= PAD_ROWS)[:, None]
    e_ids = jnp.arange(N_EXPERTS, dtype=jnp.int32)
    start_of = jnp.sum(jnp.where(experts[..., None] == e_ids, pad_start, 0), axis=-1)
    dest = jnp.where(valid, start_of + rank, n_slots).astype(jnp.int32)
    nblk = n_slots // EXPERT_TILE
    blk_start = jnp.arange(nblk, dtype=jnp.int32) * EXPERT_TILE
    block_expert = jnp.minimum(jnp.sum((pad_end[None, :] <= blk_start[:, None]).astype(jnp.int32), axis=1),
                               N_EXPERTS - 1)
    hit = block_expert[:, None] == e_ids[None, :]
    blk_pad_start = jnp.sum(jnp.where(hit, pad_start, 0), axis=1)
    blk_raw_start = jnp.sum(jnp.where(hit, raw_start, 0), axis=1)
    blk_cnt = jnp.sum(jnp.where(hit, cnt, 0), axis=1)
    n_used = (pad_end[-1] // EXPERT_TILE).astype(jnp.int32)
    blk_rows = jnp.where(jnp.arange(nblk) < n_used,
                         jnp.clip(blk_cnt - (blk_start - blk_pad_start), 0, EXPERT_TILE), 0)
    n_fetch = (blk_rows + GATHER_CHUNK - 1) // GATHER_CHUNK * GATHER_CHUNK
    order = jnp.argsort(dest.reshape(-1)).astype(jnp.int32)
    sorted_tok = jnp.concatenate([jnp.right_shift(order, 1), jnp.zeros((EXPERT_TILE,), jnp.int32)])
    first = jnp.clip(blk_raw_start + blk_start - blk_pad_start, 0, order.shape[0])
    window = jax.vmap(lambda c: lax.dynamic_slice(sorted_tok, (c,), (EXPERT_TILE,)))(first)
    off = (blk_start - blk_pad_start)[:, None] + jnp.arange(EXPERT_TILE, dtype=jnp.int32)[None, :]
    slot_tok = jnp.where(off < blk_cnt[:, None], window, 0).reshape(-1)
    pos = jnp.where(valid, dest, 0).reshape(-1)
    return (slot_tok * ROW_TILES, block_expert, n_used.reshape(1), n_fetch.astype(jnp.int32),
            pos * ROW_TILES, gates)


ROW_TILES = D_MODEL // LANES


def _to_row_tiles(dst_ref, val):
    rows = val.shape[0]
    for s in range(ROW_TILES):
        dst_ref[pl.ds(s, rows, stride=ROW_TILES), :] = val[:, LANES * s:LANES * (s + 1)]


def _from_row_tiles(src_ref, rows):
    return jnp.concatenate([src_ref[pl.ds(s, rows, stride=ROW_TILES), :] for s in range(ROW_TILES)], axis=1)


LOOKAHEAD = 2
N_BUF = LOOKAHEAD + 1


GATHER_CHUNK = 1


def _expert_kernel(be_ref, nused_ref, nfetch_ref, *refs):
    tok_refs = refs[:N_BUF]
    x_hbm, wg_ref, wu_ref, wd_ref, y_ref = refs[N_BUF:N_BUF + 5]
    bufs = refs[N_BUF + 5:2 * N_BUF + 5]
    sem_ref, wgb_ref, wub_ref, wdb_ref = refs[2 * N_BUF + 5:2 * N_BUF + 9]
    i = pl.program_id(0)
    n_used = nused_ref[0]
    n_tiles = pl.num_programs(0)
    tb = EXPERT_TILE
    turn = lax.rem(i, N_BUF)

    def gather(tok_ref, p, tile):
        nfetch = jnp.where(tile < n_tiles, nfetch_ref[jnp.minimum(tile, n_tiles - 1)], 0)
        for c in range(tb // GATHER_CHUNK):
            @pl.when(c * GATHER_CHUNK < nfetch)
            def _(c=c):
                for r in range(c * GATHER_CHUNK, (c + 1) * GATHER_CHUNK):
                    row = pl.multiple_of(tok_ref[0, 0, r], ROW_TILES)
                    pltpu.make_async_copy(x_hbm.at[pl.ds(row, ROW_TILES), :],
                                          bufs[p].at[pl.ds(ROW_TILES * r, ROW_TILES), :],
                                          sem_ref.at[p]).start(priority=r % 2)

    def wait(p, tile):
        rows = pl.multiple_of(nfetch_ref[tile] * ROW_TILES, ROW_TILES)
        pltpu.make_async_copy(x_hbm.at[pl.ds(0, rows), :], bufs[p].at[pl.ds(0, rows), :],
                              sem_ref.at[p]).wait()

    @pl.when(i == 0)
    def _():
        for p in range(N_BUF):
            bufs[p][...] = jnp.zeros_like(bufs[p])
        for d in range(LOOKAHEAD):
            gather(tok_refs[d], d, d)

    @pl.when((i < n_used) & ((i == 0) | (be_ref[i] != be_ref[jnp.maximum(i - 1, 0)])))
    def _():
        wgb_ref[...] = wg_ref[0].astype(BF16)
        wub_ref[...] = wu_ref[0].astype(BF16)
        wdb_ref[...] = wd_ref[0].astype(BF16)

    for p in range(N_BUF):
        @pl.when((turn == p) & (i < n_used))
        def _(p=p):
            wait(p, i)
            gather(tok_refs[LOOKAHEAD], (p + LOOKAHEAD) % N_BUF, i + LOOKAHEAD)
            x = _from_row_tiles(bufs[p], tb).astype(BF16)
            g = jnp.dot(x, wgb_ref[...], preferred_element_type=F32)
            u = jnp.dot(x, wub_ref[...], preferred_element_type=F32)
            hid = (g * _sigmoid(g) * u).astype(BF16)
            _to_row_tiles(y_ref, jnp.dot(hid, wdb_ref[...], preferred_element_type=F32))

    @pl.when(i >= n_used)
    def _():
        y_ref[...] = jnp.zeros_like(y_ref)


def _experts(h1r, slot_row, block_expert, n_used, n_fetch, w_gate, w_up, w_down):
    n_slots = slot_row.shape[0]
    nblk = n_slots // EXPERT_TILE
    tok3 = slot_row.reshape(nblk, 1, EXPERT_TILE)
    ahead = lambda d: pl.BlockSpec((1, 1, EXPERT_TILE), lambda i, be, nu, nf: (jnp.minimum(i + d, nblk - 1), 0, 0),
                                   memory_space=pltpu.SMEM)
    tile_buf = pltpu.VMEM((EXPERT_TILE * ROW_TILES, LANES), F32)
    grid_spec = pltpu.PrefetchScalarGridSpec(
        num_scalar_prefetch=3,
        grid=(nblk,),
        in_specs=[ahead(d) for d in range(N_BUF)] + [
            pl.BlockSpec(memory_space=pl.ANY),
            pl.BlockSpec((1, D_MODEL, D_EXPERT), lambda i, be, nu, nf: (be[i], 0, 0)),
            pl.BlockSpec((1, D_MODEL, D_EXPERT), lambda i, be, nu, nf: (be[i], 0, 0)),
            pl.BlockSpec((1, D_EXPERT, D_MODEL), lambda i, be, nu, nf: (be[i], 0, 0))],
        out_specs=pl.BlockSpec((EXPERT_TILE * ROW_TILES, LANES), lambda i, be, nu, nf: (i, 0)),
        scratch_shapes=[tile_buf] * N_BUF + [
            pltpu.SemaphoreType.DMA((N_BUF,)),
            pltpu.VMEM((D_MODEL, D_EXPERT), BF16), pltpu.VMEM((D_MODEL, D_EXPERT), BF16),
            pltpu.VMEM((D_EXPERT, D_MODEL), BF16)],
    )
    return pl.pallas_call(
        _expert_kernel,
        grid_spec=grid_spec,
        out_shape=jax.ShapeDtypeStruct((n_slots * ROW_TILES, LANES), F32),
        compiler_params=_cparams(1),
        name="moe_experts",
    )(block_expert, n_used, n_fetch, *([tok3] * N_BUF), h1r, w_gate, w_up, w_down)


def _combine_kernel(*refs, n_tiles, to_batch_major):
    pos_refs = refs[:N_BUF]
    yb_hbm, gate_ref, h_ref, g_ref, b_ref = refs[N_BUF:N_BUF + 5]
    if to_batch_major:
        o_ref, ybuf_ref, sem_ref, hs_ref = refs[N_BUF + 5:]
    else:
        w_ref, o_ref, gg_ref, rec_ref, ybuf_ref, sem_ref = refs[N_BUF + 5:]
    i = pl.program_id(0)
    tm = COMBINE_TILE
    turn = lax.rem(i, N_BUF)

    def gather(pos_ref, p):
        for r in range(tm):
            for kk in range(TOP_K_FINE):
                row = pl.multiple_of(pos_ref[0, 0, TOP_K_FINE * r + kk], ROW_TILES)
                pltpu.make_async_copy(yb_hbm.at[pl.ds(row, ROW_TILES), :],
                                      ybuf_ref.at[p, kk, pl.ds(ROW_TILES * r, ROW_TILES), :],
                                      sem_ref.at[p]).start(priority=kk)

    def finish(p, look_ahead):
        for kk in range(TOP_K_FINE):
            pltpu.make_async_copy(yb_hbm.at[pl.ds(0, tm * ROW_TILES), :], ybuf_ref.at[p, kk],
                                  sem_ref.at[p]).wait()
        if look_ahead:
            gather(pos_refs[LOOKAHEAD], (p + LOOKAHEAD) % N_BUF)
        gts = gate_ref[...]
        moe = (_from_row_tiles(ybuf_ref.at[p, 0], tm) * gts[:, 0:1]
               + _from_row_tiles(ybuf_ref.at[p, 1], tm) * gts[:, 1:2])
        h2 = _layer_norm_rows(ALPHA * h_ref[...] + moe, g_ref[...], b_ref[...])
        if not to_batch_major:
            o_ref[...] = h2
            z = jnp.dot(h2.astype(BF16), w_ref[...], preferred_element_type=F32)
            gg_ref[...] = _gelu(z[:, :D_RNN]).astype(BF16)
            rec_ref[...] = z[:, D_RNN:]
        else:
            tl = tm // NB
            for s in range(D_MODEL // LANES):
                hs_ref[s] = h2[:, LANES * s:LANES * (s + 1)]
            for b in range(NB):
                for s in range(D_MODEL // LANES):
                    o_ref[b, :, LANES * s:LANES * (s + 1)] = hs_ref[s, pl.ds(b, tl, stride=NB), :]

    @pl.when(i == 0)
    def _():
        for d in range(LOOKAHEAD):
            gather(pos_refs[d], d)

    for p in range(N_BUF):
        @pl.when((turn == p) & (i < n_tiles - LOOKAHEAD))
        def _(p=p):
            finish(p, True)

    @pl.when(i >= n_tiles - LOOKAHEAD)
    def _():
        finish(turn, False)


def _combine(yb, pos, gates, h1, ln_g, ln_b, w_next=None):
    final = w_next is None
    t_rows = h1.shape[0]
    tm = COMBINE_TILE
    first_tile = (LANES * NB) // tm if final else 0
    n_tiles = t_rows // tm - first_tile
    pos3 = pos.reshape(t_rows // tm, 1, TOP_K_FINE * tm)
    g2 = ln_g.astype(F32).reshape(1, D_MODEL)
    b2 = ln_b.astype(F32).reshape(1, D_MODEL)
    full = lambda a: pl.BlockSpec(a.shape, lambda i: (0,) * a.ndim)
    last = t_rows // tm - 1
    ahead = lambda d: pl.BlockSpec((1, 1, TOP_K_FINE * tm),
                                   lambda i: (jnp.minimum(i + first_tile + d, last), 0, 0),
                                   memory_space=pltpu.SMEM)
    in_specs = [ahead(d) for d in range(N_BUF)] + [
                pl.BlockSpec(memory_space=pl.ANY),
                pl.BlockSpec((tm, TOP_K_FINE), lambda i: (i + first_tile, 0)),
                pl.BlockSpec((tm, D_MODEL), lambda i: (i + first_tile, 0)),
                full(g2), full(b2)]
    scratch = [pltpu.VMEM((N_BUF, TOP_K_FINE, tm * ROW_TILES, LANES), F32), pltpu.SemaphoreType.DMA((N_BUF,))]
    if final:
        tl = tm // NB
        seq = t_rows // NB - LANES
        out_specs = pl.BlockSpec((NB, tl, D_MODEL), lambda i: (0, i, 0))
        out_shape = jax.ShapeDtypeStruct((NB, seq, D_MODEL), F32)
        scratch.append(pltpu.VMEM((D_MODEL // LANES, tm, LANES), F32))
        operands = (yb, gates, h1, g2, b2)
    else:
        row = lambda width: pl.BlockSpec((tm, width), lambda i: (i, 0))
        in_specs.append(full(w_next))
        out_specs = [row(D_MODEL), row(D_RNN), row(D_RNN)]
        out_shape = [jax.ShapeDtypeStruct((t_rows, D_MODEL), F32),
                     jax.ShapeDtypeStruct((t_rows, D_RNN), BF16),
                     jax.ShapeDtypeStruct((t_rows, D_RNN), F32)]
        operands = (yb, gates, h1, g2, b2, w_next)
    return pl.pallas_call(
        functools.partial(_combine_kernel, n_tiles=n_tiles, to_batch_major=final),
        grid=(n_tiles,),
        in_specs=in_specs, out_specs=out_specs, out_shape=out_shape,
        scratch_shapes=scratch,
        compiler_params=_cparams(1),
        name="moe_combine_final" if final else "moe_combine_inproj",
    )(*([pos3] * N_BUF), *operands)


def _moe_block(h1, h1r, route, counts, w_gate, w_up, w_down, ln_g, ln_b, w_next=None):
    t_rows = h1.shape[0]
    n_assign = (t_rows - PAD_ROWS) * TOP_K_FINE
    nblk = -(-(n_assign + N_EXPERTS * (EXPERT_TILE - 1)) // EXPERT_TILE)
    n_slots = nblk * EXPERT_TILE
    slot_row, block_expert, n_used, n_fetch, pos, gates = _slots(route, counts, n_slots)
    yb = _experts(h1r, slot_row, block_expert, n_used, n_fetch, w_gate, w_up, w_down)
    return _combine(yb, pos, gates, h1, ln_g, ln_b, w_next)


CONV_HALO = (CONV_WIDTH - 1) * NB


def _rglru_kernel(rec_ref, gg_ref, cw_ref, cb_ref, wax_ref, ba_ref, bx_ref, ca_ref,
                  y_ref, rp_ref, a_ref, b_ref, st_ref, *, steps):
    i = pl.program_id(0)
    rows = steps * NB
    sigmoid = lambda x: 1.0 / (1.0 + jnp.exp2(x * (-LOG2_E)))

    @pl.when(i == 0)
    def _():
        st_ref[...] = jnp.zeros_like(st_ref)
        rp_ref[0:CONV_HALO, :] = jnp.zeros((CONV_HALO, D_RNN), F32)

    rp_ref[CONV_HALO:CONV_HALO + rows, :] = rec_ref[...]
    has_pad = i * rows < PAD_ROWS
    real = (i * rows + lax.broadcasted_iota(jnp.int32, (rows, 1), 0)) >= PAD_ROWS

    @pl.when(has_pad)
    def _():
        rp_ref[CONV_HALO:CONV_HALO + rows, :] = jnp.where(real, rp_ref[CONV_HALO:CONV_HALO + rows, :], 0.0)

    xc = cb_ref[...] + sum(rp_ref[NB * j:NB * j + rows, :] * cw_ref[j:j + 1, :] for j in range(CONV_WIDTH))
    tail = rp_ref[rows:rows + CONV_HALO, :]
    rp_ref[0:CONV_HALO, :] = tail

    for n in range(LRU_BLOCKS):
        cs = slice(LRU_BLOCK_W * n, LRU_BLOCK_W * (n + 1))
        xb = xc[:, cs]
        ra = jnp.dot(xb.astype(BF16), wax_ref[n], preferred_element_type=F32)
        r = sigmoid(ra[:, :LRU_BLOCK_W] + ba_ref[:, cs])
        ig = sigmoid(ra[:, LRU_BLOCK_W:] + bx_ref[:, cs])
        a = jnp.exp2(r * ca_ref[:, cs])
        a_ref[:, cs] = a
        b_ref[:, cs] = jnp.sqrt(1.0 - a * a) * (ig * xb)

    @pl.when(has_pad)
    def _():
        b_ref[...] = jnp.where(real, b_ref[...], 0.0)

    def step(t, h):
        r0 = pl.multiple_of(t * NB, NB)
        hn = a_ref[pl.ds(r0, NB), :] * h + b_ref[pl.ds(r0, NB), :]
        b_ref[pl.ds(r0, NB), :] = hn
        return hn

    st_ref[...] = lax.fori_loop(0, steps, step, st_ref[...])
    y_ref[...] = (gg_ref[...].astype(F32) * b_ref[...]).astype(BF16)


def _rglru(rec, gg, conv_w, conv_b, w_a, b_a, w_x, b_x, lru_lambda):
    t_rows = rec.shape[0]
    steps = SCAN_STEPS
    rows = steps * NB
    cw = conv_w.astype(F32)
    cb = conv_b.astype(F32).reshape(1, D_RNN)
    wax = jnp.concatenate([w_a.astype(F32), w_x.astype(F32)], axis=2).astype(BF16)
    ba = b_a.astype(F32).reshape(1, D_RNN)
    bx = b_x.astype(F32).reshape(1, D_RNN)
    ca = (-LRU_C * LOG2_E) * jax.nn.softplus(-lru_lambda.astype(F32)).reshape(1, D_RNN)
    full = lambda a: pl.BlockSpec(a.shape, lambda i: (0,) * a.ndim)
    row = pl.BlockSpec((rows, D_RNN), lambda i: (i, 0))
    return pl.pallas_call(
        functools.partial(_rglru_kernel, steps=steps),
        grid=(t_rows // rows,),
        in_specs=[row, row, full(cw), full(cb), full(wax), full(ba), full(bx), full(ca)],
        out_specs=row,
        out_shape=jax.ShapeDtypeStruct((t_rows, D_RNN), BF16),
        scratch_shapes=[pltpu.VMEM((rows + CONV_HALO, D_RNN), F32), pltpu.VMEM((rows, D_RNN), F32),
                        pltpu.VMEM((rows, D_RNN), F32), pltpu.VMEM((NB, D_RNN), F32)],
        compiler_params=_cparams(1),
        name="l1_rglru",
    )(rec, gg, cw, cb, wax, ba, bx, ca)


def kernel(x, meta, l0_ln1_g, l0_ln1_b, l0_w_in, l0_s5_lambda_re, l0_s5_lambda_im, l0_s5_log_dt, l0_s5_b_re, l0_s5_b_im, l0_s5_c_re, l0_s5_c_im, l0_s5_d, l0_s5_w_glu, l0_s5_b_glu, l0_da_lq1, l0_da_lk1, l0_da_lq2, l0_da_lk2, l0_da_subln_g, l0_w_out, l0_ln2_g, l0_ln2_b, l0_moe_w_coarse, l0_moe_b_coarse, l0_moe_w_fine, l0_moe_b_fine, l0_moe_w_gate, l0_moe_w_up, l0_moe_w_down, l1_ln1_g, l1_ln1_b, l1_w_in, l1_conv_w, l1_conv_b, l1_lru_w_a, l1_lru_b_a, l1_lru_w_x, l1_lru_b_x, l1_lru_lambda, l1_w_out, l1_ln2_g, l1_ln2_b, l1_moe_w_coarse, l1_moe_b_coarse, l1_moe_w_fine, l1_moe_b_fine, l1_moe_w_gate, l1_moe_w_up, l1_moe_w_down):
    bsz, seq, _ = x.shape
    assert bsz == NB and seq % Q_TILE == 0
    dt = x.dtype
    lp = FRONT_PAD + N_META + seq

    col_scale = jnp.concatenate([jnp.ones((S5_WIDTH,), F32),
                                 jnp.full((DA_WIDTH,), DA_HEAD_DIM ** -0.5, F32),
                                 jnp.ones((2 * DA_WIDTH,), F32)])
    w_in0 = (l0_w_in.astype(F32) * col_scale[None, :]).astype(BF16)
    h, u, q, k, v = _inproj0(x, meta, w_in0, lp)
    s5p = _s5_params(l0_s5_lambda_re, l0_s5_lambda_im, l0_s5_log_dt,
                     l0_s5_b_re, l0_s5_b_im, l0_s5_c_re, l0_s5_c_im)
    y_s5 = _s5(u, *s5p, l0_s5_d, l0_s5_w_glu, l0_s5_b_glu)
    lam_init = 0.8 - 0.6 * math.exp(-0.3 * 0)
    lam = (jnp.exp(jnp.sum(l0_da_lq1.astype(F32) * l0_da_lk1.astype(F32)))
           - jnp.exp(jnp.sum(l0_da_lq2.astype(F32) * l0_da_lk2.astype(F32))) + lam_init)
    y_da = _diffattn(q, k, v, lam, l0_da_subln_g, lam_init)
    router0 = _router_weights(l0_moe_w_coarse, l0_moe_b_coarse, l0_moe_w_fine, l0_moe_b_fine)
    h, hr, route, counts = _outproj(y_s5, y_da, h, l0_w_out, l0_ln1_g, l0_ln1_b, router0)
    h, gg, rec = _moe_block(h, hr, route, counts, l0_moe_w_gate, l0_moe_w_up, l0_moe_w_down,
                            l0_ln2_g, l0_ln2_b, l1_w_in.astype(BF16))

    y = _rglru(rec, gg, l1_conv_w, l1_conv_b, l1_lru_w_a, l1_lru_b_a, l1_lru_w_x, l1_lru_b_x, l1_lru_lambda)
    router1 = _router_weights(l1_moe_w_coarse, l1_moe_b_coarse, l1_moe_w_fine, l1_moe_b_fine)
    h, hr, route, counts = _outproj(y, None, h, l1_w_out, l1_ln1_g, l1_ln1_b, router1)
    out = _moe_block(h, hr, route, counts, l1_moe_w_gate, l1_moe_w_up, l1_moe_w_down, l1_ln2_g, l1_ln2_b)
    return out.astype(dt)
```

```python
import functools
import math

import jax
import jax.numpy as jnp
from jax import lax
from jax.experimental import pallas as pl
from jax.experimental.pallas import tpu as pltpu

F32 = jnp.float32
BF16 = jnp.bfloat16

D_MODEL = 1024
DEPTH = 2
CHUNK = 64
N_META = 16
S5_WIDTH = 512
S5_GROUP = 16
S5_GROUPS = 32
S5_STATE = 64
DA_HEADS = 4
DA_HEAD_DIM = 64
DA_WIDTH = 512
D_RNN = 1280
LRU_BLOCKS = 10
LRU_BLOCK_W = 128
CONV_WIDTH = 4
LRU_C = 8.0
N_GROUPS = 4
EXPERTS_PER_GROUP = 8
N_EXPERTS = 32
TOP_K_FINE = 2
D_EXPERT = 512
ALPHA = (2 * DEPTH) ** 0.25
LN_EPS = 1e-5
NEG_INF = -1e30

NB = 8
LANES = 128
FRONT_PAD = LANES - N_META
PAD_ROWS = FRONT_PAD * NB
Q_TILE = 256
K_TILE = 512
TOK_TILE = 512
SCAN_STEPS = 128
EXPERT_TILE = 512
COMBINE_TILE = 256
VMEM_LIMIT = 48 * 1024 * 1024
LOG2_E = math.log2(math.e)


def _cparams(n_axes, vmem=VMEM_LIMIT):
    return pltpu.CompilerParams(dimension_semantics=("arbitrary",) * n_axes,
                                vmem_limit_bytes=vmem)


def _gelu(x):
    return 0.5 * x * (1.0 + jnp.tanh(math.sqrt(2.0 / math.pi) * (x + 0.044715 * (x * x * x))))


def _sigmoid(x):
    return 1.0 / (1.0 + jnp.exp(-x))


def _layer_norm_rows(r, g, b):
    mu = jnp.mean(r, axis=-1, keepdims=True)
    c = r - mu
    var = jnp.mean(c * c, axis=-1, keepdims=True)
    return c * lax.rsqrt(var + LN_EPS) * g + b


def _split_bf16(w):
    hi = w.astype(BF16)
    lo = (w - hi.astype(F32)).astype(BF16)
    return hi, lo


def _inproj0_kernel(head_ref, x_ref, w_ref, h_ref, u_ref, q_ref, k_ref, v_ref, zs_ref, *, tl, head_tiles):
    i = pl.program_id(0)

    @pl.when(i < head_tiles)
    def _():
        h_ref[...] = head_ref[...]

    @pl.when(i >= head_tiles)
    def _():
        for s in range(D_MODEL // LANES):
            for b in range(NB):
                zs_ref[s, pl.ds(b, tl, stride=NB), :] = x_ref[b, :, LANES * s:LANES * (s + 1)]
        for s in range(D_MODEL // LANES):
            h_ref[:, LANES * s:LANES * (s + 1)] = zs_ref[s]

    x = h_ref[...].astype(BF16)
    z = jnp.dot(x, w_ref[...], preferred_element_type=F32)
    u_ref[...] = z[:, :S5_WIDTH]
    n_slab = 3 * DA_WIDTH // LANES
    for s in range(n_slab):
        zs_ref[s] = z[:, S5_WIDTH + LANES * s:S5_WIDTH + LANES * (s + 1)]
    dsts = (q_ref, k_ref, v_ref)
    per = DA_WIDTH // LANES
    for b in range(NB):
        for s in range(n_slab):
            c = (s % per) * LANES
            blk = zs_ref[s, pl.ds(b, tl, stride=NB), :]
            if s < per:
                blk = blk * LOG2_E
            dsts[s // per][b, :, c:c + LANES] = blk.astype(BF16)


def _inproj0(x, meta, w_bf16, lp):
    t_rows = lp * NB
    tm = TOK_TILE
    tl = tm // NB
    head_tiles = LANES // tl
    head = jnp.concatenate([
        jnp.zeros((FRONT_PAD, NB, D_MODEL), x.dtype),
        jnp.broadcast_to(meta.astype(x.dtype)[:, None, :], (N_META, NB, D_MODEL))], axis=0)
    head = head.reshape(LANES * NB, D_MODEL)
    qkv_shape = jax.ShapeDtypeStruct((NB, lp, DA_WIDTH), BF16)
    qkv_spec = pl.BlockSpec((NB, tl, DA_WIDTH), lambda i: (0, i, 0))
    n_slab = max(3 * DA_WIDTH, D_MODEL) // LANES
    return pl.pallas_call(
        functools.partial(_inproj0_kernel, tl=tl, head_tiles=head_tiles),
        grid=(t_rows // tm,),
        in_specs=[pl.BlockSpec((tm, D_MODEL), lambda i: (jnp.minimum(i, head_tiles - 1), 0)),
                  pl.BlockSpec((NB, tl, D_MODEL), lambda i: (0, jnp.maximum(i - head_tiles, 0), 0)),
                  pl.BlockSpec(w_bf16.shape, lambda i: (0, 0))],
        out_specs=[pl.BlockSpec((tm, D_MODEL), lambda i: (i, 0)),
                   pl.BlockSpec((tm, S5_WIDTH), lambda i: (i, 0)), qkv_spec, qkv_spec, qkv_spec],
        out_shape=[jax.ShapeDtypeStruct((t_rows, D_MODEL), F32),
                   jax.ShapeDtypeStruct((t_rows, S5_WIDTH), F32), qkv_shape, qkv_shape, qkv_shape],
        scratch_shapes=[pltpu.VMEM((n_slab, tm, LANES), F32)],
        compiler_params=_cparams(1),
        name="l0_inproj",
    )(head, x, w_bf16)


S5_SLABS = S5_WIDTH // LANES
S5_SLAB_STATE = (S5_GROUPS // S5_SLABS) * S5_STATE
S5_NSTATE = S5_GROUPS * S5_STATE


def _s5_kernel(u_ref, bmat_ref, are_ref, aim_ref, cre_ref, cim_ref, d_ref, wglu_ref, bglu_ref,
               y_ref, hre_ref, him_ref, st_ref, *, steps):
    i = pl.program_id(0)
    rows = steps * NB

    @pl.when(i == 0)
    def _():
        st_ref[...] = jnp.zeros_like(st_ref)

    row = i * rows + lax.broadcasted_iota(jnp.int32, (rows, 1), 0)
    u = jnp.where(row >= PAD_ROWS, u_ref[...], 0.0)
    ub = u.astype(BF16)
    for s in range(S5_SLABS):
        bu = jnp.dot(ub[:, LANES * s:LANES * (s + 1)], bmat_ref[s], preferred_element_type=F32)
        hre_ref[:, S5_SLAB_STATE * s:S5_SLAB_STATE * (s + 1)] = bu[:, :S5_SLAB_STATE]
        him_ref[:, S5_SLAB_STATE * s:S5_SLAB_STATE * (s + 1)] = bu[:, S5_SLAB_STATE:]

    cw = 512
    for cg in range(S5_NSTATE // cw):
        c0 = cg * cw
        a_r = are_ref[:, c0:c0 + cw]
        a_i = aim_ref[:, c0:c0 + cw]

        def step(t, carry, c0=c0, a_r=a_r, a_i=a_i):
            sr, si = carry
            r0 = pl.multiple_of(t * NB, NB)
            br = hre_ref[pl.ds(r0, NB), c0:c0 + cw]
            bi = him_ref[pl.ds(r0, NB), c0:c0 + cw]
            nr = a_r * sr - a_i * si + br
            ni = a_r * si + a_i * sr + bi
            hre_ref[pl.ds(r0, NB), c0:c0 + cw] = nr
            him_ref[pl.ds(r0, NB), c0:c0 + cw] = ni
            return nr, ni

        sr, si = lax.fori_loop(0, steps, step, (st_ref[0, :, c0:c0 + cw], st_ref[1, :, c0:c0 + cw]))
        st_ref[0, :, c0:c0 + cw] = sr
        st_ref[1, :, c0:c0 + cw] = si

    ys = []
    for s in range(S5_SLABS):
        hr = hre_ref[:, S5_SLAB_STATE * s:S5_SLAB_STATE * (s + 1)].astype(BF16)
        hi = him_ref[:, S5_SLAB_STATE * s:S5_SLAB_STATE * (s + 1)].astype(BF16)
        ys.append(jnp.dot(hr, cre_ref[s], preferred_element_type=F32)
                  + jnp.dot(hi, cim_ref[s], preferred_element_type=F32))
    y = jnp.concatenate(ys, axis=1) + d_ref[...] * u
    y = _gelu(y)
    gate = _sigmoid(jnp.dot(y.astype(BF16), wglu_ref[...], preferred_element_type=F32) + bglu_ref[...])
    y_ref[...] = (y * gate).astype(BF16)


def _s5_params(lam_re, lam_im, log_dt, b_re, b_im, c_re, c_im):
    dt = jnp.exp(log_dt.astype(F32))[:, None]
    lr = jnp.minimum(lam_re.astype(F32), -1e-4)
    li = lam_im.astype(F32)
    mag = jnp.exp(lr * dt)
    ar = mag * jnp.cos(li * dt)
    ai = mag * jnp.sin(li * dt)
    den = lr * lr + li * li
    nr, ni = ar - 1.0, ai
    fr = ((nr * lr + ni * li) / den)[..., None]
    fi = ((ni * lr - nr * li) / den)[..., None]
    br, bi = b_re.astype(F32), b_im.astype(F32)
    bbr = fr * br - fi * bi
    bbi = fr * bi + fi * br
    gps = S5_GROUPS // S5_SLABS
    eye = jnp.eye(gps, dtype=F32)

    def in_slab(m):
        m4 = m.reshape(S5_SLABS, gps, S5_STATE, S5_GROUP)
        return jnp.einsum('sgph,gk->sghkp', m4, eye).reshape(S5_SLABS, LANES, S5_SLAB_STATE)

    def out_slab(m):
        m4 = m.reshape(S5_SLABS, gps, S5_GROUP, S5_STATE)
        return jnp.einsum('sghp,gk->sgpkh', m4, eye).reshape(S5_SLABS, S5_SLAB_STATE, LANES)

    bmat = jnp.concatenate([in_slab(bbr), in_slab(bbi)], axis=2).astype(BF16)
    cre = out_slab(c_re.astype(F32)).astype(BF16)
    cim = out_slab(-c_im.astype(F32)).astype(BF16)
    a_re = jnp.broadcast_to(ar.reshape(1, S5_NSTATE), (NB, S5_NSTATE))
    a_im = jnp.broadcast_to(ai.reshape(1, S5_NSTATE), (NB, S5_NSTATE))
    return bmat, a_re, a_im, cre, cim


def _s5(u, bmat, a_re, a_im, cre, cim, d, w_glu, b_glu):
    t_rows = u.shape[0]
    steps = SCAN_STEPS
    rows = steps * NB
    full = lambda a: pl.BlockSpec(a.shape, lambda i: (0,) * a.ndim)
    d2 = d.astype(F32).reshape(1, S5_WIDTH)
    bg2 = b_glu.astype(F32).reshape(1, S5_WIDTH)
    wg = w_glu.astype(BF16)
    return pl.pallas_call(
        functools.partial(_s5_kernel, steps=steps),
        grid=(t_rows // rows,),
        in_specs=[pl.BlockSpec((rows, S5_WIDTH), lambda i: (i, 0)),
                  full(bmat), full(a_re), full(a_im), full(cre), full(cim), full(d2), full(wg), full(bg2)],
        out_specs=pl.BlockSpec((rows, S5_WIDTH), lambda i: (i, 0)),
        out_shape=jax.ShapeDtypeStruct((t_rows, S5_WIDTH), BF16),
        scratch_shapes=[pltpu.VMEM((rows, S5_NSTATE), F32), pltpu.VMEM((rows, S5_NSTATE), F32),
                        pltpu.VMEM((2, NB, S5_NSTATE), F32)],
        compiler_params=_cparams(1),
        name="l0_s5",
    )(u, bmat, a_re, a_im, cre, cim, d2, wg, bg2)


def _diffattn_kernel(lam_ref, q_ref, k_ref, v_ref, g_ref, o_ref,
                     s_ref, qm_ref, m_ref, a_ref, *, lam_init, lp):
    lam = lam_ref[0]
    tq, tk = Q_TILE, K_TILE
    lane = lax.broadcasted_iota(jnp.int32, (tq, LANES), 1)
    qrow = jnp.bitwise_and(lax.broadcasted_iota(jnp.int32, (2 * tq, tk), 0), tq - 1)
    kloc = lax.broadcasted_iota(jnp.int32, (2 * tq, tk), 1)
    nt = (((1,), (1,)), ((), ()))
    bf16_rows = 16

    def key_start(j):
        return pl.multiple_of(jnp.minimum(FRONT_PAD + j * tk, lp - tk), bf16_rows)

    def chunk_of(pos):
        return jnp.right_shift(pos - CHUNK, 6)

    def lane_fold(x, op):
        r = x[:, :LANES]
        for c in range(1, x.shape[1] // LANES):
            r = op(r, x[:, LANES * c:LANES * (c + 1)])
        return r

    def q_start(i):
        return pl.multiple_of(jnp.minimum(i * tq, lp - tq), LANES)

    def n_key_tiles(i):
        return (q_start(i) + tq - FRONT_PAD + tk - 1) // tk

    def n_full_tiles(i):
        return jnp.maximum(q_start(i) - FRONT_PAD, 0) // tk


    def prep(i, par):
        q = q_ref[0, pl.ds(q_start(i), tq), :]
        zero = jnp.zeros_like(q)
        qm_ref[par, :tq] = jnp.where(lane < DA_HEAD_DIM, q, zero)
        qm_ref[par, tq:] = jnp.where(lane >= DA_HEAD_DIM, q, zero)
        m_ref[par] = jnp.full(m_ref.shape[1:], NEG_INF, F32)

    def scores(i, par, j, masked):
        k0 = key_start(j)
        kt = k_ref[0, pl.ds(k0, tk), :]
        if masked:
            kpos = k0 + kloc
            mask = (chunk_of(kpos) <= chunk_of(q_start(i) + qrow)) & (kpos >= FRONT_PAD + j * tk)
        s = lax.dot_general(qm_ref[par], kt, nt, preferred_element_type=F32)
        if masked:
            s = jnp.where(mask, s, NEG_INF)
        s_ref[par, j] = s
        m_ref[par] = jnp.maximum(m_ref[par], lane_fold(s, jnp.maximum))

    ones_col = jnp.where(lax.broadcasted_iota(jnp.int32, (tk, LANES), 1) == 0, 1.0, 0.0).astype(BF16)

    def values(par, j, m):
        vt = jnp.concatenate([v_ref[0, pl.ds(key_start(j), tk), :], ones_col], axis=1)
        p = jnp.exp2((s_ref[par, j] - m).astype(BF16))
        a_ref[...] += jnp.dot(p, vt, preferred_element_type=F32)

    def row_max(par):
        a_ref[...] = jnp.zeros(a_ref.shape, F32)
        return jnp.max(m_ref[par], axis=-1, keepdims=True)

    def finish(i):
        l1 = a_ref[:tq, LANES:LANES + 1]
        l2 = a_ref[tq:, LANES:LANES + 1]
        o = a_ref[:tq, :LANES] / l1 - lam * (a_ref[tq:, :LANES] / l2)
        o = o * lax.rsqrt(jnp.mean(o * o, axis=-1, keepdims=True) + LN_EPS) * g_ref[...]
        o_ref[0, pl.ds(q_start(i), tq), :] = o * (1.0 - lam_init)

    def loop(lo, hi, body):
        def wrapped(j, _):
            body(j)
            return 0
        lax.fori_loop(lo, hi, wrapped, 0)

    def step(i, par):
        m = row_max(par)
        prep(i + 1, 1 - par)

        def both(j, masked):
            scores(i + 1, 1 - par, j, masked)
            values(par, j, m)

        loop(0, n_full_tiles(i + 1), functools.partial(both, masked=False))
        loop(n_full_tiles(i + 1), n_key_tiles(i), functools.partial(both, masked=True))
        loop(n_key_tiles(i), n_key_tiles(i + 1), lambda j: scores(i + 1, 1 - par, j, True))
        finish(i)

    def step_pair(i2):
        step(2 * i2, 0)
        step(2 * i2 + 1, 1)

    n_q = pl.cdiv(lp, tq)
    prep(0, 0)
    loop(0, n_key_tiles(0), lambda j: scores(0, 0, j, True))
    loop(0, (n_q - 1) // 2, step_pair)
    if (n_q - 1) % 2:
        step(n_q - 2, 0)
    last = (n_q - 1) % 2
    m = row_max(last)
    loop(0, n_key_tiles(n_q - 1), lambda j: values(last, j, m))
    finish(n_q - 1)


def _diffattn(q, k, v, lam, subln_g, lam_init):
    nb, lp, _ = q.shape
    g2 = subln_g.astype(F32).reshape(1, 2 * DA_HEAD_DIM)
    seq_spec = pl.BlockSpec((1, lp, LANES), lambda b, h: (b, 0, h))
    return pl.pallas_call(
        functools.partial(_diffattn_kernel, lam_init=lam_init, lp=lp),
        grid=(nb, DA_HEADS),
        in_specs=[pl.BlockSpec(memory_space=pltpu.SMEM), seq_spec, seq_spec, seq_spec,
                  pl.BlockSpec((1, LANES), lambda b, h: (0, 0))],
        out_specs=seq_spec,
        out_shape=jax.ShapeDtypeStruct((nb, lp, DA_WIDTH), F32),
        scratch_shapes=[pltpu.VMEM((2, pl.cdiv(lp, K_TILE), 2 * Q_TILE, K_TILE), F32),
                        pltpu.VMEM((2, 2 * Q_TILE, LANES), BF16),
                        pltpu.VMEM((2, 2 * Q_TILE, LANES), F32),
                        pltpu.VMEM((2 * Q_TILE, 2 * LANES), F32)],
        compiler_params=_cparams(2),
        name="l0_diffattn",
    )(lam.reshape(1), q, k, v, g2)


ROUTE_ROWS = 64
ROUTE_E, ROUTE_GATE, ROUTE_RANK = 0, 2, 4


def _router_logits_t(h1, whi_ref, wlo_ref, rb_ref):
    hi = h1.astype(BF16)
    lo = (h1 - hi.astype(F32)).astype(BF16)
    nt = (((1,), (1,)), ((), ()))
    return (lax.dot_general(whi_ref[...], hi, nt, preferred_element_type=F32)
            + lax.dot_general(whi_ref[...], lo, nt, preferred_element_type=F32)
            + lax.dot_general(wlo_ref[...], hi, nt, preferred_element_type=F32) + rb_ref[...])


def _route_cols(lg, cnt_ref, tri_ref, first_tok):
    toks = lg.shape[1]
    row = lax.broadcasted_iota(jnp.int32, lg.shape, 0)
    rowf = row.astype(F32)
    valid = (first_tok + lax.broadcasted_iota(jnp.int32, (1, toks), 1)) >= PAD_ROWS
    ninf = float('-inf')
    first = lambda hit: jnp.min(jnp.where(hit, rowf, float(ROUTE_ROWS)), axis=0, keepdims=True)
    cm = jnp.where(row < N_GROUPS, lg, ninf)
    cmax = jnp.max(cm, axis=0, keepdims=True)
    p_grp = 1.0 / jnp.sum(jnp.exp(cm - cmax), axis=0, keepdims=True)
    lo = N_GROUPS + EXPERTS_PER_GROUP * first(cm == cmax)
    fm = jnp.where((rowf >= lo) & (rowf < lo + EXPERTS_PER_GROUP), lg, ninf)
    v1 = jnp.max(fm, axis=0, keepdims=True)
    i1 = first(fm == v1)
    fm2 = jnp.where(rowf == i1, ninf, fm)
    v2 = jnp.max(fm2, axis=0, keepdims=True)
    i2 = first(fm2 == v2)
    t = jnp.exp(v2 - v1)
    g1 = p_grp / (1.0 + t)
    g2 = p_grp * t / (1.0 + t)
    oh1 = (rowf == i1) & valid
    oh2 = (rowf == i2) & valid
    m = jnp.where(oh1 | oh2, 1.0, 0.0)
    before = jnp.dot(m.astype(BF16), tri_ref[...], preferred_element_type=F32) + cnt_ref[:, 0:1]
    r1 = jnp.sum(jnp.where(oh1, before, 0.0), axis=0, keepdims=True)
    r2 = jnp.sum(jnp.where(oh2, before, 0.0), axis=0, keepdims=True)
    cnt_ref[...] = cnt_ref[...] + jnp.sum(m, axis=1, keepdims=True)
    zero = jnp.zeros_like(g1)
    rec_row = lax.broadcasted_iota(jnp.int32, (NB, toks), 0)
    rec = jnp.zeros((NB, toks), F32)
    for k, val in enumerate((i1 - N_GROUPS, i2 - N_GROUPS, jnp.where(valid, g1, zero),
                             jnp.where(valid, g2, zero), r1, r2)):
        rec = jnp.where(rec_row == k, val, rec)
    return rec


def _route_tile(h1, whi_ref, wlo_ref, rb_ref, rt_ref, cnt_ref, tri_ref):
    i = pl.program_id(0)
    toks = h1.shape[0]

    @pl.when(i == 0)
    def _():
        cnt_ref[...] = jnp.zeros_like(cnt_ref)
        ri = lax.broadcasted_iota(jnp.int32, (toks, toks), 0)
        ci = lax.broadcasted_iota(jnp.int32, (toks, toks), 1)
        tri_ref[...] = jnp.where(ri < ci, 1.0, 0.0).astype(BF16)

    lg = _router_logits_t(h1, whi_ref, wlo_ref, rb_ref)
    rt_ref[...] = _route_cols(lg, cnt_ref, tri_ref, i * toks)


def _outproj0_kernel(ys_ref, yda_ref, h_ref, w_ref, g_ref, b_ref, whi_ref, wlo_ref, rb_ref,
                     h1_ref, h1r_ref, rt_ref, cnt_ref, das_ref, tri_ref, *, tl):
    per = DA_WIDTH // LANES
    for b in range(NB):
        for s in range(per):
            das_ref[s, pl.ds(b, tl, stride=NB), :] = yda_ref[b, :, LANES * s:LANES * (s + 1)]
    da = jnp.concatenate([das_ref[s] for s in range(per)], axis=1).astype(BF16)
    mix = (jnp.dot(ys_ref[...], w_ref[:S5_WIDTH, :], preferred_element_type=F32)
           + jnp.dot(da, w_ref[S5_WIDTH:, :], preferred_element_type=F32))
    h1 = _layer_norm_rows(ALPHA * h_ref[...] + mix, g_ref[...], b_ref[...])
    h1_ref[...] = h1
    _to_row_tiles(h1r_ref, h1)
    _route_tile(h1, whi_ref, wlo_ref, rb_ref, rt_ref, cnt_ref, tri_ref)


def _outproj1_kernel(y_ref, h_ref, w_ref, g_ref, b_ref, whi_ref, wlo_ref, rb_ref,
                     h1_ref, h1r_ref, rt_ref, cnt_ref, tri_ref):
    mix = jnp.dot(y_ref[...], w_ref[...], preferred_element_type=F32)
    h1 = _layer_norm_rows(ALPHA * h_ref[...] + mix, g_ref[...], b_ref[...])
    h1_ref[...] = h1
    _to_row_tiles(h1r_ref, h1)
    _route_tile(h1, whi_ref, wlo_ref, rb_ref, rt_ref, cnt_ref, tri_ref)


def _router_weights(w_coarse, b_coarse, w_fine, b_fine):
    wf = jnp.transpose(w_fine.astype(F32), (1, 0, 2)).reshape(D_MODEL, N_EXPERTS)
    w = jnp.concatenate([w_coarse.astype(F32), wf], axis=1).T
    w = jnp.pad(w, ((0, ROUTE_ROWS - w.shape[0]), (0, 0)))
    b = jnp.concatenate([b_coarse.astype(F32), b_fine.astype(F32).reshape(-1)])
    b = jnp.pad(b, (0, ROUTE_ROWS - b.shape[0])).reshape(ROUTE_ROWS, 1)
    whi, wlo = _split_bf16(w)
    return whi, wlo, b


def _outproj(ys, yda, h, w_out, ln_g, ln_b, router):
    t_rows = h.shape[0]
    tm = TOK_TILE
    tl = tm // NB
    whi, wlo, rb = router
    w = w_out.astype(BF16)
    g2 = ln_g.astype(F32).reshape(1, D_MODEL)
    b2 = ln_b.astype(F32).reshape(1, D_MODEL)
    full = lambda a: pl.BlockSpec(a.shape, lambda i: (0,) * a.ndim)
    row = lambda width: pl.BlockSpec((tm, width), lambda i: (i, 0))
    common_in = [row(D_MODEL), full(w), full(g2), full(b2), full(whi), full(wlo), full(rb)]
    out_specs = [row(D_MODEL), pl.BlockSpec((tm * ROW_TILES, LANES), lambda i: (i, 0)),
                 pl.BlockSpec((NB, tm), lambda i: (0, i)),
                 pl.BlockSpec((ROUTE_ROWS, LANES), lambda i: (0, 0))]
    out_shape = [jax.ShapeDtypeStruct((t_rows, D_MODEL), F32),
                 jax.ShapeDtypeStruct((t_rows * ROW_TILES, LANES), F32),
                 jax.ShapeDtypeStruct((NB, t_rows), F32),
                 jax.ShapeDtypeStruct((ROUTE_ROWS, LANES), F32)]
    tri = pltpu.VMEM((tm, tm), BF16)
    if yda is not None:
        return pl.pallas_call(
            functools.partial(_outproj0_kernel, tl=tl),
            grid=(t_rows // tm,),
            in_specs=[row(S5_WIDTH), pl.BlockSpec((NB, tl, DA_WIDTH), lambda i: (0, i, 0))] + common_in,
            out_specs=out_specs, out_shape=out_shape,
            scratch_shapes=[pltpu.VMEM((DA_WIDTH // LANES, tm, LANES), F32), tri],
            compiler_params=_cparams(1),
            name="l0_outproj",
        )(ys, yda, h, w, g2, b2, whi, wlo, rb)
    return pl.pallas_call(
        _outproj1_kernel,
        grid=(t_rows // tm,),
        in_specs=[row(ys.shape[1])] + common_in,
        out_specs=out_specs, out_shape=out_shape,
        scratch_shapes=[tri],
        compiler_params=_cparams(1),
        name="l1_outproj",
    )(ys, h, w, g2, b2, whi, wlo, rb)


def _slots(route, counts, n_slots):
    t_rows = route.shape[1]
    experts = route[ROUTE_E:ROUTE_E + TOP_K_FINE].T.astype(jnp.int32)
    gates = route[ROUTE_GATE:ROUTE_GATE + TOP_K_FINE].T
    rank = route[ROUTE_RANK:ROUTE_RANK + TOP_K_FINE].T.astype(jnp.int32)
    cnt = counts[N_GROUPS:N_GROUPS + N_EXPERTS, 0].astype(jnp.int32)
    padded = (cnt + EXPERT_TILE - 1) // EXPERT_TILE * EXPERT_TILE
    pad_end = jnp.cumsum(padded)
    pad_start = pad_end - padded
    raw_start = jnp.cumsum(cnt) - cnt
    valid = (jnp.arange(t_rows) >= PAD_ROWS)[:, None]
    e_ids = jnp.arange(N_EXPERTS, dtype=jnp.int32)
    start_of = jnp.sum(jnp.where(experts[..., None] == e_ids, pad_start, 0), axis=-1)
    dest = jnp.where(valid, start_of + rank, n_slots).astype(jnp.int32)
    nblk = n_slots // EXPERT_TILE
    blk_start = jnp.arange(nblk, dtype=jnp.int32) * EXPERT_TILE
    block_expert = jnp.minimum(jnp.sum((pad_end[None, :] <= blk_start[:, None]).astype(jnp.int32), axis=1),
                               N_EXPERTS - 1)
    hit = block_expert[:, None] == e_ids[None, :]
    blk_pad_start = jnp.sum(jnp.where(hit, pad_start, 0), axis=1)
    blk_raw_start = jnp.sum(jnp.where(hit, raw_start, 0), axis=1)
    blk_cnt = jnp.sum(jnp.where(hit, cnt, 0), axis=1)
    n_used = (pad_end[-1] // EXPERT_TILE).astype(jnp.int32)
    blk_rows = jnp.where(jnp.arange(nblk) < n_used,
                         jnp.clip(blk_cnt - (blk_start - blk_pad_start), 0, EXPERT_TILE), 0)
    n_fetch = (blk_rows + GATHER_CHUNK - 1) // GATHER_CHUNK * GATHER_CHUNK
    order = jnp.argsort(dest.reshape(-1)).astype(jnp.int32)
    off = (blk_start - blk_pad_start)[:, None] + jnp.arange(EXPERT_TILE, dtype=jnp.int32)[None, :]
    src = jnp.clip(blk_raw_start[:, None] + off, 0, order.shape[0] - 1)
    slot_tok = jnp.where(off < blk_cnt[:, None], jnp.right_shift(order[src], 1), 0).reshape(-1)
    pos = jnp.where(valid, dest, 0).reshape(-1)
    return (slot_tok * ROW_TILES, block_expert, n_used.reshape(1), n_fetch.astype(jnp.int32),
            pos * ROW_TILES, gates)


ROW_TILES = D_MODEL // LANES


def _to_row_tiles(dst_ref, val):
    rows = val.shape[0]
    for s in range(ROW_TILES):
        dst_ref[pl.ds(s, rows, stride=ROW_TILES), :] = val[:, LANES * s:LANES * (s + 1)]


def _from_row_tiles(src_ref, rows):
    return jnp.concatenate([src_ref[pl.ds(s, rows, stride=ROW_TILES), :] for s in range(ROW_TILES)], axis=1)


LOOKAHEAD = 2
N_BUF = LOOKAHEAD + 1


GATHER_CHUNK = 1


def _expert_kernel(be_ref, nused_ref, nfetch_ref, *refs):
    tok_refs = refs[:N_BUF]
    x_hbm, wg_ref, wu_ref, wd_ref, y_ref = refs[N_BUF:N_BUF + 5]
    bufs = refs[N_BUF + 5:2 * N_BUF + 5]
    sem_ref, wgb_ref, wub_ref, wdb_ref = refs[2 * N_BUF + 5:2 * N_BUF + 9]
    i = pl.program_id(0)
    n_used = nused_ref[0]
    n_tiles = pl.num_programs(0)
    tb = EXPERT_TILE
    turn = lax.rem(i, N_BUF)

    def gather(tok_ref, p, tile):
        nfetch = jnp.where(tile < n_tiles, nfetch_ref[jnp.minimum(tile, n_tiles - 1)], 0)
        for c in range(tb // GATHER_CHUNK):
            @pl.when(c * GATHER_CHUNK < nfetch)
            def _(c=c):
                for r in range(c * GATHER_CHUNK, (c + 1) * GATHER_CHUNK):
                    row = pl.multiple_of(tok_ref[0, 0, r], ROW_TILES)
                    pltpu.make_async_copy(x_hbm.at[pl.ds(row, ROW_TILES), :],
                                          bufs[p].at[pl.ds(ROW_TILES * r, ROW_TILES), :],
                                          sem_ref.at[p]).start(priority=r % 2)

    def wait(p, tile):
        rows = pl.multiple_of(nfetch_ref[tile] * ROW_TILES, ROW_TILES)
        pltpu.make_async_copy(x_hbm.at[pl.ds(0, rows), :], bufs[p].at[pl.ds(0, rows), :],
                              sem_ref.at[p]).wait()

    @pl.when(i == 0)
    def _():
        for p in range(N_BUF):
            bufs[p][...] = jnp.zeros_like(bufs[p])
        for d in range(LOOKAHEAD):
            gather(tok_refs[d], d, d)

    @pl.when((i < n_used) & ((i == 0) | (be_ref[i] != be_ref[jnp.maximum(i - 1, 0)])))
    def _():
        wgb_ref[...] = wg_ref[0].astype(BF16)
        wub_ref[...] = wu_ref[0].astype(BF16)
        wdb_ref[...] = wd_ref[0].astype(BF16)

    for p in range(N_BUF):
        @pl.when((turn == p) & (i < n_used))
        def _(p=p):
            wait(p, i)
            gather(tok_refs[LOOKAHEAD], (p + LOOKAHEAD) % N_BUF, i + LOOKAHEAD)
            x = _from_row_tiles(bufs[p], tb).astype(BF16)
            g = jnp.dot(x, wgb_ref[...], preferred_element_type=F32)
            u = jnp.dot(x, wub_ref[...], preferred_element_type=F32)
            hid = (g * _sigmoid(g) * u).astype(BF16)
            _to_row_tiles(y_ref, jnp.dot(hid, wdb_ref[...], preferred_element_type=F32))

    @pl.when(i >= n_used)
    def _():
        y_ref[...] = jnp.zeros_like(y_ref)


def _experts(h1r, slot_row, block_expert, n_used, n_fetch, w_gate, w_up, w_down):
    n_slots = slot_row.shape[0]
    nblk = n_slots // EXPERT_TILE
    tok3 = slot_row.reshape(nblk, 1, EXPERT_TILE)
    ahead = lambda d: pl.BlockSpec((1, 1, EXPERT_TILE), lambda i, be, nu, nf: (jnp.minimum(i + d, nblk - 1), 0, 0),
                                   memory_space=pltpu.SMEM)
    tile_buf = pltpu.VMEM((EXPERT_TILE * ROW_TILES, LANES), F32)
    grid_spec = pltpu.PrefetchScalarGridSpec(
        num_scalar_prefetch=3,
        grid=(nblk,),
        in_specs=[ahead(d) for d in range(N_BUF)] + [
            pl.BlockSpec(memory_space=pl.ANY),
            pl.BlockSpec((1, D_MODEL, D_EXPERT), lambda i, be, nu, nf: (be[i], 0, 0)),
            pl.BlockSpec((1, D_MODEL, D_EXPERT), lambda i, be, nu, nf: (be[i], 0, 0)),
            pl.BlockSpec((1, D_EXPERT, D_MODEL), lambda i, be, nu, nf: (be[i], 0, 0))],
        out_specs=pl.BlockSpec((EXPERT_TILE * ROW_TILES, LANES), lambda i, be, nu, nf: (i, 0)),
        scratch_shapes=[tile_buf] * N_BUF + [
            pltpu.SemaphoreType.DMA((N_BUF,)),
            pltpu.VMEM((D_MODEL, D_EXPERT), BF16), pltpu.VMEM((D_MODEL, D_EXPERT), BF16),
            pltpu.VMEM((D_EXPERT, D_MODEL), BF16)],
    )
    return pl.pallas_call(
        _expert_kernel,
        grid_spec=grid_spec,
        out_shape=jax.ShapeDtypeStruct((n_slots * ROW_TILES, LANES), F32),
        compiler_params=_cparams(1),
        name="moe_experts",
    )(block_expert, n_used, n_fetch, *([tok3] * N_BUF), h1r, w_gate, w_up, w_down)


def _combine_kernel(*refs, n_tiles, to_batch_major):
    pos_refs = refs[:N_BUF]
    yb_hbm, gate_ref, h_ref, g_ref, b_ref = refs[N_BUF:N_BUF + 5]
    if to_batch_major:
        o_ref, ybuf_ref, sem_ref, hs_ref = refs[N_BUF + 5:]
    else:
        w_ref, o_ref, gg_ref, rec_ref, ybuf_ref, sem_ref = refs[N_BUF + 5:]
    i = pl.program_id(0)
    tm = COMBINE_TILE
    turn = lax.rem(i, N_BUF)

    def gather(pos_ref, p):
        for r in range(tm):
            for kk in range(TOP_K_FINE):
                row = pl.multiple_of(pos_ref[0, 0, TOP_K_FINE * r + kk], ROW_TILES)
                pltpu.make_async_copy(yb_hbm.at[pl.ds(row, ROW_TILES), :],
                                      ybuf_ref.at[p, kk, pl.ds(ROW_TILES * r, ROW_TILES), :],
                                      sem_ref.at[p]).start(priority=kk)

    def finish(p, look_ahead):
        for kk in range(TOP_K_FINE):
            pltpu.make_async_copy(yb_hbm.at[pl.ds(0, tm * ROW_TILES), :], ybuf_ref.at[p, kk],
                                  sem_ref.at[p]).wait()
        if look_ahead:
            gather(pos_refs[LOOKAHEAD], (p + LOOKAHEAD) % N_BUF)
        gts = gate_ref[...]
        moe = (_from_row_tiles(ybuf_ref.at[p, 0], tm) * gts[:, 0:1]
               + _from_row_tiles(ybuf_ref.at[p, 1], tm) * gts[:, 1:2])
        h2 = _layer_norm_rows(ALPHA * h_ref[...] + moe, g_ref[...], b_ref[...])
        if not to_batch_major:
            o_ref[...] = h2
            z = jnp.dot(h2.astype(BF16), w_ref[...], preferred_element_type=F32)
            gg_ref[...] = _gelu(z[:, :D_RNN]).astype(BF16)
            rec_ref[...] = z[:, D_RNN:]
        else:
            tl = tm // NB
            for s in range(D_MODEL // LANES):
                hs_ref[s] = h2[:, LANES * s:LANES * (s + 1)]
            for b in range(NB):
                for s in range(D_MODEL // LANES):
                    o_ref[b, :, LANES * s:LANES * (s + 1)] = hs_ref[s, pl.ds(b, tl, stride=NB), :]

    @pl.when(i == 0)
    def _():
        for d in range(LOOKAHEAD):
            gather(pos_refs[d], d)

    for p in range(N_BUF):
        @pl.when((turn == p) & (i < n_tiles - LOOKAHEAD))
        def _(p=p):
            finish(p, True)

    @pl.when(i >= n_tiles - LOOKAHEAD)
    def _():
        finish(turn, False)


def _combine(yb, pos, gates, h1, ln_g, ln_b, w_next=None):
    final = w_next is None
    t_rows = h1.shape[0]
    tm = COMBINE_TILE
    first_tile = (LANES * NB) // tm if final else 0
    n_tiles = t_rows // tm - first_tile
    pos3 = pos.reshape(t_rows // tm, 1, TOP_K_FINE * tm)
    g2 = ln_g.astype(F32).reshape(1, D_MODEL)
    b2 = ln_b.astype(F32).reshape(1, D_MODEL)
    full = lambda a: pl.BlockSpec(a.shape, lambda i: (0,) * a.ndim)
    last = t_rows // tm - 1
    ahead = lambda d: pl.BlockSpec((1, 1, TOP_K_FINE * tm),
                                   lambda i: (jnp.minimum(i + first_tile + d, last), 0, 0),
                                   memory_space=pltpu.SMEM)
    in_specs = [ahead(d) for d in range(N_BUF)] + [
                pl.BlockSpec(memory_space=pl.ANY),
                pl.BlockSpec((tm, TOP_K_FINE), lambda i: (i + first_tile, 0)),
                pl.BlockSpec((tm, D_MODEL), lambda i: (i + first_tile, 0)),
                full(g2), full(b2)]
    scratch = [pltpu.VMEM((N_BUF, TOP_K_FINE, tm * ROW_TILES, LANES), F32), pltpu.SemaphoreType.DMA((N_BUF,))]
    if final:
        tl = tm // NB
        seq = t_rows // NB - LANES
        out_specs = pl.BlockSpec((NB, tl, D_MODEL), lambda i: (0, i, 0))
        out_shape = jax.ShapeDtypeStruct((NB, seq, D_MODEL), F32)
        scratch.append(pltpu.VMEM((D_MODEL // LANES, tm, LANES), F32))
        operands = (yb, gates, h1, g2, b2)
    else:
        row = lambda width: pl.BlockSpec((tm, width), lambda i: (i, 0))
        in_specs.append(full(w_next))
        out_specs = [row(D_MODEL), row(D_RNN), row(D_RNN)]
        out_shape = [jax.ShapeDtypeStruct((t_rows, D_MODEL), F32),
                     jax.ShapeDtypeStruct((t_rows, D_RNN), BF16),
                     jax.ShapeDtypeStruct((t_rows, D_RNN), F32)]
        operands = (yb, gates, h1, g2, b2, w_next)
    return pl.pallas_call(
        functools.partial(_combine_kernel, n_tiles=n_tiles, to_batch_major=final),
        grid=(n_tiles,),
        in_specs=in_specs, out_specs=out_specs, out_shape=out_shape,
        scratch_shapes=scratch,
        compiler_params=_cparams(1),
        name="moe_combine_final" if final else "moe_combine_inproj",
    )(*([pos3] * N_BUF), *operands)


def _moe_block(h1, h1r, route, counts, w_gate, w_up, w_down, ln_g, ln_b, w_next=None):
    t_rows = h1.shape[0]
    n_assign = (t_rows - PAD_ROWS) * TOP_K_FINE
    nblk = -(-(n_assign + N_EXPERTS * (EXPERT_TILE - 1)) // EXPERT_TILE)
    n_slots = nblk * EXPERT_TILE
    slot_row, block_expert, n_used, n_fetch, pos, gates = _slots(route, counts, n_slots)
    yb = _experts(h1r, slot_row, block_expert, n_used, n_fetch, w_gate, w_up, w_down)
    return _combine(yb, pos, gates, h1, ln_g, ln_b, w_next)


CONV_HALO = (CONV_WIDTH - 1) * NB


def _rglru_kernel(rec_ref, gg_ref, cw_ref, cb_ref, wax_ref, ba_ref, bx_ref, ca_ref,
                  y_ref, rp_ref, a_ref, b_ref, st_ref, *, steps):
    i = pl.program_id(0)
    rows = steps * NB
    sigmoid = lambda x: 1.0 / (1.0 + jnp.exp2(x * (-LOG2_E)))

    @pl.when(i == 0)
    def _():
        st_ref[...] = jnp.zeros_like(st_ref)
        rp_ref[0:CONV_HALO, :] = jnp.zeros((CONV_HALO, D_RNN), F32)

    rp_ref[CONV_HALO:CONV_HALO + rows, :] = rec_ref[...]
    has_pad = i * rows < PAD_ROWS
    real = (i * rows + lax.broadcasted_iota(jnp.int32, (rows, 1), 0)) >= PAD_ROWS

    @pl.when(has_pad)
    def _():
        rp_ref[CONV_HALO:CONV_HALO + rows, :] = jnp.where(real, rp_ref[CONV_HALO:CONV_HALO + rows, :], 0.0)

    xc = cb_ref[...] + sum(rp_ref[NB * j:NB * j + rows, :] * cw_ref[j:j + 1, :] for j in range(CONV_WIDTH))
    tail = rp_ref[rows:rows + CONV_HALO, :]
    rp_ref[0:CONV_HALO, :] = tail

    for n in range(LRU_BLOCKS):
        cs = slice(LRU_BLOCK_W * n, LRU_BLOCK_W * (n + 1))
        xb = xc[:, cs]
        ra = jnp.dot(xb.astype(BF16), wax_ref[n], preferred_element_type=F32)
        r = sigmoid(ra[:, :LRU_BLOCK_W] + ba_ref[:, cs])
        ig = sigmoid(ra[:, LRU_BLOCK_W:] + bx_ref[:, cs])
        a = jnp.exp2(r * ca_ref[:, cs])
        a_ref[:, cs] = a
        b_ref[:, cs] = jnp.sqrt(1.0 - a * a) * (ig * xb)

    @pl.when(has_pad)
    def _():
        b_ref[...] = jnp.where(real, b_ref[...], 0.0)

    def step(t, h):
        r0 = pl.multiple_of(t * NB, NB)
        hn = a_ref[pl.ds(r0, NB), :] * h + b_ref[pl.ds(r0, NB), :]
        b_ref[pl.ds(r0, NB), :] = hn
        return hn

    st_ref[...] = lax.fori_loop(0, steps, step, st_ref[...])
    y_ref[...] = (gg_ref[...].astype(F32) * b_ref[...]).astype(BF16)


def _rglru(rec, gg, conv_w, conv_b, w_a, b_a, w_x, b_x, lru_lambda):
    t_rows = rec.shape[0]
    steps = SCAN_STEPS
    rows = steps * NB
    cw = conv_w.astype(F32)
    cb = conv_b.astype(F32).reshape(1, D_RNN)
    wax = jnp.concatenate([w_a.astype(F32), w_x.astype(F32)], axis=2).astype(BF16)
    ba = b_a.astype(F32).reshape(1, D_RNN)
    bx = b_x.astype(F32).reshape(1, D_RNN)
    ca = (-LRU_C * LOG2_E) * jax.nn.softplus(-lru_lambda.astype(F32)).reshape(1, D_RNN)
    full = lambda a: pl.BlockSpec(a.shape, lambda i: (0,) * a.ndim)
    row = pl.BlockSpec((rows, D_RNN), lambda i: (i, 0))
    return pl.pallas_call(
        functools.partial(_rglru_kernel, steps=steps),
        grid=(t_rows // rows,),
        in_specs=[row, row, full(cw), full(cb), full(wax), full(ba), full(bx), full(ca)],
        out_specs=row,
        out_shape=jax.ShapeDtypeStruct((t_rows, D_RNN), BF16),
        scratch_shapes=[pltpu.VMEM((rows + CONV_HALO, D_RNN), F32), pltpu.VMEM((rows, D_RNN), F32),
                        pltpu.VMEM((rows, D_RNN), F32), pltpu.VMEM((NB, D_RNN), F32)],
        compiler_params=_cparams(1),
        name="l1_rglru",
    )(rec, gg, cw, cb, wax, ba, bx, ca)


def kernel(x, meta, l0_ln1_g, l0_ln1_b, l0_w_in, l0_s5_lambda_re, l0_s5_lambda_im, l0_s5_log_dt, l0_s5_b_re, l0_s5_b_im, l0_s5_c_re, l0_s5_c_im, l0_s5_d, l0_s5_w_glu, l0_s5_b_glu, l0_da_lq1, l0_da_lk1, l0_da_lq2, l0_da_lk2, l0_da_subln_g, l0_w_out, l0_ln2_g, l0_ln2_b, l0_moe_w_coarse, l0_moe_b_coarse, l0_moe_w_fine, l0_moe_b_fine, l0_moe_w_gate, l0_moe_w_up, l0_moe_w_down, l1_ln1_g, l1_ln1_b, l1_w_in, l1_conv_w, l1_conv_b, l1_lru_w_a, l1_lru_b_a, l1_lru_w_x, l1_lru_b_x, l1_lru_lambda, l1_w_out, l1_ln2_g, l1_ln2_b, l1_moe_w_coarse, l1_moe_b_coarse, l1_moe_w_fine, l1_moe_b_fine, l1_moe_w_gate, l1_moe_w_up, l1_moe_w_down):
    bsz, seq, _ = x.shape
    assert bsz == NB and seq % Q_TILE == 0
    dt = x.dtype
    lp = FRONT_PAD + N_META + seq

    col_scale = jnp.concatenate([jnp.ones((S5_WIDTH,), F32),
                                 jnp.full((DA_WIDTH,), DA_HEAD_DIM ** -0.5, F32),
                                 jnp.ones((2 * DA_WIDTH,), F32)])
    w_in0 = (l0_w_in.astype(F32) * col_scale[None, :]).astype(BF16)
    h, u, q, k, v = _inproj0(x, meta, w_in0, lp)
    s5p = _s5_params(l0_s5_lambda_re, l0_s5_lambda_im, l0_s5_log_dt,
                     l0_s5_b_re, l0_s5_b_im, l0_s5_c_re, l0_s5_c_im)
    y_s5 = _s5(u, *s5p, l0_s5_d, l0_s5_w_glu, l0_s5_b_glu)
    lam_init = 0.8 - 0.6 * math.exp(-0.3 * 0)
    lam = (jnp.exp(jnp.sum(l0_da_lq1.astype(F32) * l0_da_lk1.astype(F32)))
           - jnp.exp(jnp.sum(l0_da_lq2.astype(F32) * l0_da_lk2.astype(F32))) + lam_init)
    y_da = _diffattn(q, k, v, lam, l0_da_subln_g, lam_init)
    router0 = _router_weights(l0_moe_w_coarse, l0_moe_b_coarse, l0_moe_w_fine, l0_moe_b_fine)
    h, hr, route, counts = _outproj(y_s5, y_da, h, l0_w_out, l0_ln1_g, l0_ln1_b, router0)
    h, gg, rec = _moe_block(h, hr, route, counts, l0_moe_w_gate, l0_moe_w_up, l0_moe_w_down,
                            l0_ln2_g, l0_ln2_b, l1_w_in.astype(BF16))

    y = _rglru(rec, gg, l1_conv_w, l1_conv_b, l1_lru_w_a, l1_lru_b_a, l1_lru_w_x, l1_lru_b_x, l1_lru_lambda)
    router1 = _router_weights(l1_moe_w_coarse, l1_moe_b_coarse, l1_moe_w_fine, l1_moe_b_fine)
    h, hr, route, counts = _outproj(y, None, h, l1_w_out, l1_ln1_g, l1_ln1_b, router1)
    out = _moe_block(h, hr, route, counts, l1_moe_w_gate, l1_moe_w_up, l1_moe_w_down, l1_ln2_g, l1_ln2_b)
    return out.astype(dt)
```

```python
import functools
import math

import jax
import jax.numpy as jnp
from jax import lax
from jax.experimental import pallas as pl
from jax.experimental.pallas import tpu as pltpu

F32 = jnp.float32
BF16 = jnp.bfloat16

D_MODEL = 1024
DEPTH = 2
CHUNK = 64
N_META = 16
S5_WIDTH = 512
S5_GROUP = 16
S5_GROUPS = 32
S5_STATE = 64
DA_HEADS = 4
DA_HEAD_DIM = 64
DA_WIDTH = 512
D_RNN = 1280
LRU_BLOCKS = 10
LRU_BLOCK_W = 128
CONV_WIDTH = 4
LRU_C = 8.0
N_GROUPS = 4
EXPERTS_PER_GROUP = 8
N_EXPERTS = 32
TOP_K_FINE = 2
D_EXPERT = 512
ALPHA = (2 * DEPTH) ** 0.25
LN_EPS = 1e-5
NEG_INF = -1e30

NB = 8
LANES = 128
FRONT_PAD = LANES - N_META
PAD_ROWS = FRONT_PAD * NB
Q_TILE = 256
K_TILE = 512
TOK_TILE = 512
SCAN_STEPS = 128
EXPERT_TILE = 512
COMBINE_TILE = 256
VMEM_LIMIT = 48 * 1024 * 1024
LOG2_E = math.log2(math.e)


def _cparams(n_axes, vmem=VMEM_LIMIT):
    return pltpu.CompilerParams(dimension_semantics=("arbitrary",) * n_axes,
                                vmem_limit_bytes=vmem)


def _gelu(x):
    return 0.5 * x * (1.0 + jnp.tanh(math.sqrt(2.0 / math.pi) * (x + 0.044715 * (x * x * x))))


def _sigmoid(x):
    return 1.0 / (1.0 + jnp.exp(-x))


def _layer_norm_rows(r, g, b):
    mu = jnp.mean(r, axis=-1, keepdims=True)
    c = r - mu
    var = jnp.mean(c * c, axis=-1, keepdims=True)
    return c * lax.rsqrt(var + LN_EPS) * g + b


def _split_bf16(w):
    hi = w.astype(BF16)
    lo = (w - hi.astype(F32)).astype(BF16)
    return hi, lo


def _inproj0_kernel(head_ref, x_ref, w_ref, h_ref, u_ref, q_ref, k_ref, v_ref, zs_ref, *, tl, head_tiles):
    i = pl.program_id(0)

    @pl.when(i < head_tiles)
    def _():
        h_ref[...] = head_ref[...]

    @pl.when(i >= head_tiles)
    def _():
        for s in range(D_MODEL // LANES):
            for b in range(NB):
                zs_ref[s, pl.ds(b, tl, stride=NB), :] = x_ref[b, :, LANES * s:LANES * (s + 1)]
        for s in range(D_MODEL // LANES):
            h_ref[:, LANES * s:LANES * (s + 1)] = zs_ref[s]

    x = h_ref[...].astype(BF16)
    z = jnp.dot(x, w_ref[...], preferred_element_type=F32)
    u_ref[...] = z[:, :S5_WIDTH]
    n_slab = 3 * DA_WIDTH // LANES
    for s in range(n_slab):
        zs_ref[s] = z[:, S5_WIDTH + LANES * s:S5_WIDTH + LANES * (s + 1)]
    dsts = (q_ref, k_ref, v_ref)
    per = DA_WIDTH // LANES
    for b in range(NB):
        for s in range(n_slab):
            c = (s % per) * LANES
            blk = zs_ref[s, pl.ds(b, tl, stride=NB), :]
            if s < per:
                blk = blk * LOG2_E
            dsts[s // per][b, :, c:c + LANES] = blk.astype(BF16)


def _inproj0(x, meta, w_bf16, lp):
    t_rows = lp * NB
    tm = TOK_TILE
    tl = tm // NB
    head_tiles = LANES // tl
    head = jnp.concatenate([
        jnp.zeros((FRONT_PAD, NB, D_MODEL), x.dtype),
        jnp.broadcast_to(meta.astype(x.dtype)[:, None, :], (N_META, NB, D_MODEL))], axis=0)
    head = head.reshape(LANES * NB, D_MODEL)
    qkv_shape = jax.ShapeDtypeStruct((NB, lp, DA_WIDTH), BF16)
    qkv_spec = pl.BlockSpec((NB, tl, DA_WIDTH), lambda i: (0, i, 0))
    n_slab = max(3 * DA_WIDTH, D_MODEL) // LANES
    return pl.pallas_call(
        functools.partial(_inproj0_kernel, tl=tl, head_tiles=head_tiles),
        grid=(t_rows // tm,),
        in_specs=[pl.BlockSpec((tm, D_MODEL), lambda i: (jnp.minimum(i, head_tiles - 1), 0)),
                  pl.BlockSpec((NB, tl, D_MODEL), lambda i: (0, jnp.maximum(i - head_tiles, 0), 0)),
                  pl.BlockSpec(w_bf16.shape, lambda i: (0, 0))],
        out_specs=[pl.BlockSpec((tm, D_MODEL), lambda i: (i, 0)),
                   pl.BlockSpec((tm, S5_WIDTH), lambda i: (i, 0)), qkv_spec, qkv_spec, qkv_spec],
        out_shape=[jax.ShapeDtypeStruct((t_rows, D_MODEL), F32),
                   jax.ShapeDtypeStruct((t_rows, S5_WIDTH), F32), qkv_shape, qkv_shape, qkv_shape],
        scratch_shapes=[pltpu.VMEM((n_slab, tm, LANES), F32)],
        compiler_params=_cparams(1),
        name="l0_inproj",
    )(head, x, w_bf16)


S5_SLABS = S5_WIDTH // LANES
S5_SLAB_STATE = (S5_GROUPS // S5_SLABS) * S5_STATE
S5_NSTATE = S5_GROUPS * S5_STATE


def _s5_kernel(u_ref, bmat_ref, are_ref, aim_ref, cre_ref, cim_ref, d_ref, wglu_ref, bglu_ref,
               y_ref, hre_ref, him_ref, st_ref, *, steps):
    i = pl.program_id(0)
    rows = steps * NB

    @pl.when(i == 0)
    def _():
        st_ref[...] = jnp.zeros_like(st_ref)

    row = i * rows + lax.broadcasted_iota(jnp.int32, (rows, 1), 0)
    u = jnp.where(row >= PAD_ROWS, u_ref[...], 0.0)
    ub = u.astype(BF16)
    for s in range(S5_SLABS):
        bu = jnp.dot(ub[:, LANES * s:LANES * (s + 1)], bmat_ref[s], preferred_element_type=F32)
        hre_ref[:, S5_SLAB_STATE * s:S5_SLAB_STATE * (s + 1)] = bu[:, :S5_SLAB_STATE]
        him_ref[:, S5_SLAB_STATE * s:S5_SLAB_STATE * (s + 1)] = bu[:, S5_SLAB_STATE:]

    cw = 512
    for cg in range(S5_NSTATE // cw):
        c0 = cg * cw
        a_r = are_ref[:, c0:c0 + cw]
        a_i = aim_ref[:, c0:c0 + cw]

        def step(t, carry, c0=c0, a_r=a_r, a_i=a_i):
            sr, si = carry
            r0 = pl.multiple_of(t * NB, NB)
            br = hre_ref[pl.ds(r0, NB), c0:c0 + cw]
            bi = him_ref[pl.ds(r0, NB), c0:c0 + cw]
            nr = a_r * sr - a_i * si + br
            ni = a_r * si + a_i * sr + bi
            hre_ref[pl.ds(r0, NB), c0:c0 + cw] = nr
            him_ref[pl.ds(r0, NB), c0:c0 + cw] = ni
            return nr, ni

        sr, si = lax.fori_loop(0, steps, step, (st_ref[0, :, c0:c0 + cw], st_ref[1, :, c0:c0 + cw]))
        st_ref[0, :, c0:c0 + cw] = sr
        st_ref[1, :, c0:c0 + cw] = si

    ys = []
    for s in range(S5_SLABS):
        hr = hre_ref[:, S5_SLAB_STATE * s:S5_SLAB_STATE * (s + 1)].astype(BF16)
        hi = him_ref[:, S5_SLAB_STATE * s:S5_SLAB_STATE * (s + 1)].astype(BF16)
        ys.append(jnp.dot(hr, cre_ref[s], preferred_element_type=F32)
                  + jnp.dot(hi, cim_ref[s], preferred_element_type=F32))
    y = jnp.concatenate(ys, axis=1) + d_ref[...] * u
    y = _gelu(y)
    gate = _sigmoid(jnp.dot(y.astype(BF16), wglu_ref[...], preferred_element_type=F32) + bglu_ref[...])
    y_ref[...] = (y * gate).astype(BF16)


def _s5_params(lam_re, lam_im, log_dt, b_re, b_im, c_re, c_im):
    dt = jnp.exp(log_dt.astype(F32))[:, None]
    lr = jnp.minimum(lam_re.astype(F32), -1e-4)
    li = lam_im.astype(F32)
    mag = jnp.exp(lr * dt)
    ar = mag * jnp.cos(li * dt)
    ai = mag * jnp.sin(li * dt)
    den = lr * lr + li * li
    nr, ni = ar - 1.0, ai
    fr = ((nr * lr + ni * li) / den)[..., None]
    fi = ((ni * lr - nr * li) / den)[..., None]
    br, bi = b_re.astype(F32), b_im.astype(F32)
    bbr = fr * br - fi * bi
    bbi = fr * bi + fi * br
    gps = S5_GROUPS // S5_SLABS
    eye = jnp.eye(gps, dtype=F32)

    def in_slab(m):
        m4 = m.reshape(S5_SLABS, gps, S5_STATE, S5_GROUP)
        return jnp.einsum('sgph,gk->sghkp', m4, eye).reshape(S5_SLABS, LANES, S5_SLAB_STATE)

    def out_slab(m):
        m4 = m.reshape(S5_SLABS, gps, S5_GROUP, S5_STATE)
        return jnp.einsum('sghp,gk->sgpkh', m4, eye).reshape(S5_SLABS, S5_SLAB_STATE, LANES)

    bmat = jnp.concatenate([in_slab(bbr), in_slab(bbi)], axis=2).astype(BF16)
    cre = out_slab(c_re.astype(F32)).astype(BF16)
    cim = out_slab(-c_im.astype(F32)).astype(BF16)
    a_re = jnp.broadcast_to(ar.reshape(1, S5_NSTATE), (NB, S5_NSTATE))
    a_im = jnp.broadcast_to(ai.reshape(1, S5_NSTATE), (NB, S5_NSTATE))
    return bmat, a_re, a_im, cre, cim


def _s5(u, bmat, a_re, a_im, cre, cim, d, w_glu, b_glu):
    t_rows = u.shape[0]
    steps = SCAN_STEPS
    rows = steps * NB
    full = lambda a: pl.BlockSpec(a.shape, lambda i: (0,) * a.ndim)
    d2 = d.astype(F32).reshape(1, S5_WIDTH)
    bg2 = b_glu.astype(F32).reshape(1, S5_WIDTH)
    wg = w_glu.astype(BF16)
    return pl.pallas_call(
        functools.partial(_s5_kernel, steps=steps),
        grid=(t_rows // rows,),
        in_specs=[pl.BlockSpec((rows, S5_WIDTH), lambda i: (i, 0)),
                  full(bmat), full(a_re), full(a_im), full(cre), full(cim), full(d2), full(wg), full(bg2)],
        out_specs=pl.BlockSpec((rows, S5_WIDTH), lambda i: (i, 0)),
        out_shape=jax.ShapeDtypeStruct((t_rows, S5_WIDTH), BF16),
        scratch_shapes=[pltpu.VMEM((rows, S5_NSTATE), F32), pltpu.VMEM((rows, S5_NSTATE), F32),
                        pltpu.VMEM((2, NB, S5_NSTATE), F32)],
        compiler_params=_cparams(1),
        name="l0_s5",
    )(u, bmat, a_re, a_im, cre, cim, d2, wg, bg2)


def _diffattn_kernel(lam_ref, q_ref, k_ref, v_ref, g_ref, o_ref,
                     s_ref, qm_ref, m_ref, a_ref, *, lam_init, lp):
    lam = lam_ref[0]
    tq, tk = Q_TILE, K_TILE
    lane = lax.broadcasted_iota(jnp.int32, (tq, LANES), 1)
    qrow = jnp.bitwise_and(lax.broadcasted_iota(jnp.int32, (2 * tq, tk), 0), tq - 1)
    kloc = lax.broadcasted_iota(jnp.int32, (2 * tq, tk), 1)
    nt = (((1,), (1,)), ((), ()))
    bf16_rows = 16

    def key_start(j):
        return pl.multiple_of(jnp.minimum(FRONT_PAD + j * tk, lp - tk), bf16_rows)

    def chunk_of(pos):
        return jnp.right_shift(pos - CHUNK, 6)

    def lane_fold(x, op):
        r = x[:, :LANES]
        for c in range(1, x.shape[1] // LANES):
            r = op(r, x[:, LANES * c:LANES * (c + 1)])
        return r

    def q_start(i):
        return pl.multiple_of(jnp.minimum(i * tq, lp - tq), LANES)

    def n_key_tiles(i):
        return (q_start(i) + tq - FRONT_PAD + tk - 1) // tk

    def n_full_tiles(i):
        return jnp.maximum(q_start(i) - FRONT_PAD, 0) // tk


    def prep(i, par):
        q = q_ref[0, pl.ds(q_start(i), tq), :]
        zero = jnp.zeros_like(q)
        qm_ref[par, :tq] = jnp.where(lane < DA_HEAD_DIM, q, zero)
        qm_ref[par, tq:] = jnp.where(lane >= DA_HEAD_DIM, q, zero)
        m_ref[par] = jnp.full(m_ref.shape[1:], NEG_INF, F32)

    def scores(i, par, j, masked):
        k0 = key_start(j)
        kt = k_ref[0, pl.ds(k0, tk), :]
        if masked:
            kpos = k0 + kloc
            mask = (chunk_of(kpos) <= chunk_of(q_start(i) + qrow)) & (kpos >= FRONT_PAD + j * tk)
        s = lax.dot_general(qm_ref[par], kt, nt, preferred_element_type=F32)
        if masked:
            s = jnp.where(mask, s, NEG_INF)
        s_ref[par, j] = s
        m_ref[par] = jnp.maximum(m_ref[par], lane_fold(s, jnp.maximum))

    ones_col = jnp.where(lax.broadcasted_iota(jnp.int32, (tk, LANES), 1) == 0, 1.0, 0.0).astype(BF16)

    def values(par, j, m):
        vt = jnp.concatenate([v_ref[0, pl.ds(key_start(j), tk), :], ones_col], axis=1)
        p = jnp.exp2((s_ref[par, j] - m).astype(BF16))
        a_ref[...] += jnp.dot(p, vt, preferred_element_type=F32)

    def row_max(par):
        a_ref[...] = jnp.zeros(a_ref.shape, F32)
        return jnp.max(m_ref[par], axis=-1, keepdims=True)

    def finish(i):
        l1 = a_ref[:tq, LANES:LANES + 1]
        l2 = a_ref[tq:, LANES:LANES + 1]
        o = a_ref[:tq, :LANES] / l1 - lam * (a_ref[tq:, :LANES] / l2)
        o = o * lax.rsqrt(jnp.mean(o * o, axis=-1, keepdims=True) + LN_EPS) * g_ref[...]
        o_ref[0, pl.ds(q_start(i), tq), :] = o * (1.0 - lam_init)

    def loop(lo, hi, body):
        def wrapped(j, _):
            body(j)
            return 0
        lax.fori_loop(lo, hi, wrapped, 0)

    def step(i, par):
        m = row_max(par)
        prep(i + 1, 1 - par)

        def both(j, masked):
            scores(i + 1, 1 - par, j, masked)
            values(par, j, m)

        loop(0, n_full_tiles(i + 1), functools.partial(both, masked=False))
        loop(n_full_tiles(i + 1), n_key_tiles(i), functools.partial(both, masked=True))
        loop(n_key_tiles(i), n_key_tiles(i + 1), lambda j: scores(i + 1, 1 - par, j, True))
        finish(i)

    def step_pair(i2):
        step(2 * i2, 0)
        step(2 * i2 + 1, 1)

    n_q = pl.cdiv(lp, tq)
    prep(0, 0)
    loop(0, n_key_tiles(0), lambda j: scores(0, 0, j, True))
    loop(0, (n_q - 1) // 2, step_pair)
    if (n_q - 1) % 2:
        step(n_q - 2, 0)
    last = (n_q - 1) % 2
    m = row_max(last)
    loop(0, n_key_tiles(n_q - 1), lambda j: values(last, j, m))
    finish(n_q - 1)


def _diffattn(q, k, v, lam, subln_g, lam_init):
    nb, lp, _ = q.shape
    g2 = subln_g.astype(F32).reshape(1, 2 * DA_HEAD_DIM)
    seq_spec = pl.BlockSpec((1, lp, LANES), lambda b, h: (b, 0, h))
    return pl.pallas_call(
        functools.partial(_diffattn_kernel, lam_init=lam_init, lp=lp),
        grid=(nb, DA_HEADS),
        in_specs=[pl.BlockSpec(memory_space=pltpu.SMEM), seq_spec, seq_spec, seq_spec,
                  pl.BlockSpec((1, LANES), lambda b, h: (0, 0))],
        out_specs=seq_spec,
        out_shape=jax.ShapeDtypeStruct((nb, lp, DA_WIDTH), F32),
        scratch_shapes=[pltpu.VMEM((2, pl.cdiv(lp, K_TILE), 2 * Q_TILE, K_TILE), F32),
                        pltpu.VMEM((2, 2 * Q_TILE, LANES), BF16),
                        pltpu.VMEM((2, 2 * Q_TILE, LANES), F32),
                        pltpu.VMEM((2 * Q_TILE, 2 * LANES), F32)],
        compiler_params=_cparams(2),
        name="l0_diffattn",
    )(lam.reshape(1), q, k, v, g2)


ROUTE_ROWS = 64
ROUTE_E, ROUTE_GATE, ROUTE_RANK = 0, 2, 4


def _router_logits_t(h1, whi_ref, wlo_ref, rb_ref):
    hi = h1.astype(BF16)
    lo = (h1 - hi.astype(F32)).astype(BF16)
    nt = (((1,), (1,)), ((), ()))
    return (lax.dot_general(whi_ref[...], hi, nt, preferred_element_type=F32)
            + lax.dot_general(whi_ref[...], lo, nt, preferred_element_type=F32)
            + lax.dot_general(wlo_ref[...], hi, nt, preferred_element_type=F32) + rb_ref[...])


def _route_cols(lg, cnt_ref, tri_ref, first_tok):
    toks = lg.shape[1]
    row = lax.broadcasted_iota(jnp.int32, lg.shape, 0)
    rowf = row.astype(F32)
    valid = (first_tok + lax.broadcasted_iota(jnp.int32, (1, toks), 1)) >= PAD_ROWS
    ninf = float('-inf')
    first = lambda hit: jnp.min(jnp.where(hit, rowf, float(ROUTE_ROWS)), axis=0, keepdims=True)
    cm = jnp.where(row < N_GROUPS, lg, ninf)
    cmax = jnp.max(cm, axis=0, keepdims=True)
    p_grp = 1.0 / jnp.sum(jnp.exp(cm - cmax), axis=0, keepdims=True)
    lo = N_GROUPS + EXPERTS_PER_GROUP * first(cm == cmax)
    fm = jnp.where((rowf >= lo) & (rowf < lo + EXPERTS_PER_GROUP), lg, ninf)
    v1 = jnp.max(fm, axis=0, keepdims=True)
    i1 = first(fm == v1)
    fm2 = jnp.where(rowf == i1, ninf, fm)
    v2 = jnp.max(fm2, axis=0, keepdims=True)
    i2 = first(fm2 == v2)
    t = jnp.exp(v2 - v1)
    g1 = p_grp / (1.0 + t)
    g2 = p_grp * t / (1.0 + t)
    oh1 = (rowf == i1) & valid
    oh2 = (rowf == i2) & valid
    m = jnp.where(oh1 | oh2, 1.0, 0.0)
    before = jnp.dot(m.astype(BF16), tri_ref[...], preferred_element_type=F32) + cnt_ref[:, 0:1]
    r1 = jnp.sum(jnp.where(oh1, before, 0.0), axis=0, keepdims=True)
    r2 = jnp.sum(jnp.where(oh2, before, 0.0), axis=0, keepdims=True)
    cnt_ref[...] = cnt_ref[...] + jnp.sum(m, axis=1, keepdims=True)
    zero = jnp.zeros_like(g1)
    rec_row = lax.broadcasted_iota(jnp.int32, (NB, toks), 0)
    rec = jnp.zeros((NB, toks), F32)
    for k, val in enumerate((i1 - N_GROUPS, i2 - N_GROUPS, jnp.where(valid, g1, zero),
                             jnp.where(valid, g2, zero), r1, r2)):
        rec = jnp.where(rec_row == k, val, rec)
    return rec


def _route_tile(h1, whi_ref, wlo_ref, rb_ref, rt_ref, cnt_ref, tri_ref):
    i = pl.program_id(0)
    toks = h1.shape[0]

    @pl.when(i == 0)
    def _():
        cnt_ref[...] = jnp.zeros_like(cnt_ref)
        ri = lax.broadcasted_iota(jnp.int32, (toks, toks), 0)
        ci = lax.broadcasted_iota(jnp.int32, (toks, toks), 1)
        tri_ref[...] = jnp.where(ri < ci, 1.0, 0.0).astype(BF16)

    lg = _router_logits_t(h1, whi_ref, wlo_ref, rb_ref)
    rt_ref[...] = _route_cols(lg, cnt_ref, tri_ref, i * toks)


def _outproj0_kernel(ys_ref, yda_ref, h_ref, w_ref, g_ref, b_ref, whi_ref, wlo_ref, rb_ref,
                     h1_ref, h1r_ref, rt_ref, cnt_ref, das_ref, tri_ref, *, tl):
    per = DA_WIDTH // LANES
    for b in range(NB):
        for s in range(per):
            das_ref[s, pl.ds(b, tl, stride=NB), :] = yda_ref[b, :, LANES * s:LANES * (s + 1)]
    da = jnp.concatenate([das_ref[s] for s in range(per)], axis=1).astype(BF16)
    mix = (jnp.dot(ys_ref[...], w_ref[:S5_WIDTH, :], preferred_element_type=F32)
           + jnp.dot(da, w_ref[S5_WIDTH:, :], preferred_element_type=F32))
    h1 = _layer_norm_rows(ALPHA * h_ref[...] + mix, g_ref[...], b_ref[...])
    h1_ref[...] = h1
    _to_row_tiles(h1r_ref, h1)
    _route_tile(h1, whi_ref, wlo_ref, rb_ref, rt_ref, cnt_ref, tri_ref)


def _outproj1_kernel(y_ref, h_ref, w_ref, g_ref, b_ref, whi_ref, wlo_ref, rb_ref,
                     h1_ref, h1r_ref, rt_ref, cnt_ref, tri_ref):
    mix = jnp.dot(y_ref[...], w_ref[...], preferred_element_type=F32)
    h1 = _layer_norm_rows(ALPHA * h_ref[...] + mix, g_ref[...], b_ref[...])
    h1_ref[...] = h1
    _to_row_tiles(h1r_ref, h1)
    _route_tile(h1, whi_ref, wlo_ref, rb_ref, rt_ref, cnt_ref, tri_ref)


def _router_weights(w_coarse, b_coarse, w_fine, b_fine):
    wf = jnp.transpose(w_fine.astype(F32), (1, 0, 2)).reshape(D_MODEL, N_EXPERTS)
    w = jnp.concatenate([w_coarse.astype(F32), wf], axis=1).T
    w = jnp.pad(w, ((0, ROUTE_ROWS - w.shape[0]), (0, 0)))
    b = jnp.concatenate([b_coarse.astype(F32), b_fine.astype(F32).reshape(-1)])
    b = jnp.pad(b, (0, ROUTE_ROWS - b.shape[0])).reshape(ROUTE_ROWS, 1)
    whi, wlo = _split_bf16(w)
    return whi, wlo, b


def _outproj(ys, yda, h, w_out, ln_g, ln_b, router):
    t_rows = h.shape[0]
    tm = TOK_TILE
    tl = tm // NB
    whi, wlo, rb = router
    w = w_out.astype(BF16)
    g2 = ln_g.astype(F32).reshape(1, D_MODEL)
    b2 = ln_b.astype(F32).reshape(1, D_MODEL)
    full = lambda a: pl.BlockSpec(a.shape, lambda i: (0,) * a.ndim)
    row = lambda width: pl.BlockSpec((tm, width), lambda i: (i, 0))
    common_in = [row(D_MODEL), full(w), full(g2), full(b2), full(whi), full(wlo), full(rb)]
    out_specs = [row(D_MODEL), pl.BlockSpec((tm * ROW_TILES, LANES), lambda i: (i, 0)),
                 pl.BlockSpec((NB, tm), lambda i: (0, i)),
                 pl.BlockSpec((ROUTE_ROWS, LANES), lambda i: (0, 0))]
    out_shape = [jax.ShapeDtypeStruct((t_rows, D_MODEL), F32),
                 jax.ShapeDtypeStruct((t_rows * ROW_TILES, LANES), F32),
                 jax.ShapeDtypeStruct((NB, t_rows), F32),
                 jax.ShapeDtypeStruct((ROUTE_ROWS, LANES), F32)]
    tri = pltpu.VMEM((tm, tm), BF16)
    if yda is not None:
        return pl.pallas_call(
            functools.partial(_outproj0_kernel, tl=tl),
            grid=(t_rows // tm,),
            in_specs=[row(S5_WIDTH), pl.BlockSpec((NB, tl, DA_WIDTH), lambda i: (0, i, 0))] + common_in,
            out_specs=out_specs, out_shape=out_shape,
            scratch_shapes=[pltpu.VMEM((DA_WIDTH // LANES, tm, LANES), F32), tri],
            compiler_params=_cparams(1),
            name="l0_outproj",
        )(ys, yda, h, w, g2, b2, whi, wlo, rb)
    return pl.pallas_call(
        _outproj1_kernel,
        grid=(t_rows // tm,),
        in_specs=[row(ys.shape[1])] + common_in,
        out_specs=out_specs, out_shape=out_shape,
        scratch_shapes=[tri],
        compiler_params=_cparams(1),
        name="l1_outproj",
    )(ys, h, w, g2, b2, whi, wlo, rb)


def _slots(route, counts, n_slots):
    t_rows = route.shape[1]
    experts = route[ROUTE_E:ROUTE_E + TOP_K_FINE].T.astype(jnp.int32)
    gates = route[ROUTE_GATE:ROUTE_GATE + TOP_K_FINE].T
    rank = route[ROUTE_RANK:ROUTE_RANK + TOP_K_FINE].T.astype(jnp.int32)
    cnt = counts[N_GROUPS:N_GROUPS + N_EXPERTS, 0].astype(jnp.int32)
    padded = (cnt + EXPERT_TILE - 1) // EXPERT_TILE * EXPERT_TILE
    pad_end = jnp.cumsum(padded)
    pad_start = pad_end - padded
    raw_start = jnp.cumsum(cnt) - cnt
    valid = (jnp.arange(t_rows) >= PAD_ROWS)[:, None]
    e_ids = jnp.arange(N_EXPERTS, dtype=jnp.int32)
    start_of = jnp.sum(jnp.where(experts[..., None] == e_ids, pad_start, 0), axis=-1)
    dest = jnp.where(valid, start_of + rank, n_slots).astype(jnp.int32)
    nblk = n_slots // EXPERT_TILE
    blk_start = jnp.arange(nblk, dtype=jnp.int32) * EXPERT_TILE
    block_expert = jnp.minimum(jnp.sum((pad_end[None, :] <= blk_start[:, None]).astype(jnp.int32), axis=1),
                               N_EXPERTS - 1)
    hit = block_expert[:, None] == e_ids[None, :]
    blk_pad_start = jnp.sum(jnp.where(hit, pad_start, 0), axis=1)
    blk_raw_start = jnp.sum(jnp.where(hit, raw_start, 0), axis=1)
    blk_cnt = jnp.sum(jnp.where(hit, cnt, 0), axis=1)
    n_used = (pad_end[-1] // EXPERT_TILE).astype(jnp.int32)
    blk_rows = jnp.where(jnp.arange(nblk) < n_used,
                         jnp.clip(blk_cnt - (blk_start - blk_pad_start), 0, EXPERT_TILE), 0)
    n_fetch = (blk_rows + GATHER_CHUNK - 1) // GATHER_CHUNK * GATHER_CHUNK
    order = jnp.argsort(dest.reshape(-1)).astype(jnp.int32)
    off = (blk_start - blk_pad_start)[:, None] + jnp.arange(EXPERT_TILE, dtype=jnp.int32)[None, :]
    src = jnp.clip(blk_raw_start[:, None] + off, 0, order.shape[0] - 1)
    picked = jnp.take(order, src.reshape(-1), axis=0).reshape(src.shape)
    slot_tok = jnp.where(off < blk_cnt[:, None], jnp.right_shift(picked, 1), 0).reshape(-1)
    pos = jnp.where(valid, dest, 0).reshape(-1)
    return (slot_tok * ROW_TILES, block_expert, n_used.reshape(1), n_fetch.astype(jnp.int32),
            pos * ROW_TILES, gates)


ROW_TILES = D_MODEL // LANES


def _to_row_tiles(dst_ref, val):
    rows = val.shape[0]
    for s in range(ROW_TILES):
        dst_ref[pl.ds(s, rows, stride=ROW_TILES), :] = val[:, LANES * s:LANES * (s + 1)]


def _from_row_tiles(src_ref, rows):
    return jnp.concatenate([src_ref[pl.ds(s, rows, stride=ROW_TILES), :] for s in range(ROW_TILES)], axis=1)


LOOKAHEAD = 2
N_BUF = LOOKAHEAD + 1


GATHER_CHUNK = 1


def _expert_kernel(be_ref, nused_ref, nfetch_ref, *refs):
    tok_refs = refs[:N_BUF]
    x_hbm, wg_ref, wu_ref, wd_ref, y_ref = refs[N_BUF:N_BUF + 5]
    bufs = refs[N_BUF + 5:2 * N_BUF + 5]
    sem_ref, wgb_ref, wub_ref, wdb_ref = refs[2 * N_BUF + 5:2 * N_BUF + 9]
    i = pl.program_id(0)
    n_used = nused_ref[0]
    n_tiles = pl.num_programs(0)
    tb = EXPERT_TILE
    turn = lax.rem(i, N_BUF)

    def gather(tok_ref, p, tile):
        nfetch = jnp.where(tile < n_tiles, nfetch_ref[jnp.minimum(tile, n_tiles - 1)], 0)
        for c in range(tb // GATHER_CHUNK):
            @pl.when(c * GATHER_CHUNK < nfetch)
            def _(c=c):
                for r in range(c * GATHER_CHUNK, (c + 1) * GATHER_CHUNK):
                    row = pl.multiple_of(tok_ref[0, 0, r], ROW_TILES)
                    pltpu.make_async_copy(x_hbm.at[pl.ds(row, ROW_TILES), :],
                                          bufs[p].at[pl.ds(ROW_TILES * r, ROW_TILES), :],
                                          sem_ref.at[p]).start(priority=r % 2)

    def wait(p, tile):
        rows = pl.multiple_of(nfetch_ref[tile] * ROW_TILES, ROW_TILES)
        pltpu.make_async_copy(x_hbm.at[pl.ds(0, rows), :], bufs[p].at[pl.ds(0, rows), :],
                              sem_ref.at[p]).wait()

    @pl.when(i == 0)
    def _():
        for p in range(N_BUF):
            bufs[p][...] = jnp.zeros_like(bufs[p])
        for d in range(LOOKAHEAD):
            gather(tok_refs[d], d, d)

    @pl.when((i < n_used) & ((i == 0) | (be_ref[i] != be_ref[jnp.maximum(i - 1, 0)])))
    def _():
        wgb_ref[...] = wg_ref[0].astype(BF16)
        wub_ref[...] = wu_ref[0].astype(BF16)
        wdb_ref[...] = wd_ref[0].astype(BF16)

    for p in range(N_BUF):
        @pl.when((turn == p) & (i < n_used))
        def _(p=p):
            wait(p, i)
            gather(tok_refs[LOOKAHEAD], (p + LOOKAHEAD) % N_BUF, i + LOOKAHEAD)
            x = _from_row_tiles(bufs[p], tb).astype(BF16)
            g = jnp.dot(x, wgb_ref[...], preferred_element_type=F32)
            u = jnp.dot(x, wub_ref[...], preferred_element_type=F32)
            hid = (g * _sigmoid(g) * u).astype(BF16)
            _to_row_tiles(y_ref, jnp.dot(hid, wdb_ref[...], preferred_element_type=F32))

    @pl.when(i >= n_used)
    def _():
        y_ref[...] = jnp.zeros_like(y_ref)


def _experts(h1r, slot_row, block_expert, n_used, n_fetch, w_gate, w_up, w_down):
    n_slots = slot_row.shape[0]
    nblk = n_slots // EXPERT_TILE
    tok3 = slot_row.reshape(nblk, 1, EXPERT_TILE)
    ahead = lambda d: pl.BlockSpec((1, 1, EXPERT_TILE), lambda i, be, nu, nf: (jnp.minimum(i + d, nblk - 1), 0, 0),
                                   memory_space=pltpu.SMEM)
    tile_buf = pltpu.VMEM((EXPERT_TILE * ROW_TILES, LANES), F32)
    grid_spec = pltpu.PrefetchScalarGridSpec(
        num_scalar_prefetch=3,
        grid=(nblk,),
        in_specs=[ahead(d) for d in range(N_BUF)] + [
            pl.BlockSpec(memory_space=pl.ANY),
            pl.BlockSpec((1, D_MODEL, D_EXPERT), lambda i, be, nu, nf: (be[i], 0, 0)),
            pl.BlockSpec((1, D_MODEL, D_EXPERT), lambda i, be, nu, nf: (be[i], 0, 0)),
            pl.BlockSpec((1, D_EXPERT, D_MODEL), lambda i, be, nu, nf: (be[i], 0, 0))],
        out_specs=pl.BlockSpec((EXPERT_TILE * ROW_TILES, LANES), lambda i, be, nu, nf: (i, 0)),
        scratch_shapes=[tile_buf] * N_BUF + [
            pltpu.SemaphoreType.DMA((N_BUF,)),
            pltpu.VMEM((D_MODEL, D_EXPERT), BF16), pltpu.VMEM((D_MODEL, D_EXPERT), BF16),
            pltpu.VMEM((D_EXPERT, D_MODEL), BF16)],
    )
    return pl.pallas_call(
        _expert_kernel,
        grid_spec=grid_spec,
        out_shape=jax.ShapeDtypeStruct((n_slots * ROW_TILES, LANES), F32),
        compiler_params=_cparams(1),
        name="moe_experts",
    )(block_expert, n_used, n_fetch, *([tok3] * N_BUF), h1r, w_gate, w_up, w_down)


def _combine_kernel(*refs, n_tiles, to_batch_major):
    pos_refs = refs[:N_BUF]
    yb_hbm, gate_ref, h_ref, g_ref, b_ref = refs[N_BUF:N_BUF + 5]
    if to_batch_major:
        o_ref, ybuf_ref, sem_ref, hs_ref = refs[N_BUF + 5:]
    else:
        w_ref, o_ref, gg_ref, rec_ref, ybuf_ref, sem_ref = refs[N_BUF + 5:]
    i = pl.program_id(0)
    tm = COMBINE_TILE
    turn = lax.rem(i, N_BUF)

    def gather(pos_ref, p):
        for r in range(tm):
            for kk in range(TOP_K_FINE):
                row = pl.multiple_of(pos_ref[0, 0, TOP_K_FINE * r + kk], ROW_TILES)
                pltpu.make_async_copy(yb_hbm.at[pl.ds(row, ROW_TILES), :],
                                      ybuf_ref.at[p, kk, pl.ds(ROW_TILES * r, ROW_TILES), :],
                                      sem_ref.at[p]).start(priority=kk)

    def finish(p, look_ahead):
        for kk in range(TOP_K_FINE):
            pltpu.make_async_copy(yb_hbm.at[pl.ds(0, tm * ROW_TILES), :], ybuf_ref.at[p, kk],
                                  sem_ref.at[p]).wait()
        if look_ahead:
            gather(pos_refs[LOOKAHEAD], (p + LOOKAHEAD) % N_BUF)
        gts = gate_ref[...]
        moe = (_from_row_tiles(ybuf_ref.at[p, 0], tm) * gts[:, 0:1]
               + _from_row_tiles(ybuf_ref.at[p, 1], tm) * gts[:, 1:2])
        h2 = _layer_norm_rows(ALPHA * h_ref[...] + moe, g_ref[...], b_ref[...])
        if not to_batch_major:
            o_ref[...] = h2
            z = jnp.dot(h2.astype(BF16), w_ref[...], preferred_element_type=F32)
            gg_ref[...] = _gelu(z[:, :D_RNN]).astype(BF16)
            rec_ref[...] = z[:, D_RNN:]
        else:
            tl = tm // NB
            for s in range(D_MODEL // LANES):
                hs_ref[s] = h2[:, LANES * s:LANES * (s + 1)]
            for b in range(NB):
                for s in range(D_MODEL // LANES):
                    o_ref[b, :, LANES * s:LANES * (s + 1)] = hs_ref[s, pl.ds(b, tl, stride=NB), :]

    @pl.when(i == 0)
    def _():
        for d in range(LOOKAHEAD):
            gather(pos_refs[d], d)

    for p in range(N_BUF):
        @pl.when((turn == p) & (i < n_tiles - LOOKAHEAD))
        def _(p=p):
            finish(p, True)

    @pl.when(i >= n_tiles - LOOKAHEAD)
    def _():
        finish(turn, False)


def _combine(yb, pos, gates, h1, ln_g, ln_b, w_next=None):
    final = w_next is None
    t_rows = h1.shape[0]
    tm = COMBINE_TILE
    first_tile = (LANES * NB) // tm if final else 0
    n_tiles = t_rows // tm - first_tile
    pos3 = pos.reshape(t_rows // tm, 1, TOP_K_FINE * tm)
    g2 = ln_g.astype(F32).reshape(1, D_MODEL)
    b2 = ln_b.astype(F32).reshape(1, D_MODEL)
    full = lambda a: pl.BlockSpec(a.shape, lambda i: (0,) * a.ndim)
    last = t_rows // tm - 1
    ahead = lambda d: pl.BlockSpec((1, 1, TOP_K_FINE * tm),
                                   lambda i: (jnp.minimum(i + first_tile + d, last), 0, 0),
                                   memory_space=pltpu.SMEM)
    in_specs = [ahead(d) for d in range(N_BUF)] + [
                pl.BlockSpec(memory_space=pl.ANY),
                pl.BlockSpec((tm, TOP_K_FINE), lambda i: (i + first_tile, 0)),
                pl.BlockSpec((tm, D_MODEL), lambda i: (i + first_tile, 0)),
                full(g2), full(b2)]
    scratch = [pltpu.VMEM((N_BUF, TOP_K_FINE, tm * ROW_TILES, LANES), F32), pltpu.SemaphoreType.DMA((N_BUF,))]
    if final:
        tl = tm // NB
        seq = t_rows // NB - LANES
        out_specs = pl.BlockSpec((NB, tl, D_MODEL), lambda i: (0, i, 0))
        out_shape = jax.ShapeDtypeStruct((NB, seq, D_MODEL), F32)
        scratch.append(pltpu.VMEM((D_MODEL // LANES, tm, LANES), F32))
        operands = (yb, gates, h1, g2, b2)
    else:
        row = lambda width: pl.BlockSpec((tm, width), lambda i: (i, 0))
        in_specs.append(full(w_next))
        out_specs = [row(D_MODEL), row(D_RNN), row(D_RNN)]
        out_shape = [jax.ShapeDtypeStruct((t_rows, D_MODEL), F32),
                     jax.ShapeDtypeStruct((t_rows, D_RNN), BF16),
                     jax.ShapeDtypeStruct((t_rows, D_RNN), F32)]
        operands = (yb, gates, h1, g2, b2, w_next)
    return pl.pallas_call(
        functools.partial(_combine_kernel, n_tiles=n_tiles, to_batch_major=final),
        grid=(n_tiles,),
        in_specs=in_specs, out_specs=out_specs, out_shape=out_shape,
        scratch_shapes=scratch,
        compiler_params=_cparams(1),
        name="moe_combine_final" if final else "moe_combine_inproj",
    )(*([pos3] * N_BUF), *operands)


def _moe_block(h1, h1r, route, counts, w_gate, w_up, w_down, ln_g, ln_b, w_next=None):
    t_rows = h1.shape[0]
    n_assign = (t_rows - PAD_ROWS) * TOP_K_FINE
    nblk = -(-(n_assign + N_EXPERTS * (EXPERT_TILE - 1)) // EXPERT_TILE)
    n_slots = nblk * EXPERT_TILE
    slot_row, block_expert, n_used, n_fetch, pos, gates = _slots(route, counts, n_slots)
    yb = _experts(h1r, slot_row, block_expert, n_used, n_fetch, w_gate, w_up, w_down)
    return _combine(yb, pos, gates, h1, ln_g, ln_b, w_next)


CONV_HALO = (CONV_WIDTH - 1) * NB


def _rglru_kernel(rec_ref, gg_ref, cw_ref, cb_ref, wax_ref, ba_ref, bx_ref, ca_ref,
                  y_ref, rp_ref, a_ref, b_ref, st_ref, *, steps):
    i = pl.program_id(0)
    rows = steps * NB
    sigmoid = lambda x: 1.0 / (1.0 + jnp.exp2(x * (-LOG2_E)))

    @pl.when(i == 0)
    def _():
        st_ref[...] = jnp.zeros_like(st_ref)
        rp_ref[0:CONV_HALO, :] = jnp.zeros((CONV_HALO, D_RNN), F32)

    rp_ref[CONV_HALO:CONV_HALO + rows, :] = rec_ref[...]
    has_pad = i * rows < PAD_ROWS
    real = (i * rows + lax.broadcasted_iota(jnp.int32, (rows, 1), 0)) >= PAD_ROWS

    @pl.when(has_pad)
    def _():
        rp_ref[CONV_HALO:CONV_HALO + rows, :] = jnp.where(real, rp_ref[CONV_HALO:CONV_HALO + rows, :], 0.0)

    xc = cb_ref[...] + sum(rp_ref[NB * j:NB * j + rows, :] * cw_ref[j:j + 1, :] for j in range(CONV_WIDTH))
    tail = rp_ref[rows:rows + CONV_HALO, :]
    rp_ref[0:CONV_HALO, :] = tail

    for n in range(LRU_BLOCKS):
        cs = slice(LRU_BLOCK_W * n, LRU_BLOCK_W * (n + 1))
        xb = xc[:, cs]
        ra = jnp.dot(xb.astype(BF16), wax_ref[n], preferred_element_type=F32)
        r = sigmoid(ra[:, :LRU_BLOCK_W] + ba_ref[:, cs])
        ig = sigmoid(ra[:, LRU_BLOCK_W:] + bx_ref[:, cs])
        a = jnp.exp2(r * ca_ref[:, cs])
        a_ref[:, cs] = a
        b_ref[:, cs] = jnp.sqrt(1.0 - a * a) * (ig * xb)

    @pl.when(has_pad)
    def _():
        b_ref[...] = jnp.where(real, b_ref[...], 0.0)

    def step(t, h):
        r0 = pl.multiple_of(t * NB, NB)
        hn = a_ref[pl.ds(r0, NB), :] * h + b_ref[pl.ds(r0, NB), :]
        b_ref[pl.ds(r0, NB), :] = hn
        return hn

    st_ref[...] = lax.fori_loop(0, steps, step, st_ref[...])
    y_ref[...] = (gg_ref[...].astype(F32) * b_ref[...]).astype(BF16)


def _rglru(rec, gg, conv_w, conv_b, w_a, b_a, w_x, b_x, lru_lambda):
    t_rows = rec.shape[0]
    steps = SCAN_STEPS
    rows = steps * NB
    cw = conv_w.astype(F32)
    cb = conv_b.astype(F32).reshape(1, D_RNN)
    wax = jnp.concatenate([w_a.astype(F32), w_x.astype(F32)], axis=2).astype(BF16)
    ba = b_a.astype(F32).reshape(1, D_RNN)
    bx = b_x.astype(F32).reshape(1, D_RNN)
    ca = (-LRU_C * LOG2_E) * jax.nn.softplus(-lru_lambda.astype(F32)).reshape(1, D_RNN)
    full = lambda a: pl.BlockSpec(a.shape, lambda i: (0,) * a.ndim)
    row = pl.BlockSpec((rows, D_RNN), lambda i: (i, 0))
    return pl.pallas_call(
        functools.partial(_rglru_kernel, steps=steps),
        grid=(t_rows // rows,),
        in_specs=[row, row, full(cw), full(cb), full(wax), full(ba), full(bx), full(ca)],
        out_specs=row,
        out_shape=jax.ShapeDtypeStruct((t_rows, D_RNN), BF16),
        scratch_shapes=[pltpu.VMEM((rows + CONV_HALO, D_RNN), F32), pltpu.VMEM((rows, D_RNN), F32),
                        pltpu.VMEM((rows, D_RNN), F32), pltpu.VMEM((NB, D_RNN), F32)],
        compiler_params=_cparams(1),
        name="l1_rglru",
    )(rec, gg, cw, cb, wax, ba, bx, ca)


def kernel(x, meta, l0_ln1_g, l0_ln1_b, l0_w_in, l0_s5_lambda_re, l0_s5_lambda_im, l0_s5_log_dt, l0_s5_b_re, l0_s5_b_im, l0_s5_c_re, l0_s5_c_im, l0_s5_d, l0_s5_w_glu, l0_s5_b_glu, l0_da_lq1, l0_da_lk1, l0_da_lq2, l0_da_lk2, l0_da_subln_g, l0_w_out, l0_ln2_g, l0_ln2_b, l0_moe_w_coarse, l0_moe_b_coarse, l0_moe_w_fine, l0_moe_b_fine, l0_moe_w_gate, l0_moe_w_up, l0_moe_w_down, l1_ln1_g, l1_ln1_b, l1_w_in, l1_conv_w, l1_conv_b, l1_lru_w_a, l1_lru_b_a, l1_lru_w_x, l1_lru_b_x, l1_lru_lambda, l1_w_out, l1_ln2_g, l1_ln2_b, l1_moe_w_coarse, l1_moe_b_coarse, l1_moe_w_fine, l1_moe_b_fine, l1_moe_w_gate, l1_moe_w_up, l1_moe_w_down):
    bsz, seq, _ = x.shape
    assert bsz == NB and seq % Q_TILE == 0
    dt = x.dtype
    lp = FRONT_PAD + N_META + seq

    col_scale = jnp.concatenate([jnp.ones((S5_WIDTH,), F32),
                                 jnp.full((DA_WIDTH,), DA_HEAD_DIM ** -0.5, F32),
                                 jnp.ones((2 * DA_WIDTH,), F32)])
    w_in0 = (l0_w_in.astype(F32) * col_scale[None, :]).astype(BF16)
    h, u, q, k, v = _inproj0(x, meta, w_in0, lp)
    s5p = _s5_params(l0_s5_lambda_re, l0_s5_lambda_im, l0_s5_log_dt,
                     l0_s5_b_re, l0_s5_b_im, l0_s5_c_re, l0_s5_c_im)
    y_s5 = _s5(u, *s5p, l0_s5_d, l0_s5_w_glu, l0_s5_b_glu)
    lam_init = 0.8 - 0.6 * math.exp(-0.3 * 0)
    lam = (jnp.exp(jnp.sum(l0_da_lq1.astype(F32) * l0_da_lk1.astype(F32)))
           - jnp.exp(jnp.sum(l0_da_lq2.astype(F32) * l0_da_lk2.astype(F32))) + lam_init)
    y_da = _diffattn(q, k, v, lam, l0_da_subln_g, lam_init)
    router0 = _router_weights(l0_moe_w_coarse, l0_moe_b_coarse, l0_moe_w_fine, l0_moe_b_fine)
    h, hr, route, counts = _outproj(y_s5, y_da, h, l0_w_out, l0_ln1_g, l0_ln1_b, router0)
    h, gg, rec = _moe_block(h, hr, route, counts, l0_moe_w_gate, l0_moe_w_up, l0_moe_w_down,
                            l0_ln2_g, l0_ln2_b, l1_w_in.astype(BF16))

    y = _rglru(rec, gg, l1_conv_w, l1_conv_b, l1_lru_w_a, l1_lru_b_a, l1_lru_w_x, l1_lru_b_x, l1_lru_lambda)
    router1 = _router_weights(l1_moe_w_coarse, l1_moe_b_coarse, l1_moe_w_fine, l1_moe_b_fine)
    h, hr, route, counts = _outproj(y, None, h, l1_w_out, l1_ln1_g, l1_ln1_b, router1)
    out = _moe_block(h, hr, route, counts, l1_moe_w_gate, l1_moe_w_up, l1_moe_w_down, l1_ln2_g, l1_ln2_b)
    return out.astype(dt)
```

```python
import functools
import math

import jax
import jax.numpy as jnp
from jax import lax
from jax.experimental import pallas as pl
from jax.experimental.pallas import tpu as pltpu

F32 = jnp.float32
BF16 = jnp.bfloat16

D_MODEL = 1024
DEPTH = 2
CHUNK = 64
N_META = 16
S5_WIDTH = 512
S5_GROUP = 16
S5_GROUPS = 32
S5_STATE = 64
DA_HEADS = 4
DA_HEAD_DIM = 64
DA_WIDTH = 512
D_RNN = 1280
LRU_BLOCKS = 10
LRU_BLOCK_W = 128
CONV_WIDTH = 4
LRU_C = 8.0
N_GROUPS = 4
EXPERTS_PER_GROUP = 8
N_EXPERTS = 32
TOP_K_FINE = 2
D_EXPERT = 512
ALPHA = (2 * DEPTH) ** 0.25
LN_EPS = 1e-5
NEG_INF = -1e30

NB = 8
LANES = 128
FRONT_PAD = LANES - N_META
PAD_ROWS = FRONT_PAD * NB
Q_TILE = 256
K_TILE = 512
TOK_TILE = 512
SCAN_STEPS = 128
EXPERT_TILE = 512
COMBINE_TILE = 256
VMEM_LIMIT = 48 * 1024 * 1024
LOG2_E = math.log2(math.e)


def _cparams(n_axes, vmem=VMEM_LIMIT):
    return pltpu.CompilerParams(dimension_semantics=("arbitrary",) * n_axes,
                                vmem_limit_bytes=vmem)


def _gelu(x):
    return 0.5 * x * (1.0 + jnp.tanh(math.sqrt(2.0 / math.pi) * (x + 0.044715 * (x * x * x))))


def _sigmoid(x):
    return 1.0 / (1.0 + jnp.exp(-x))


def _layer_norm_rows(r, g, b):
    mu = jnp.mean(r, axis=-1, keepdims=True)
    c = r - mu
    var = jnp.mean(c * c, axis=-1, keepdims=True)
    return c * lax.rsqrt(var + LN_EPS) * g + b


def _split_bf16(w):
    hi = w.astype(BF16)
    lo = (w - hi.astype(F32)).astype(BF16)
    return hi, lo


def _inproj0_kernel(head_ref, x_ref, w_ref, h_ref, u_ref, q_ref, k_ref, v_ref, zs_ref, *, tl, head_tiles):
    i = pl.program_id(0)

    @pl.when(i < head_tiles)
    def _():
        h_ref[...] = head_ref[...]

    @pl.when(i >= head_tiles)
    def _():
        for s in range(D_MODEL // LANES):
            for b in range(NB):
                zs_ref[s, pl.ds(b, tl, stride=NB), :] = x_ref[b, :, LANES * s:LANES * (s + 1)]
        for s in range(D_MODEL // LANES):
            h_ref[:, LANES * s:LANES * (s + 1)] = zs_ref[s]

    x = h_ref[...].astype(BF16)
    z = jnp.dot(x, w_ref[...], preferred_element_type=F32)
    u_ref[...] = z[:, :S5_WIDTH]
    n_slab = 3 * DA_WIDTH // LANES
    for s in range(n_slab):
        zs_ref[s] = z[:, S5_WIDTH + LANES * s:S5_WIDTH + LANES * (s + 1)]
    dsts = (q_ref, k_ref, v_ref)
    per = DA_WIDTH // LANES
    for b in range(NB):
        for s in range(n_slab):
            c = (s % per) * LANES
            blk = zs_ref[s, pl.ds(b, tl, stride=NB), :]
            if s < per:
                blk = blk * LOG2_E
            dsts[s // per][b, :, c:c + LANES] = blk.astype(BF16)


def _inproj0(x, meta, w_bf16, lp):
    t_rows = lp * NB
    tm = TOK_TILE
    tl = tm // NB
    head_tiles = LANES // tl
    head = jnp.concatenate([
        jnp.zeros((FRONT_PAD, NB, D_MODEL), x.dtype),
        jnp.broadcast_to(meta.astype(x.dtype)[:, None, :], (N_META, NB, D_MODEL))], axis=0)
    head = head.reshape(LANES * NB, D_MODEL)
    qkv_shape = jax.ShapeDtypeStruct((NB, lp, DA_WIDTH), BF16)
    qkv_spec = pl.BlockSpec((NB, tl, DA_WIDTH), lambda i: (0, i, 0))
    n_slab = max(3 * DA_WIDTH, D_MODEL) // LANES
    return pl.pallas_call(
        functools.partial(_inproj0_kernel, tl=tl, head_tiles=head_tiles),
        grid=(t_rows // tm,),
        in_specs=[pl.BlockSpec((tm, D_MODEL), lambda i: (jnp.minimum(i, head_tiles - 1), 0)),
                  pl.BlockSpec((NB, tl, D_MODEL), lambda i: (0, jnp.maximum(i - head_tiles, 0), 0)),
                  pl.BlockSpec(w_bf16.shape, lambda i: (0, 0))],
        out_specs=[pl.BlockSpec((tm, D_MODEL), lambda i: (i, 0)),
                   pl.BlockSpec((tm, S5_WIDTH), lambda i: (i, 0)), qkv_spec, qkv_spec, qkv_spec],
        out_shape=[jax.ShapeDtypeStruct((t_rows, D_MODEL), F32),
                   jax.ShapeDtypeStruct((t_rows, S5_WIDTH), F32), qkv_shape, qkv_shape, qkv_shape],
        scratch_shapes=[pltpu.VMEM((n_slab, tm, LANES), F32)],
        compiler_params=_cparams(1),
        name="l0_inproj",
    )(head, x, w_bf16)


S5_SLABS = S5_WIDTH // LANES
S5_SLAB_STATE = (S5_GROUPS // S5_SLABS) * S5_STATE
S5_NSTATE = S5_GROUPS * S5_STATE


def _s5_kernel(u_ref, bmat_ref, are_ref, aim_ref, cre_ref, cim_ref, d_ref, wglu_ref, bglu_ref,
               y_ref, hre_ref, him_ref, st_ref, *, steps):
    i = pl.program_id(0)
    rows = steps * NB

    @pl.when(i == 0)
    def _():
        st_ref[...] = jnp.zeros_like(st_ref)

    row = i * rows + lax.broadcasted_iota(jnp.int32, (rows, 1), 0)
    u = jnp.where(row >= PAD_ROWS, u_ref[...], 0.0)
    ub = u.astype(BF16)
    for s in range(S5_SLABS):
        bu = jnp.dot(ub[:, LANES * s:LANES * (s + 1)], bmat_ref[s], preferred_element_type=F32)
        hre_ref[:, S5_SLAB_STATE * s:S5_SLAB_STATE * (s + 1)] = bu[:, :S5_SLAB_STATE]
        him_ref[:, S5_SLAB_STATE * s:S5_SLAB_STATE * (s + 1)] = bu[:, S5_SLAB_STATE:]

    cw = 512
    for cg in range(S5_NSTATE // cw):
        c0 = cg * cw
        a_r = are_ref[:, c0:c0 + cw]
        a_i = aim_ref[:, c0:c0 + cw]

        def step(t, carry, c0=c0, a_r=a_r, a_i=a_i):
            sr, si = carry
            r0 = pl.multiple_of(t * NB, NB)
            br = hre_ref[pl.ds(r0, NB), c0:c0 + cw]
            bi = him_ref[pl.ds(r0, NB), c0:c0 + cw]
            nr = a_r * sr - a_i * si + br
            ni = a_r * si + a_i * sr + bi
            hre_ref[pl.ds(r0, NB), c0:c0 + cw] = nr
            him_ref[pl.ds(r0, NB), c0:c0 + cw] = ni
            return nr, ni

        sr, si = lax.fori_loop(0, steps, step, (st_ref[0, :, c0:c0 + cw], st_ref[1, :, c0:c0 + cw]))
        st_ref[0, :, c0:c0 + cw] = sr
        st_ref[1, :, c0:c0 + cw] = si

    ys = []
    for s in range(S5_SLABS):
        hr = hre_ref[:, S5_SLAB_STATE * s:S5_SLAB_STATE * (s + 1)].astype(BF16)
        hi = him_ref[:, S5_SLAB_STATE * s:S5_SLAB_STATE * (s + 1)].astype(BF16)
        ys.append(jnp.dot(hr, cre_ref[s], preferred_element_type=F32)
                  + jnp.dot(hi, cim_ref[s], preferred_element_type=F32))
    y = jnp.concatenate(ys, axis=1) + d_ref[...] * u
    y = _gelu(y)
    gate = _sigmoid(jnp.dot(y.astype(BF16), wglu_ref[...], preferred_element_type=F32) + bglu_ref[...])
    y_ref[...] = (y * gate).astype(BF16)


def _s5_params(lam_re, lam_im, log_dt, b_re, b_im, c_re, c_im):
    dt = jnp.exp(log_dt.astype(F32))[:, None]
    lr = jnp.minimum(lam_re.astype(F32), -1e-4)
    li = lam_im.astype(F32)
    mag = jnp.exp(lr * dt)
    ar = mag * jnp.cos(li * dt)
    ai = mag * jnp.sin(li * dt)
    den = lr * lr + li * li
    nr, ni = ar - 1.0, ai
    fr = ((nr * lr + ni * li) / den)[..., None]
    fi = ((ni * lr - nr * li) / den)[..., None]
    br, bi = b_re.astype(F32), b_im.astype(F32)
    bbr = fr * br - fi * bi
    bbi = fr * bi + fi * br
    gps = S5_GROUPS // S5_SLABS
    eye = jnp.eye(gps, dtype=F32)

    def in_slab(m):
        m4 = m.reshape(S5_SLABS, gps, S5_STATE, S5_GROUP)
        return jnp.einsum('sgph,gk->sghkp', m4, eye).reshape(S5_SLABS, LANES, S5_SLAB_STATE)

    def out_slab(m):
        m4 = m.reshape(S5_SLABS, gps, S5_GROUP, S5_STATE)
        return jnp.einsum('sghp,gk->sgpkh', m4, eye).reshape(S5_SLABS, S5_SLAB_STATE, LANES)

    bmat = jnp.concatenate([in_slab(bbr), in_slab(bbi)], axis=2).astype(BF16)
    cre = out_slab(c_re.astype(F32)).astype(BF16)
    cim = out_slab(-c_im.astype(F32)).astype(BF16)
    a_re = jnp.broadcast_to(ar.reshape(1, S5_NSTATE), (NB, S5_NSTATE))
    a_im = jnp.broadcast_to(ai.reshape(1, S5_NSTATE), (NB, S5_NSTATE))
    return bmat, a_re, a_im, cre, cim


def _s5(u, bmat, a_re, a_im, cre, cim, d, w_glu, b_glu):
    t_rows = u.shape[0]
    steps = SCAN_STEPS
    rows = steps * NB
    full = lambda a: pl.BlockSpec(a.shape, lambda i: (0,) * a.ndim)
    d2 = d.astype(F32).reshape(1, S5_WIDTH)
    bg2 = b_glu.astype(F32).reshape(1, S5_WIDTH)
    wg = w_glu.astype(BF16)
    return pl.pallas_call(
        functools.partial(_s5_kernel, steps=steps),
        grid=(t_rows // rows,),
        in_specs=[pl.BlockSpec((rows, S5_WIDTH), lambda i: (i, 0)),
                  full(bmat), full(a_re), full(a_im), full(cre), full(cim), full(d2), full(wg), full(bg2)],
        out_specs=pl.BlockSpec((rows, S5_WIDTH), lambda i: (i, 0)),
        out_shape=jax.ShapeDtypeStruct((t_rows, S5_WIDTH), BF16),
        scratch_shapes=[pltpu.VMEM((rows, S5_NSTATE), F32), pltpu.VMEM((rows, S5_NSTATE), F32),
                        pltpu.VMEM((2, NB, S5_NSTATE), F32)],
        compiler_params=_cparams(1),
        name="l0_s5",
    )(u, bmat, a_re, a_im, cre, cim, d2, wg, bg2)


def _diffattn_kernel(lam_ref, q_ref, k_ref, v_ref, g_ref, o_ref,
                     s_ref, qm_ref, m_ref, a_ref, *, lam_init, lp):
    lam = lam_ref[0]
    tq, tk = Q_TILE, K_TILE
    lane = lax.broadcasted_iota(jnp.int32, (tq, LANES), 1)
    qrow = jnp.bitwise_and(lax.broadcasted_iota(jnp.int32, (2 * tq, tk), 0), tq - 1)
    kloc = lax.broadcasted_iota(jnp.int32, (2 * tq, tk), 1)
    nt = (((1,), (1,)), ((), ()))
    bf16_rows = 16

    def key_start(j):
        return pl.multiple_of(jnp.minimum(FRONT_PAD + j * tk, lp - tk), bf16_rows)

    def chunk_of(pos):
        return jnp.right_shift(pos - CHUNK, 6)

    def lane_fold(x, op):
        r = x[:, :LANES]
        for c in range(1, x.shape[1] // LANES):
            r = op(r, x[:, LANES * c:LANES * (c + 1)])
        return r

    def q_start(i):
        return pl.multiple_of(jnp.minimum(i * tq, lp - tq), LANES)

    def n_key_tiles(i):
        return (q_start(i) + tq - FRONT_PAD + tk - 1) // tk

    def n_full_tiles(i):
        return jnp.maximum(q_start(i) - FRONT_PAD, 0) // tk


    def prep(i, par):
        q = q_ref[0, pl.ds(q_start(i), tq), :]
        zero = jnp.zeros_like(q)
        qm_ref[par, :tq] = jnp.where(lane < DA_HEAD_DIM, q, zero)
        qm_ref[par, tq:] = jnp.where(lane >= DA_HEAD_DIM, q, zero)
        m_ref[par] = jnp.full(m_ref.shape[1:], NEG_INF, F32)

    def scores(i, par, j, masked):
        k0 = key_start(j)
        kt = k_ref[0, pl.ds(k0, tk), :]
        if masked:
            kpos = k0 + kloc
            mask = (chunk_of(kpos) <= chunk_of(q_start(i) + qrow)) & (kpos >= FRONT_PAD + j * tk)
        s = lax.dot_general(qm_ref[par], kt, nt, preferred_element_type=F32)
        if masked:
            s = jnp.where(mask, s, NEG_INF)
        s_ref[par, j] = s
        m_ref[par] = jnp.maximum(m_ref[par], lane_fold(s, jnp.maximum))

    ones_col = jnp.where(lax.broadcasted_iota(jnp.int32, (tk, LANES), 1) == 0, 1.0, 0.0).astype(BF16)

    def values(par, j, m):
        vt = jnp.concatenate([v_ref[0, pl.ds(key_start(j), tk), :], ones_col], axis=1)
        p = jnp.exp2((s_ref[par, j] - m).astype(BF16))
        a_ref[...] += jnp.dot(p, vt, preferred_element_type=F32)

    def row_max(par):
        a_ref[...] = jnp.zeros(a_ref.shape, F32)
        return jnp.max(m_ref[par], axis=-1, keepdims=True)

    def finish(i):
        l1 = a_ref[:tq, LANES:LANES + 1]
        l2 = a_ref[tq:, LANES:LANES + 1]
        o = a_ref[:tq, :LANES] / l1 - lam * (a_ref[tq:, :LANES] / l2)
        o = o * lax.rsqrt(jnp.mean(o * o, axis=-1, keepdims=True) + LN_EPS) * g_ref[...]
        o_ref[0, pl.ds(q_start(i), tq), :] = o * (1.0 - lam_init)

    def loop(lo, hi, body):
        def wrapped(j, _):
            body(j)
            return 0
        lax.fori_loop(lo, hi, wrapped, 0)

    def step(i, par):
        m = row_max(par)
        prep(i + 1, 1 - par)

        def both(j, masked):
            scores(i + 1, 1 - par, j, masked)
            values(par, j, m)

        loop(0, n_full_tiles(i + 1), functools.partial(both, masked=False))
        loop(n_full_tiles(i + 1), n_key_tiles(i), functools.partial(both, masked=True))
        loop(n_key_tiles(i), n_key_tiles(i + 1), lambda j: scores(i + 1, 1 - par, j, True))
        finish(i)

    def step_pair(i2):
        step(2 * i2, 0)
        step(2 * i2 + 1, 1)

    n_q = pl.cdiv(lp, tq)
    prep(0, 0)
    loop(0, n_key_tiles(0), lambda j: scores(0, 0, j, True))
    loop(0, (n_q - 1) // 2, step_pair)
    if (n_q - 1) % 2:
        step(n_q - 2, 0)
    last = (n_q - 1) % 2
    m = row_max(last)
    loop(0, n_key_tiles(n_q - 1), lambda j: values(last, j, m))
    finish(n_q - 1)


def _diffattn(q, k, v, lam, subln_g, lam_init):
    nb, lp, _ = q.shape
    g2 = subln_g.astype(F32).reshape(1, 2 * DA_HEAD_DIM)
    seq_spec = pl.BlockSpec((1, lp, LANES), lambda b, h: (b, 0, h))
    return pl.pallas_call(
        functools.partial(_diffattn_kernel, lam_init=lam_init, lp=lp),
        grid=(nb, DA_HEADS),
        in_specs=[pl.BlockSpec(memory_space=pltpu.SMEM), seq_spec, seq_spec, seq_spec,
                  pl.BlockSpec((1, LANES), lambda b, h: (0, 0))],
        out_specs=seq_spec,
        out_shape=jax.ShapeDtypeStruct((nb, lp, DA_WIDTH), F32),
        scratch_shapes=[pltpu.VMEM((2, pl.cdiv(lp, K_TILE), 2 * Q_TILE, K_TILE), F32),
                        pltpu.VMEM((2, 2 * Q_TILE, LANES), BF16),
                        pltpu.VMEM((2, 2 * Q_TILE, LANES), F32),
                        pltpu.VMEM((2 * Q_TILE, 2 * LANES), F32)],
        compiler_params=_cparams(2),
        name="l0_diffattn",
    )(lam.reshape(1), q, k, v, g2)


ROUTE_ROWS = 64
ROUTE_E, ROUTE_GATE, ROUTE_RANK = 0, 2, 4


def _router_logits_t(h1, whi_ref, wlo_ref, rb_ref):
    hi = h1.astype(BF16)
    lo = (h1 - hi.astype(F32)).astype(BF16)
    nt = (((1,), (1,)), ((), ()))
    return (lax.dot_general(whi_ref[...], hi, nt, preferred_element_type=F32)
            + lax.dot_general(whi_ref[...], lo, nt, preferred_element_type=F32)
            + lax.dot_general(wlo_ref[...], hi, nt, preferred_element_type=F32) + rb_ref[...])


def _route_cols(lg, cnt_ref, tri_ref, first_tok):
    toks = lg.shape[1]
    row = lax.broadcasted_iota(jnp.int32, lg.shape, 0)
    rowf = row.astype(F32)
    valid = (first_tok + lax.broadcasted_iota(jnp.int32, (1, toks), 1)) >= PAD_ROWS
    ninf = float('-inf')
    first = lambda hit: jnp.min(jnp.where(hit, rowf, float(ROUTE_ROWS)), axis=0, keepdims=True)
    cm = jnp.where(row < N_GROUPS, lg, ninf)
    cmax = jnp.max(cm, axis=0, keepdims=True)
    p_grp = 1.0 / jnp.sum(jnp.exp(cm - cmax), axis=0, keepdims=True)
    lo = N_GROUPS + EXPERTS_PER_GROUP * first(cm == cmax)
    fm = jnp.where((rowf >= lo) & (rowf < lo + EXPERTS_PER_GROUP), lg, ninf)
    v1 = jnp.max(fm, axis=0, keepdims=True)
    i1 = first(fm == v1)
    fm2 = jnp.where(rowf == i1, ninf, fm)
    v2 = jnp.max(fm2, axis=0, keepdims=True)
    i2 = first(fm2 == v2)
    t = jnp.exp(v2 - v1)
    g1 = p_grp / (1.0 + t)
    g2 = p_grp * t / (1.0 + t)
    oh1 = (rowf == i1) & valid
    oh2 = (rowf == i2) & valid
    m = jnp.where(oh1 | oh2, 1.0, 0.0)
    before = jnp.dot(m.astype(BF16), tri_ref[...], preferred_element_type=F32) + cnt_ref[:, 0:1]
    r1 = jnp.sum(jnp.where(oh1, before, 0.0), axis=0, keepdims=True)
    r2 = jnp.sum(jnp.where(oh2, before, 0.0), axis=0, keepdims=True)
    cnt_ref[...] = cnt_ref[...] + jnp.sum(m, axis=1, keepdims=True)
    zero = jnp.zeros_like(g1)
    rec_row = lax.broadcasted_iota(jnp.int32, (NB, toks), 0)
    rec = jnp.zeros((NB, toks), F32)
    for k, val in enumerate((i1 - N_GROUPS, i2 - N_GROUPS, jnp.where(valid, g1, zero),
                             jnp.where(valid, g2, zero), r1, r2)):
        rec = jnp.where(rec_row == k, val, rec)
    return rec


def _route_tile(h1, whi_ref, wlo_ref, rb_ref, rt_ref, cnt_ref, tri_ref):
    i = pl.program_id(0)
    toks = h1.shape[0]

    @pl.when(i == 0)
    def _():
        cnt_ref[...] = jnp.zeros_like(cnt_ref)
        ri = lax.broadcasted_iota(jnp.int32, (toks, toks), 0)
        ci = lax.broadcasted_iota(jnp.int32, (toks, toks), 1)
        tri_ref[...] = jnp.where(ri < ci, 1.0, 0.0).astype(BF16)

    lg = _router_logits_t(h1, whi_ref, wlo_ref, rb_ref)
    rt_ref[...] = _route_cols(lg, cnt_ref, tri_ref, i * toks)


def _outproj0_kernel(ys_ref, yda_ref, h_ref, w_ref, g_ref, b_ref, whi_ref, wlo_ref, rb_ref,
                     h1_ref, h1r_ref, rt_ref, cnt_ref, das_ref, tri_ref, *, tl):
    per = DA_WIDTH // LANES
    for b in range(NB):
        for s in range(per):
            das_ref[s, pl.ds(b, tl, stride=NB), :] = yda_ref[b, :, LANES * s:LANES * (s + 1)]
    da = jnp.concatenate([das_ref[s] for s in range(per)], axis=1).astype(BF16)
    mix = (jnp.dot(ys_ref[...], w_ref[:S5_WIDTH, :], preferred_element_type=F32)
           + jnp.dot(da, w_ref[S5_WIDTH:, :], preferred_element_type=F32))
    h1 = _layer_norm_rows(ALPHA * h_ref[...] + mix, g_ref[...], b_ref[...])
    h1_ref[...] = h1
    _to_row_tiles(h1r_ref, h1)
    _route_tile(h1, whi_ref, wlo_ref, rb_ref, rt_ref, cnt_ref, tri_ref)


def _outproj1_kernel(y_ref, h_ref, w_ref, g_ref, b_ref, whi_ref, wlo_ref, rb_ref,
                     h1_ref, h1r_ref, rt_ref, cnt_ref, tri_ref):
    mix = jnp.dot(y_ref[...], w_ref[...], preferred_element_type=F32)
    h1 = _layer_norm_rows(ALPHA * h_ref[...] + mix, g_ref[...], b_ref[...])
    h1_ref[...] = h1
    _to_row_tiles(h1r_ref, h1)
    _route_tile(h1, whi_ref, wlo_ref, rb_ref, rt_ref, cnt_ref, tri_ref)


def _router_weights(w_coarse, b_coarse, w_fine, b_fine):
    wf = jnp.transpose(w_fine.astype(F32), (1, 0, 2)).reshape(D_MODEL, N_EXPERTS)
    w = jnp.concatenate([w_coarse.astype(F32), wf], axis=1).T
    w = jnp.pad(w, ((0, ROUTE_ROWS - w.shape[0]), (0, 0)))
    b = jnp.concatenate([b_coarse.astype(F32), b_fine.astype(F32).reshape(-1)])
    b = jnp.pad(b, (0, ROUTE_ROWS - b.shape[0])).reshape(ROUTE_ROWS, 1)
    whi, wlo = _split_bf16(w)
    return whi, wlo, b


def _outproj(ys, yda, h, w_out, ln_g, ln_b, router):
    t_rows = h.shape[0]
    tm = 2 * TOK_TILE
    tl = tm // NB
    whi, wlo, rb = router
    w = w_out.astype(BF16)
    g2 = ln_g.astype(F32).reshape(1, D_MODEL)
    b2 = ln_b.astype(F32).reshape(1, D_MODEL)
    full = lambda a: pl.BlockSpec(a.shape, lambda i: (0,) * a.ndim)
    row = lambda width: pl.BlockSpec((tm, width), lambda i: (i, 0))
    common_in = [row(D_MODEL), full(w), full(g2), full(b2), full(whi), full(wlo), full(rb)]
    out_specs = [row(D_MODEL), pl.BlockSpec((tm * ROW_TILES, LANES), lambda i: (i, 0)),
                 pl.BlockSpec((NB, tm), lambda i: (0, i)),
                 pl.BlockSpec((ROUTE_ROWS, LANES), lambda i: (0, 0))]
    out_shape = [jax.ShapeDtypeStruct((t_rows, D_MODEL), F32),
                 jax.ShapeDtypeStruct((t_rows * ROW_TILES, LANES), F32),
                 jax.ShapeDtypeStruct((NB, t_rows), F32),
                 jax.ShapeDtypeStruct((ROUTE_ROWS, LANES), F32)]
    tri = pltpu.VMEM((tm, tm), BF16)
    if yda is not None:
        return pl.pallas_call(
            functools.partial(_outproj0_kernel, tl=tl),
            grid=(t_rows // tm,),
            in_specs=[row(S5_WIDTH), pl.BlockSpec((NB, tl, DA_WIDTH), lambda i: (0, i, 0))] + common_in,
            out_specs=out_specs, out_shape=out_shape,
            scratch_shapes=[pltpu.VMEM((DA_WIDTH // LANES, tm, LANES), F32), tri],
            compiler_params=_cparams(1),
            name="l0_outproj",
        )(ys, yda, h, w, g2, b2, whi, wlo, rb)
    return pl.pallas_call(
        _outproj1_kernel,
        grid=(t_rows // tm,),
        in_specs=[row(ys.shape[1])] + common_in,
        out_specs=out_specs, out_shape=out_shape,
        scratch_shapes=[tri],
        compiler_params=_cparams(1),
        name="l1_outproj",
    )(ys, h, w, g2, b2, whi, wlo, rb)


def _slots(route, counts, n_slots):
    t_rows = route.shape[1]
    experts = route[ROUTE_E:ROUTE_E + TOP_K_FINE].T.astype(jnp.int32)
    gates = route[ROUTE_GATE:ROUTE_GATE + TOP_K_FINE].T
    rank = route[ROUTE_RANK:ROUTE_RANK + TOP_K_FINE].T.astype(jnp.int32)
    cnt = counts[N_GROUPS:N_GROUPS + N_EXPERTS, 0].astype(jnp.int32)
    padded = (cnt + EXPERT_TILE - 1) // EXPERT_TILE * EXPERT_TILE
    pad_end = jnp.cumsum(padded)
    pad_start = pad_end - padded
    raw_start = jnp.cumsum(cnt) - cnt
    valid = (jnp.arange(t_rows) >= PAD_ROWS)[:, None]
    e_ids = jnp.arange(N_EXPERTS, dtype=jnp.int32)
    start_of = jnp.sum(jnp.where(experts[..., None] == e_ids, pad_start, 0), axis=-1)
    dest = jnp.where(valid, start_of + rank, n_slots).astype(jnp.int32)
    nblk = n_slots // EXPERT_TILE
    blk_start = jnp.arange(nblk, dtype=jnp.int32) * EXPERT_TILE
    block_expert = jnp.minimum(jnp.sum((pad_end[None, :] <= blk_start[:, None]).astype(jnp.int32), axis=1),
                               N_EXPERTS - 1)
    hit = block_expert[:, None] == e_ids[None, :]
    blk_pad_start = jnp.sum(jnp.where(hit, pad_start, 0), axis=1)
    blk_raw_start = jnp.sum(jnp.where(hit, raw_start, 0), axis=1)
    blk_cnt = jnp.sum(jnp.where(hit, cnt, 0), axis=1)
    n_used = (pad_end[-1] // EXPERT_TILE).astype(jnp.int32)
    blk_rows = jnp.where(jnp.arange(nblk) < n_used,
                         jnp.clip(blk_cnt - (blk_start - blk_pad_start), 0, EXPERT_TILE), 0)
    n_fetch = (blk_rows + GATHER_CHUNK - 1) // GATHER_CHUNK * GATHER_CHUNK
    order = jnp.argsort(dest.reshape(-1)).astype(jnp.int32)
    off = (blk_start - blk_pad_start)[:, None] + jnp.arange(EXPERT_TILE, dtype=jnp.int32)[None, :]
    src = jnp.clip(blk_raw_start[:, None] + off, 0, order.shape[0] - 1)
    slot_tok = jnp.where(off < blk_cnt[:, None], jnp.right_shift(order[src], 1), 0).reshape(-1)
    pos = jnp.where(valid, dest, 0).reshape(-1)
    return (slot_tok * ROW_TILES, block_expert, n_used.reshape(1), n_fetch.astype(jnp.int32),
            pos * ROW_TILES, gates)


ROW_TILES = D_MODEL // LANES


def _to_row_tiles(dst_ref, val):
    rows = val.shape[0]
    for s in range(ROW_TILES):
        dst_ref[pl.ds(s, rows, stride=ROW_TILES), :] = val[:, LANES * s:LANES * (s + 1)]


def _from_row_tiles(src_ref, rows):
    return jnp.concatenate([src_ref[pl.ds(s, rows, stride=ROW_TILES), :] for s in range(ROW_TILES)], axis=1)


LOOKAHEAD = 2
N_BUF = LOOKAHEAD + 1


GATHER_CHUNK = 1


def _expert_kernel(be_ref, nused_ref, nfetch_ref, *refs):
    tok_refs = refs[:N_BUF]
    x_hbm, wg_ref, wu_ref, wd_ref, y_ref = refs[N_BUF:N_BUF + 5]
    bufs = refs[N_BUF + 5:2 * N_BUF + 5]
    sem_ref, wgb_ref, wub_ref, wdb_ref = refs[2 * N_BUF + 5:2 * N_BUF + 9]
    i = pl.program_id(0)
    n_used = nused_ref[0]
    n_tiles = pl.num_programs(0)
    tb = EXPERT_TILE
    turn = lax.rem(i, N_BUF)

    def gather(tok_ref, p, tile):
        nfetch = jnp.where(tile < n_tiles, nfetch_ref[jnp.minimum(tile, n_tiles - 1)], 0)
        for c in range(tb // GATHER_CHUNK):
            @pl.when(c * GATHER_CHUNK < nfetch)
            def _(c=c):
                for r in range(c * GATHER_CHUNK, (c + 1) * GATHER_CHUNK):
                    row = pl.multiple_of(tok_ref[0, 0, r], ROW_TILES)
                    pltpu.make_async_copy(x_hbm.at[pl.ds(row, ROW_TILES), :],
                                          bufs[p].at[pl.ds(ROW_TILES * r, ROW_TILES), :],
                                          sem_ref.at[p]).start(priority=r % 2)

    def wait(p, tile):
        rows = pl.multiple_of(nfetch_ref[tile] * ROW_TILES, ROW_TILES)
        pltpu.make_async_copy(x_hbm.at[pl.ds(0, rows), :], bufs[p].at[pl.ds(0, rows), :],
                              sem_ref.at[p]).wait()

    @pl.when(i == 0)
    def _():
        for p in range(N_BUF):
            bufs[p][...] = jnp.zeros_like(bufs[p])
        for d in range(LOOKAHEAD):
            gather(tok_refs[d], d, d)

    @pl.when((i < n_used) & ((i == 0) | (be_ref[i] != be_ref[jnp.maximum(i - 1, 0)])))
    def _():
        wgb_ref[...] = wg_ref[0].astype(BF16)
        wub_ref[...] = wu_ref[0].astype(BF16)
        wdb_ref[...] = wd_ref[0].astype(BF16)

    for p in range(N_BUF):
        @pl.when((turn == p) & (i < n_used))
        def _(p=p):
            wait(p, i)
            gather(tok_refs[LOOKAHEAD], (p + LOOKAHEAD) % N_BUF, i + LOOKAHEAD)
            x = _from_row_tiles(bufs[p], tb).astype(BF16)
            g = jnp.dot(x, wgb_ref[...], preferred_element_type=F32)
            u = jnp.dot(x, wub_ref[...], preferred_element_type=F32)
            hid = (g * _sigmoid(g) * u).astype(BF16)
            _to_row_tiles(y_ref, jnp.dot(hid, wdb_ref[...], preferred_element_type=F32))

    @pl.when(i >= n_used)
    def _():
        y_ref[...] = jnp.zeros_like(y_ref)


def _experts(h1r, slot_row, block_expert, n_used, n_fetch, w_gate, w_up, w_down):
    n_slots = slot_row.shape[0]
    nblk = n_slots // EXPERT_TILE
    tok3 = slot_row.reshape(nblk, 1, EXPERT_TILE)
    ahead = lambda d: pl.BlockSpec((1, 1, EXPERT_TILE), lambda i, be, nu, nf: (jnp.minimum(i + d, nblk - 1), 0, 0),
                                   memory_space=pltpu.SMEM)
    tile_buf = pltpu.VMEM((EXPERT_TILE * ROW_TILES, LANES), F32)
    grid_spec = pltpu.PrefetchScalarGridSpec(
        num_scalar_prefetch=3,
        grid=(nblk,),
        in_specs=[ahead(d) for d in range(N_BUF)] + [
            pl.BlockSpec(memory_space=pl.ANY),
            pl.BlockSpec((1, D_MODEL, D_EXPERT), lambda i, be, nu, nf: (be[i], 0, 0)),
            pl.BlockSpec((1, D_MODEL, D_EXPERT), lambda i, be, nu, nf: (be[i], 0, 0)),
            pl.BlockSpec((1, D_EXPERT, D_MODEL), lambda i, be, nu, nf: (be[i], 0, 0))],
        out_specs=pl.BlockSpec((EXPERT_TILE * ROW_TILES, LANES), lambda i, be, nu, nf: (i, 0)),
        scratch_shapes=[tile_buf] * N_BUF + [
            pltpu.SemaphoreType.DMA((N_BUF,)),
            pltpu.VMEM((D_MODEL, D_EXPERT), BF16), pltpu.VMEM((D_MODEL, D_EXPERT), BF16),
            pltpu.VMEM((D_EXPERT, D_MODEL), BF16)],
    )
    return pl.pallas_call(
        _expert_kernel,
        grid_spec=grid_spec,
        out_shape=jax.ShapeDtypeStruct((n_slots * ROW_TILES, LANES), F32),
        compiler_params=_cparams(1),
        name="moe_experts",
    )(block_expert, n_used, n_fetch, *([tok3] * N_BUF), h1r, w_gate, w_up, w_down)


def _combine_kernel(*refs, n_tiles, to_batch_major):
    pos_refs = refs[:N_BUF]
    yb_hbm, gate_ref, h_ref, g_ref, b_ref = refs[N_BUF:N_BUF + 5]
    if to_batch_major:
        o_ref, ybuf_ref, sem_ref, hs_ref = refs[N_BUF + 5:]
    else:
        w_ref, o_ref, gg_ref, rec_ref, ybuf_ref, sem_ref = refs[N_BUF + 5:]
    i = pl.program_id(0)
    tm = COMBINE_TILE
    turn = lax.rem(i, N_BUF)

    def gather(pos_ref, p):
        for r in range(tm):
            for kk in range(TOP_K_FINE):
                row = pl.multiple_of(pos_ref[0, 0, TOP_K_FINE * r + kk], ROW_TILES)
                pltpu.make_async_copy(yb_hbm.at[pl.ds(row, ROW_TILES), :],
                                      ybuf_ref.at[p, kk, pl.ds(ROW_TILES * r, ROW_TILES), :],
                                      sem_ref.at[p]).start(priority=kk)

    def finish(p, look_ahead):
        for kk in range(TOP_K_FINE):
            pltpu.make_async_copy(yb_hbm.at[pl.ds(0, tm * ROW_TILES), :], ybuf_ref.at[p, kk],
                                  sem_ref.at[p]).wait()
        if look_ahead:
            gather(pos_refs[LOOKAHEAD], (p + LOOKAHEAD) % N_BUF)
        gts = gate_ref[...]
        moe = (_from_row_tiles(ybuf_ref.at[p, 0], tm) * gts[:, 0:1]
               + _from_row_tiles(ybuf_ref.at[p, 1], tm) * gts[:, 1:2])
        h2 = _layer_norm_rows(ALPHA * h_ref[...] + moe, g_ref[...], b_ref[...])
        if not to_batch_major:
            o_ref[...] = h2
            z = jnp.dot(h2.astype(BF16), w_ref[...], preferred_element_type=F32)
            gg_ref[...] = _gelu(z[:, :D_RNN]).astype(BF16)
            rec_ref[...] = z[:, D_RNN:]
        else:
            tl = tm // NB
            for s in range(D_MODEL // LANES):
                hs_ref[s] = h2[:, LANES * s:LANES * (s + 1)]
            for b in range(NB):
                for s in range(D_MODEL // LANES):
                    o_ref[b, :, LANES * s:LANES * (s + 1)] = hs_ref[s, pl.ds(b, tl, stride=NB), :]

    @pl.when(i == 0)
    def _():
        for d in range(LOOKAHEAD):
            gather(pos_refs[d], d)

    for p in range(N_BUF):
        @pl.when((turn == p) & (i < n_tiles - LOOKAHEAD))
        def _(p=p):
            finish(p, True)

    @pl.when(i >= n_tiles - LOOKAHEAD)
    def _():
        finish(turn, False)


def _combine(yb, pos, gates, h1, ln_g, ln_b, w_next=None):
    final = w_next is None
    t_rows = h1.shape[0]
    tm = COMBINE_TILE
    first_tile = (LANES * NB) // tm if final else 0
    n_tiles = t_rows // tm - first_tile
    pos3 = pos.reshape(t_rows // tm, 1, TOP_K_FINE * tm)
    g2 = ln_g.astype(F32).reshape(1, D_MODEL)
    b2 = ln_b.astype(F32).reshape(1, D_MODEL)
    full = lambda a: pl.BlockSpec(a.shape, lambda i: (0,) * a.ndim)
    last = t_rows // tm - 1
    ahead = lambda d: pl.BlockSpec((1, 1, TOP_K_FINE * tm),
                                   lambda i: (jnp.minimum(i + first_tile + d, last), 0, 0),
                                   memory_space=pltpu.SMEM)
    in_specs = [ahead(d) for d in range(N_BUF)] + [
                pl.BlockSpec(memory_space=pl.ANY),
                pl.BlockSpec((tm, TOP_K_FINE), lambda i: (i + first_tile, 0)),
                pl.BlockSpec((tm, D_MODEL), lambda i: (i + first_tile, 0)),
                full(g2), full(b2)]
    scratch = [pltpu.VMEM((N_BUF, TOP_K_FINE, tm * ROW_TILES, LANES), F32), pltpu.SemaphoreType.DMA((N_BUF,))]
    if final:
        tl = tm // NB
        seq = t_rows // NB - LANES
        out_specs = pl.BlockSpec((NB, tl, D_MODEL), lambda i: (0, i, 0))
        out_shape = jax.ShapeDtypeStruct((NB, seq, D_MODEL), F32)
        scratch.append(pltpu.VMEM((D_MODEL // LANES, tm, LANES), F32))
        operands = (yb, gates, h1, g2, b2)
    else:
        row = lambda width: pl.BlockSpec((tm, width), lambda i: (i, 0))
        in_specs.append(full(w_next))
        out_specs = [row(D_MODEL), row(D_RNN), row(D_RNN)]
        out_shape = [jax.ShapeDtypeStruct((t_rows, D_MODEL), F32),
                     jax.ShapeDtypeStruct((t_rows, D_RNN), BF16),
                     jax.ShapeDtypeStruct((t_rows, D_RNN), F32)]
        operands = (yb, gates, h1, g2, b2, w_next)
    return pl.pallas_call(
        functools.partial(_combine_kernel, n_tiles=n_tiles, to_batch_major=final),
        grid=(n_tiles,),
        in_specs=in_specs, out_specs=out_specs, out_shape=out_shape,
        scratch_shapes=scratch,
        compiler_params=_cparams(1),
        name="moe_combine_final" if final else "moe_combine_inproj",
    )(*([pos3] * N_BUF), *operands)


def _moe_block(h1, h1r, route, counts, w_gate, w_up, w_down, ln_g, ln_b, w_next=None):
    t_rows = h1.shape[0]
    n_assign = (t_rows - PAD_ROWS) * TOP_K_FINE
    nblk = -(-(n_assign + N_EXPERTS * (EXPERT_TILE - 1)) // EXPERT_TILE)
    n_slots = nblk * EXPERT_TILE
    slot_row, block_expert, n_used, n_fetch, pos, gates = _slots(route, counts, n_slots)
    yb = _experts(h1r, slot_row, block_expert, n_used, n_fetch, w_gate, w_up, w_down)
    return _combine(yb, pos, gates, h1, ln_g, ln_b, w_next)


CONV_HALO = (CONV_WIDTH - 1) * NB


def _rglru_kernel(rec_ref, gg_ref, cw_ref, cb_ref, wax_ref, ba_ref, bx_ref, ca_ref,
                  y_ref, rp_ref, a_ref, b_ref, st_ref, *, steps):
    i = pl.program_id(0)
    rows = steps * NB
    sigmoid = lambda x: 1.0 / (1.0 + jnp.exp2(x * (-LOG2_E)))

    @pl.when(i == 0)
    def _():
        st_ref[...] = jnp.zeros_like(st_ref)
        rp_ref[0:CONV_HALO, :] = jnp.zeros((CONV_HALO, D_RNN), F32)

    rp_ref[CONV_HALO:CONV_HALO + rows, :] = rec_ref[...]
    has_pad = i * rows < PAD_ROWS
    real = (i * rows + lax.broadcasted_iota(jnp.int32, (rows, 1), 0)) >= PAD_ROWS

    @pl.when(has_pad)
    def _():
        rp_ref[CONV_HALO:CONV_HALO + rows, :] = jnp.where(real, rp_ref[CONV_HALO:CONV_HALO + rows, :], 0.0)

    xc = cb_ref[...] + sum(rp_ref[NB * j:NB * j + rows, :] * cw_ref[j:j + 1, :] for j in range(CONV_WIDTH))
    tail = rp_ref[rows:rows + CONV_HALO, :]
    rp_ref[0:CONV_HALO, :] = tail

    for n in range(LRU_BLOCKS):
        cs = slice(LRU_BLOCK_W * n, LRU_BLOCK_W * (n + 1))
        xb = xc[:, cs]
        ra = jnp.dot(xb.astype(BF16), wax_ref[n], preferred_element_type=F32)
        r = sigmoid(ra[:, :LRU_BLOCK_W] + ba_ref[:, cs])
        ig = sigmoid(ra[:, LRU_BLOCK_W:] + bx_ref[:, cs])
        a = jnp.exp2(r * ca_ref[:, cs])
        a_ref[:, cs] = a
        b_ref[:, cs] = jnp.sqrt(1.0 - a * a) * (ig * xb)

    @pl.when(has_pad)
    def _():
        b_ref[...] = jnp.where(real, b_ref[...], 0.0)

    def step(t, h):
        r0 = pl.multiple_of(t * NB, NB)
        hn = a_ref[pl.ds(r0, NB), :] * h + b_ref[pl.ds(r0, NB), :]
        b_ref[pl.ds(r0, NB), :] = hn
        return hn

    st_ref[...] = lax.fori_loop(0, steps, step, st_ref[...])
    y_ref[...] = (gg_ref[...].astype(F32) * b_ref[...]).astype(BF16)


def _rglru(rec, gg, conv_w, conv_b, w_a, b_a, w_x, b_x, lru_lambda):
    t_rows = rec.shape[0]
    steps = SCAN_STEPS
    rows = steps * NB
    cw = conv_w.astype(F32)
    cb = conv_b.astype(F32).reshape(1, D_RNN)
    wax = jnp.concatenate([w_a.astype(F32), w_x.astype(F32)], axis=2).astype(BF16)
    ba = b_a.astype(F32).reshape(1, D_RNN)
    bx = b_x.astype(F32).reshape(1, D_RNN)
    ca = (-LRU_C * LOG2_E) * jax.nn.softplus(-lru_lambda.astype(F32)).reshape(1, D_RNN)
    full = lambda a: pl.BlockSpec(a.shape, lambda i: (0,) * a.ndim)
    row = pl.BlockSpec((rows, D_RNN), lambda i: (i, 0))
    return pl.pallas_call(
        functools.partial(_rglru_kernel, steps=steps),
        grid=(t_rows // rows,),
        in_specs=[row, row, full(cw), full(cb), full(wax), full(ba), full(bx), full(ca)],
        out_specs=row,
        out_shape=jax.ShapeDtypeStruct((t_rows, D_RNN), BF16),
        scratch_shapes=[pltpu.VMEM((rows + CONV_HALO, D_RNN), F32), pltpu.VMEM((rows, D_RNN), F32),
                        pltpu.VMEM((rows, D_RNN), F32), pltpu.VMEM((NB, D_RNN), F32)],
        compiler_params=_cparams(1),
        name="l1_rglru",
    )(rec, gg, cw, cb, wax, ba, bx, ca)


def kernel(x, meta, l0_ln1_g, l0_ln1_b, l0_w_in, l0_s5_lambda_re, l0_s5_lambda_im, l0_s5_log_dt, l0_s5_b_re, l0_s5_b_im, l0_s5_c_re, l0_s5_c_im, l0_s5_d, l0_s5_w_glu, l0_s5_b_glu, l0_da_lq1, l0_da_lk1, l0_da_lq2, l0_da_lk2, l0_da_subln_g, l0_w_out, l0_ln2_g, l0_ln2_b, l0_moe_w_coarse, l0_moe_b_coarse, l0_moe_w_fine, l0_moe_b_fine, l0_moe_w_gate, l0_moe_w_up, l0_moe_w_down, l1_ln1_g, l1_ln1_b, l1_w_in, l1_conv_w, l1_conv_b, l1_lru_w_a, l1_lru_b_a, l1_lru_w_x, l1_lru_b_x, l1_lru_lambda, l1_w_out, l1_ln2_g, l1_ln2_b, l1_moe_w_coarse, l1_moe_b_coarse, l1_moe_w_fine, l1_moe_b_fine, l1_moe_w_gate, l1_moe_w_up, l1_moe_w_down):
    bsz, seq, _ = x.shape
    assert bsz == NB and seq % Q_TILE == 0
    dt = x.dtype
    lp = FRONT_PAD + N_META + seq

    col_scale = jnp.concatenate([jnp.ones((S5_WIDTH,), F32),
                                 jnp.full((DA_WIDTH,), DA_HEAD_DIM ** -0.5, F32),
                                 jnp.ones((2 * DA_WIDTH,), F32)])
    w_in0 = (l0_w_in.astype(F32) * col_scale[None, :]).astype(BF16)
    h, u, q, k, v = _inproj0(x, meta, w_in0, lp)
    s5p = _s5_params(l0_s5_lambda_re, l0_s5_lambda_im, l0_s5_log_dt,
                     l0_s5_b_re, l0_s5_b_im, l0_s5_c_re, l0_s5_c_im)
    y_s5 = _s5(u, *s5p, l0_s5_d, l0_s5_w_glu, l0_s5_b_glu)
    lam_init = 0.8 - 0.6 * math.exp(-0.3 * 0)
    lam = (jnp.exp(jnp.sum(l0_da_lq1.astype(F32) * l0_da_lk1.astype(F32)))
           - jnp.exp(jnp.sum(l0_da_lq2.astype(F32) * l0_da_lk2.astype(F32))) + lam_init)
    y_da = _diffattn(q, k, v, lam, l0_da_subln_g, lam_init)
    router0 = _router_weights(l0_moe_w_coarse, l0_moe_b_coarse, l0_moe_w_fine, l0_moe_b_fine)
    h, hr, route, counts = _outproj(y_s5, y_da, h, l0_w_out, l0_ln1_g, l0_ln1_b, router0)
    h, gg, rec = _moe_block(h, hr, route, counts, l0_moe_w_gate, l0_moe_w_up, l0_moe_w_down,
                            l0_ln2_g, l0_ln2_b, l1_w_in.astype(BF16))

    y = _rglru(rec, gg, l1_conv_w, l1_conv_b, l1_lru_w_a, l1_lru_b_a, l1_lru_w_x, l1_lru_b_x, l1_lru_lambda)
    router1 = _router_weights(l1_moe_w_coarse, l1_moe_b_coarse, l1_moe_w_fine, l1_moe_b_fine)
    h, hr, route, counts = _outproj(y, None, h, l1_w_out, l1_ln1_g, l1_ln1_b, router1)
    out = _moe_block(h, hr, route, counts, l1_moe_w_gate, l1_moe_w_up, l1_moe_w_down, l1_ln2_g, l1_ln2_b)
    return out.astype(dt)
```
